```python
import math
import jax
import jax.numpy as jnp
from jax import lax
import numpy as np

D_MODEL = 1024
BATCH = 8
SEQ = 2048
DEPTH = 2

HEAD_DIM = 64
HEADS_PER_MIXER = 4
GROUP_W = HEADS_PER_MIXER * HEAD_DIM
DIFF_QK_DIM = HEAD_DIM // 2
IDX_HEADS = 8
IDX_DIM = 32
TOPK_MAX = 256
Q_BLOCK = 128
N_GROUPS = 4
EXPERTS_PER_GROUP = 8
EXPERT_FF = 256
EXPERT_TOPK = 2
N_ALIBI_HEADS = 2 * HEADS_PER_MIXER
RMS_EPS = 1e-6
NEG_INF = -1e30

IN_SPLITS = (GROUP_W, GROUP_W, GROUP_W, HEADS_PER_MIXER,
             GROUP_W, GROUP_W, GROUP_W,
             GROUP_W, GROUP_W, GROUP_W,
             GROUP_W, HEAD_DIM, HEAD_DIM,
             IDX_HEADS * IDX_DIM, IDX_DIM, IDX_HEADS)
P_IN = sum(IN_SPLITS)

kernel_name = 'hybrid_fox_stickbreak_diff_dsa_hmoe_adaln'


def rms_norm(x, g):
    xf = x.astype(jnp.float32)
    y = xf * lax.rsqrt(jnp.mean(xf * xf, axis=-1, keepdims=True) + RMS_EPS)
    return (y * g.astype(jnp.float32)).astype(x.dtype)


def sweep_query_blocks(fn, *q_side):
    B, S = q_side[0].shape[:2]
    nb = S // Q_BLOCK
    blocks = tuple(a.reshape(B, nb, Q_BLOCK, *a.shape[2:]).swapaxes(0, 1) for a in q_side)
    out = lax.map(lambda args: fn(args[0] * Q_BLOCK + jnp.arange(Q_BLOCK), *args[1:]),
                  (jnp.arange(nb),) + blocks)
    return out.swapaxes(0, 1).reshape(B, S, *out.shape[3:])


def forgetting_attention(q, k, v, log_f):
    S, d = q.shape[1], q.shape[-1]
    kpos = jnp.arange(S)
    kf = k.astype(jnp.float32)
    cum_bsh = jnp.cumsum(log_f, axis=1)
    cum = cum_bsh.transpose(0, 2, 1)

    def block(qpos, qb, cb):
        s = jnp.einsum('bqhd,bkhd->bhqk', qb.astype(jnp.float32), kf) * d ** -0.5
        s = s + cb.transpose(0, 2, 1)[..., None] - cum[:, :, None, :]
        s = jnp.where(qpos[:, None] >= kpos[None, :], s, NEG_INF)
        p = jax.nn.softmax(s, axis=-1)
        return jnp.einsum('bhqk,bkhd->bqhd', p.astype(v.dtype), v)

    return sweep_query_blocks(block, q, cum_bsh)


def stick_breaking_attention(q, k, v):
    S, d = q.shape[1], q.shape[-1]
    kpos = jnp.arange(S)
    kf = k.astype(jnp.float32)

    def block(qpos, qb):
        z = jnp.einsum('bqhd,bkhd->bhqk', qb.astype(jnp.float32), kf) * d ** -0.5
        strict = qpos[:, None] > kpos[None, :]
        log_beta = jax.nn.log_sigmoid(z)
        log_1m = jnp.where(strict, jax.nn.log_sigmoid(-z), 0.0)
        suffix = lax.cumsum(log_1m, axis=3, reverse=True) - log_1m
        w = jnp.where(strict, jnp.exp(log_beta + suffix), 0.0)
        return jnp.einsum('bhqk,bkhd->bqhd', w.astype(v.dtype), v)

    return sweep_query_blocks(block, q)


def differential_attention(q, k, v, slopes, lam, subln_g, lambda_init):
    S, dq = q.shape[1], q.shape[-1]
    kpos = jnp.arange(S)
    kf = k.astype(jnp.float32)

    def block(qpos, qb):
        dist = (qpos[:, None] - kpos[None, :]).astype(jnp.float32)
        s = jnp.einsum('bqhcd,bkhcd->bhcqk', qb.astype(jnp.float32), kf) * dq ** -0.5
        s = s - slopes[:, None, None, None] * dist
        s = jnp.where(dist >= 0, s, NEG_INF)
        p = jax.nn.softmax(s, axis=-1)
        a = p[:, :, 0] - lam * p[:, :, 1]
        return jnp.einsum('bhqk,bkhd->bqhd', a.astype(v.dtype), v)

    o = sweep_query_blocks(block, q)
    return rms_norm(o, subln_g) * (1.0 - lambda_init)


def indexed_sparse_attention(q, k, v, iq, ik, iw, slopes, topk):
    S, d = q.shape[1], q.shape[-1]
    kpos = jnp.arange(S)
    ikf = ik.astype(jnp.float32)
    gather = jax.vmap(lambda t, i: t[i])

    def block(qpos, qb, iqb, iwb):
        rel = jax.nn.relu(jnp.einsum('bqhe,bse->bqhs', iqb.astype(jnp.float32), ikf) * IDX_DIM ** -0.5)
        score = jnp.einsum('bqh,bqhs->bqs', iwb.astype(jnp.float32) * IDX_HEADS ** -0.5, rel)
        score = jnp.where(qpos[:, None] >= kpos[None, :], score, NEG_INF)
        _, idx = lax.top_k(score, topk)
        ks = gather(k, idx).astype(jnp.float32)
        vs = gather(v, idx)
        dist = (qpos[None, :, None] - idx).astype(jnp.float32)
        s = jnp.einsum('bqhd,bqkd->bhqk', qb.astype(jnp.float32), ks) * d ** -0.5
        s = s - slopes[None, :, None, None] * dist[:, None]
        s = jnp.where(dist[:, None] >= 0, s, NEG_INF)
        p = jax.nn.softmax(s, axis=-1)
        return jnp.einsum('bhqk,bqkd->bqhd', p.astype(vs.dtype), vs)

    return sweep_query_blocks(block, q, iq, iw)


def hierarchical_moe(h, w_group, b_group, w_expert, b_expert, w1, w3, w2):
    B, S, D = h.shape
    t = h.reshape(B * S, D)
    g_logits = jnp.einsum('nd,dg->ng', t, w_group).astype(jnp.float32) + b_group.astype(jnp.float32)
    g_onehot = jax.nn.one_hot(jnp.argmax(g_logits, axis=-1), N_GROUPS, dtype=jnp.float32)
    g_prob = jnp.sum(jax.nn.softmax(g_logits, axis=-1) * g_onehot, axis=-1, keepdims=True)
    e_logits = (jnp.einsum('nd,de->ne', t, w_expert).astype(jnp.float32)
                + b_expert.astype(jnp.float32)).reshape(-1, N_GROUPS, EXPERTS_PER_GROUP)
    e_sel = jnp.einsum('nge,ng->ne', e_logits, g_onehot)
    top_val, top_idx = lax.top_k(e_sel, EXPERT_TOPK)
    top_w = jax.nn.softmax(top_val, axis=-1) * g_prob
    in_group = jnp.sum(jax.nn.one_hot(top_idx, EXPERTS_PER_GROUP, dtype=jnp.float32) * top_w[..., None], axis=1)
    comb = (g_onehot[:, :, None] * in_group[:, None, :]).astype(t.dtype)
    out = jnp.zeros_like(t)
    for g in range(N_GROUPS):
        hid = jax.nn.silu(jnp.einsum('nd,edf->nef', t, w1[g])) * jnp.einsum('nd,edf->nef', t, w3[g])
        out = out + jnp.einsum('nef,efd->nd', hid * comb[:, g, :, None], w2[g])
    return out.reshape(B, S, D)


def setup_inputs(seed: int = 0) -> dict:
    key = jax.random.key(seed)
    ks = jax.random.split(key, 32)
    L, D = DEPTH, D_MODEL
    G, E, F = N_GROUPS, EXPERTS_PER_GROUP, EXPERT_FF

    def nrm(k, shape, s):
        return jax.random.normal(k, shape, jnp.float32) * s

    return {
        'x': nrm(ks[0], (BATCH, SEQ, D), 1.0),
        'c': nrm(ks[1], (BATCH, D), 1.0),
        'ada_w': nrm(ks[2], (L, D, 6 * D), 0.5 * D ** -0.5),
        'ada_b': nrm(ks[3], (L, 6 * D), 0.02),
        'norm1_g': 1.0 + nrm(ks[4], (L, D), 0.02),
        'norm2_g': 1.0 + nrm(ks[5], (L, D), 0.02),
        'w_in': nrm(ks[6], (L, D, P_IN), D ** -0.5),
        'b_f': 3.0 + nrm(ks[7], (L, HEADS_PER_MIXER), 0.5),
        'qn_a': 1.0 + nrm(ks[8], (L, HEAD_DIM), 0.02),
        'kn_a': 1.0 + nrm(ks[9], (L, HEAD_DIM), 0.02),
        'qn_c': 1.0 + nrm(ks[10], (L, DIFF_QK_DIM), 0.02),
        'kn_c': 1.0 + nrm(ks[11], (L, DIFF_QK_DIM), 0.02),
        'lam_q1': nrm(ks[12], (L, DIFF_QK_DIM), 0.1),
        'lam_k1': nrm(ks[13], (L, DIFF_QK_DIM), 0.1),
        'lam_q2': nrm(ks[14], (L, DIFF_QK_DIM), 0.1),
        'lam_k2': nrm(ks[15], (L, DIFF_QK_DIM), 0.1),
        'subln_g': 1.0 + nrm(ks[16], (L, HEAD_DIM), 0.02),
        'qn_d': 1.0 + nrm(ks[17], (L, HEAD_DIM), 0.02),
        'kn_d': 1.0 + nrm(ks[18], (L, HEAD_DIM), 0.02),
        'mix_beta': 1.0 + nrm(ks[19], (L, D), 0.02),
        'w_out': nrm(ks[20], (L, D, D), D ** -0.5),
        'w_group': nrm(ks[21], (L, D, G), D ** -0.5),
        'b_group': nrm(ks[22], (L, G), 0.01),
        'w_expert': nrm(ks[23], (L, D, G * E), D ** -0.5),
        'b_expert': nrm(ks[24], (L, G * E), 0.01),
        'w1': nrm(ks[25], (L, G, E, D, F), D ** -0.5),
        'w3': nrm(ks[26], (L, G, E, D, F), D ** -0.5),
        'w2': nrm(ks[27], (L, G, E, F, D), F ** -0.5),
    }


def reference(x, c, ada_w, ada_b, norm1_g, norm2_g, w_in, b_f, qn_a, kn_a, qn_c, kn_c,
              lam_q1, lam_k1, lam_q2, lam_k2, subln_g, qn_d, kn_d, mix_beta, w_out,
              w_group, b_group, w_expert, b_expert, w1, w3, w2):
    B, S, _ = x.shape
    H = HEADS_PER_MIXER
    topk = min(TOPK_MAX, S // 4)
    offsets = tuple(np.cumsum(IN_SPLITS)[:-1].tolist())
    slopes = jnp.exp2(-8.0 * jnp.arange(1, N_ALIBI_HEADS + 1, dtype=jnp.float32) / N_ALIBI_HEADS)
    slopes_c, slopes_d = slopes[0::2], slopes[1::2]
    c_act = jax.nn.silu(c)

    def heads(t):
        return t.reshape(B, S, H, -1)

    for l in range(DEPTH):
        mod = jnp.einsum('bd,de->be', c_act, ada_w[l]) + ada_b[l]
        shift1, scale1, gate1, shift2, scale2, gate2 = jnp.split(mod[:, None, :], 6, axis=-1)

        h = rms_norm(x, norm1_g[l]) * (1.0 + scale1) + shift1
        (a_q, a_k, a_v, a_f, b_q, b_k, b_v, c_q, c_k, c_v,
         d_q, d_k, d_v, i_q, i_k, i_w) = jnp.split(h @ w_in[l], offsets, axis=-1)

        log_f = jax.nn.log_sigmoid(a_f.astype(jnp.float32) + b_f[l].astype(jnp.float32))
        o_a = forgetting_attention(rms_norm(heads(a_q), qn_a[l]), rms_norm(heads(a_k), kn_a[l]),
                                   heads(a_v), log_f)

        o_b = stick_breaking_attention(heads(b_q), heads(b_k), heads(b_v))

        lambda_init = 0.8 - 0.6 * math.exp(-0.3 * l)
        lam = (jnp.exp(jnp.sum(lam_q1[l].astype(jnp.float32) * lam_k1[l].astype(jnp.float32)))
               - jnp.exp(jnp.sum(lam_q2[l].astype(jnp.float32) * lam_k2[l].astype(jnp.float32)))
               + lambda_init)
        cq = rms_norm(c_q.reshape(B, S, H, 2, DIFF_QK_DIM), qn_c[l])
        ck = rms_norm(c_k.reshape(B, S, H, 2, DIFF_QK_DIM), kn_c[l])
        o_c = differential_attention(cq, ck, heads(c_v), slopes_c, lam, subln_g[l], lambda_init)

        o_d = indexed_sparse_attention(rms_norm(heads(d_q), qn_d[l]), rms_norm(d_k, kn_d[l]), d_v,
                                       i_q.reshape(B, S, IDX_HEADS, IDX_DIM), i_k, i_w,
                                       slopes_d, topk)

        mix = jnp.concatenate([o.reshape(B, S, GROUP_W) for o in (o_a, o_b, o_c, o_d)],
                              axis=-1) * mix_beta[l]
        x = x + gate1 * (mix @ w_out[l])

        h = rms_norm(x, norm2_g[l]) * (1.0 + scale2) + shift2
        x = x + gate2 * hierarchical_moe(h, w_group[l], b_group[l], w_expert[l], b_expert[l],
                                         w1[l], w3[l], w2[l])
    return x
```

```python
import functools
import math

import jax
import jax.numpy as jnp
from jax import lax
from jax.experimental import pallas as pl
from jax.experimental.pallas import tpu as pltpu

F32 = jnp.float32
BF16 = jnp.bfloat16
I32 = jnp.int32

HEAD_DIM = 64
HEADS = 4
GROUP_W = HEADS * HEAD_DIM
DIFF_DIM = HEAD_DIM // 2
IDX_HEADS = 8
IDX_DIM = 32
TOPK_MAX = 256
N_GROUPS = 4
EXPERTS_PER_GROUP = 8
N_EXPERTS = N_GROUPS * EXPERTS_PER_GROUP
EXPERT_FF = 256
RMS_EPS = 1e-6
NEG_INF = -1e30
INT_MIN = -(2 ** 31)

LANES = 128
ATTN_BLOCK = 256
SELECT_ROWS = 64
VMEM_LIMIT = 56 * 1024 * 1024

_NT = (((1,), (1,)), ((), ()))


def _params(*sem):
    return pltpu.CompilerParams(dimension_semantics=sem, vmem_limit_bytes=VMEM_LIMIT)


def _log_sigmoid(z):
    return jnp.minimum(z, 0.0) - jnp.log1p(jnp.exp(-jnp.abs(z)))


def _split_bf16(x, parts):
    out = []
    rem = x
    for _ in range(parts):
        p = rem.astype(BF16)
        out.append(p)
        rem = rem - p.astype(F32)
    return out


def _ada_kernel(c_ref, w_ref, b_ref, o_ref):
    c = c_ref[...]
    ca = c * jax.nn.sigmoid(c)
    o_ref[...] = jnp.dot(ca, w_ref[...], precision=lax.Precision.HIGHEST,
                         preferred_element_type=F32) + b_ref[...]


def _ada_call(c, ada_w, ada_b):
    L, D, E = ada_w.shape
    B = c.shape[0]
    tn = 1536
    return pl.pallas_call(
        _ada_kernel,
        grid=(L, E // tn),
        in_specs=[pl.BlockSpec((B, D), lambda l, j: (0, 0)),
                  pl.BlockSpec((None, D, tn), lambda l, j: (l, 0, j)),
                  pl.BlockSpec((None, 1, tn), lambda l, j: (l, 0, j))],
        out_specs=pl.BlockSpec((None, B, tn), lambda l, j: (l, 0, j)),
        out_shape=jax.ShapeDtypeStruct((L, B, E), F32),
        compiler_params=_params("arbitrary", "arbitrary"),
        name="ada_mod",
    )(c, ada_w, ada_b.reshape(L, 1, E))


N_WIDE = 11


def _in_proj_kernel(x_ref, scale_ref, shift_ref, g1_ref, w_ref, gains_ref, g64_ref, g32_ref,
                    gd_ref, dkvg_ref, *out_refs):
    wide = out_refs[:N_WIDE]
    dkv_ref, misc_ref = out_refs[N_WIDE:]
    x = x_ref[...]
    ms = jnp.mean(x * x, axis=-1, keepdims=True)
    h = x * lax.rsqrt(ms + RMS_EPS) * g1_ref[...]
    h = h * (1.0 + scale_ref[...]) + shift_ref[...]
    hb = h.astype(BF16)

    def proj(i, width=GROUP_W):
        return jnp.dot(hb, w_ref[:, i * GROUP_W:i * GROUP_W + width], preferred_element_type=F32)

    def group_norm(sec, gmat_ref, gain_row):
        msq = jnp.dot((sec * sec).astype(BF16), gmat_ref[...], preferred_element_type=F32)
        return sec * lax.rsqrt(msq + RMS_EPS) * gains_ref[gain_row:gain_row + 1, :]

    wide[0][...] = group_norm(proj(0), g64_ref, 0).astype(BF16)
    wide[1][...] = group_norm(proj(1), g64_ref, 1).astype(BF16)
    wide[2][...] = proj(2).astype(BF16)
    wide[3][...] = (proj(3) * HEAD_DIM ** -0.5).astype(BF16)
    wide[4][...] = proj(4).astype(BF16)
    wide[5][...] = proj(5).astype(BF16)
    wide[6][...] = group_norm(proj(6), g32_ref, 2).astype(BF16)
    wide[7][...] = group_norm(proj(7), g32_ref, 3).astype(BF16)
    wide[8][...] = proj(8).astype(BF16)
    wide[9][...] = group_norm(proj(9), g64_ref, 4).astype(BF16)
    wide[10][...] = (proj(10) * IDX_DIM ** -0.5).astype(BF16)

    sec = proj(N_WIDE, LANES)
    msq = jnp.dot((sec * sec).astype(BF16), gd_ref[...], preferred_element_type=F32)
    lane = lax.broadcasted_iota(I32, sec.shape, 1)
    fac = jnp.where(lane < HEAD_DIM, lax.rsqrt(msq + RMS_EPS), 1.0)
    dkv_ref[...] = (sec * fac * dkvg_ref[...]).astype(BF16)

    wm = w_ref[:, N_WIDE * GROUP_W + LANES:N_WIDE * GROUP_W + 2 * LANES]
    h_hi, h_lo = _split_bf16(h, 2)
    misc_ref[...] = (jnp.dot(h_hi, wm, preferred_element_type=F32)
                     + jnp.dot(h_lo, wm, preferred_element_type=F32))


def _in_proj_call(xf, scale1, shift1, g1, w_all, gains, g64, g32, gd, dkvg, seq, tm):
    N, D = xf.shape
    P = w_all.shape[1]
    per_b = seq // tm
    row = lambda i: (i, 0)
    full = lambda i: (0, 0)
    out_shape = ([jax.ShapeDtypeStruct((N, GROUP_W), BF16)] * N_WIDE
                 + [jax.ShapeDtypeStruct((N, LANES), BF16), jax.ShapeDtypeStruct((N, LANES), F32)])
    out_specs = ([pl.BlockSpec((tm, GROUP_W), row)] * N_WIDE
                 + [pl.BlockSpec((tm, LANES), row), pl.BlockSpec((tm, LANES), row)])
    return pl.pallas_call(
        _in_proj_kernel,
        grid=(N // tm,),
        in_specs=[pl.BlockSpec((tm, D), row),
                  pl.BlockSpec((None, 1, D), lambda i: (i // per_b, 0, 0)),
                  pl.BlockSpec((None, 1, D), lambda i: (i // per_b, 0, 0)),
                  pl.BlockSpec((1, D), full),
                  pl.BlockSpec((D, P), full),
                  pl.BlockSpec(gains.shape, full),
                  pl.BlockSpec(g64.shape, full),
                  pl.BlockSpec(g32.shape, full),
                  pl.BlockSpec(gd.shape, full),
                  pl.BlockSpec(dkvg.shape, full)],
        out_specs=out_specs,
        out_shape=out_shape,
        compiler_params=_params("arbitrary"),
        name="in_proj",
    )(xf, scale1, shift1, g1, w_all, gains, g64, g32, gd, dkvg)


def _cum_kernel(af_ref, bf_ref, tri_ref, o_ref, *, blk):
    z = af_ref[...] + bf_ref[...]
    lf = _log_sigmoid(z)
    carry = jnp.zeros((lf.shape[0], 1), F32)
    tri = tri_ref[...]
    for j in range(lf.shape[1] // blk):
        piece = lf[:, j * blk:(j + 1) * blk]
        c = carry
        for part in _split_bf16(piece, 3):
            c = c + jnp.dot(part, tri, preferred_element_type=F32)
        o_ref[:, j * blk:(j + 1) * blk] = c
        carry = c[:, blk - 1:blk]


def _cum_call(af_t, bf_col, tri_incl):
    B, R, S = af_t.shape
    blk = tri_incl.shape[0]
    return pl.pallas_call(
        functools.partial(_cum_kernel, blk=blk),
        grid=(B,),
        in_specs=[pl.BlockSpec((None, R, S), lambda b: (b, 0, 0)),
                  pl.BlockSpec((R, 1), lambda b: (0, 0)),
                  pl.BlockSpec((blk, blk), lambda b: (0, 0))],
        out_specs=pl.BlockSpec((None, R, S), lambda b: (b, 0, 0)),
        out_shape=jax.ShapeDtypeStruct((B, R, S), F32),
        compiler_params=_params("arbitrary"),
        name="forget_cumsum",
    )(af_t, bf_col, tri_incl)


def _attn_specs(seq, t, n_q_side, n_kv_side, kv_width):
    q_spec = pl.BlockSpec((None, t, GROUP_W), lambda b, i: (b, i, 0))
    kv_specs = [pl.BlockSpec((None, seq, w), lambda b, i: (b, 0, 0)) for w in kv_width]
    return [q_spec] * n_q_side, kv_specs


def _tri_masks(t):
    row = lax.broadcasted_iota(I32, (t, t), 0)
    col = lax.broadcasted_iota(I32, (t, t), 1)
    return row, col


def _attn_a_kernel(q_ref, k_ref, v_ref, cum_ref, o_ref, m_sc, l_sc, acc_sc, *, t):
    qi = pl.program_id(1)
    row, col = _tri_masks(t)
    causal = row >= col
    for h in range(HEADS):
        sl = slice(HEAD_DIM * h, HEAD_DIM * (h + 1))
        qh = q_ref[:, sl]

        def scores(j, qh=qh, sl=sl, h=h):
            kh = k_ref[pl.ds(pl.multiple_of(j * t, t), t), sl]
            s = lax.dot_general(qh, kh, _NT, preferred_element_type=F32)
            return s - cum_ref[h, pl.ds(j, 1), :]

        def vblock(j, sl=sl):
            return v_ref[pl.ds(pl.multiple_of(j * t, t), t), sl]

        s = jnp.where(causal, scores(qi), NEG_INF)
        m = jnp.max(s, axis=-1, keepdims=True)
        p = jnp.exp(s - m)
        m_sc[...] = m
        l_sc[...] = jnp.sum(p, axis=-1, keepdims=True)
        acc_sc[...] = jnp.dot(p.astype(BF16), vblock(qi), preferred_element_type=F32)

        def body(j, carry, scores=scores, vblock=vblock):
            s = scores(j)
            m_old = m_sc[...]
            m_new = jnp.maximum(m_old, jnp.max(s, axis=-1, keepdims=True))
            alpha = jnp.exp(m_old - m_new)
            p = jnp.exp(s - m_new)
            l_sc[...] = alpha * l_sc[...] + jnp.sum(p, axis=-1, keepdims=True)
            acc_sc[...] = alpha * acc_sc[...] + jnp.dot(p.astype(BF16), vblock(j),
                                                        preferred_element_type=F32)
            m_sc[...] = m_new
            return carry

        lax.fori_loop(0, qi, body, 0)
        o_ref[:, sl] = acc_sc[...] / l_sc[...]


def _attn_a_call(q, k, v, cum4):
    B, S, _ = q.shape
    t = ATTN_BLOCK
    nb = S // t
    q_specs, kv_specs = _attn_specs(S, t, 1, 2, (GROUP_W, GROUP_W))
    return pl.pallas_call(
        functools.partial(_attn_a_kernel, t=t),
        grid=(B, nb),
        in_specs=q_specs + kv_specs + [pl.BlockSpec((None, HEADS, nb, t), lambda b, i: (b, 0, 0, 0))],
        out_specs=pl.BlockSpec((None, t, GROUP_W), lambda b, i: (b, i, 0)),
        out_shape=jax.ShapeDtypeStruct((B, S, GROUP_W), F32),
        scratch_shapes=[pltpu.VMEM((t, 1), F32), pltpu.VMEM((t, 1), F32),
                        pltpu.VMEM((t, HEAD_DIM), F32)],
        compiler_params=_params("arbitrary", "arbitrary"),
        name="attn_forget",
    )(q, k, v, cum4)


def _attn_b_kernel(q_ref, k_ref, v_ref, later_ref, o_ref, r_sc, acc_sc, *, t):
    qi = pl.program_id(1)
    row, col = _tri_masks(t)
    strict = row > col
    later = later_ref[...]

    def suffix(lm):
        hi, lo = _split_bf16(lm, 2)
        return (jnp.dot(hi, later, preferred_element_type=F32)
                + jnp.dot(lo, later, preferred_element_type=F32))

    for h in range(HEADS):
        sl = slice(HEAD_DIM * h, HEAD_DIM * (h + 1))
        qh = q_ref[:, sl]

        def logits(j, qh=qh, sl=sl):
            kh = k_ref[pl.ds(pl.multiple_of(j * t, t), t), sl]
            return lax.dot_general(qh, kh, _NT, preferred_element_type=F32)

        def vblock(j, sl=sl):
            return v_ref[pl.ds(pl.multiple_of(j * t, t), t), sl]

        z = logits(qi)
        lb = _log_sigmoid(z)
        lm = jnp.where(strict, lb - z, 0.0)
        w = jnp.where(strict, jnp.exp(lb + suffix(lm)), 0.0)
        acc_sc[...] = jnp.dot(w.astype(BF16), vblock(qi), preferred_element_type=F32)
        r_sc[...] = jnp.sum(lm, axis=-1, keepdims=True)

        def body(i, carry, logits=logits, vblock=vblock):
            j = qi - 1 - i
            z = logits(j)
            lb = _log_sigmoid(z)
            lm = lb - z
            w = jnp.exp(lb + suffix(lm) + r_sc[...])
            acc_sc[...] += jnp.dot(w.astype(BF16), vblock(j), preferred_element_type=F32)
            r_sc[...] += jnp.sum(lm, axis=-1, keepdims=True)
            return carry

        lax.fori_loop(0, qi, body, 0)
        o_ref[:, sl] = acc_sc[...]


def _attn_b_call(q, k, v, later):
    B, S, _ = q.shape
    t = ATTN_BLOCK
    q_specs, kv_specs = _attn_specs(S, t, 1, 2, (GROUP_W, GROUP_W))
    return pl.pallas_call(
        functools.partial(_attn_b_kernel, t=t),
        grid=(B, S // t),
        in_specs=q_specs + kv_specs + [pl.BlockSpec((t, t), lambda b, i: (0, 0))],
        out_specs=pl.BlockSpec((None, t, GROUP_W), lambda b, i: (b, i, 0)),
        out_shape=jax.ShapeDtypeStruct((B, S, GROUP_W), F32),
        scratch_shapes=[pltpu.VMEM((t, 1), F32), pltpu.VMEM((t, HEAD_DIM), F32)],
        compiler_params=_params("arbitrary", "arbitrary"),
        name="attn_stick",
    )(q, k, v, later)


def _attn_c_kernel(q_ref, k_ref, v_ref, lamv_ref, subg_ref, o_ref, m_sc, l_sc, acc_sc,
                   *, t, slopes, lambda_init):
    qi = pl.program_id(1)
    row, col = _tri_masks(t)
    causal = row >= col
    lv = lamv_ref[...]
    lam = (jnp.exp(jnp.sum(lv[0:1] * lv[1:2], axis=-1, keepdims=True))
           - jnp.exp(jnp.sum(lv[2:3] * lv[3:4], axis=-1, keepdims=True)) + lambda_init)
    kpos = lax.broadcasted_iota(I32, (1, t), 1)

    for h in range(HEADS):
        vsl = slice(HEAD_DIM * h, HEAD_DIM * (h + 1))
        halves = []
        for c in range(2):
            sl = slice(HEAD_DIM * h + DIFF_DIM * c, HEAD_DIM * h + DIFF_DIM * (c + 1))
            qh = q_ref[:, sl]

            def scores(j, qh=qh, sl=sl, h=h):
                kh = k_ref[pl.ds(pl.multiple_of(j * t, t), t), sl]
                s = lax.dot_general(qh, kh, _NT, preferred_element_type=F32)
                return s + slopes[h] * (kpos + (j - qi) * t).astype(F32)

            def vblock(j, vsl=vsl):
                return v_ref[pl.ds(pl.multiple_of(j * t, t), t), vsl]

            s = jnp.where(causal, scores(qi), NEG_INF)
            m = jnp.max(s, axis=-1, keepdims=True)
            p = jnp.exp(s - m)
            m_sc[...] = m
            l_sc[...] = jnp.sum(p, axis=-1, keepdims=True)
            acc_sc[...] = jnp.dot(p.astype(BF16), vblock(qi), preferred_element_type=F32)

            def body(j, carry, scores=scores, vblock=vblock):
                s = scores(j)
                m_old = m_sc[...]
                m_new = jnp.maximum(m_old, jnp.max(s, axis=-1, keepdims=True))
                alpha = jnp.exp(m_old - m_new)
                p = jnp.exp(s - m_new)
                l_sc[...] = alpha * l_sc[...] + jnp.sum(p, axis=-1, keepdims=True)
                acc_sc[...] = alpha * acc_sc[...] + jnp.dot(p.astype(BF16), vblock(j),
                                                            preferred_element_type=F32)
                m_sc[...] = m_new
                return carry

            lax.fori_loop(0, qi, body, 0)
            halves.append(acc_sc[...] / l_sc[...])

        o = halves[0] - lam * halves[1]
        ms = jnp.mean(o * o, axis=-1, keepdims=True)
        o_ref[:, vsl] = o * lax.rsqrt(ms + RMS_EPS) * subg_ref[...] * (1.0 - lambda_init)


def _attn_c_call(q, k, v, lamv, subg, slopes, lambda_init):
    B, S, _ = q.shape
    t = ATTN_BLOCK
    q_specs, kv_specs = _attn_specs(S, t, 1, 2, (GROUP_W, GROUP_W))
    return pl.pallas_call(
        functools.partial(_attn_c_kernel, t=t, slopes=slopes, lambda_init=lambda_init),
        grid=(B, S // t),
        in_specs=q_specs + kv_specs + [pl.BlockSpec(lamv.shape, lambda b, i: (0, 0)),
                                       pl.BlockSpec(subg.shape, lambda b, i: (0, 0))],
        out_specs=pl.BlockSpec((None, t, GROUP_W), lambda b, i: (b, i, 0)),
        out_shape=jax.ShapeDtypeStruct((B, S, GROUP_W), F32),
        scratch_shapes=[pltpu.VMEM((t, 1), F32), pltpu.VMEM((t, 1), F32),
                        pltpu.VMEM((t, HEAD_DIM), F32)],
        compiler_params=_params("arbitrary", "arbitrary"),
        name="attn_diff",
    )(q, k, v, lamv, subg)


def _attn_d_kernel(q_ref, iq_ref, iw_ref, kv_ref, ik_ref, earlier_ref, o_ref,
                   keys_sc, tau_sc, m_sc, l_sc, acc_sc, *, t, topk, slopes):
    qi = pl.program_id(1)
    row, col = _tri_masks(t)
    causal = row >= col
    earlier = earlier_ref[...]

    w = iw_ref[...] * IDX_HEADS ** -0.5

    def index_block(j, masked):
        ikb = ik_ref[pl.ds(pl.multiple_of(j * t, t), t), :]
        sc = jnp.zeros((t, t), F32)
        for hh in range(IDX_HEADS):
            z = lax.dot_general(iq_ref[:, IDX_DIM * hh:IDX_DIM * (hh + 1)], ikb, _NT,
                                preferred_element_type=F32)
            sc = sc + w[:, hh:hh + 1] * jnp.maximum(z, 0.0)
        sc = jnp.where(sc == 0.0, 0.0, sc)
        if masked:
            sc = jnp.where(causal, sc, NEG_INF)
        bits = pltpu.bitcast(sc, I32)
        keys_sc[j] = jnp.where(bits < 0, bits ^ 0x7FFFFFFF, bits)

    index_block(qi, True)

    def index_body(j, carry):
        index_block(j, False)
        return carry

    lax.fori_loop(0, qi, index_body, 0)

    nchunk = t // SELECT_ROWS
    for r in range(nchunk):
        rs = slice(r * SELECT_ROWS, (r + 1) * SELECT_ROWS)
        rowpos = qi * t + r * SELECT_ROWS + lax.broadcasted_iota(I32, (SELECT_ROWS, 1), 0)
        kt = jnp.minimum(topk, rowpos + 1).astype(F32)

        def count_ge(cand, rs=rs):
            def body(j, acc):
                ge = (keys_sc[j, rs, :] >= cand).astype(F32)
                part = ge[:, :LANES]
                for q in range(1, t // LANES):
                    part = part + ge[:, q * LANES:(q + 1) * LANES]
                return acc + part
            acc = lax.fori_loop(0, qi + 1, body, jnp.zeros((SELECT_ROWS, LANES), F32))
            return jnp.sum(acc, axis=-1, keepdims=True)

        zero = jnp.zeros((SELECT_ROWS, 1), I32)
        prefix = jnp.where(count_ge(zero) >= kt, zero, INT_MIN)

        def bit_body(i, prefix, count_ge=count_ge, kt=kt):
            cand = prefix + lax.shift_left(jnp.int32(1), 30 - i)
            return jnp.where(count_ge(cand) >= kt, cand, prefix)

        tau = lax.fori_loop(0, 31, bit_body, prefix)
        tau_sc[rs, :] = tau
        need = kt - count_ge(tau + 1)

        def tie_body(j, seen, rs=rs, tau=tau, need=need):
            key = keys_sc[j, rs, :]
            eq = key == tau
            eqb = eq.astype(BF16)
            before = jnp.dot(eqb, earlier, preferred_element_type=F32) + seen
            keys_sc[j, rs, :] = jnp.where(eq & (before >= need), INT_MIN, key)
            return seen + jnp.sum(eqb.astype(F32), axis=-1, keepdims=True)

        lax.fori_loop(0, qi + 1, tie_body, jnp.zeros((SELECT_ROWS, 1), F32))

    kpos = lax.broadcasted_iota(I32, (1, t), 1)
    for h in range(HEADS):
        sl = slice(HEAD_DIM * h, HEAD_DIM * (h + 1))
        qh = q_ref[:, sl]
        m_sc[...] = jnp.full((t, 1), NEG_INF, F32)
        l_sc[...] = jnp.zeros((t, 1), F32)
        acc_sc[...] = jnp.zeros((t, HEAD_DIM), F32)

        def body(j, carry, qh=qh, h=h):
            off = pl.multiple_of(j * t, t)
            sel = keys_sc[j] >= tau_sc[...]
            kb = kv_ref[pl.ds(off, t), 0:HEAD_DIM]
            vb = kv_ref[pl.ds(off, t), HEAD_DIM:2 * HEAD_DIM]
            s = lax.dot_general(qh, kb, _NT, preferred_element_type=F32)
            s = s + slopes[h] * (kpos + (j - qi) * t).astype(F32)
            s = jnp.where(sel, s, NEG_INF)
            m_old = m_sc[...]
            m_new = jnp.maximum(m_old, jnp.max(s, axis=-1, keepdims=True))
            alpha = jnp.exp(m_old - m_new)
            p = jnp.where(sel, jnp.exp(s - m_new), 0.0)
            l_sc[...] = alpha * l_sc[...] + jnp.sum(p, axis=-1, keepdims=True)
            acc_sc[...] = alpha * acc_sc[...] + jnp.dot(p.astype(BF16), vb,
                                                        preferred_element_type=F32)
            m_sc[...] = m_new
            return carry

        lax.fori_loop(0, qi + 1, body, 0)
        o_ref[:, sl] = acc_sc[...] / l_sc[...]


def _attn_d_call(q, iq, iw, kv, ik, earlier, topk, slopes):
    B, S, _ = q.shape
    t = ATTN_BLOCK
    nb = S // t
    q_specs, kv_specs = _attn_specs(S, t, 2, 2, (2 * HEAD_DIM, IDX_DIM))
    return pl.pallas_call(
        functools.partial(_attn_d_kernel, t=t, topk=topk, slopes=slopes),
        grid=(B, nb),
        in_specs=(q_specs + [pl.BlockSpec((None, t, IDX_HEADS), lambda b, i: (b, i, 0))] + kv_specs
                  + [pl.BlockSpec((t, t), lambda b, i: (0, 0))]),
        out_specs=pl.BlockSpec((None, t, GROUP_W), lambda b, i: (b, i, 0)),
        out_shape=jax.ShapeDtypeStruct((B, S, GROUP_W), F32),
        scratch_shapes=[pltpu.VMEM((nb, t, t), I32), pltpu.VMEM((t, 1), I32),
                        pltpu.VMEM((t, 1), F32), pltpu.VMEM((t, 1), F32),
                        pltpu.VMEM((t, HEAD_DIM), F32)],
        compiler_params=_params("arbitrary", "arbitrary"),
        name="attn_sparse",
    )(q, iq, iw, kv, ik, earlier)


def _out_proj_kernel(x_ref, oa_ref, ob_ref, oc_ref, od_ref, beta_ref, wo_ref, gate_ref,
                     g2_ref, scale_ref, shift_ref, wr_ref, br_ref,
                     x1_ref, h2_ref, comb_ref):
    acc = None
    for i, o_ref in enumerate((oa_ref, ob_ref, oc_ref, od_ref)):
        sl = slice(GROUP_W * i, GROUP_W * (i + 1))
        mix = (o_ref[...] * beta_ref[:, sl]).astype(BF16)
        part = jnp.dot(mix, wo_ref[sl, :], preferred_element_type=F32)
        acc = part if acc is None else acc + part
    x1 = x_ref[...] + gate_ref[...] * acc
    x1_ref[...] = x1
    ms = jnp.mean(x1 * x1, axis=-1, keepdims=True)
    h2 = x1 * lax.rsqrt(ms + RMS_EPS) * g2_ref[...]
    h2 = h2 * (1.0 + scale_ref[...]) + shift_ref[...]
    h2_ref[...] = h2.astype(BF16)

    logits = jnp.dot(h2, wr_ref[...], precision=lax.Precision.HIGHEST,
                     preferred_element_type=F32) + br_ref[...]
    lt = logits.T
    tm = lt.shape[1]
    g = lt[0:N_GROUPS]
    gmax = jnp.max(g, axis=0, keepdims=True)
    gi = lax.broadcasted_iota(I32, g.shape, 0)
    gidx = jnp.min(jnp.where(g == gmax, gi, N_GROUPS), axis=0, keepdims=True)
    g_prob = 1.0 / jnp.sum(jnp.exp(g - gmax), axis=0, keepdims=True)
    e_sel = jnp.zeros((EXPERTS_PER_GROUP, tm), F32)
    for gg in range(N_GROUPS):
        lo = N_GROUPS + EXPERTS_PER_GROUP * gg
        e_sel = e_sel + jnp.where(gidx == gg, lt[lo:lo + EXPERTS_PER_GROUP], 0.0)
    ei = lax.broadcasted_iota(I32, e_sel.shape, 0)
    v1 = jnp.max(e_sel, axis=0, keepdims=True)
    i1 = jnp.min(jnp.where(e_sel == v1, ei, EXPERTS_PER_GROUP), axis=0, keepdims=True)
    rest = jnp.where(ei == i1, -jnp.inf, e_sel)
    v2 = jnp.max(rest, axis=0, keepdims=True)
    i2 = jnp.min(jnp.where(rest == v2, ei, EXPERTS_PER_GROUP), axis=0, keepdims=True)
    e2 = jnp.exp(v2 - v1)
    w1 = g_prob / (1.0 + e2)
    w2 = g_prob * e2 / (1.0 + e2)
    in_group = jnp.where(ei == i1, w1, 0.0) + jnp.where(ei == i2, w2, 0.0)
    pieces = [jnp.where(gidx == gg, in_group, 0.0) for gg in range(N_GROUPS)]
    pieces.append(jnp.zeros((LANES - N_EXPERTS, tm), F32))
    comb_ref[...] = jnp.concatenate(pieces, axis=0).T


def _out_proj_call(xf, outs, beta, w_out, gate1, g2, scale2, shift2, w_r, b_r, seq, tm):
    N, D = xf.shape
    per_b = seq // tm
    row = lambda i: (i, 0)
    full = lambda i: (0, 0)
    per_batch = pl.BlockSpec((None, 1, D), lambda i: (i // per_b, 0, 0))
    return pl.pallas_call(
        _out_proj_kernel,
        grid=(N // tm,),
        in_specs=([pl.BlockSpec((tm, D), row)] + [pl.BlockSpec((tm, GROUP_W), row)] * 4
                  + [pl.BlockSpec((1, D), full), pl.BlockSpec((D, D), full), per_batch,
                     pl.BlockSpec((1, D), full), per_batch, per_batch,
                     pl.BlockSpec((D, LANES), full), pl.BlockSpec((1, LANES), full)]),
        out_specs=[pl.BlockSpec((tm, D), row), pl.BlockSpec((tm, D), row),
                   pl.BlockSpec((tm, LANES), row)],
        out_shape=[jax.ShapeDtypeStruct((N, D), F32), jax.ShapeDtypeStruct((N, D), BF16),
                   jax.ShapeDtypeStruct((N, LANES), F32)],
        compiler_params=_params("arbitrary"),
        name="out_proj_router",
    )(xf, *outs, beta, w_out, gate1, g2, scale2, shift2, w_r, b_r)


def _moe_kernel(h_ref, comb_ref, w1_ref, w3_ref, w2_ref, x1_ref, gate_ref, o_ref, acc_sc):
    e = pl.program_id(1)

    @pl.when(e == 0)
    def _():
        acc_sc[...] = jnp.zeros_like(acc_sc)

    h = h_ref[...]
    a = jnp.dot(h, w1_ref[...], preferred_element_type=F32)
    b = jnp.dot(h, w3_ref[...], preferred_element_type=F32)
    comb = comb_ref[...]
    lane = lax.broadcasted_iota(I32, comb.shape, 1)
    ce = jnp.sum(jnp.where(lane == e, comb, 0.0), axis=-1, keepdims=True)
    hid = a * jax.nn.sigmoid(a) * b * ce
    acc_sc[...] += jnp.dot(hid.astype(BF16), w2_ref[...], preferred_element_type=F32)

    @pl.when(e == pl.num_programs(1) - 1)
    def _():
        o_ref[...] = x1_ref[...] + gate_ref[...] * acc_sc[...]


def _moe_call(h2, comb, w1, w3, w2, x1, gate2, seq, tm):
    N, D = x1.shape
    E, _, FF = w1.shape
    per_b = seq // tm
    return pl.pallas_call(
        _moe_kernel,
        grid=(N // tm, E),
        in_specs=[pl.BlockSpec((tm, D), lambda i, e: (i, 0)),
                  pl.BlockSpec((tm, LANES), lambda i, e: (i, 0)),
                  pl.BlockSpec((None, D, FF), lambda i, e: (e, 0, 0)),
                  pl.BlockSpec((None, D, FF), lambda i, e: (e, 0, 0)),
                  pl.BlockSpec((None, FF, D), lambda i, e: (e, 0, 0)),
                  pl.BlockSpec((tm, D), lambda i, e: (i, 0)),
                  pl.BlockSpec((None, 1, D), lambda i, e: (i // per_b, 0, 0))],
        out_specs=pl.BlockSpec((tm, D), lambda i, e: (i, 0)),
        out_shape=jax.ShapeDtypeStruct((N, D), F32),
        scratch_shapes=[pltpu.VMEM((tm, D), F32)],
        compiler_params=_params("arbitrary", "arbitrary"),
        name="moe_experts",
    )(h2, comb, w1, w3, w2, x1, gate2)


def _block_diag_mean(width, group):
    i = jnp.arange(width)
    return jnp.where((i[:, None] // group) == (i[None, :] // group), 1.0 / group, 0.0).astype(BF16)


def kernel(x, c, ada_w, ada_b, norm1_g, norm2_g, w_in, b_f, qn_a, kn_a, qn_c, kn_c,
           lam_q1, lam_k1, lam_q2, lam_k2, subln_g, qn_d, kn_d, mix_beta, w_out,
           w_group, b_group, w_expert, b_expert, w1, w3, w2):
    B, S, D = x.shape
    L = ada_w.shape[0]
    N = B * S
    topk = min(TOPK_MAX, S // 4)
    t = ATTN_BLOCK
    tm = 256
    tm_moe = min(1024, S)
    slopes = [2.0 ** (-8.0 * i / (2 * HEADS)) for i in range(1, 2 * HEADS + 1)]
    slopes_c, slopes_d = tuple(slopes[0::2]), tuple(slopes[1::2])

    splits = (GROUP_W, GROUP_W, GROUP_W, HEADS, GROUP_W, GROUP_W, GROUP_W, GROUP_W, GROUP_W,
              GROUP_W, GROUP_W, HEAD_DIM, HEAD_DIM, IDX_HEADS * IDX_DIM, IDX_DIM, IDX_HEADS)
    offs = [0]
    for s_ in splits:
        offs.append(offs[-1] + s_)
    (A_Q, A_K, A_V, A_F, B_Q, B_K, B_V, C_Q, C_K, C_V, D_Q, D_K, D_V, I_Q, I_K, I_W) = range(16)

    def cols(wl, idx):
        return wl[:, offs[idx]:offs[idx + 1]]

    idx_t = jnp.arange(t)
    later = (idx_t[:, None] > idx_t[None, :]).astype(BF16)
    earlier = (idx_t[:, None] < idx_t[None, :]).astype(BF16)
    tri_incl = (idx_t[:, None] <= idx_t[None, :]).astype(BF16)
    g64 = _block_diag_mean(GROUP_W, HEAD_DIM)
    g32 = _block_diag_mean(GROUP_W, DIFF_DIM)
    il = jnp.arange(LANES)
    gd = jnp.where((il[:, None] < HEAD_DIM) & (il[None, :] < HEAD_DIM), 1.0 / HEAD_DIM, 0.0).astype(BF16)

    mod = _ada_call(c, ada_w, ada_b)
    xf = x.reshape(N, D)

    for l in range(L):
        m6 = mod[l].reshape(B, 6, 1, D)
        shift1, scale1, gate1, shift2, scale2, gate2 = (m6[:, i] for i in range(6))

        wl = w_in[l]
        misc_w = jnp.concatenate([cols(wl, I_K), cols(wl, I_W), cols(wl, A_F),
                                  jnp.zeros((D, LANES - IDX_DIM - IDX_HEADS - HEADS), F32)], axis=1)
        w_all = jnp.concatenate(
            [cols(wl, i) for i in (A_Q, A_K, A_V, B_Q, B_K, B_V, C_Q, C_K, C_V, D_Q, I_Q, D_K, D_V)]
            + [misc_w], axis=1).astype(BF16)
        gains = jnp.stack([jnp.tile(qn_a[l], HEADS) * HEAD_DIM ** -0.5,
                           jnp.tile(kn_a[l], HEADS),
                           jnp.tile(qn_c[l], 2 * HEADS) * DIFF_DIM ** -0.5,
                           jnp.tile(kn_c[l], 2 * HEADS),
                           jnp.tile(qn_d[l], HEADS) * HEAD_DIM ** -0.5]).astype(F32)
        gains = jnp.concatenate([gains, jnp.zeros((3, GROUP_W), F32)], axis=0)
        dkvg = jnp.concatenate([kn_d[l], jnp.ones((HEAD_DIM,), F32)]).reshape(1, LANES)

        outs = _in_proj_call(xf, scale1, shift1, norm1_g[l].reshape(1, D), w_all, gains,
                             g64, g32, gd, dkvg, S, tm)
        aq, ak, av, bq, bk, bv, cq, ck, cv, dq, iq = (o.reshape(B, S, GROUP_W) for o in outs[:N_WIDE])
        dkv = outs[N_WIDE].reshape(B, S, LANES)
        misc = outs[N_WIDE + 1]
        ik = misc[:, :IDX_DIM].astype(BF16).reshape(B, S, IDX_DIM)
        iw = misc[:, IDX_DIM:IDX_DIM + IDX_HEADS].reshape(B, S, IDX_HEADS)
        af = misc[:, IDX_DIM + IDX_HEADS:IDX_DIM + IDX_HEADS + HEADS].reshape(B, S, HEADS)
        af_t = jnp.concatenate([af.transpose(0, 2, 1), jnp.zeros((B, 8 - HEADS, S), F32)], axis=1)
        bf_col = jnp.concatenate([b_f[l].astype(F32), jnp.zeros((8 - HEADS,), F32)]).reshape(8, 1)
        cum = _cum_call(af_t, bf_col, tri_incl)[:, :HEADS].reshape(B, HEADS, S // t, t)

        o_a = _attn_a_call(aq, ak, av, cum)
        o_b = _attn_b_call(bq, bk, bv, later)
        lambda_init = 0.8 - 0.6 * math.exp(-0.3 * l)
        lamv = jnp.stack([lam_q1[l], lam_k1[l], lam_q2[l], lam_k2[l]]).astype(F32)
        o_c = _attn_c_call(cq, ck, cv, lamv, subln_g[l].reshape(1, HEAD_DIM).astype(F32),
                           slopes_c, lambda_init)
        o_d = _attn_d_call(dq, iq, iw, dkv, ik, earlier, topk, slopes_d)

        w_r = jnp.concatenate([w_group[l], w_expert[l],
                               jnp.zeros((D, LANES - N_GROUPS - N_EXPERTS), F32)], axis=1)
        b_r = jnp.concatenate([b_group[l], b_expert[l],
                               jnp.zeros((LANES - N_GROUPS - N_EXPERTS,), F32)]).reshape(1, LANES)
        x1, h2, comb = _out_proj_call(
            xf, [o.reshape(N, GROUP_W) for o in (o_a, o_b, o_c, o_d)], mix_beta[l].reshape(1, D),
            w_out[l].astype(BF16), gate1, norm2_g[l].reshape(1, D), scale2, shift2, w_r, b_r, S, tm)

        xf = _moe_call(h2, comb,
                       w1[l].reshape(N_EXPERTS, D, EXPERT_FF).astype(BF16),
                       w3[l].reshape(N_EXPERTS, D, EXPERT_FF).astype(BF16),
                       w2[l].reshape(N_EXPERTS, EXPERT_FF, D).astype(BF16),
                       x1, gate2, S, tm_moe)

    return xf.reshape(B, S, D)
```

```python
import functools
import math

import jax
import jax.numpy as jnp
from jax import lax
from jax.experimental import pallas as pl
from jax.experimental.pallas import tpu as pltpu

F32 = jnp.float32
BF16 = jnp.bfloat16
I32 = jnp.int32

HEAD_DIM = 64
HEADS = 4
GROUP_W = HEADS * HEAD_DIM
DIFF_DIM = HEAD_DIM // 2
IDX_HEADS = 8
IDX_DIM = 32
TOPK_MAX = 256
N_GROUPS = 4
EXPERTS_PER_GROUP = 8
N_EXPERTS = N_GROUPS * EXPERTS_PER_GROUP
EXPERT_FF = 256
RMS_EPS = 1e-6
NEG_INF = -1e30
INT_MIN = -(2 ** 31)
LOG2E = math.log2(math.e)

LANES = 128
ATTN_BLOCK = 256
SELECT_ROWS = 64
VMEM_LIMIT = 56 * 1024 * 1024

_NT = (((1,), (1,)), ((), ()))


def _params(*sem):
    return pltpu.CompilerParams(dimension_semantics=sem, vmem_limit_bytes=VMEM_LIMIT)


def _log_sigmoid(z):
    return jnp.minimum(z, 0.0) - jnp.log1p(jnp.exp(-jnp.abs(z)))


def _split_bf16(x, parts):
    out = []
    rem = x
    for _ in range(parts):
        p = rem.astype(BF16)
        out.append(p)
        rem = rem - p.astype(F32)
    return out


def _ada_kernel(c_ref, w_ref, b_ref, o_ref):
    c = c_ref[...]
    ca = c * jax.nn.sigmoid(c)
    o_ref[...] = jnp.dot(ca, w_ref[...], precision=lax.Precision.HIGHEST,
                         preferred_element_type=F32) + b_ref[...]


def _ada_call(c, ada_w, ada_b):
    L, D, E = ada_w.shape
    B = c.shape[0]
    tn = 1536
    return pl.pallas_call(
        _ada_kernel,
        grid=(L, E // tn),
        in_specs=[pl.BlockSpec((B, D), lambda l, j: (0, 0)),
                  pl.BlockSpec((None, D, tn), lambda l, j: (l, 0, j)),
                  pl.BlockSpec((None, 1, tn), lambda l, j: (l, 0, j))],
        out_specs=pl.BlockSpec((None, B, tn), lambda l, j: (l, 0, j)),
        out_shape=jax.ShapeDtypeStruct((L, B, E), F32),
        compiler_params=_params("arbitrary", "arbitrary"),
        name="ada_mod",
    )(c, ada_w, ada_b.reshape(L, 1, E))


SECTIONS = (
    ("a_q", GROUP_W, "n64"), ("a_k", GROUP_W, "n64"), ("a_vx", HEADS * LANES, "vones"),
    ("b_q", GROUP_W, "scale"), ("b_k", GROUP_W, "plain"), ("b_v", GROUP_W, "plain"),
    ("c_q", GROUP_W, "n32"), ("c_k", GROUP_W, "n32"), ("c_vx", HEADS * LANES, "vones"),
    ("d_q", GROUP_W, "n64"), ("i_q", GROUP_W, "scale"),
    ("d_k", LANES, "dk"), ("d_vx", LANES, "vones"), ("misc", LANES, "misc"),
)
SECTION_OFFSETS = tuple(sum(w for _, w, _ in SECTIONS[:i]) for i in range(len(SECTIONS) + 1))
SECTION_SCALE = {"b_q": HEAD_DIM ** -0.5, "i_q": IDX_DIM ** -0.5}
GAIN_ROW = {"a_q": 0, "a_k": 1, "c_q": 2, "c_k": 3, "d_q": 4}


def _in_proj_kernel(x_ref, scale_ref, shift_ref, g1_ref, w_ref, gains_ref, g64_ref, g32_ref,
                    gd_ref, dkg_ref, *out_refs):
    x = x_ref[...]
    ms = jnp.mean(x * x, axis=-1, keepdims=True)
    h = x * lax.rsqrt(ms + RMS_EPS) * g1_ref[...]
    h = h * (1.0 + scale_ref[...]) + shift_ref[...]
    hb = h.astype(BF16)

    for i, (name, width, kind) in enumerate(SECTIONS):
        w = w_ref[:, SECTION_OFFSETS[i]:SECTION_OFFSETS[i + 1]]
        o_ref = out_refs[i]
        if kind == "misc":
            h_hi, h_lo = _split_bf16(h, 2)
            o_ref[...] = (jnp.dot(h_hi, w, preferred_element_type=F32)
                          + jnp.dot(h_lo, w, preferred_element_type=F32))
            continue
        sec = jnp.dot(hb, w, preferred_element_type=F32)
        if kind in ("n64", "n32"):
            gmat = g64_ref if kind == "n64" else g32_ref
            msq = jnp.dot((sec * sec).astype(BF16), gmat[...], preferred_element_type=F32)
            r = GAIN_ROW[name]
            sec = sec * lax.rsqrt(msq + RMS_EPS) * gains_ref[r:r + 1, :]
        elif kind == "scale":
            sec = sec * SECTION_SCALE[name]
        elif kind == "vones":
            lane = lax.broadcasted_iota(I32, sec.shape, 1)
            sec = jnp.where((lane & (LANES - 1)) >= HEAD_DIM, 1.0, sec)
        elif kind == "dk":
            msq = jnp.dot((sec * sec).astype(BF16), gd_ref[...], preferred_element_type=F32)
            sec = sec * lax.rsqrt(msq + RMS_EPS) * dkg_ref[...]
        o_ref[...] = sec.astype(BF16)


def _in_proj_call(xf, scale1, shift1, g1, w_all, gains, g64, g32, gd, dkg, seq, tm):
    N, D = xf.shape
    P = w_all.shape[1]
    per_b = seq // tm
    row = lambda i: (i, 0)
    full = lambda i: (0, 0)
    out_shape = [jax.ShapeDtypeStruct((N, w), F32 if kind == "misc" else BF16)
                 for _, w, kind in SECTIONS]
    out_specs = [pl.BlockSpec((tm, w), row) for _, w, _ in SECTIONS]
    return pl.pallas_call(
        _in_proj_kernel,
        grid=(N // tm,),
        in_specs=[pl.BlockSpec((tm, D), row),
                  pl.BlockSpec((None, 1, D), lambda i: (i // per_b, 0, 0)),
                  pl.BlockSpec((None, 1, D), lambda i: (i // per_b, 0, 0)),
                  pl.BlockSpec((1, D), full),
                  pl.BlockSpec((D, P), full),
                  pl.BlockSpec(gains.shape, full),
                  pl.BlockSpec(g64.shape, full),
                  pl.BlockSpec(g32.shape, full),
                  pl.BlockSpec(gd.shape, full),
                  pl.BlockSpec(dkg.shape, full)],
        out_specs=out_specs,
        out_shape=out_shape,
        compiler_params=_params("arbitrary"),
        name="in_proj",
    )(xf, scale1, shift1, g1, w_all, gains, g64, g32, gd, dkg)


def _cum_kernel(af_ref, bf_ref, tri_ref, o_ref, *, blk):
    z = af_ref[...] + bf_ref[...]
    lf = _log_sigmoid(z)
    carry = jnp.zeros((lf.shape[0], 1), F32)
    tri = tri_ref[...]
    for j in range(lf.shape[1] // blk):
        piece = lf[:, j * blk:(j + 1) * blk]
        c = carry
        for part in _split_bf16(piece, 3):
            c = c + jnp.dot(part, tri, preferred_element_type=F32)
        o_ref[:, j * blk:(j + 1) * blk] = c * LOG2E
        carry = c[:, blk - 1:blk]


def _cum_call(af_t, bf_col, tri_incl):
    B, R, S = af_t.shape
    blk = tri_incl.shape[0]
    return pl.pallas_call(
        functools.partial(_cum_kernel, blk=blk),
        grid=(B,),
        in_specs=[pl.BlockSpec((None, R, S), lambda b: (b, 0, 0)),
                  pl.BlockSpec((R, 1), lambda b: (0, 0)),
                  pl.BlockSpec((blk, blk), lambda b: (0, 0))],
        out_specs=pl.BlockSpec((None, R, S), lambda b: (b, 0, 0)),
        out_shape=jax.ShapeDtypeStruct((B, R, S), F32),
        compiler_params=_params("arbitrary"),
        name="forget_cumsum",
    )(af_t, bf_col, tri_incl)


def _attn_specs(seq, t, n_q_side, kv_width):
    q_spec = pl.BlockSpec((None, t, GROUP_W), lambda b, i: (b, i, 0))
    kv_specs = [pl.BlockSpec((None, seq, w), lambda b, i: (b, 0, 0)) for w in kv_width]
    return [q_spec] * n_q_side, kv_specs


def _tri_masks(t):
    row = lax.broadcasted_iota(I32, (t, t), 0)
    col = lax.broadcasted_iota(I32, (t, t), 1)
    return row, col


def _lane_tiles(x):
    return [x[:, q * LANES:(q + 1) * LANES] for q in range(x.shape[1] // LANES)]


def _softmax_block(s, vx, m_ref, acc_ref):
    m_old = m_ref[...]
    tiles = _lane_tiles(s)
    smax = tiles[0]
    for tile in tiles[1:]:
        smax = jnp.maximum(smax, tile)
    m_new = jnp.maximum(m_old, jnp.max(smax, axis=-1, keepdims=True))
    alpha = jnp.exp2(m_old - m_new)
    p = jnp.concatenate([jnp.exp2(tile - m_new) for tile in tiles], axis=1)
    acc_ref[...] = alpha * acc_ref[...] + jnp.dot(p.astype(BF16), vx, preferred_element_type=F32)
    m_ref[...] = m_new


def _softmax_init(m_sc, acc_sc):
    m_sc[...] = jnp.full(m_sc.shape, NEG_INF, F32)
    acc_sc[...] = jnp.zeros(acc_sc.shape, F32)


def _softmax_result(acc):
    return acc[:, :HEAD_DIM] / acc[:, HEAD_DIM:]


def _attn_a_kernel(q_ref, k_ref, vx_ref, cum_ref, o_ref, m_sc, acc_sc, *, t):
    qi = pl.program_id(1)
    row, col = _tri_masks(t)
    causal = row >= col
    _softmax_init(m_sc, acc_sc)

    def block(j, masked):
        off = pl.multiple_of(j * t, t)
        for h in range(HEADS):
            sl = slice(HEAD_DIM * h, HEAD_DIM * (h + 1))
            s = lax.dot_general(q_ref[:, sl], k_ref[pl.ds(off, t), sl], _NT,
                                preferred_element_type=F32)
            s = s - cum_ref[h, pl.ds(j, 1), :]
            if masked:
                s = jnp.where(causal, s, -jnp.inf)
            _softmax_block(s, vx_ref[pl.ds(off, t), LANES * h:LANES * (h + 1)],
                           m_sc.at[h], acc_sc.at[h])

    block(qi, True)

    def body(j, carry):
        block(j, False)
        return carry

    lax.fori_loop(0, qi, body, 0)
    for h in range(HEADS):
        o_ref[:, HEAD_DIM * h:HEAD_DIM * (h + 1)] = _softmax_result(acc_sc[h])


def _attn_a_call(q, k, vx, cum4):
    B, S, _ = q.shape
    t = ATTN_BLOCK
    nb = S // t
    q_specs, kv_specs = _attn_specs(S, t, 1, (GROUP_W, HEADS * LANES))
    return pl.pallas_call(
        functools.partial(_attn_a_kernel, t=t),
        grid=(B, nb),
        in_specs=q_specs + kv_specs + [pl.BlockSpec((None, HEADS, nb, t), lambda b, i: (b, 0, 0, 0))],
        out_specs=pl.BlockSpec((None, t, GROUP_W), lambda b, i: (b, i, 0)),
        out_shape=jax.ShapeDtypeStruct((B, S, GROUP_W), F32),
        scratch_shapes=[pltpu.VMEM((HEADS, t, LANES), F32), pltpu.VMEM((HEADS, t, LANES), F32)],
        compiler_params=_params("arbitrary", "arbitrary"),
        name="attn_forget",
    )(q, k, vx, cum4)


def _attn_b_kernel(q_ref, k_ref, v_ref, later_ref, o_ref, r_sc, acc_sc, *, t):
    qi = pl.program_id(1)
    row, col = _tri_masks(t)
    strict = row > col
    r_sc[...] = jnp.zeros(r_sc.shape, F32)
    acc_sc[...] = jnp.zeros(acc_sc.shape, F32)

    def block(j, masked):
        off = pl.multiple_of(j * t, t)
        later = later_ref[...]
        for h in range(HEADS):
            sl = slice(HEAD_DIM * h, HEAD_DIM * (h + 1))
            z = lax.dot_general(q_ref[:, sl], k_ref[pl.ds(off, t), sl], _NT,
                                preferred_element_type=F32)
            lb = _log_sigmoid(z)
            lm = lb - z
            if masked:
                lm = jnp.where(strict, lm, 0.0)
            hi, lo = _split_bf16(lm, 2)
            suffix = (jnp.dot(hi, later, preferred_element_type=F32)
                      + jnp.dot(lo, later, preferred_element_type=F32))
            r_old = r_sc[h]
            w = jnp.exp(lb + suffix + jnp.concatenate([r_old] * (t // LANES), axis=1))
            if masked:
                w = jnp.where(strict, w, 0.0)
            acc_sc[h] += jnp.dot(w.astype(BF16), v_ref[pl.ds(off, t), sl],
                                 preferred_element_type=F32)
            tiles = _lane_tiles(lm)
            tot = tiles[0]
            for tile in tiles[1:]:
                tot = tot + tile
            r_sc[h] = r_old + jnp.sum(tot, axis=-1, keepdims=True)

    block(qi, True)

    def body(i, carry):
        block(qi - 1 - i, False)
        return carry

    lax.fori_loop(0, qi, body, 0)
    for h in range(HEADS):
        o_ref[:, HEAD_DIM * h:HEAD_DIM * (h + 1)] = acc_sc[h]


def _attn_b_call(q, k, v, later):
    B, S, _ = q.shape
    t = ATTN_BLOCK
    q_specs, kv_specs = _attn_specs(S, t, 1, (GROUP_W, GROUP_W))
    return pl.pallas_call(
        functools.partial(_attn_b_kernel, t=t),
        grid=(B, S // t),
        in_specs=q_specs + kv_specs + [pl.BlockSpec((t, t), lambda b, i: (0, 0))],
        out_specs=pl.BlockSpec((None, t, GROUP_W), lambda b, i: (b, i, 0)),
        out_shape=jax.ShapeDtypeStruct((B, S, GROUP_W), F32),
        scratch_shapes=[pltpu.VMEM((HEADS, t, LANES), F32), pltpu.VMEM((HEADS, t, HEAD_DIM), F32)],
        compiler_params=_params("arbitrary", "arbitrary"),
        name="attn_stick",
    )(q, k, v, later)


def _alibi_row(slope, j, qi, t):
    kpos = lax.broadcasted_iota(I32, (1, t), 1)
    return (slope * LOG2E) * (kpos + (j - qi) * t).astype(F32)


def _attn_c_kernel(q_ref, k_ref, vx_ref, lamv_ref, subg_ref, o_ref, m_sc, acc_sc,
                   *, t, slopes, lambda_init):
    qi = pl.program_id(1)
    row, col = _tri_masks(t)
    causal = row >= col
    _softmax_init(m_sc, acc_sc)

    def block(j, masked):
        off = pl.multiple_of(j * t, t)
        for h in range(HEADS):
            bias = _alibi_row(slopes[h], j, qi, t)
            vx = vx_ref[pl.ds(off, t), LANES * h:LANES * (h + 1)]
            for c in range(2):
                sl = slice(HEAD_DIM * h + DIFF_DIM * c, HEAD_DIM * h + DIFF_DIM * (c + 1))
                s = lax.dot_general(q_ref[:, sl], k_ref[pl.ds(off, t), sl], _NT,
                                    preferred_element_type=F32) + bias
                if masked:
                    s = jnp.where(causal, s, -jnp.inf)
                _softmax_block(s, vx, m_sc.at[2 * h + c], acc_sc.at[2 * h + c])

    block(qi, True)

    def body(j, carry):
        block(j, False)
        return carry

    lax.fori_loop(0, qi, body, 0)

    lv = lamv_ref[...]
    lam = (jnp.exp(jnp.sum(lv[0:1] * lv[1:2], axis=-1, keepdims=True))
           - jnp.exp(jnp.sum(lv[2:3] * lv[3:4], axis=-1, keepdims=True)) + lambda_init)
    for h in range(HEADS):
        o = _softmax_result(acc_sc[2 * h]) - lam * _softmax_result(acc_sc[2 * h + 1])
        ms = jnp.mean(o * o, axis=-1, keepdims=True)
        o_ref[:, HEAD_DIM * h:HEAD_DIM * (h + 1)] = (
            o * lax.rsqrt(ms + RMS_EPS) * subg_ref[...] * (1.0 - lambda_init))


def _attn_c_call(q, k, vx, lamv, subg, slopes, lambda_init):
    B, S, _ = q.shape
    t = ATTN_BLOCK
    q_specs, kv_specs = _attn_specs(S, t, 1, (GROUP_W, HEADS * LANES))
    return pl.pallas_call(
        functools.partial(_attn_c_kernel, t=t, slopes=slopes, lambda_init=lambda_init),
        grid=(B, S // t),
        in_specs=q_specs + kv_specs + [pl.BlockSpec(lamv.shape, lambda b, i: (0, 0)),
                                       pl.BlockSpec(subg.shape, lambda b, i: (0, 0))],
        out_specs=pl.BlockSpec((None, t, GROUP_W), lambda b, i: (b, i, 0)),
        out_shape=jax.ShapeDtypeStruct((B, S, GROUP_W), F32),
        scratch_shapes=[pltpu.VMEM((2 * HEADS, t, LANES), F32),
                        pltpu.VMEM((2 * HEADS, t, LANES), F32)],
        compiler_params=_params("arbitrary", "arbitrary"),
        name="attn_diff",
    )(q, k, vx, lamv, subg)


def _attn_d_kernel(q_ref, iq_ref, iw_ref, dk_ref, dvx_ref, ik_ref, earlier_ref, o_ref,
                   keys_sc, tau_sc, m_sc, acc_sc, *, t, topk, slopes):
    qi = pl.program_id(1)
    row, col = _tri_masks(t)
    causal = row >= col

    w = iw_ref[...] * IDX_HEADS ** -0.5

    def index_block(j, masked):
        ikb = ik_ref[pl.ds(pl.multiple_of(j * t, t), t), :]
        sc = jnp.zeros((t, t), F32)
        for hh in range(IDX_HEADS):
            z = lax.dot_general(iq_ref[:, IDX_DIM * hh:IDX_DIM * (hh + 1)], ikb, _NT,
                                preferred_element_type=F32)
            sc = sc + w[:, hh:hh + 1] * jnp.maximum(z, 0.0)
        sc = jnp.where(sc == 0.0, 0.0, sc)
        if masked:
            sc = jnp.where(causal, sc, NEG_INF)
        bits = pltpu.bitcast(sc, I32)
        keys_sc[j] = jnp.where(bits < 0, bits ^ 0x7FFFFFFF, bits)

    index_block(qi, True)

    def index_body(j, carry):
        index_block(j, False)
        return carry

    lax.fori_loop(0, qi, index_body, 0)

    nchunk = t // SELECT_ROWS
    rowpos = qi * t + lax.broadcasted_iota(I32, (t, 1), 0)
    kt = jnp.minimum(topk, rowpos + 1).astype(F32)

    def count_ge(cand):
        def body(j, accs):
            out = []
            for r in range(nchunk):
                rs = slice(r * SELECT_ROWS, (r + 1) * SELECT_ROWS)
                ge = (keys_sc[j, rs, :] >= cand[rs]).astype(F32)
                tiles = _lane_tiles(ge)
                part = tiles[0]
                for tile in tiles[1:]:
                    part = part + tile
                out.append(accs[r] + part)
            return tuple(out)
        accs = lax.fori_loop(0, qi + 1, body,
                             tuple(jnp.zeros((SELECT_ROWS, LANES), F32) for _ in range(nchunk)))
        return jnp.concatenate([jnp.sum(a, axis=-1, keepdims=True) for a in accs], axis=0)

    zero = jnp.zeros((t, 1), I32)
    prefix = jnp.where(count_ge(zero) >= kt, zero, INT_MIN)

    def bit_body(i, prefix):
        cand = prefix + lax.shift_left(jnp.int32(1), 30 - i)
        return jnp.where(count_ge(cand) >= kt, cand, prefix)

    tau = lax.fori_loop(0, 31, bit_body, prefix)
    tau_sc[...] = tau
    need = kt - count_ge(tau + 1)

    def tie_body(j, seen):
        key = keys_sc[j]
        eq = key == tau
        eqb = eq.astype(BF16)
        before = jnp.dot(eqb, earlier_ref[...], preferred_element_type=F32) + seen
        keys_sc[j] = jnp.where(eq & (before >= need), INT_MIN, key)
        return seen + jnp.sum(eqb.astype(F32), axis=-1, keepdims=True)

    lax.fori_loop(0, qi + 1, tie_body, jnp.zeros((t, 1), F32))

    _softmax_init(m_sc, acc_sc)

    def body(j, carry):
        off = pl.multiple_of(j * t, t)
        sel = keys_sc[j] >= tau_sc[...]
        kb = dk_ref[pl.ds(off, t), 0:HEAD_DIM]
        vx = dvx_ref[pl.ds(off, t), :]
        for h in range(HEADS):
            s = lax.dot_general(q_ref[:, HEAD_DIM * h:HEAD_DIM * (h + 1)], kb, _NT,
                                preferred_element_type=F32)
            s = jnp.where(sel, s + _alibi_row(slopes[h], j, qi, t), -jnp.inf)
            _softmax_block(s, vx, m_sc.at[h], acc_sc.at[h])
        return carry

    lax.fori_loop(0, qi + 1, body, 0)
    for h in range(HEADS):
        o_ref[:, HEAD_DIM * h:HEAD_DIM * (h + 1)] = _softmax_result(acc_sc[h])


def _attn_d_call(q, iq, iw, dk, dvx, ik, earlier, topk, slopes):
    B, S, _ = q.shape
    t = ATTN_BLOCK
    nb = S // t
    q_specs, kv_specs = _attn_specs(S, t, 2, (LANES, LANES, IDX_DIM))
    return pl.pallas_call(
        functools.partial(_attn_d_kernel, t=t, topk=topk, slopes=slopes),
        grid=(B, nb),
        in_specs=(q_specs + [pl.BlockSpec((None, t, IDX_HEADS), lambda b, i: (b, i, 0))] + kv_specs
                  + [pl.BlockSpec((t, t), lambda b, i: (0, 0))]),
        out_specs=pl.BlockSpec((None, t, GROUP_W), lambda b, i: (b, i, 0)),
        out_shape=jax.ShapeDtypeStruct((B, S, GROUP_W), F32),
        scratch_shapes=[pltpu.VMEM((nb, t, t), I32), pltpu.VMEM((t, 1), I32),
                        pltpu.VMEM((HEADS, t, LANES), F32), pltpu.VMEM((HEADS, t, LANES), F32)],
        compiler_params=_params("arbitrary", "arbitrary"),
        name="attn_sparse",
    )(q, iq, iw, dk, dvx, ik, earlier)


def _out_proj_kernel(x_ref, oa_ref, ob_ref, oc_ref, od_ref, beta_ref, wo_ref, gate_ref,
                     g2_ref, scale_ref, shift_ref, wr_ref, br_ref,
                     x1_ref, h2_ref, comb_ref):
    acc = None
    for i, o_ref in enumerate((oa_ref, ob_ref, oc_ref, od_ref)):
        sl = slice(GROUP_W * i, GROUP_W * (i + 1))
        mix = (o_ref[...] * beta_ref[:, sl]).astype(BF16)
        part = jnp.dot(mix, wo_ref[sl, :], preferred_element_type=F32)
        acc = part if acc is None else acc + part
    x1 = x_ref[...] + gate_ref[...] * acc
    x1_ref[...] = x1
    ms = jnp.mean(x1 * x1, axis=-1, keepdims=True)
    h2 = x1 * lax.rsqrt(ms + RMS_EPS) * g2_ref[...]
    h2 = h2 * (1.0 + scale_ref[...]) + shift_ref[...]
    h2_ref[...] = h2.astype(BF16)

    logits = jnp.dot(h2, wr_ref[...], precision=lax.Precision.HIGHEST,
                     preferred_element_type=F32) + br_ref[...]
    lt = logits.T
    tm = lt.shape[1]
    g = lt[0:N_GROUPS]
    gmax = jnp.max(g, axis=0, keepdims=True)
    gi = lax.broadcasted_iota(I32, g.shape, 0)
    gidx = jnp.min(jnp.where(g == gmax, gi, N_GROUPS), axis=0, keepdims=True)
    g_prob = 1.0 / jnp.sum(jnp.exp(g - gmax), axis=0, keepdims=True)
    e_sel = jnp.zeros((EXPERTS_PER_GROUP, tm), F32)
    for gg in range(N_GROUPS):
        lo = N_GROUPS + EXPERTS_PER_GROUP * gg
        e_sel = e_sel + jnp.where(gidx == gg, lt[lo:lo + EXPERTS_PER_GROUP], 0.0)
    ei = lax.broadcasted_iota(I32, e_sel.shape, 0)
    v1 = jnp.max(e_sel, axis=0, keepdims=True)
    i1 = jnp.min(jnp.where(e_sel == v1, ei, EXPERTS_PER_GROUP), axis=0, keepdims=True)
    rest = jnp.where(ei == i1, -jnp.inf, e_sel)
    v2 = jnp.max(rest, axis=0, keepdims=True)
    i2 = jnp.min(jnp.where(rest == v2, ei, EXPERTS_PER_GROUP), axis=0, keepdims=True)
    e2 = jnp.exp(v2 - v1)
    w1 = g_prob / (1.0 + e2)
    w2 = g_prob * e2 / (1.0 + e2)
    in_group = jnp.where(ei == i1, w1, 0.0) + jnp.where(ei == i2, w2, 0.0)
    pieces = [jnp.where(gidx == gg, in_group, 0.0) for gg in range(N_GROUPS)]
    pieces.append(jnp.zeros((LANES - N_EXPERTS, tm), F32))
    comb_ref[...] = jnp.concatenate(pieces, axis=0).T


def _out_proj_call(xf, outs, beta, w_out, gate1, g2, scale2, shift2, w_r, b_r, seq, tm):
    N, D = xf.shape
    per_b = seq // tm
    row = lambda i: (i, 0)
    full = lambda i: (0, 0)
    per_batch = pl.BlockSpec((None, 1, D), lambda i: (i // per_b, 0, 0))
    return pl.pallas_call(
        _out_proj_kernel,
        grid=(N // tm,),
        in_specs=([pl.BlockSpec((tm, D), row)] + [pl.BlockSpec((tm, GROUP_W), row)] * 4
                  + [pl.BlockSpec((1, D), full), pl.BlockSpec((D, D), full), per_batch,
                     pl.BlockSpec((1, D), full), per_batch, per_batch,
                     pl.BlockSpec((D, LANES), full), pl.BlockSpec((1, LANES), full)]),
        out_specs=[pl.BlockSpec((tm, D), row), pl.BlockSpec((tm, D), row),
                   pl.BlockSpec((tm, LANES), row)],
        out_shape=[jax.ShapeDtypeStruct((N, D), F32), jax.ShapeDtypeStruct((N, D), BF16),
                   jax.ShapeDtypeStruct((N, LANES), F32)],
        compiler_params=_params("arbitrary"),
        name="out_proj_router",
    )(xf, *outs, beta, w_out, gate1, g2, scale2, shift2, w_r, b_r)


def _moe_kernel(h_ref, comb_ref, w1_ref, w3_ref, w2_ref, x1_ref, gate_ref, o_ref, acc_sc):
    e = pl.program_id(1)

    @pl.when(e == 0)
    def _():
        acc_sc[...] = jnp.zeros_like(acc_sc)

    h = h_ref[...]
    a = jnp.dot(h, w1_ref[...], preferred_element_type=F32)
    b = jnp.dot(h, w3_ref[...], preferred_element_type=F32)
    comb = comb_ref[...]
    lane = lax.broadcasted_iota(I32, comb.shape, 1)
    ce = jnp.sum(jnp.where(lane == e, comb, 0.0), axis=-1, keepdims=True)
    hid = a * jax.nn.sigmoid(a) * b * ce
    acc_sc[...] += jnp.dot(hid.astype(BF16), w2_ref[...], preferred_element_type=F32)

    @pl.when(e == pl.num_programs(1) - 1)
    def _():
        o_ref[...] = x1_ref[...] + gate_ref[...] * acc_sc[...]


def _moe_call(h2, comb, w1, w3, w2, x1, gate2, seq, tm):
    N, D = x1.shape
    E, _, FF = w1.shape
    per_b = seq // tm
    return pl.pallas_call(
        _moe_kernel,
        grid=(N // tm, E),
        in_specs=[pl.BlockSpec((tm, D), lambda i, e: (i, 0)),
                  pl.BlockSpec((tm, LANES), lambda i, e: (i, 0)),
                  pl.BlockSpec((None, D, FF), lambda i, e: (e, 0, 0)),
                  pl.BlockSpec((None, D, FF), lambda i, e: (e, 0, 0)),
                  pl.BlockSpec((None, FF, D), lambda i, e: (e, 0, 0)),
                  pl.BlockSpec((tm, D), lambda i, e: (i, 0)),
                  pl.BlockSpec((None, 1, D), lambda i, e: (i // per_b, 0, 0))],
        out_specs=pl.BlockSpec((tm, D), lambda i, e: (i, 0)),
        out_shape=jax.ShapeDtypeStruct((N, D), F32),
        scratch_shapes=[pltpu.VMEM((tm, D), F32)],
        compiler_params=_params("arbitrary", "arbitrary"),
        name="moe_experts",
    )(h2, comb, w1, w3, w2, x1, gate2)


def _block_diag_mean(width, group, valid=None):
    i = jnp.arange(width)
    same = (i[:, None] // group) == (i[None, :] // group)
    if valid is not None:
        same = same & (i[:, None] < valid) & (i[None, :] < valid)
    return jnp.where(same, 1.0 / group, 0.0).astype(BF16)


def _head_padded(w):
    D = w.shape[0]
    w4 = w.reshape(D, -1, HEAD_DIM)
    return jnp.concatenate([w4, jnp.zeros_like(w4)], axis=-1).reshape(D, -1)


def kernel(x, c, ada_w, ada_b, norm1_g, norm2_g, w_in, b_f, qn_a, kn_a, qn_c, kn_c,
           lam_q1, lam_k1, lam_q2, lam_k2, subln_g, qn_d, kn_d, mix_beta, w_out,
           w_group, b_group, w_expert, b_expert, w1, w3, w2):
    B, S, D = x.shape
    L = ada_w.shape[0]
    N = B * S
    topk = min(TOPK_MAX, S // 4)
    t = ATTN_BLOCK
    tm = 256
    tm_moe = min(1024, S)
    slopes = [2.0 ** (-8.0 * i / (2 * HEADS)) for i in range(1, 2 * HEADS + 1)]
    slopes_c, slopes_d = tuple(slopes[0::2]), tuple(slopes[1::2])

    splits = (GROUP_W, GROUP_W, GROUP_W, HEADS, GROUP_W, GROUP_W, GROUP_W, GROUP_W, GROUP_W,
              GROUP_W, GROUP_W, HEAD_DIM, HEAD_DIM, IDX_HEADS * IDX_DIM, IDX_DIM, IDX_HEADS)
    offs = [0]
    for s_ in splits:
        offs.append(offs[-1] + s_)
    (A_Q, A_K, A_V, A_F, B_Q, B_K, B_V, C_Q, C_K, C_V, D_Q, D_K, D_V, I_Q, I_K, I_W) = range(16)

    idx_t = jnp.arange(t)
    later = (idx_t[:, None] > idx_t[None, :]).astype(BF16)
    earlier = (idx_t[:, None] < idx_t[None, :]).astype(BF16)
    tri_incl = (idx_t[:, None] <= idx_t[None, :]).astype(BF16)
    g64 = _block_diag_mean(GROUP_W, HEAD_DIM)
    g32 = _block_diag_mean(GROUP_W, DIFF_DIM)
    gd = _block_diag_mean(LANES, HEAD_DIM, valid=HEAD_DIM)

    mod = _ada_call(c, ada_w, ada_b)
    xf = x.reshape(N, D)

    for l in range(L):
        m6 = mod[l].reshape(B, 6, 1, D)
        shift1, scale1, gate1, shift2, scale2, gate2 = (m6[:, i] for i in range(6))

        wl = w_in[l]

        def cols(idx, wl=wl):
            return wl[:, offs[idx]:offs[idx + 1]]

        zeros64 = jnp.zeros((D, HEAD_DIM), F32)
        pieces = {
            "a_q": cols(A_Q), "a_k": cols(A_K), "a_vx": _head_padded(cols(A_V)),
            "b_q": cols(B_Q), "b_k": cols(B_K), "b_v": cols(B_V),
            "c_q": cols(C_Q), "c_k": cols(C_K), "c_vx": _head_padded(cols(C_V)),
            "d_q": cols(D_Q), "i_q": cols(I_Q),
            "d_k": jnp.concatenate([cols(D_K), zeros64], axis=1),
            "d_vx": jnp.concatenate([cols(D_V), zeros64], axis=1),
            "misc": jnp.concatenate([cols(I_K), cols(I_W), cols(A_F),
                                     jnp.zeros((D, LANES - IDX_DIM - IDX_HEADS - HEADS), F32)], axis=1),
        }
        w_all = jnp.concatenate([pieces[name] for name, _, _ in SECTIONS], axis=1).astype(BF16)
        gains = jnp.stack([jnp.tile(qn_a[l], HEADS) * (HEAD_DIM ** -0.5 * LOG2E),
                           jnp.tile(kn_a[l], HEADS),
                           jnp.tile(qn_c[l], 2 * HEADS) * (DIFF_DIM ** -0.5 * LOG2E),
                           jnp.tile(kn_c[l], 2 * HEADS),
                           jnp.tile(qn_d[l], HEADS) * (HEAD_DIM ** -0.5 * LOG2E)]).astype(F32)
        gains = jnp.concatenate([gains, jnp.zeros((3, GROUP_W), F32)], axis=0)
        dkg = jnp.concatenate([kn_d[l], jnp.zeros((HEAD_DIM,), F32)]).reshape(1, LANES)

        outs = _in_proj_call(xf, scale1, shift1, norm1_g[l].reshape(1, D), w_all, gains,
                             g64, g32, gd, dkg, S, tm)
        sec = {name: o.reshape(B, S, o.shape[-1]) for (name, _, _), o in zip(SECTIONS, outs)}
        misc = sec["misc"]
        ik = misc[..., :IDX_DIM].astype(BF16)
        iw = misc[..., IDX_DIM:IDX_DIM + IDX_HEADS]
        af = misc[..., IDX_DIM + IDX_HEADS:IDX_DIM + IDX_HEADS + HEADS]
        af_t = jnp.concatenate([af.transpose(0, 2, 1), jnp.zeros((B, 8 - HEADS, S), F32)], axis=1)
        bf_col = jnp.concatenate([b_f[l].astype(F32), jnp.zeros((8 - HEADS,), F32)]).reshape(8, 1)
        cum = _cum_call(af_t, bf_col, tri_incl)[:, :HEADS].reshape(B, HEADS, S // t, t)

        o_a = _attn_a_call(sec["a_q"], sec["a_k"], sec["a_vx"], cum)
        o_b = _attn_b_call(sec["b_q"], sec["b_k"], sec["b_v"], later)
        lambda_init = 0.8 - 0.6 * math.exp(-0.3 * l)
        lamv = jnp.stack([lam_q1[l], lam_k1[l], lam_q2[l], lam_k2[l]]).astype(F32)
        o_c = _attn_c_call(sec["c_q"], sec["c_k"], sec["c_vx"], lamv,
                           subln_g[l].reshape(1, HEAD_DIM).astype(F32), slopes_c, lambda_init)
        o_d = _attn_d_call(sec["d_q"], sec["i_q"], iw, sec["d_k"], sec["d_vx"], ik, earlier,
                           topk, slopes_d)

        w_r = jnp.concatenate([w_group[l], w_expert[l],
                               jnp.zeros((D, LANES - N_GROUPS - N_EXPERTS), F32)], axis=1)
        b_r = jnp.concatenate([b_group[l], b_expert[l],
                               jnp.zeros((LANES - N_GROUPS - N_EXPERTS,), F32)]).reshape(1, LANES)
        x1, h2, comb = _out_proj_call(
            xf, [o.reshape(N, GROUP_W) for o in (o_a, o_b, o_c, o_d)], mix_beta[l].reshape(1, D),
            w_out[l].astype(BF16), gate1, norm2_g[l].reshape(1, D), scale2, shift2, w_r, b_r, S, tm)

        xf = _moe_call(h2, comb,
                       w1[l].reshape(N_EXPERTS, D, EXPERT_FF).astype(BF16),
                       w3[l].reshape(N_EXPERTS, D, EXPERT_FF).astype(BF16),
                       w2[l].reshape(N_EXPERTS, EXPERT_FF, D).astype(BF16),
                       x1, gate2, S, tm_moe)

    return xf.reshape(B, S, D)
```

```python
import functools
import math

import jax
import jax.numpy as jnp
from jax import lax
from jax.experimental import pallas as pl
from jax.experimental.pallas import tpu as pltpu

F32 = jnp.float32
BF16 = jnp.bfloat16
I32 = jnp.int32

HEAD_DIM = 64
HEADS = 4
GROUP_W = HEADS * HEAD_DIM
DIFF_DIM = HEAD_DIM // 2
IDX_HEADS = 8
IDX_DIM = 32
TOPK_MAX = 256
N_GROUPS = 4
EXPERTS_PER_GROUP = 8
N_EXPERTS = N_GROUPS * EXPERTS_PER_GROUP
EXPERT_FF = 256
RMS_EPS = 1e-6
NEG_INF = -1e30
INT_MIN = -(2 ** 31)
LOG2E = math.log2(math.e)

LANES = 128
SUBLANES = 8
ATTN_BLOCK = 256
VMEM_LIMIT = 56 * 1024 * 1024

MISC_IK = 0
MISC_IW = IDX_DIM
MISC_AF = IDX_DIM + IDX_HEADS


def _params(*sem):
    return pltpu.CompilerParams(dimension_semantics=sem, vmem_limit_bytes=VMEM_LIMIT)


def _log_sigmoid(z):
    return jnp.minimum(z, 0.0) - jnp.log1p(jnp.exp(-jnp.abs(z)))


def _log2_sigmoid(z2):
    return jnp.minimum(z2, 0.0) - jnp.log2(1.0 + jnp.exp2(-jnp.abs(z2)))


def _split_bf16(x, parts):
    out = []
    rem = x
    for _ in range(parts):
        p = rem.astype(BF16)
        out.append(p)
        rem = rem - p.astype(F32)
    return out


def _dot(a, b):
    return jnp.dot(a, b, preferred_element_type=F32)


def _ada_kernel(c_ref, w_ref, b_ref, o_ref):
    c = c_ref[...]
    ca = c * jax.nn.sigmoid(c)
    o_ref[...] = jnp.dot(ca, w_ref[...], precision=lax.Precision.HIGHEST,
                         preferred_element_type=F32) + b_ref[...]


def _ada_call(c, ada_w, ada_b):
    L, D, E = ada_w.shape
    B = c.shape[0]
    tn = 1536
    return pl.pallas_call(
        _ada_kernel,
        grid=(L, E // tn),
        in_specs=[pl.BlockSpec((B, D), lambda l, j: (0, 0)),
                  pl.BlockSpec((None, D, tn), lambda l, j: (l, 0, j)),
                  pl.BlockSpec((None, 1, tn), lambda l, j: (l, 0, j))],
        out_specs=pl.BlockSpec((None, B, tn), lambda l, j: (l, 0, j)),
        out_shape=jax.ShapeDtypeStruct((L, B, E), F32),
        compiler_params=_params("arbitrary", "arbitrary"),
        name="ada_mod",
    )(c, ada_w, ada_b.reshape(L, 1, E))


SOURCES = ("a_q", "a_k", "a_v", "b_q", "b_k", "b_v", "c_q", "c_k", "c_v", "d_q", "i_q", "d_kv")
SRC_NORM = {"a_q": ("n64", 0), "a_k": ("n64", 1), "c_q": ("n32", 2), "c_k": ("n32", 3),
            "d_q": ("n64", 4), "d_kv": ("n64", 5)}
SRC_SCALE = {"b_q": HEAD_DIM ** -0.5 * LOG2E, "i_q": IDX_DIM ** -0.5}
HW = HEADS * LANES
OUTPUTS = (
    ("a_qT", "a_q", HW, "cols", False), ("a_kp", "a_k", HW, "rows", False),
    ("a_vxT", "a_v", HW, "cols", True),
    ("b_qT", "b_q", HW, "cols", False), ("b_kp", "b_k", HW, "rows", False),
    ("b_vT", "b_v", GROUP_W, "cols", False),
    ("c_qT", "c_q", 2 * HW, "cols", False), ("c_kp", "c_k", HW, "rows", False),
    ("c_vxT", "c_v", HW, "cols", True),
    ("d_qT", "d_q", HW, "cols", False), ("i_qT", "i_q", IDX_HEADS * LANES, "cols", False),
    ("d_kp", "d_kv", LANES, "rows", False), ("d_vxT", "d_kv", LANES, "cols", True),
    ("i_kp", "misc", LANES, "rows", False),
)
N_MAIN = len(SOURCES) * GROUP_W


def _in_proj_kernel(*refs):
    (x_ref, scale_ref, shift_ref, g1_ref, w_ref, gains_ref, g64_ref, g32_ref) = refs[:8]
    place_refs = refs[8:8 + len(OUTPUTS)]
    out_refs = refs[8 + len(OUTPUTS):]
    x = x_ref[...]
    ms = jnp.mean(x * x, axis=-1, keepdims=True)
    h = x * lax.rsqrt(ms + RMS_EPS) * g1_ref[...]
    h = h * (1.0 + scale_ref[...]) + shift_ref[...]
    hb = h.astype(BF16)

    wm = w_ref[:, N_MAIN:N_MAIN + LANES]
    h_hi, h_lo = _split_bf16(h, 2)
    misc = _dot(h_hi, wm) + _dot(h_lo, wm)
    out_refs[len(OUTPUTS)][...] = misc
    out_refs[len(OUTPUTS) + 1][...] = misc.T

    compact = {"misc": misc.astype(BF16)}
    for i, name in enumerate(SOURCES):
        sec = _dot(hb, w_ref[:, i * GROUP_W:(i + 1) * GROUP_W])
        if name in SRC_NORM:
            kind, r = SRC_NORM[name]
            gmat = g64_ref if kind == "n64" else g32_ref
            msq = _dot((sec * sec).astype(BF16), gmat[...])
            fac = lax.rsqrt(msq + RMS_EPS)
            if name == "d_kv":
                lane = lax.broadcasted_iota(I32, sec.shape, 1)
                fac = jnp.where(lane < HEAD_DIM, fac, 1.0)
            sec = sec * fac * gains_ref[r:r + 1, :]
        elif name in SRC_SCALE:
            sec = sec * SRC_SCALE[name]
        compact[name] = sec.astype(BF16)

    for (name, src, width, orient, ones), p_ref, o_ref in zip(OUTPUTS, place_refs, out_refs):
        if orient == "rows":
            o_ref[...] = _dot(compact[src], p_ref[...]).astype(BF16)
        else:
            res = lax.dot_general(p_ref[...], compact[src], (((1,), (1,)), ((), ())),
                                  preferred_element_type=F32)
            if ones:
                r = lax.broadcasted_iota(I32, res.shape, 0)
                res = jnp.where((r & (LANES - 1)) >= HEAD_DIM, 1.0, res)
            o_ref[...] = res.astype(BF16)


def _in_proj_call(xf, scale1, shift1, g1, w_all, gains, g64, g32, places, batch, seq, tm):
    N, D = xf.shape
    per_b = seq // tm
    row = lambda i: (i, 0)
    full = lambda i: (0, 0)
    col4 = lambda i: (i // per_b, i % per_b, 0, 0)
    out_shape, out_specs = [], []
    for name, src, width, orient, ones in OUTPUTS:
        if orient == "rows":
            out_shape.append(jax.ShapeDtypeStruct((N, width), BF16))
            out_specs.append(pl.BlockSpec((tm, width), row))
        else:
            out_shape.append(jax.ShapeDtypeStruct((batch, per_b, width, tm), BF16))
            out_specs.append(pl.BlockSpec((None, None, width, tm), col4))
    out_shape += [jax.ShapeDtypeStruct((N, LANES), F32),
                  jax.ShapeDtypeStruct((batch, per_b, LANES, tm), F32)]
    out_specs += [pl.BlockSpec((tm, LANES), row), pl.BlockSpec((None, None, LANES, tm), col4)]
    return pl.pallas_call(
        _in_proj_kernel,
        grid=(N // tm,),
        in_specs=([pl.BlockSpec((tm, D), row),
                   pl.BlockSpec((None, 1, D), lambda i: (i // per_b, 0, 0)),
                   pl.BlockSpec((None, 1, D), lambda i: (i // per_b, 0, 0)),
                   pl.BlockSpec((1, D), full),
                   pl.BlockSpec(w_all.shape, full),
                   pl.BlockSpec(gains.shape, full),
                   pl.BlockSpec(g64.shape, full),
                   pl.BlockSpec(g32.shape, full)]
                  + [pl.BlockSpec(p.shape, full) for p in places]),
        out_specs=out_specs,
        out_shape=out_shape,
        compiler_params=_params("arbitrary"),
        name="in_proj",
    )(xf, scale1, shift1, g1, w_all, gains, g64, g32, *places)


def _cum_kernel(misc_ref, bf_ref, tri_ref, sel_ref, o_ref, *, blk):
    seq = misc_ref.shape[0]
    carry = jnp.zeros((1, LANES), F32)
    tri = tri_ref[...]
    for j in range(seq // blk):
        rows = slice(j * blk, (j + 1) * blk)
        lf = _log_sigmoid(misc_ref[rows, :] + bf_ref[...])
        c = carry
        for part in _split_bf16(lf, 3):
            c = c + _dot(tri, part)
        carry = c[blk - 1:blk, :]
        parts = _split_bf16(c, 3)
        for h in range(HEADS):
            rep = _dot(parts[0], sel_ref[h]) + _dot(parts[1], sel_ref[h]) + _dot(parts[2], sel_ref[h])
            o_ref[h, rows, :] = rep * LOG2E


def _cum_call(misc, bf_row, tri, sel):
    B, S, _ = misc.shape
    blk = tri.shape[0]
    return pl.pallas_call(
        functools.partial(_cum_kernel, blk=blk),
        grid=(B,),
        in_specs=[pl.BlockSpec((None, S, LANES), lambda b: (b, 0, 0)),
                  pl.BlockSpec((1, LANES), lambda b: (0, 0)),
                  pl.BlockSpec((blk, blk), lambda b: (0, 0)),
                  pl.BlockSpec((HEADS, LANES, LANES), lambda b: (0, 0, 0))],
        out_specs=pl.BlockSpec((None, HEADS, S, LANES), lambda b: (b, 0, 0, 0)),
        out_shape=jax.ShapeDtypeStruct((B, HEADS, S, LANES), F32),
        compiler_params=_params("arbitrary"),
        name="forget_cumsum",
    )(misc, bf_row, tri, sel)


def _q_spec(width, t):
    return pl.BlockSpec((None, None, width, t), lambda b, i: (b, i, 0, 0))


def _seq_rows_spec(seq, width):
    return pl.BlockSpec((None, seq, width), lambda b, i: (b, 0, 0))


def _seq_cols_spec(nb, width, t):
    return pl.BlockSpec((None, nb, width, t), lambda b, i: (b, 0, 0, 0))


def _const_spec(shape):
    return pl.BlockSpec(shape, lambda b, i: (0,) * len(shape))


def _key_query_iota(t):
    key = lax.broadcasted_iota(I32, (t, t), 0)
    query = lax.broadcasted_iota(I32, (t, t), 1)
    return key, query


def _lane_repeat(x, t):
    return jnp.concatenate([x] * (t // LANES), axis=1)


def _softmax_steps(scores, values, m_refs, acc_refs):
    probs, alphas = [], []
    for s, m_ref in zip(scores, m_refs):
        m_old = m_ref[...]
        m_new = jnp.maximum(m_old, jnp.max(s, axis=0, keepdims=True))
        alphas.append(jnp.exp2(m_old - m_new))
        probs.append(jnp.exp2(s - m_new).astype(BF16))
        m_ref[...] = m_new
    for p, vx, alpha, acc_ref in zip(probs, values, alphas, acc_refs):
        acc_ref[...] = alpha * acc_ref[...] + _dot(vx, p)


def _softmax_init(m_scs, acc_scs):
    for m_sc, acc_sc in zip(m_scs, acc_scs):
        m_sc[...] = jnp.full(m_sc.shape, NEG_INF, F32)
        acc_sc[...] = jnp.zeros(acc_sc.shape, F32)


def _softmax_result(acc):
    return acc[:HEAD_DIM] / acc[HEAD_DIM:]


def _alibi(slope, j, qi, t):
    key = lax.broadcasted_iota(I32, (t, LANES), 0)
    return (slope * LOG2E) * (key + (j - qi) * t).astype(F32)


def _attn_a_kernel(q_ref, k_ref, vx_ref, cum_ref, o_ref, *scratch, t):
    m_sc, acc_sc = scratch[:HEADS], scratch[HEADS:]
    qi = pl.program_id(1)
    key, query = _key_query_iota(t)
    causal = key <= query
    _softmax_init(m_sc, acc_sc)

    def block(j, masked):
        off = pl.multiple_of(j * t, t)
        heads = [slice(LANES * h, LANES * (h + 1)) for h in range(HEADS)]
        scores = [_dot(k_ref[pl.ds(off, t), hs], q_ref[hs, :]) for hs in heads]
        for h in range(HEADS):
            s = scores[h] - _lane_repeat(cum_ref[h, pl.ds(off, t), :], t)
            scores[h] = jnp.where(causal, s, -jnp.inf) if masked else s
        _softmax_steps(scores, [vx_ref[j, hs, :] for hs in heads], m_sc, acc_sc)

    block(qi, True)

    def body(j, carry):
        block(j, False)
        return carry

    lax.fori_loop(0, qi, body, 0)
    o_ref[...] = jnp.concatenate([_softmax_result(acc_sc[h][...]) for h in range(HEADS)], axis=0).T


def _attn_scratch(n, t, rows=LANES):
    return [pltpu.VMEM((1, t), F32)] * n + [pltpu.VMEM((rows, t), F32)] * n


def _attn_a_call(qT, kp, vxT, cumrep):
    B, nb, _, t = qT.shape
    S = nb * t
    return pl.pallas_call(
        functools.partial(_attn_a_kernel, t=t),
        grid=(B, nb),
        in_specs=[_q_spec(HW, t), _seq_rows_spec(S, HW), _seq_cols_spec(nb, HW, t),
                  pl.BlockSpec((None, HEADS, S, LANES), lambda b, i: (b, 0, 0, 0))],
        out_specs=pl.BlockSpec((None, t, GROUP_W), lambda b, i: (b, i, 0)),
        out_shape=jax.ShapeDtypeStruct((B, S, GROUP_W), F32),
        scratch_shapes=_attn_scratch(HEADS, t),
        compiler_params=_params("arbitrary", "arbitrary"),
        name="attn_forget",
    )(qT, kp, vxT, cumrep)


def _attn_b_kernel(q_ref, k_ref, v_ref, after_ref, o_ref, *scratch, t):
    r_sc, acc_sc = scratch[:HEADS], scratch[HEADS:]
    qi = pl.program_id(1)
    key, query = _key_query_iota(t)
    strict = key < query
    for h in range(HEADS):
        r_sc[h][...] = jnp.zeros(r_sc[h].shape, F32)
        acc_sc[h][...] = jnp.zeros(acc_sc[h].shape, F32)

    def block(j, masked):
        off = pl.multiple_of(j * t, t)
        after = after_ref[...]
        heads = [slice(LANES * h, LANES * (h + 1)) for h in range(HEADS)]
        zs = [_dot(k_ref[pl.ds(off, t), hs], q_ref[hs, :]) for hs in heads]
        lbs, splits = [], []
        for h in range(HEADS):
            lb = _log2_sigmoid(zs[h])
            lm = lb - zs[h]
            if masked:
                lm = jnp.where(strict, lm, 0.0)
            splits.append(_split_bf16(lm, 2))
            r_old = r_sc[h][...]
            r_sc[h][...] = r_old + jnp.sum(lm, axis=0, keepdims=True)
            lbs.append(lb + r_old)
        suffixes = [_dot(after, hi) + _dot(after, lo) for hi, lo in splits]
        ws = []
        for h in range(HEADS):
            w = jnp.exp2(lbs[h] + suffixes[h])
            if masked:
                w = jnp.where(strict, w, 0.0)
            ws.append(w.astype(BF16))
        for h in range(HEADS):
            acc_sc[h][...] += _dot(v_ref[j, HEAD_DIM * h:HEAD_DIM * (h + 1), :], ws[h])

    block(qi, True)

    def body(i, carry):
        block(qi - 1 - i, False)
        return carry

    lax.fori_loop(0, qi, body, 0)
    o_ref[...] = jnp.concatenate([acc_sc[h][...] for h in range(HEADS)], axis=0).T


def _attn_b_call(qT, kp, vT, after):
    B, nb, _, t = qT.shape
    S = nb * t
    return pl.pallas_call(
        functools.partial(_attn_b_kernel, t=t),
        grid=(B, nb),
        in_specs=[_q_spec(HW, t), _seq_rows_spec(S, HW), _seq_cols_spec(nb, GROUP_W, t),
                  _const_spec((t, t))],
        out_specs=pl.BlockSpec((None, t, GROUP_W), lambda b, i: (b, i, 0)),
        out_shape=jax.ShapeDtypeStruct((B, S, GROUP_W), F32),
        scratch_shapes=_attn_scratch(HEADS, t, HEAD_DIM),
        compiler_params=_params("arbitrary", "arbitrary"),
        name="attn_stick",
    )(qT, kp, vT, after)


def _attn_c_kernel(q_ref, k_ref, vx_ref, lamv_ref, subg_ref, o_ref, *scratch,
                   t, slopes, lambda_init):
    m_sc, acc_sc = scratch[:2 * HEADS], scratch[2 * HEADS:]
    qi = pl.program_id(1)
    key, query = _key_query_iota(t)
    causal = key <= query
    _softmax_init(m_sc, acc_sc)

    def block(j, masked):
        off = pl.multiple_of(j * t, t)
        heads = [slice(LANES * h, LANES * (h + 1)) for h in range(HEADS)]
        scores = [_dot(k_ref[pl.ds(off, t), heads[g // 2]], q_ref[LANES * g:LANES * (g + 1), :])
                  for g in range(2 * HEADS)]
        for g in range(2 * HEADS):
            s = scores[g] + _lane_repeat(_alibi(slopes[g // 2], j, qi, t), t)
            scores[g] = jnp.where(causal, s, -jnp.inf) if masked else s
        _softmax_steps(scores, [vx_ref[j, heads[g // 2], :] for g in range(2 * HEADS)],
                       m_sc, acc_sc)

    block(qi, True)

    def body(j, carry):
        block(j, False)
        return carry

    lax.fori_loop(0, qi, body, 0)

    lv = lamv_ref[...]
    lam = (jnp.exp(jnp.sum(lv[0:1] * lv[1:2], axis=-1, keepdims=True))
           - jnp.exp(jnp.sum(lv[2:3] * lv[3:4], axis=-1, keepdims=True)) + lambda_init)
    outs = []
    for h in range(HEADS):
        o = (_softmax_result(acc_sc[2 * h][...])
             - lam * _softmax_result(acc_sc[2 * h + 1][...]))
        ms = jnp.mean(o * o, axis=0, keepdims=True)
        outs.append(o * lax.rsqrt(ms + RMS_EPS) * subg_ref[...] * (1.0 - lambda_init))
    o_ref[...] = jnp.concatenate(outs, axis=0).T


def _attn_c_call(qT, kp, vxT, lamv, subg_col, slopes, lambda_init):
    B, nb, _, t = qT.shape
    S = nb * t
    return pl.pallas_call(
        functools.partial(_attn_c_kernel, t=t, slopes=slopes, lambda_init=lambda_init),
        grid=(B, nb),
        in_specs=[_q_spec(2 * HW, t), _seq_rows_spec(S, HW), _seq_cols_spec(nb, HW, t),
                  _const_spec(lamv.shape), _const_spec(subg_col.shape)],
        out_specs=pl.BlockSpec((None, t, GROUP_W), lambda b, i: (b, i, 0)),
        out_shape=jax.ShapeDtypeStruct((B, S, GROUP_W), F32),
        scratch_shapes=_attn_scratch(2 * HEADS, t),
        compiler_params=_params("arbitrary", "arbitrary"),
        name="attn_diff",
    )(qT, kp, vxT, lamv, subg_col)


def _fold_rows(x):
    return jnp.sum(x.reshape(x.shape[0] // SUBLANES, SUBLANES, x.shape[1]), axis=0)


def _attn_d_kernel(q_ref, iq_ref, iw_ref, dk_ref, dvx_ref, ik_ref, before_ref, o_ref,
                   keys_sc, tau_sc, *scratch, t, topk, slopes):
    m_sc, acc_sc = scratch[:HEADS], scratch[HEADS:]
    qi = pl.program_id(1)
    key, query = _key_query_iota(t)
    causal = key <= query

    w = iw_ref[...] * IDX_HEADS ** -0.5

    def index_block(j, masked):
        ikb = ik_ref[pl.ds(pl.multiple_of(j * t, t), t), :]
        sc = jnp.zeros((t, t), F32)
        for hh in range(IDX_HEADS):
            z = _dot(ikb, iq_ref[LANES * hh:LANES * (hh + 1), :])
            sc = sc + w[hh:hh + 1, :] * jnp.maximum(z, 0.0)
        sc = jnp.where(sc == 0.0, 0.0, sc)
        if masked:
            sc = jnp.where(causal, sc, NEG_INF)
        bits = pltpu.bitcast(sc, I32)
        keys_sc[j] = jnp.where(bits < 0, bits ^ 0x7FFFFFFF, bits)

    index_block(qi, True)

    def index_body(j, carry):
        index_block(j, False)
        return carry

    lax.fori_loop(0, qi, index_body, 0)

    qpos = qi * t + lax.broadcasted_iota(I32, (1, t), 1)
    kt = jnp.minimum(topk, qpos + 1).astype(F32)

    def count_ge(cand):
        def body(j, acc):
            return acc + _fold_rows((keys_sc[j] >= cand).astype(F32))
        acc = lax.fori_loop(0, qi + 1, body, jnp.zeros((SUBLANES, t), F32))
        return jnp.sum(acc, axis=0, keepdims=True)

    zero = jnp.zeros((1, t), I32)
    prefix = jnp.where(count_ge(zero) >= kt, zero, INT_MIN)

    def bit_body(i, prefix):
        cand = prefix + lax.shift_left(jnp.int32(1), 30 - i)
        return jnp.where(count_ge(cand) >= kt, cand, prefix)

    tau = lax.fori_loop(0, 31, bit_body, prefix)
    tau_sc[...] = tau
    excess = jnp.max(count_ge(tau) - kt)

    @pl.when(excess > 0.0)
    def _():
        need = kt - count_ge(tau + 1)

        def tie_body(j, seen):
            kj = keys_sc[j]
            eq = kj == tau
            eqb = eq.astype(BF16)
            earlier = _dot(before_ref[...], eqb) + seen
            keys_sc[j] = jnp.where(eq & (earlier >= need), INT_MIN, kj)
            return seen + jnp.sum(eqb.astype(F32), axis=0, keepdims=True)

        lax.fori_loop(0, qi + 1, tie_body, jnp.zeros((1, t), F32))

    _softmax_init(m_sc, acc_sc)

    def body(j, carry):
        off = pl.multiple_of(j * t, t)
        sel = keys_sc[j] >= tau_sc[...]
        kb = dk_ref[pl.ds(off, t), :]
        vx = dvx_ref[j]
        scores = [_dot(kb, q_ref[LANES * h:LANES * (h + 1), :]) for h in range(HEADS)]
        for h in range(HEADS):
            scores[h] = jnp.where(sel, scores[h] + _lane_repeat(_alibi(slopes[h], j, qi, t), t),
                                  -jnp.inf)
        _softmax_steps(scores, [vx] * HEADS, m_sc, acc_sc)
        return carry

    lax.fori_loop(0, qi + 1, body, 0)
    o_ref[...] = jnp.concatenate([_softmax_result(acc_sc[h][...]) for h in range(HEADS)], axis=0).T


def _attn_d_call(qT, iqT, miscT, dkp, dvxT, ikp, before, topk, slopes):
    B, nb, _, t = qT.shape
    S = nb * t
    iw_block = MISC_IW // IDX_HEADS
    return pl.pallas_call(
        functools.partial(_attn_d_kernel, t=t, topk=topk, slopes=slopes),
        grid=(B, nb),
        in_specs=[_q_spec(HW, t), _q_spec(IDX_HEADS * LANES, t),
                  pl.BlockSpec((None, None, IDX_HEADS, t), lambda b, i: (b, i, iw_block, 0)),
                  _seq_rows_spec(S, LANES), _seq_cols_spec(nb, LANES, t), _seq_rows_spec(S, LANES),
                  _const_spec((t, t))],
        out_specs=pl.BlockSpec((None, t, GROUP_W), lambda b, i: (b, i, 0)),
        out_shape=jax.ShapeDtypeStruct((B, S, GROUP_W), F32),
        scratch_shapes=([pltpu.VMEM((nb, t, t), I32), pltpu.VMEM((1, t), I32)]
                        + _attn_scratch(HEADS, t)),
        compiler_params=_params("arbitrary", "arbitrary"),
        name="attn_sparse",
    )(qT, iqT, miscT, dkp, dvxT, ikp, before)


def _out_proj_kernel(x_ref, oa_ref, ob_ref, oc_ref, od_ref, beta_ref, wo_ref, gate_ref,
                     g2_ref, scale_ref, shift_ref, wr_ref, br_ref,
                     x1_ref, h2_ref, comb_ref):
    acc = None
    for i, o_ref in enumerate((oa_ref, ob_ref, oc_ref, od_ref)):
        sl = slice(GROUP_W * i, GROUP_W * (i + 1))
        mix = (o_ref[...] * beta_ref[:, sl]).astype(BF16)
        part = jnp.dot(mix, wo_ref[sl, :], preferred_element_type=F32)
        acc = part if acc is None else acc + part
    x1 = x_ref[...] + gate_ref[...] * acc
    x1_ref[...] = x1
    ms = jnp.mean(x1 * x1, axis=-1, keepdims=True)
    h2 = x1 * lax.rsqrt(ms + RMS_EPS) * g2_ref[...]
    h2 = h2 * (1.0 + scale_ref[...]) + shift_ref[...]
    h2_ref[...] = h2.astype(BF16)

    logits = jnp.dot(h2, wr_ref[...], precision=lax.Precision.HIGHEST,
                     preferred_element_type=F32) + br_ref[...]
    lt = logits.T
    tm = lt.shape[1]
    g = lt[0:N_GROUPS]
    gmax = jnp.max(g, axis=0, keepdims=True)
    gi = lax.broadcasted_iota(I32, g.shape, 0)
    gidx = jnp.min(jnp.where(g == gmax, gi, N_GROUPS), axis=0, keepdims=True)
    g_prob = 1.0 / jnp.sum(jnp.exp(g - gmax), axis=0, keepdims=True)
    e_sel = jnp.zeros((EXPERTS_PER_GROUP, tm), F32)
    for gg in range(N_GROUPS):
        lo = N_GROUPS + EXPERTS_PER_GROUP * gg
        e_sel = e_sel + jnp.where(gidx == gg, lt[lo:lo + EXPERTS_PER_GROUP], 0.0)
    ei = lax.broadcasted_iota(I32, e_sel.shape, 0)
    v1 = jnp.max(e_sel, axis=0, keepdims=True)
    i1 = jnp.min(jnp.where(e_sel == v1, ei, EXPERTS_PER_GROUP), axis=0, keepdims=True)
    rest = jnp.where(ei == i1, -jnp.inf, e_sel)
    v2 = jnp.max(rest, axis=0, keepdims=True)
    i2 = jnp.min(jnp.where(rest == v2, ei, EXPERTS_PER_GROUP), axis=0, keepdims=True)
    e2 = jnp.exp(v2 - v1)
    w1 = g_prob / (1.0 + e2)
    w2 = g_prob * e2 / (1.0 + e2)
    in_group = jnp.where(ei == i1, w1, 0.0) + jnp.where(ei == i2, w2, 0.0)
    pieces = [jnp.where(gidx == gg, in_group, 0.0) for gg in range(N_GROUPS)]
    pieces.append(jnp.zeros((LANES - N_EXPERTS, tm), F32))
    comb_ref[...] = jnp.concatenate(pieces, axis=0).T


def _out_proj_call(xf, outs, beta, w_out, gate1, g2, scale2, shift2, w_r, b_r, seq, tm):
    N, D = xf.shape
    per_b = seq // tm
    row = lambda i: (i, 0)
    full = lambda i: (0, 0)
    per_batch = pl.BlockSpec((None, 1, D), lambda i: (i // per_b, 0, 0))
    return pl.pallas_call(
        _out_proj_kernel,
        grid=(N // tm,),
        in_specs=([pl.BlockSpec((tm, D), row)] + [pl.BlockSpec((tm, GROUP_W), row)] * 4
                  + [pl.BlockSpec((1, D), full), pl.BlockSpec((D, D), full), per_batch,
                     pl.BlockSpec((1, D), full), per_batch, per_batch,
                     pl.BlockSpec((D, LANES), full), pl.BlockSpec((1, LANES), full)]),
        out_specs=[pl.BlockSpec((tm, D), row), pl.BlockSpec((tm, D), row),
                   pl.BlockSpec((tm, LANES), row)],
        out_shape=[jax.ShapeDtypeStruct((N, D), F32), jax.ShapeDtypeStruct((N, D), BF16),
                   jax.ShapeDtypeStruct((N, LANES), F32)],
        compiler_params=_params("arbitrary"),
        name="out_proj_router",
    )(xf, *outs, beta, w_out, gate1, g2, scale2, shift2, w_r, b_r)


def _moe_kernel(h_ref, comb_ref, w1_ref, w3_ref, w2_ref, x1_ref, gate_ref, o_ref, acc_sc):
    e = pl.program_id(1)

    @pl.when(e == 0)
    def _():
        acc_sc[...] = jnp.zeros_like(acc_sc)

    h = h_ref[...]
    a = jnp.dot(h, w1_ref[...], preferred_element_type=F32)
    b = jnp.dot(h, w3_ref[...], preferred_element_type=F32)
    comb = comb_ref[...]
    lane = lax.broadcasted_iota(I32, comb.shape, 1)
    ce = jnp.sum(jnp.where(lane == e, comb, 0.0), axis=-1, keepdims=True)
    hid = a * jax.nn.sigmoid(a) * b * ce
    acc_sc[...] += jnp.dot(hid.astype(BF16), w2_ref[...], preferred_element_type=F32)

    @pl.when(e == pl.num_programs(1) - 1)
    def _():
        o_ref[...] = x1_ref[...] + gate_ref[...] * acc_sc[...]


def _moe_call(h2, comb, w1, w3, w2, x1, gate2, seq, tm):
    N, D = x1.shape
    E, _, FF = w1.shape
    per_b = seq // tm
    return pl.pallas_call(
        _moe_kernel,
        grid=(N // tm, E),
        in_specs=[pl.BlockSpec((tm, D), lambda i, e: (i, 0)),
                  pl.BlockSpec((tm, LANES), lambda i, e: (i, 0)),
                  pl.BlockSpec((None, D, FF), lambda i, e: (e, 0, 0)),
                  pl.BlockSpec((None, D, FF), lambda i, e: (e, 0, 0)),
                  pl.BlockSpec((None, FF, D), lambda i, e: (e, 0, 0)),
                  pl.BlockSpec((tm, D), lambda i, e: (i, 0)),
                  pl.BlockSpec((None, 1, D), lambda i, e: (i // per_b, 0, 0))],
        out_specs=pl.BlockSpec((tm, D), lambda i, e: (i, 0)),
        out_shape=jax.ShapeDtypeStruct((N, D), F32),
        scratch_shapes=[pltpu.VMEM((tm, D), F32)],
        compiler_params=_params("arbitrary", "arbitrary"),
        name="moe_experts",
    )(h2, comb, w1, w3, w2, x1, gate2)


def _block_diag_mean(width, group, valid_in_128=None):
    i = jnp.arange(width)
    same = (i[:, None] // group) == (i[None, :] // group)
    if valid_in_128 is not None:
        same = same & ((i[:, None] % LANES) < valid_in_128) & ((i[None, :] % LANES) < valid_in_128)
    return jnp.where(same, 1.0 / group, 0.0).astype(BF16)


def _placement(src_width, dst_width, pairs):
    src = jnp.array([p[0] for p in pairs], I32)
    dst = jnp.array([p[1] for p in pairs], I32)
    return jnp.zeros((src_width, dst_width), F32).at[src, dst].set(1.0).astype(BF16)


def _placements():
    per_head = [(HEAD_DIM * h + d, LANES * h + d) for h in range(HEADS) for d in range(HEAD_DIM)]
    diff_q = [(HEAD_DIM * h + DIFF_DIM * c + d, LANES * (2 * h + c) + DIFF_DIM * c + d)
              for h in range(HEADS) for c in range(2) for d in range(DIFF_DIM)]
    idx_q = [(IDX_DIM * hh + d, LANES * hh + d) for hh in range(IDX_HEADS) for d in range(IDX_DIM)]
    ident = [(d, d) for d in range(GROUP_W)]
    first64 = [(d, d) for d in range(HEAD_DIM)]
    second64 = [(HEAD_DIM + d, d) for d in range(HEAD_DIM)]
    idx_k = [(MISC_IK + d, d) for d in range(IDX_DIM)]
    table = {
        "a_qT": (GROUP_W, per_head), "a_kp": (GROUP_W, per_head), "a_vxT": (GROUP_W, per_head),
        "b_qT": (GROUP_W, per_head), "b_kp": (GROUP_W, per_head), "b_vT": (GROUP_W, ident),
        "c_qT": (GROUP_W, diff_q), "c_kp": (GROUP_W, per_head), "c_vxT": (GROUP_W, per_head),
        "d_qT": (GROUP_W, per_head), "i_qT": (GROUP_W, idx_q),
        "d_kp": (GROUP_W, first64), "d_vxT": (GROUP_W, second64), "i_kp": (LANES, idx_k),
    }
    out = []
    for name, src, width, orient, ones in OUTPUTS:
        src_width, pairs = table[name]
        p = _placement(src_width, width, pairs)
        out.append(p if orient == "rows" else p.T)
    return out


def kernel(x, c, ada_w, ada_b, norm1_g, norm2_g, w_in, b_f, qn_a, kn_a, qn_c, kn_c,
           lam_q1, lam_k1, lam_q2, lam_k2, subln_g, qn_d, kn_d, mix_beta, w_out,
           w_group, b_group, w_expert, b_expert, w1, w3, w2):
    B, S, D = x.shape
    L = ada_w.shape[0]
    N = B * S
    topk = min(TOPK_MAX, S // 4)
    t = ATTN_BLOCK
    tm = t
    tm_moe = min(1024, S)
    slopes = [2.0 ** (-8.0 * i / (2 * HEADS)) for i in range(1, 2 * HEADS + 1)]
    slopes_c, slopes_d = tuple(slopes[0::2]), tuple(slopes[1::2])

    splits = (GROUP_W, GROUP_W, GROUP_W, HEADS, GROUP_W, GROUP_W, GROUP_W, GROUP_W, GROUP_W,
              GROUP_W, GROUP_W, HEAD_DIM, HEAD_DIM, IDX_HEADS * IDX_DIM, IDX_DIM, IDX_HEADS)
    offs = [0]
    for s_ in splits:
        offs.append(offs[-1] + s_)
    (A_Q, A_K, A_V, A_F, B_Q, B_K, B_V, C_Q, C_K, C_V, D_Q, D_K, D_V, I_Q, I_K, I_W) = range(16)

    idx_t = jnp.arange(t)
    after = (idx_t[None, :] > idx_t[:, None]).astype(BF16)
    before = (idx_t[None, :] < idx_t[:, None]).astype(BF16)
    upto = (idx_t[None, :] <= idx_t[:, None]).astype(BF16)
    g64 = _block_diag_mean(GROUP_W, HEAD_DIM)
    g32 = _block_diag_mean(GROUP_W, DIFF_DIM)
    places = _placements()
    lane = jnp.arange(LANES)
    cum_sel = jnp.stack([jnp.broadcast_to((lane == MISC_AF + h)[:, None], (LANES, LANES))
                         for h in range(HEADS)]).astype(BF16)

    mod = _ada_call(c, ada_w, ada_b)
    xf = x.reshape(N, D)

    for l in range(L):
        m6 = mod[l].reshape(B, 6, 1, D)
        shift1, scale1, gate1, shift2, scale2, gate2 = (m6[:, i] for i in range(6))

        wl = w_in[l]

        def cols(idx, wl=wl):
            return wl[:, offs[idx]:offs[idx + 1]]

        pieces = {
            "a_q": cols(A_Q), "a_k": cols(A_K), "a_v": cols(A_V),
            "b_q": cols(B_Q), "b_k": cols(B_K), "b_v": cols(B_V),
            "c_q": cols(C_Q), "c_k": cols(C_K), "c_v": cols(C_V),
            "d_q": cols(D_Q), "i_q": cols(I_Q),
            "d_kv": jnp.concatenate([cols(D_K), cols(D_V), jnp.zeros((D, GROUP_W - 2 * HEAD_DIM), F32)],
                                    axis=1),
        }
        misc_w = jnp.concatenate([cols(I_K), cols(I_W), cols(A_F),
                                  jnp.zeros((D, LANES - IDX_DIM - IDX_HEADS - HEADS), F32)], axis=1)
        w_all = jnp.concatenate([pieces[name] for name in SOURCES] + [misc_w], axis=1).astype(BF16)
        ones = jnp.ones((GROUP_W - HEAD_DIM,), F32)
        gains = jnp.stack([jnp.tile(qn_a[l], HEADS) * (HEAD_DIM ** -0.5 * LOG2E),
                           jnp.tile(kn_a[l], HEADS),
                           jnp.tile(qn_c[l], 2 * HEADS) * (DIFF_DIM ** -0.5 * LOG2E),
                           jnp.tile(kn_c[l], 2 * HEADS),
                           jnp.tile(qn_d[l], HEADS) * (HEAD_DIM ** -0.5 * LOG2E),
                           jnp.concatenate([kn_d[l], ones])]).astype(F32)
        gains = jnp.concatenate([gains, jnp.zeros((2, GROUP_W), F32)], axis=0)

        outs = _in_proj_call(xf, scale1, shift1, norm1_g[l].reshape(1, D), w_all, gains,
                             g64, g32, places, B, S, tm)
        sec = {name: o for (name, _, _, _, _), o in zip(OUTPUTS, outs)}
        for name, src, width, orient, ones_ in OUTPUTS:
            if orient == "rows":
                sec[name] = sec[name].reshape(B, S, width)
        misc = outs[len(OUTPUTS)].reshape(B, S, LANES)
        miscT = outs[len(OUTPUTS) + 1]

        bf_row = jnp.zeros((1, LANES), F32).at[0, MISC_AF:MISC_AF + HEADS].set(b_f[l].astype(F32))
        cumrep = _cum_call(misc, bf_row, upto, cum_sel)

        o_a = _attn_a_call(sec["a_qT"], sec["a_kp"], sec["a_vxT"], cumrep)
        o_b = _attn_b_call(sec["b_qT"], sec["b_kp"], sec["b_vT"], after)
        lambda_init = 0.8 - 0.6 * math.exp(-0.3 * l)
        lamv = jnp.stack([lam_q1[l], lam_k1[l], lam_q2[l], lam_k2[l]]).astype(F32)
        o_c = _attn_c_call(sec["c_qT"], sec["c_kp"], sec["c_vxT"], lamv,
                           subln_g[l].reshape(HEAD_DIM, 1).astype(F32), slopes_c, lambda_init)
        o_d = _attn_d_call(sec["d_qT"], sec["i_qT"], miscT, sec["d_kp"], sec["d_vxT"], sec["i_kp"],
                           before, topk, slopes_d)

        w_r = jnp.concatenate([w_group[l], w_expert[l],
                               jnp.zeros((D, LANES - N_GROUPS - N_EXPERTS), F32)], axis=1)
        b_r = jnp.concatenate([b_group[l], b_expert[l],
                               jnp.zeros((LANES - N_GROUPS - N_EXPERTS,), F32)]).reshape(1, LANES)
        x1, h2, comb = _out_proj_call(
            xf, [o.reshape(N, GROUP_W) for o in (o_a, o_b, o_c, o_d)], mix_beta[l].reshape(1, D),
            w_out[l].astype(BF16), gate1, norm2_g[l].reshape(1, D), scale2, shift2, w_r, b_r, S, tm)

        xf = _moe_call(h2, comb,
                       w1[l].reshape(N_EXPERTS, D, EXPERT_FF).astype(BF16),
                       w3[l].reshape(N_EXPERTS, D, EXPERT_FF).astype(BF16),
                       w2[l].reshape(N_EXPERTS, EXPERT_FF, D).astype(BF16),
                       x1, gate2, S, tm_moe)

    return xf.reshape(B, S, D)
```

```python
import functools
import math

import jax
import jax.numpy as jnp
from jax import lax
from jax.experimental import pallas as pl
from jax.experimental.pallas import tpu as pltpu

F32 = jnp.float32
BF16 = jnp.bfloat16
I32 = jnp.int32

HEAD_DIM = 64
HEADS = 4
GROUP_W = HEADS * HEAD_DIM
DIFF_DIM = HEAD_DIM // 2
IDX_HEADS = 8
IDX_DIM = 32
TOPK_MAX = 256
N_GROUPS = 4
EXPERTS_PER_GROUP = 8
N_EXPERTS = N_GROUPS * EXPERTS_PER_GROUP
EXPERT_FF = 256
RMS_EPS = 1e-6
NEG_INF = -1e30
INT_MIN = -(2 ** 31)
LOG2E = math.log2(math.e)

LANES = 128
SUBLANES = 8
ATTN_BLOCK = 256
VMEM_LIMIT = 56 * 1024 * 1024

MISC_IK = 0
MISC_IW = IDX_DIM
MISC_AF = IDX_DIM + IDX_HEADS


def _params(*sem):
    return pltpu.CompilerParams(dimension_semantics=sem, vmem_limit_bytes=VMEM_LIMIT)


def _log_sigmoid(z):
    return jnp.minimum(z, 0.0) - jnp.log1p(jnp.exp(-jnp.abs(z)))


def _log2_sigmoid(z2):
    return jnp.minimum(z2, 0.0) - jnp.log2(1.0 + jnp.exp2(-jnp.abs(z2)))


def _split_bf16(x, parts):
    out = []
    rem = x
    for _ in range(parts):
        p = rem.astype(BF16)
        out.append(p)
        rem = rem - p.astype(F32)
    return out


def _dot(a, b):
    return jnp.dot(a, b, preferred_element_type=F32)


def _ada_kernel(c_ref, w_ref, b_ref, o_ref):
    c = c_ref[...]
    ca = c * jax.nn.sigmoid(c)
    o_ref[...] = jnp.dot(ca, w_ref[...], precision=lax.Precision.HIGHEST,
                         preferred_element_type=F32) + b_ref[...]


def _ada_call(c, ada_w, ada_b):
    L, D, E = ada_w.shape
    B = c.shape[0]
    tn = 1536
    return pl.pallas_call(
        _ada_kernel,
        grid=(L, E // tn),
        in_specs=[pl.BlockSpec((B, D), lambda l, j: (0, 0)),
                  pl.BlockSpec((None, D, tn), lambda l, j: (l, 0, j)),
                  pl.BlockSpec((None, 1, tn), lambda l, j: (l, 0, j))],
        out_specs=pl.BlockSpec((None, B, tn), lambda l, j: (l, 0, j)),
        out_shape=jax.ShapeDtypeStruct((L, B, E), F32),
        compiler_params=_params("arbitrary", "arbitrary"),
        name="ada_mod",
    )(c, ada_w, ada_b.reshape(L, 1, E))


SOURCES = ("a_q", "a_k", "a_v", "b_q", "b_k", "b_v", "c_q", "c_k", "c_v", "d_q", "i_q", "d_kv")
SRC_NORM = {"a_q": ("n64", 0), "a_k": ("n64", 1), "c_q": ("n32", 2), "c_k": ("n32", 3),
            "d_q": ("n64", 4), "d_kv": ("n64", 5)}
SRC_SCALE = {"b_q": HEAD_DIM ** -0.5 * LOG2E, "i_q": IDX_DIM ** -0.5}
HW = HEADS * LANES
OUTPUTS = (
    ("a_qT", "a_q", HW, "cols", False), ("a_kp", "a_k", HW, "rows", False),
    ("a_vxT", "a_v", HW, "cols", True),
    ("b_qT", "b_q", HW, "cols", False), ("b_kp", "b_k", HW, "rows", False),
    ("b_vT", "b_v", GROUP_W, "cols", False),
    ("c_qT", "c_q", 2 * HW, "cols", False), ("c_kp", "c_k", HW, "rows", False),
    ("c_vxT", "c_v", HW, "cols", True),
    ("d_qT", "d_q", HW, "cols", False), ("i_qT", "i_q", IDX_HEADS * LANES, "cols", False),
    ("d_kp", "d_kv", LANES, "rows", False), ("d_vxT", "d_kv", LANES, "cols", True),
    ("i_kp", "misc", LANES, "rows", False),
)
N_MAIN = len(SOURCES) * GROUP_W

IN_SPLITS = (GROUP_W, GROUP_W, GROUP_W, HEADS, GROUP_W, GROUP_W, GROUP_W, GROUP_W, GROUP_W,
             GROUP_W, GROUP_W, HEAD_DIM, HEAD_DIM, IDX_HEADS * IDX_DIM, IDX_DIM, IDX_HEADS)
IN_NAMES = ("a_q", "a_k", "a_v", "a_f", "b_q", "b_k", "b_v", "c_q", "c_k", "c_v",
            "d_q", "d_k", "d_v", "i_q", "i_k", "i_w")
IN_OFFSETS = {n: (sum(IN_SPLITS[:i]), IN_SPLITS[i]) for i, n in enumerate(IN_NAMES)}
P_IN = sum(IN_SPLITS)
SECTION_PARTS = {name: (name,) for name in SOURCES if name != "d_kv"}
SECTION_PARTS["d_kv"] = ("d_k", "d_v")
SECTION_PARTS["misc"] = ("i_k", "i_w", "a_f")
REALIGN_ROWS = 256


def _realign_weights(w_ref, w_sc):
    d_model = w_ref.shape[0]
    for r0 in range(0, d_model, REALIGN_ROWS):
        rows = slice(r0, r0 + REALIGN_ROWS)
        for i, name in enumerate(SOURCES + ("misc",)):
            width = LANES if name == "misc" else GROUP_W
            pieces, used = [], 0
            for part in SECTION_PARTS[name]:
                off, w = IN_OFFSETS[part]
                base = off // LANES * LANES
                end = min(-(-(off + w) // LANES) * LANES, P_IN)
                window = w_ref[rows, base:end]
                pieces.append(window[:, off - base:off - base + w])
                used += w
            if used < width:
                pieces.append(jnp.zeros((REALIGN_ROWS, width - used), BF16))
            block = pieces[0] if len(pieces) == 1 else jnp.concatenate(pieces, axis=1)
            w_sc[rows, i * GROUP_W:i * GROUP_W + width] = block


def _in_proj_kernel(*refs):
    (x_ref, scale_ref, shift_ref, g1_ref, w_in_ref, gains_ref, g64_ref, g32_ref) = refs[:8]
    place_refs = refs[8:8 + len(OUTPUTS)]
    out_refs = refs[8 + len(OUTPUTS):-1]
    w_ref = refs[-1]

    @pl.when(pl.program_id(0) == 0)
    def _():
        _realign_weights(w_in_ref, w_ref)

    x = x_ref[...]
    ms = jnp.mean(x * x, axis=-1, keepdims=True)
    h = x * lax.rsqrt(ms + RMS_EPS) * g1_ref[...]
    h = h * (1.0 + scale_ref[...]) + shift_ref[...]
    hb = h.astype(BF16)

    wm = w_ref[:, N_MAIN:N_MAIN + LANES]
    h_hi, h_lo = _split_bf16(h, 2)
    misc = _dot(h_hi, wm) + _dot(h_lo, wm)
    out_refs[len(OUTPUTS)][...] = misc
    out_refs[len(OUTPUTS) + 1][...] = misc.T

    compact = {"misc": misc.astype(BF16)}
    for i, name in enumerate(SOURCES):
        sec = _dot(hb, w_ref[:, i * GROUP_W:(i + 1) * GROUP_W])
        if name in SRC_NORM:
            kind, r = SRC_NORM[name]
            gmat = g64_ref if kind == "n64" else g32_ref
            msq = _dot((sec * sec).astype(BF16), gmat[...])
            fac = lax.rsqrt(msq + RMS_EPS)
            if name == "d_kv":
                lane = lax.broadcasted_iota(I32, sec.shape, 1)
                fac = jnp.where(lane < HEAD_DIM, fac, 1.0)
            sec = sec * fac * gains_ref[r:r + 1, :]
        elif name in SRC_SCALE:
            sec = sec * SRC_SCALE[name]
        compact[name] = sec.astype(BF16)

    for (name, src, width, orient, ones), p_ref, o_ref in zip(OUTPUTS, place_refs, out_refs):
        if orient == "rows":
            o_ref[...] = _dot(compact[src], p_ref[...]).astype(BF16)
        else:
            res = lax.dot_general(p_ref[...], compact[src], (((1,), (1,)), ((), ())),
                                  preferred_element_type=F32)
            if ones:
                r = lax.broadcasted_iota(I32, res.shape, 0)
                res = jnp.where((r & (LANES - 1)) >= HEAD_DIM, 1.0, res)
            o_ref[...] = res.astype(BF16)


def _in_proj_call(xf, scale1, shift1, g1, w_all, gains, g64, g32, places, batch, seq, tm):
    N, D = xf.shape
    per_b = seq // tm
    row = lambda i: (i, 0)
    full = lambda i: (0, 0)
    col4 = lambda i: (i // per_b, i % per_b, 0, 0)
    out_shape, out_specs = [], []
    for name, src, width, orient, ones in OUTPUTS:
        if orient == "rows":
            out_shape.append(jax.ShapeDtypeStruct((N, width), BF16))
            out_specs.append(pl.BlockSpec((tm, width), row))
        else:
            out_shape.append(jax.ShapeDtypeStruct((batch, per_b, width, tm), BF16))
            out_specs.append(pl.BlockSpec((None, None, width, tm), col4))
    out_shape += [jax.ShapeDtypeStruct((N, LANES), F32),
                  jax.ShapeDtypeStruct((batch, per_b, LANES, tm), F32)]
    out_specs += [pl.BlockSpec((tm, LANES), row), pl.BlockSpec((None, None, LANES, tm), col4)]
    return pl.pallas_call(
        _in_proj_kernel,
        grid=(N // tm,),
        in_specs=([pl.BlockSpec((tm, D), row),
                   pl.BlockSpec((None, 1, D), lambda i: (i // per_b, 0, 0)),
                   pl.BlockSpec((None, 1, D), lambda i: (i // per_b, 0, 0)),
                   pl.BlockSpec((1, D), full),
                   pl.BlockSpec(w_all.shape, full),
                   pl.BlockSpec(gains.shape, full),
                   pl.BlockSpec(g64.shape, full),
                   pl.BlockSpec(g32.shape, full)]
                  + [pl.BlockSpec(p.shape, full) for p in places]),
        out_specs=out_specs,
        out_shape=out_shape,
        scratch_shapes=[pltpu.VMEM((D, N_MAIN + LANES), BF16)],
        compiler_params=_params("arbitrary"),
        name="in_proj",
    )(xf, scale1, shift1, g1, w_all, gains, g64, g32, *places)


def _cum_kernel(misc_ref, bf_ref, tri_ref, sel_ref, o_ref, *, blk):
    seq = misc_ref.shape[0]
    carry = jnp.zeros((1, LANES), F32)
    tri = tri_ref[...]
    for j in range(seq // blk):
        rows = slice(j * blk, (j + 1) * blk)
        lf = _log_sigmoid(misc_ref[rows, :] + bf_ref[...])
        c = carry
        for part in _split_bf16(lf, 3):
            c = c + _dot(tri, part)
        carry = c[blk - 1:blk, :]
        parts = _split_bf16(c, 3)
        for h in range(HEADS):
            rep = _dot(parts[0], sel_ref[h]) + _dot(parts[1], sel_ref[h]) + _dot(parts[2], sel_ref[h])
            o_ref[h, rows, :] = rep * LOG2E


def _cum_call(misc, bf_row, tri, sel):
    B, S, _ = misc.shape
    blk = tri.shape[0]
    return pl.pallas_call(
        functools.partial(_cum_kernel, blk=blk),
        grid=(B,),
        in_specs=[pl.BlockSpec((None, S, LANES), lambda b: (b, 0, 0)),
                  pl.BlockSpec((1, LANES), lambda b: (0, 0)),
                  pl.BlockSpec((blk, blk), lambda b: (0, 0)),
                  pl.BlockSpec((HEADS, LANES, LANES), lambda b: (0, 0, 0))],
        out_specs=pl.BlockSpec((None, HEADS, S, LANES), lambda b: (b, 0, 0, 0)),
        out_shape=jax.ShapeDtypeStruct((B, HEADS, S, LANES), F32),
        compiler_params=_params("arbitrary"),
        name="forget_cumsum",
    )(misc, bf_row, tri, sel)


def _q_spec(width, t):
    return pl.BlockSpec((None, None, width, t), lambda b, i: (b, i, 0, 0))


def _seq_rows_spec(seq, width):
    return pl.BlockSpec((None, seq, width), lambda b, i: (b, 0, 0))


def _seq_cols_spec(nb, width, t):
    return pl.BlockSpec((None, nb, width, t), lambda b, i: (b, 0, 0, 0))


def _const_spec(shape):
    return pl.BlockSpec(shape, lambda b, i: (0,) * len(shape))


def _key_query_iota(t):
    key = lax.broadcasted_iota(I32, (t, t), 0)
    query = lax.broadcasted_iota(I32, (t, t), 1)
    return key, query


def _lane_repeat(x, t):
    return jnp.concatenate([x] * (t // LANES), axis=1)


def _softmax_steps(scores, values, m_refs, acc_refs):
    probs, alphas = [], []
    for s, m_ref in zip(scores, m_refs):
        m_old = m_ref[...]
        m_new = jnp.maximum(m_old, jnp.max(s, axis=0, keepdims=True))
        alphas.append(jnp.exp2(m_old - m_new))
        probs.append(jnp.exp2(s - m_new).astype(BF16))
        m_ref[...] = m_new
    for p, vx, alpha, acc_ref in zip(probs, values, alphas, acc_refs):
        acc_ref[...] = alpha * acc_ref[...] + _dot(vx, p)


def _softmax_init(m_scs, acc_scs):
    for m_sc, acc_sc in zip(m_scs, acc_scs):
        m_sc[...] = jnp.full(m_sc.shape, NEG_INF, F32)
        acc_sc[...] = jnp.zeros(acc_sc.shape, F32)


def _softmax_result(acc):
    return acc[:HEAD_DIM] / acc[HEAD_DIM:]


def _alibi(slope, j, qi, t):
    key = lax.broadcasted_iota(I32, (t, LANES), 0)
    return (slope * LOG2E) * (key + (j - qi) * t).astype(F32)


def _attn_a_kernel(q_ref, k_ref, vx_ref, cum_ref, o_ref, *scratch, t):
    m_sc, acc_sc = scratch[:HEADS], scratch[HEADS:]
    qi = pl.program_id(1)
    key, query = _key_query_iota(t)
    causal = key <= query
    _softmax_init(m_sc, acc_sc)

    def block(j, masked):
        off = pl.multiple_of(j * t, t)
        heads = [slice(LANES * h, LANES * (h + 1)) for h in range(HEADS)]
        scores = [_dot(k_ref[pl.ds(off, t), hs], q_ref[hs, :]) for hs in heads]
        for h in range(HEADS):
            s = scores[h] - _lane_repeat(cum_ref[h, pl.ds(off, t), :], t)
            scores[h] = jnp.where(causal, s, -jnp.inf) if masked else s
        _softmax_steps(scores, [vx_ref[j, hs, :] for hs in heads], m_sc, acc_sc)

    block(qi, True)

    def body(j, carry):
        block(j, False)
        return carry

    lax.fori_loop(0, qi, body, 0)
    o_ref[...] = jnp.concatenate([_softmax_result(acc_sc[h][...]) for h in range(HEADS)], axis=0).T


def _attn_scratch(n, t, rows=LANES):
    return [pltpu.VMEM((1, t), F32)] * n + [pltpu.VMEM((rows, t), F32)] * n


def _attn_a_call(qT, kp, vxT, cumrep):
    B, nb, _, t = qT.shape
    S = nb * t
    return pl.pallas_call(
        functools.partial(_attn_a_kernel, t=t),
        grid=(B, nb),
        in_specs=[_q_spec(HW, t), _seq_rows_spec(S, HW), _seq_cols_spec(nb, HW, t),
                  pl.BlockSpec((None, HEADS, S, LANES), lambda b, i: (b, 0, 0, 0))],
        out_specs=pl.BlockSpec((None, t, GROUP_W), lambda b, i: (b, i, 0)),
        out_shape=jax.ShapeDtypeStruct((B, S, GROUP_W), F32),
        scratch_shapes=_attn_scratch(HEADS, t),
        compiler_params=_params("arbitrary", "arbitrary"),
        name="attn_forget",
    )(qT, kp, vxT, cumrep)


def _attn_b_kernel(q_ref, k_ref, v_ref, after_ref, o_ref, *scratch, t):
    r_sc, acc_sc = scratch[:HEADS], scratch[HEADS:]
    qi = pl.program_id(1)
    key, query = _key_query_iota(t)
    strict = key < query
    for h in range(HEADS):
        r_sc[h][...] = jnp.zeros(r_sc[h].shape, F32)
        acc_sc[h][...] = jnp.zeros(acc_sc[h].shape, F32)

    def block(j, masked):
        off = pl.multiple_of(j * t, t)
        after = after_ref[...]
        heads = [slice(LANES * h, LANES * (h + 1)) for h in range(HEADS)]
        zs = [_dot(k_ref[pl.ds(off, t), hs], q_ref[hs, :]) for hs in heads]
        lbs, splits = [], []
        for h in range(HEADS):
            lb = _log2_sigmoid(zs[h])
            lm = lb - zs[h]
            if masked:
                lm = jnp.where(strict, lm, 0.0)
            splits.append(_split_bf16(lm, 2))
            r_old = r_sc[h][...]
            r_sc[h][...] = r_old + jnp.sum(lm, axis=0, keepdims=True)
            lbs.append(lb + r_old)
        suffixes = [_dot(after, hi) + _dot(after, lo) for hi, lo in splits]
        ws = []
        for h in range(HEADS):
            w = jnp.exp2(lbs[h] + suffixes[h])
            if masked:
                w = jnp.where(strict, w, 0.0)
            ws.append(w.astype(BF16))
        for h in range(HEADS):
            acc_sc[h][...] += _dot(v_ref[j, HEAD_DIM * h:HEAD_DIM * (h + 1), :], ws[h])

    block(qi, True)

    def body(i, carry):
        block(qi - 1 - i, False)
        return carry

    lax.fori_loop(0, qi, body, 0)
    o_ref[...] = jnp.concatenate([acc_sc[h][...] for h in range(HEADS)], axis=0).T


def _attn_b_call(qT, kp, vT, after):
    B, nb, _, t = qT.shape
    S = nb * t
    return pl.pallas_call(
        functools.partial(_attn_b_kernel, t=t),
        grid=(B, nb),
        in_specs=[_q_spec(HW, t), _seq_rows_spec(S, HW), _seq_cols_spec(nb, GROUP_W, t),
                  _const_spec((t, t))],
        out_specs=pl.BlockSpec((None, t, GROUP_W), lambda b, i: (b, i, 0)),
        out_shape=jax.ShapeDtypeStruct((B, S, GROUP_W), F32),
        scratch_shapes=_attn_scratch(HEADS, t, HEAD_DIM),
        compiler_params=_params("arbitrary", "arbitrary"),
        name="attn_stick",
    )(qT, kp, vT, after)


def _attn_c_kernel(q_ref, k_ref, vx_ref, lamv_ref, subg_ref, o_ref, *scratch,
                   t, slopes, lambda_init):
    m_sc, acc_sc = scratch[:2 * HEADS], scratch[2 * HEADS:]
    qi = pl.program_id(1)
    key, query = _key_query_iota(t)
    causal = key <= query
    _softmax_init(m_sc, acc_sc)

    def block(j, masked):
        off = pl.multiple_of(j * t, t)
        heads = [slice(LANES * h, LANES * (h + 1)) for h in range(HEADS)]
        scores = [_dot(k_ref[pl.ds(off, t), heads[g // 2]], q_ref[LANES * g:LANES * (g + 1), :])
                  for g in range(2 * HEADS)]
        for g in range(2 * HEADS):
            s = scores[g] + _lane_repeat(_alibi(slopes[g // 2], j, qi, t), t)
            scores[g] = jnp.where(causal, s, -jnp.inf) if masked else s
        _softmax_steps(scores, [vx_ref[j, heads[g // 2], :] for g in range(2 * HEADS)],
                       m_sc, acc_sc)

    block(qi, True)

    def body(j, carry):
        block(j, False)
        return carry

    lax.fori_loop(0, qi, body, 0)

    lv = lamv_ref[...]
    lam = (jnp.exp(jnp.sum(lv[0:1] * lv[1:2], axis=-1, keepdims=True))
           - jnp.exp(jnp.sum(lv[2:3] * lv[3:4], axis=-1, keepdims=True)) + lambda_init)
    outs = []
    for h in range(HEADS):
        o = (_softmax_result(acc_sc[2 * h][...])
             - lam * _softmax_result(acc_sc[2 * h + 1][...]))
        ms = jnp.mean(o * o, axis=0, keepdims=True)
        outs.append(o * lax.rsqrt(ms + RMS_EPS) * subg_ref[...] * (1.0 - lambda_init))
    o_ref[...] = jnp.concatenate(outs, axis=0).T


def _attn_c_call(qT, kp, vxT, lamv, subg_col, slopes, lambda_init):
    B, nb, _, t = qT.shape
    S = nb * t
    return pl.pallas_call(
        functools.partial(_attn_c_kernel, t=t, slopes=slopes, lambda_init=lambda_init),
        grid=(B, nb),
        in_specs=[_q_spec(2 * HW, t), _seq_rows_spec(S, HW), _seq_cols_spec(nb, HW, t),
                  _const_spec(lamv.shape), _const_spec(subg_col.shape)],
        out_specs=pl.BlockSpec((None, t, GROUP_W), lambda b, i: (b, i, 0)),
        out_shape=jax.ShapeDtypeStruct((B, S, GROUP_W), F32),
        scratch_shapes=_attn_scratch(2 * HEADS, t),
        compiler_params=_params("arbitrary", "arbitrary"),
        name="attn_diff",
    )(qT, kp, vxT, lamv, subg_col)


def _fold_rows(x):
    return jnp.sum(x.reshape(x.shape[0] // SUBLANES, SUBLANES, x.shape[1]), axis=0)


def _attn_d_kernel(q_ref, iq_ref, iw_ref, dk_ref, dvx_ref, ik_ref, before_ref, o_ref,
                   keys_sc, tau_sc, *scratch, t, topk, slopes):
    m_sc, acc_sc = scratch[:HEADS], scratch[HEADS:]
    qi = pl.program_id(1)
    key, query = _key_query_iota(t)
    causal = key <= query

    w = iw_ref[...] * IDX_HEADS ** -0.5

    def index_block(j, masked):
        ikb = ik_ref[pl.ds(pl.multiple_of(j * t, t), t), :]
        sc = jnp.zeros((t, t), F32)
        for hh in range(IDX_HEADS):
            z = _dot(ikb, iq_ref[LANES * hh:LANES * (hh + 1), :])
            sc = sc + w[hh:hh + 1, :] * jnp.maximum(z, 0.0)
        sc = jnp.where(sc == 0.0, 0.0, sc)
        if masked:
            sc = jnp.where(causal, sc, NEG_INF)
        bits = pltpu.bitcast(sc, I32)
        keys_sc[j] = jnp.where(bits < 0, bits ^ 0x7FFFFFFF, bits)

    index_block(qi, True)

    def index_body(j, carry):
        index_block(j, False)
        return carry

    lax.fori_loop(0, qi, index_body, 0)

    qpos = qi * t + lax.broadcasted_iota(I32, (1, t), 1)
    kt = jnp.minimum(topk, qpos + 1).astype(F32)

    def count_ge(cand):
        def body(j, acc):
            return acc + _fold_rows((keys_sc[j] >= cand).astype(F32))
        acc = lax.fori_loop(0, qi + 1, body, jnp.zeros((SUBLANES, t), F32))
        return jnp.sum(acc, axis=0, keepdims=True)

    zero = jnp.zeros((1, t), I32)
    prefix = jnp.where(count_ge(zero) >= kt, zero, INT_MIN)

    def bit_body(i, prefix):
        cand = prefix + lax.shift_left(jnp.int32(1), 30 - i)
        return jnp.where(count_ge(cand) >= kt, cand, prefix)

    tau = lax.fori_loop(0, 31, bit_body, prefix)
    tau_sc[...] = tau
    excess = jnp.max(count_ge(tau) - kt)

    @pl.when(excess > 0.0)
    def _():
        need = kt - count_ge(tau + 1)

        def tie_body(j, seen):
            kj = keys_sc[j]
            eq = kj == tau
            eqb = eq.astype(BF16)
            earlier = _dot(before_ref[...], eqb) + seen
            keys_sc[j] = jnp.where(eq & (earlier >= need), INT_MIN, kj)
            return seen + jnp.sum(eqb.astype(F32), axis=0, keepdims=True)

        lax.fori_loop(0, qi + 1, tie_body, jnp.zeros((1, t), F32))

    _softmax_init(m_sc, acc_sc)

    def body(j, carry):
        off = pl.multiple_of(j * t, t)
        sel = keys_sc[j] >= tau_sc[...]
        kb = dk_ref[pl.ds(off, t), :]
        vx = dvx_ref[j]
        scores = [_dot(kb, q_ref[LANES * h:LANES * (h + 1), :]) for h in range(HEADS)]
        for h in range(HEADS):
            scores[h] = jnp.where(sel, scores[h] + _lane_repeat(_alibi(slopes[h], j, qi, t), t),
                                  -jnp.inf)
        _softmax_steps(scores, [vx] * HEADS, m_sc, acc_sc)
        return carry

    lax.fori_loop(0, qi + 1, body, 0)
    o_ref[...] = jnp.concatenate([_softmax_result(acc_sc[h][...]) for h in range(HEADS)], axis=0).T


def _attn_d_call(qT, iqT, miscT, dkp, dvxT, ikp, before, topk, slopes):
    B, nb, _, t = qT.shape
    S = nb * t
    iw_block = MISC_IW // IDX_HEADS
    return pl.pallas_call(
        functools.partial(_attn_d_kernel, t=t, topk=topk, slopes=slopes),
        grid=(B, nb),
        in_specs=[_q_spec(HW, t), _q_spec(IDX_HEADS * LANES, t),
                  pl.BlockSpec((None, None, IDX_HEADS, t), lambda b, i: (b, i, iw_block, 0)),
                  _seq_rows_spec(S, LANES), _seq_cols_spec(nb, LANES, t), _seq_rows_spec(S, LANES),
                  _const_spec((t, t))],
        out_specs=pl.BlockSpec((None, t, GROUP_W), lambda b, i: (b, i, 0)),
        out_shape=jax.ShapeDtypeStruct((B, S, GROUP_W), F32),
        scratch_shapes=([pltpu.VMEM((nb, t, t), I32), pltpu.VMEM((1, t), I32)]
                        + _attn_scratch(HEADS, t)),
        compiler_params=_params("arbitrary", "arbitrary"),
        name="attn_sparse",
    )(qT, iqT, miscT, dkp, dvxT, ikp, before)


def _out_proj_kernel(x_ref, oa_ref, ob_ref, oc_ref, od_ref, beta_ref, wo_ref, gate_ref,
                     g2_ref, scale_ref, shift_ref, wr_ref, br_ref, earlier_ref,
                     x1_ref, h2x_ref, route_ref, cnt_ref, cnt_sc):
    d_model = x_ref.shape[1]
    acc = None
    for i, o_ref in enumerate((oa_ref, ob_ref, oc_ref, od_ref)):
        sl = slice(GROUP_W * i, GROUP_W * (i + 1))
        mix = (o_ref[...] * beta_ref[:, sl]).astype(BF16)
        part = jnp.dot(mix, wo_ref[sl, :], preferred_element_type=F32)
        acc = part if acc is None else acc + part
    x1 = x_ref[...] + gate_ref[...] * acc
    x1_ref[...] = x1
    ms = jnp.mean(x1 * x1, axis=-1, keepdims=True)
    h2 = x1 * lax.rsqrt(ms + RMS_EPS) * g2_ref[...]
    h2 = h2 * (1.0 + scale_ref[...]) + shift_ref[...]
    h2x_ref[:, :d_model] = h2

    logits = jnp.dot(h2, wr_ref[...], precision=lax.Precision.HIGHEST,
                     preferred_element_type=F32) + br_ref[...]
    lt = logits.T
    tm = lt.shape[1]
    g = lt[0:N_GROUPS]
    gmax = jnp.max(g, axis=0, keepdims=True)
    gi = lax.broadcasted_iota(I32, g.shape, 0)
    gidx = jnp.min(jnp.where(g == gmax, gi, N_GROUPS), axis=0, keepdims=True)
    g_prob = 1.0 / jnp.sum(jnp.exp(g - gmax), axis=0, keepdims=True)
    e_sel = jnp.zeros((EXPERTS_PER_GROUP, tm), F32)
    for gg in range(N_GROUPS):
        lo = N_GROUPS + EXPERTS_PER_GROUP * gg
        e_sel = e_sel + jnp.where(gidx == gg, lt[lo:lo + EXPERTS_PER_GROUP], 0.0)
    ei = lax.broadcasted_iota(I32, e_sel.shape, 0)
    v1 = jnp.max(e_sel, axis=0, keepdims=True)
    i1 = jnp.min(jnp.where(e_sel == v1, ei, EXPERTS_PER_GROUP), axis=0, keepdims=True)
    rest = jnp.where(ei == i1, -jnp.inf, e_sel)
    v2 = jnp.max(rest, axis=0, keepdims=True)
    i2 = jnp.min(jnp.where(rest == v2, ei, EXPERTS_PER_GROUP), axis=0, keepdims=True)
    e2 = jnp.exp(v2 - v1)
    w1 = g_prob / (1.0 + e2)
    w2 = g_prob * e2 / (1.0 + e2)
    in_group = jnp.where(ei == i1, w1, 0.0) + jnp.where(ei == i2, w2, 0.0)
    cw = jnp.concatenate([in_group, jnp.zeros((LANES - EXPERTS_PER_GROUP, tm), F32)], axis=0)
    h2x_ref[:, d_model:] = cw.T

    @pl.when(pl.program_id(0) == 0)
    def _():
        cnt_sc[...] = jnp.zeros(cnt_sc.shape, F32)

    rows = lax.broadcasted_iota(I32, (SUBLANES, tm), 0)
    onehot = (rows == gidx).astype(F32)
    seen = _dot(onehot.astype(BF16), earlier_ref[...]) + cnt_sc[...]
    rank = jnp.sum(onehot * seen, axis=0, keepdims=True)
    route_ref[...] = jnp.concatenate(
        [gidx, rank.astype(I32), jnp.zeros((SUBLANES - 2, tm), I32)], axis=0)
    cnt_sc[...] = cnt_sc[...] + jnp.sum(onehot, axis=1, keepdims=True)
    cnt_ref[...] = cnt_sc[:, :LANES]


def _out_proj_call(xf, outs, beta, w_out, gate1, g2, scale2, shift2, w_r, b_r, earlier, seq, tm):
    N, D = xf.shape
    per_b = seq // tm
    row = lambda i: (i, 0)
    full = lambda i: (0, 0)
    per_batch = pl.BlockSpec((None, 1, D), lambda i: (i // per_b, 0, 0))
    return pl.pallas_call(
        _out_proj_kernel,
        grid=(N // tm,),
        in_specs=([pl.BlockSpec((tm, D), row)] + [pl.BlockSpec((tm, GROUP_W), row)] * 4
                  + [pl.BlockSpec((1, D), full), pl.BlockSpec((D, D), full), per_batch,
                     pl.BlockSpec((1, D), full), per_batch, per_batch,
                     pl.BlockSpec((D, LANES), full), pl.BlockSpec((1, LANES), full),
                     pl.BlockSpec((tm, tm), full)]),
        out_specs=[pl.BlockSpec((tm, D), row), pl.BlockSpec((tm, D + LANES), row),
                   pl.BlockSpec((SUBLANES, tm), lambda i: (0, i)),
                   pl.BlockSpec((SUBLANES, LANES), full)],
        out_shape=[jax.ShapeDtypeStruct((N, D), F32), jax.ShapeDtypeStruct((N, D + LANES), F32),
                   jax.ShapeDtypeStruct((SUBLANES, N), I32),
                   jax.ShapeDtypeStruct((SUBLANES, LANES), F32)],
        scratch_shapes=[pltpu.VMEM((SUBLANES, tm), F32)],
        compiler_params=_params("arbitrary"),
        name="out_proj_router",
    )(xf, *outs, beta, w_out, gate1, g2, scale2, shift2, w_r, b_r, earlier)


MOE_TILE = 256
DMA_UNROLL = 8


def _row_copy(src_ref, src_row, dst_ref, dst_row, sem):
    return pltpu.make_async_copy(src_ref.at[pl.ds(src_row, 1), :],
                                 dst_ref.at[pl.ds(dst_row, 1), :], sem)


def _dispatch_kernel(slot_ref, h2x_ref, zeros_ref, xs_ref, sem):
    del zeros_ref
    rows = h2x_ref.shape[0]

    def issue(r, carry):
        _row_copy(h2x_ref, r, xs_ref, slot_ref[0, r], sem).start()
        return carry

    lax.fori_loop(0, rows, issue, 0, unroll=DMA_UNROLL)

    def drain(r, carry):
        _row_copy(h2x_ref, r, xs_ref, slot_ref[0, r], sem).wait()
        return carry

    lax.fori_loop(0, rows, drain, 0, unroll=DMA_UNROLL)


def _dispatch_call(slot, h2x, xs_zero, tm):
    N, DX = h2x.shape
    return pl.pallas_call(
        _dispatch_kernel,
        grid=(N // tm,),
        in_specs=[pl.BlockSpec((None, 1, tm), lambda i: (i, 0, 0), memory_space=pltpu.SMEM),
                  pl.BlockSpec((tm, DX), lambda i: (i, 0)),
                  pl.BlockSpec(memory_space=pl.ANY)],
        out_specs=pl.BlockSpec(memory_space=pl.ANY),
        out_shape=jax.ShapeDtypeStruct(xs_zero.shape, F32),
        scratch_shapes=[pltpu.SemaphoreType.DMA(())],
        input_output_aliases={2: 0},
        compiler_params=_params("arbitrary"),
        name="moe_dispatch",
    )(slot.reshape(N // tm, 1, tm), h2x, xs_zero)


def _expert_kernel(group_ref, valid_ref, xs_ref, token_ref, w1_ref, w3_ref, w2_ref, expand_ref,
                   y_ref, y_sc, sem):
    del group_ref
    i = pl.program_id(0)
    valid = valid_ref[i]
    d_model = y_sc.shape[1]
    ff = w1_ref.shape[2]

    @pl.when(valid > 0)
    def _():
        x = xs_ref[:, :d_model].astype(BF16)
        cw_hi, cw_lo = _split_bf16(xs_ref[:, d_model:], 2)
        cwx = _dot(cw_hi, expand_ref[...]) + _dot(cw_lo, expand_ref[...])
        acc = None
        for e in range(EXPERTS_PER_GROUP):
            a = _dot(x, w1_ref[e])
            b = _dot(x, w3_ref[e])
            hid = a * jax.nn.sigmoid(a) * b * cwx[:, ff * e:ff * (e + 1)]
            part = _dot(hid.astype(BF16), w2_ref[e])
            acc = part if acc is None else acc + part
        y_sc[...] = acc

        def issue(r, carry):
            _row_copy(y_sc, r, y_ref, token_ref[0, r], sem).start()
            return carry

        def drain(r, carry):
            _row_copy(y_sc, r, y_ref, token_ref[0, r], sem).wait()
            return carry

        tile = y_sc.shape[0]

        @pl.when(valid == tile)
        def _():
            lax.fori_loop(0, tile, issue, 0, unroll=DMA_UNROLL)
            lax.fori_loop(0, tile, drain, 0, unroll=DMA_UNROLL)

        @pl.when(valid < tile)
        def _():
            lax.fori_loop(0, valid, issue, 0)
            lax.fori_loop(0, valid, drain, 0)


def _expert_call(tile_group, tile_valid, xs, tile_token, w1, w3, w2, expand, n_tokens):
    P, DX = xs.shape
    _, _, D, FF = w1.shape
    tr = MOE_TILE
    n_tiles = P // tr
    grid_spec = pltpu.PrefetchScalarGridSpec(
        num_scalar_prefetch=2,
        grid=(n_tiles,),
        in_specs=[pl.BlockSpec((tr, DX), lambda i, g, v: (i, 0)),
                  pl.BlockSpec((None, 1, tr), lambda i, g, v: (i, 0, 0), memory_space=pltpu.SMEM),
                  pl.BlockSpec((None, EXPERTS_PER_GROUP, D, FF), lambda i, g, v: (g[i], 0, 0, 0)),
                  pl.BlockSpec((None, EXPERTS_PER_GROUP, D, FF), lambda i, g, v: (g[i], 0, 0, 0)),
                  pl.BlockSpec((None, EXPERTS_PER_GROUP, FF, D), lambda i, g, v: (g[i], 0, 0, 0)),
                  pl.BlockSpec(expand.shape, lambda i, g, v: (0, 0))],
        out_specs=pl.BlockSpec(memory_space=pl.ANY),
        scratch_shapes=[pltpu.VMEM((tr, D), F32), pltpu.SemaphoreType.DMA(())],
    )
    return pl.pallas_call(
        _expert_kernel,
        grid_spec=grid_spec,
        out_shape=jax.ShapeDtypeStruct((n_tokens, D), F32),
        compiler_params=_params("arbitrary"),
        name="moe_experts",
    )(tile_group, tile_valid, xs, tile_token.reshape(n_tiles, 1, tr), w1, w3, w2, expand)


def _residual_kernel(x1_ref, gate_ref, y_ref, o_ref):
    o_ref[...] = x1_ref[...] + gate_ref[...] * y_ref[...]


def _residual_call(x1, gate2, y, seq, tm):
    N, D = x1.shape
    per_b = seq // tm
    row = pl.BlockSpec((tm, D), lambda i: (i, 0))
    return pl.pallas_call(
        _residual_kernel,
        grid=(N // tm,),
        in_specs=[row, pl.BlockSpec((None, 1, D), lambda i: (i // per_b, 0, 0)), row],
        out_specs=row,
        out_shape=jax.ShapeDtypeStruct((N, D), F32),
        compiler_params=_params("arbitrary"),
        name="moe_residual",
    )(x1, gate2, y)


def _moe_routing(route, cnt, n_tokens):
    tr = MOE_TILE
    n_tiles = n_tokens // tr + N_GROUPS
    counts = cnt[:N_GROUPS, 0].astype(I32)
    padded = (counts + tr - 1) // tr * tr
    ends = jnp.cumsum(padded)
    starts = ends - padded
    group, rank = route[0], route[1]
    slot = starts[group] + rank
    tile_start = jnp.arange(n_tiles, dtype=I32) * tr
    tile_group = jnp.minimum(jnp.sum((tile_start[:, None] >= ends[None, :]).astype(I32), axis=1),
                             N_GROUPS - 1)
    tile_valid = jnp.clip(starts[tile_group] + counts[tile_group] - tile_start, 0, tr)
    tile_valid = jnp.where(tile_start < ends[-1], tile_valid, 0)
    tile_token = jnp.zeros((n_tiles * tr,), I32).at[slot].set(jnp.arange(n_tokens, dtype=I32))
    return slot, tile_group, tile_valid, tile_token


def _block_diag_mean(width, group, valid_in_128=None):
    i = jnp.arange(width)
    same = (i[:, None] // group) == (i[None, :] // group)
    if valid_in_128 is not None:
        same = same & ((i[:, None] % LANES) < valid_in_128) & ((i[None, :] % LANES) < valid_in_128)
    return jnp.where(same, 1.0 / group, 0.0).astype(BF16)


def _placement(src_width, dst_width, pairs):
    src = jnp.array([p[0] for p in pairs], I32)
    dst = jnp.array([p[1] for p in pairs], I32)
    return jnp.zeros((src_width, dst_width), F32).at[src, dst].set(1.0).astype(BF16)


def _placements():
    per_head = [(HEAD_DIM * h + d, LANES * h + d) for h in range(HEADS) for d in range(HEAD_DIM)]
    diff_q = [(HEAD_DIM * h + DIFF_DIM * c + d, LANES * (2 * h + c) + DIFF_DIM * c + d)
              for h in range(HEADS) for c in range(2) for d in range(DIFF_DIM)]
    idx_q = [(IDX_DIM * hh + d, LANES * hh + d) for hh in range(IDX_HEADS) for d in range(IDX_DIM)]
    ident = [(d, d) for d in range(GROUP_W)]
    first64 = [(d, d) for d in range(HEAD_DIM)]
    second64 = [(HEAD_DIM + d, d) for d in range(HEAD_DIM)]
    idx_k = [(MISC_IK + d, d) for d in range(IDX_DIM)]
    table = {
        "a_qT": (GROUP_W, per_head), "a_kp": (GROUP_W, per_head), "a_vxT": (GROUP_W, per_head),
        "b_qT": (GROUP_W, per_head), "b_kp": (GROUP_W, per_head), "b_vT": (GROUP_W, ident),
        "c_qT": (GROUP_W, diff_q), "c_kp": (GROUP_W, per_head), "c_vxT": (GROUP_W, per_head),
        "d_qT": (GROUP_W, per_head), "i_qT": (GROUP_W, idx_q),
        "d_kp": (GROUP_W, first64), "d_vxT": (GROUP_W, second64), "i_kp": (LANES, idx_k),
    }
    out = []
    for name, src, width, orient, ones in OUTPUTS:
        src_width, pairs = table[name]
        p = _placement(src_width, width, pairs)
        out.append(p if orient == "rows" else p.T)
    return out


def kernel(x, c, ada_w, ada_b, norm1_g, norm2_g, w_in, b_f, qn_a, kn_a, qn_c, kn_c,
           lam_q1, lam_k1, lam_q2, lam_k2, subln_g, qn_d, kn_d, mix_beta, w_out,
           w_group, b_group, w_expert, b_expert, w1, w3, w2):
    B, S, D = x.shape
    L = ada_w.shape[0]
    N = B * S
    topk = min(TOPK_MAX, S // 4)
    t = ATTN_BLOCK
    tm = t
    slopes = [2.0 ** (-8.0 * i / (2 * HEADS)) for i in range(1, 2 * HEADS + 1)]
    slopes_c, slopes_d = tuple(slopes[0::2]), tuple(slopes[1::2])

    idx_t = jnp.arange(t)
    after = (idx_t[None, :] > idx_t[:, None]).astype(BF16)
    before = (idx_t[None, :] < idx_t[:, None]).astype(BF16)
    upto = (idx_t[None, :] <= idx_t[:, None]).astype(BF16)
    g64 = _block_diag_mean(GROUP_W, HEAD_DIM)
    g32 = _block_diag_mean(GROUP_W, DIFF_DIM)
    places = _placements()
    lane = jnp.arange(LANES)
    cum_sel = jnp.stack([jnp.broadcast_to((lane == MISC_AF + h)[:, None], (LANES, LANES))
                         for h in range(HEADS)]).astype(BF16)
    ff_col = jnp.arange(EXPERTS_PER_GROUP * EXPERT_FF)
    expand = (lane[:, None] == ff_col[None, :] // EXPERT_FF).astype(BF16)

    mod = _ada_call(c, ada_w, ada_b)
    xf = x.reshape(N, D)

    for l in range(L):
        m6 = mod[l].reshape(B, 6, 1, D)
        shift1, scale1, gate1, shift2, scale2, gate2 = (m6[:, i] for i in range(6))

        w_all = w_in[l].astype(BF16)
        ones = jnp.ones((GROUP_W - HEAD_DIM,), F32)
        gains = jnp.stack([jnp.tile(qn_a[l], HEADS) * (HEAD_DIM ** -0.5 * LOG2E),
                           jnp.tile(kn_a[l], HEADS),
                           jnp.tile(qn_c[l], 2 * HEADS) * (DIFF_DIM ** -0.5 * LOG2E),
                           jnp.tile(kn_c[l], 2 * HEADS),
                           jnp.tile(qn_d[l], HEADS) * (HEAD_DIM ** -0.5 * LOG2E),
                           jnp.concatenate([kn_d[l], ones])]).astype(F32)
        gains = jnp.concatenate([gains, jnp.zeros((2, GROUP_W), F32)], axis=0)

        outs = _in_proj_call(xf, scale1, shift1, norm1_g[l].reshape(1, D), w_all, gains,
                             g64, g32, places, B, S, tm)
        sec = {name: o for (name, _, _, _, _), o in zip(OUTPUTS, outs)}
        for name, src, width, orient, ones_ in OUTPUTS:
            if orient == "rows":
                sec[name] = sec[name].reshape(B, S, width)
        misc = outs[len(OUTPUTS)].reshape(B, S, LANES)
        miscT = outs[len(OUTPUTS) + 1]

        bf_row = jnp.zeros((1, LANES), F32).at[0, MISC_AF:MISC_AF + HEADS].set(b_f[l].astype(F32))
        cumrep = _cum_call(misc, bf_row, upto, cum_sel)

        o_a = _attn_a_call(sec["a_qT"], sec["a_kp"], sec["a_vxT"], cumrep)
        o_b = _attn_b_call(sec["b_qT"], sec["b_kp"], sec["b_vT"], after)
        lambda_init = 0.8 - 0.6 * math.exp(-0.3 * l)
        lamv = jnp.stack([lam_q1[l], lam_k1[l], lam_q2[l], lam_k2[l]]).astype(F32)
        o_c = _attn_c_call(sec["c_qT"], sec["c_kp"], sec["c_vxT"], lamv,
                           subln_g[l].reshape(HEAD_DIM, 1).astype(F32), slopes_c, lambda_init)
        o_d = _attn_d_call(sec["d_qT"], sec["i_qT"], miscT, sec["d_kp"], sec["d_vxT"], sec["i_kp"],
                           before, topk, slopes_d)

        w_r = jnp.concatenate([w_group[l], w_expert[l],
                               jnp.zeros((D, LANES - N_GROUPS - N_EXPERTS), F32)], axis=1)
        b_r = jnp.concatenate([b_group[l], b_expert[l],
                               jnp.zeros((LANES - N_GROUPS - N_EXPERTS,), F32)]).reshape(1, LANES)
        x1, h2x, route, cnt = _out_proj_call(
            xf, [o.reshape(N, GROUP_W) for o in (o_a, o_b, o_c, o_d)], mix_beta[l].reshape(1, D),
            w_out[l].astype(BF16), gate1, norm2_g[l].reshape(1, D), scale2, shift2, w_r, b_r,
            after, S, tm)

        slot, tile_group, tile_valid, tile_token = _moe_routing(route, cnt, N)
        xs = _dispatch_call(slot, h2x, jnp.zeros((tile_token.shape[0], D + LANES), F32), tm)
        y = _expert_call(tile_group, tile_valid, xs, tile_token,
                         w1[l].astype(BF16), w3[l].astype(BF16), w2[l].astype(BF16), expand, N)
        xf = _residual_call(x1, gate2, y, S, tm)

    return xf.reshape(B, S, D)
```

```python
import functools
import math

import jax
import jax.numpy as jnp
from jax import lax
from jax.experimental import pallas as pl
from jax.experimental.pallas import tpu as pltpu

F32 = jnp.float32
BF16 = jnp.bfloat16
I32 = jnp.int32

HEAD_DIM = 64
HEADS = 4
GROUP_W = HEADS * HEAD_DIM
DIFF_DIM = HEAD_DIM // 2
IDX_HEADS = 8
IDX_DIM = 32
TOPK_MAX = 256
N_GROUPS = 4
EXPERTS_PER_GROUP = 8
N_EXPERTS = N_GROUPS * EXPERTS_PER_GROUP
EXPERT_FF = 256
RMS_EPS = 1e-6
NEG_INF = -1e30
INT_MIN = -(2 ** 31)
LOG2E = math.log2(math.e)

LANES = 128
SUBLANES = 8
ATTN_BLOCK = 256
VMEM_LIMIT = 56 * 1024 * 1024

MISC_IK = 0
MISC_IW = IDX_DIM
MISC_AF = IDX_DIM + IDX_HEADS


def _params(*sem):
    return pltpu.CompilerParams(dimension_semantics=sem, vmem_limit_bytes=VMEM_LIMIT)


def _log_sigmoid(z):
    return jnp.minimum(z, 0.0) - jnp.log1p(jnp.exp(-jnp.abs(z)))


def _log2_sigmoid(z2):
    return jnp.minimum(z2, 0.0) - jnp.log2(1.0 + jnp.exp2(-jnp.abs(z2)))


def _split_bf16(x, parts):
    out = []
    rem = x
    for _ in range(parts):
        p = rem.astype(BF16)
        out.append(p)
        rem = rem - p.astype(F32)
    return out


def _dot(a, b):
    return jnp.dot(a, b, preferred_element_type=F32)


def _ada_kernel(c_ref, w_ref, b_ref, o_ref):
    c = c_ref[...]
    ca = c * jax.nn.sigmoid(c)
    o_ref[...] = jnp.dot(ca, w_ref[...], precision=lax.Precision.HIGHEST,
                         preferred_element_type=F32) + b_ref[...]


def _ada_call(c, ada_w, ada_b):
    L, D, E = ada_w.shape
    B = c.shape[0]
    tn = 1536
    return pl.pallas_call(
        _ada_kernel,
        grid=(L, E // tn),
        in_specs=[pl.BlockSpec((B, D), lambda l, j: (0, 0)),
                  pl.BlockSpec((None, D, tn), lambda l, j: (l, 0, j)),
                  pl.BlockSpec((None, 1, tn), lambda l, j: (l, 0, j))],
        out_specs=pl.BlockSpec((None, B, tn), lambda l, j: (l, 0, j)),
        out_shape=jax.ShapeDtypeStruct((L, B, E), F32),
        compiler_params=_params("arbitrary", "arbitrary"),
        name="ada_mod",
    )(c, ada_w, ada_b.reshape(L, 1, E))


SOURCES = ("a_q", "a_k", "a_v", "b_q", "b_k", "b_v", "c_q", "c_k", "c_v", "d_q", "i_q", "d_kv")
SRC_NORM = {"a_q": ("n64", 0), "a_k": ("n64", 1), "c_q": ("n32", 2), "c_k": ("n32", 3),
            "d_q": ("n64", 4), "d_kv": ("n64", 5)}
SRC_SCALE = {"b_q": HEAD_DIM ** -0.5 * LOG2E, "i_q": IDX_DIM ** -0.5}
HW = HEADS * LANES
OUTPUTS = (
    ("a_qT", "a_q", HW, "cols", False), ("a_kp", "a_k", HW, "rows", False),
    ("a_vxT", "a_v", HW, "cols", True),
    ("b_qT", "b_q", HW, "cols", False), ("b_kp", "b_k", HW, "rows", False),
    ("b_vT", "b_v", GROUP_W, "cols", False),
    ("c_qT", "c_q", 2 * HW, "cols", False), ("c_kp", "c_k", HW, "rows", False),
    ("c_vxT", "c_v", HW, "cols", True),
    ("d_qT", "d_q", HW, "cols", False), ("i_qT", "i_q", IDX_HEADS * LANES, "cols", False),
    ("d_kp", "d_kv", LANES, "rows", False), ("d_vxT", "d_kv", LANES, "cols", True),
    ("i_kp", "misc", LANES, "rows", False),
)
N_MAIN = len(SOURCES) * GROUP_W

IN_SPLITS = (GROUP_W, GROUP_W, GROUP_W, HEADS, GROUP_W, GROUP_W, GROUP_W, GROUP_W, GROUP_W,
             GROUP_W, GROUP_W, HEAD_DIM, HEAD_DIM, IDX_HEADS * IDX_DIM, IDX_DIM, IDX_HEADS)
IN_NAMES = ("a_q", "a_k", "a_v", "a_f", "b_q", "b_k", "b_v", "c_q", "c_k", "c_v",
            "d_q", "d_k", "d_v", "i_q", "i_k", "i_w")
IN_OFFSETS = {n: (sum(IN_SPLITS[:i]), IN_SPLITS[i]) for i, n in enumerate(IN_NAMES)}
P_IN = sum(IN_SPLITS)
SECTION_PARTS = {name: (name,) for name in SOURCES if name != "d_kv"}
SECTION_PARTS["d_kv"] = ("d_k", "d_v")
SECTION_PARTS["misc"] = ("i_k", "i_w", "a_f")
REALIGN_ROWS = 256


def _realign_weights(w_ref, w_sc):
    d_model = w_ref.shape[0]
    for r0 in range(0, d_model, REALIGN_ROWS):
        rows = slice(r0, r0 + REALIGN_ROWS)
        for i, name in enumerate(SOURCES + ("misc",)):
            width = LANES if name == "misc" else GROUP_W
            pieces, used = [], 0
            for part in SECTION_PARTS[name]:
                off, w = IN_OFFSETS[part]
                base = off // LANES * LANES
                end = min(-(-(off + w) // LANES) * LANES, P_IN)
                window = w_ref[rows, base:end]
                pieces.append(window[:, off - base:off - base + w])
                used += w
            if used < width:
                pieces.append(jnp.zeros((REALIGN_ROWS, width - used), BF16))
            block = pieces[0] if len(pieces) == 1 else jnp.concatenate(pieces, axis=1)
            w_sc[rows, i * GROUP_W:i * GROUP_W + width] = block


def _in_proj_kernel(*refs):
    (x_ref, scale_ref, shift_ref, g1_ref, w_in_ref, gains_ref, g64_ref, g32_ref) = refs[:8]
    place_refs = refs[8:8 + len(OUTPUTS)]
    out_refs = refs[8 + len(OUTPUTS):-1]
    w_ref = refs[-1]

    @pl.when(pl.program_id(0) == 0)
    def _():
        _realign_weights(w_in_ref, w_ref)

    x = x_ref[...]
    ms = jnp.mean(x * x, axis=-1, keepdims=True)
    h = x * lax.rsqrt(ms + RMS_EPS) * g1_ref[...]
    h = h * (1.0 + scale_ref[...]) + shift_ref[...]
    hb = h.astype(BF16)

    wm = w_ref[:, N_MAIN:N_MAIN + LANES]
    h_hi, h_lo = _split_bf16(h, 2)
    misc = _dot(h_hi, wm) + _dot(h_lo, wm)
    out_refs[len(OUTPUTS)][...] = misc
    out_refs[len(OUTPUTS) + 1][...] = misc.T

    compact = {"misc": misc.astype(BF16)}
    for i, name in enumerate(SOURCES):
        sec = _dot(hb, w_ref[:, i * GROUP_W:(i + 1) * GROUP_W])
        if name in SRC_NORM:
            kind, r = SRC_NORM[name]
            gmat = g64_ref if kind == "n64" else g32_ref
            msq = _dot((sec * sec).astype(BF16), gmat[...])
            fac = lax.rsqrt(msq + RMS_EPS)
            if name == "d_kv":
                lane = lax.broadcasted_iota(I32, sec.shape, 1)
                fac = jnp.where(lane < HEAD_DIM, fac, 1.0)
            sec = sec * fac * gains_ref[r:r + 1, :]
        elif name in SRC_SCALE:
            sec = sec * SRC_SCALE[name]
        compact[name] = sec.astype(BF16)

    for (name, src, width, orient, ones), p_ref, o_ref in zip(OUTPUTS, place_refs, out_refs):
        if orient == "rows":
            o_ref[...] = _dot(compact[src], p_ref[...]).astype(BF16)
        else:
            res = lax.dot_general(p_ref[...], compact[src], (((1,), (1,)), ((), ())),
                                  preferred_element_type=F32)
            if ones:
                r = lax.broadcasted_iota(I32, res.shape, 0)
                res = jnp.where((r & (LANES - 1)) >= HEAD_DIM, 1.0, res)
            o_ref[...] = res.astype(BF16)


def _in_proj_call(xf, scale1, shift1, g1, w_all, gains, g64, g32, places, batch, seq, tm):
    N, D = xf.shape
    per_b = seq // tm
    row = lambda i: (i, 0)
    full = lambda i: (0, 0)
    col4 = lambda i: (i // per_b, i % per_b, 0, 0)
    out_shape, out_specs = [], []
    for name, src, width, orient, ones in OUTPUTS:
        if orient == "rows":
            out_shape.append(jax.ShapeDtypeStruct((N, width), BF16))
            out_specs.append(pl.BlockSpec((tm, width), row))
        else:
            out_shape.append(jax.ShapeDtypeStruct((batch, per_b, width, tm), BF16))
            out_specs.append(pl.BlockSpec((None, None, width, tm), col4))
    out_shape += [jax.ShapeDtypeStruct((N, LANES), F32),
                  jax.ShapeDtypeStruct((batch, per_b, LANES, tm), F32)]
    out_specs += [pl.BlockSpec((tm, LANES), row), pl.BlockSpec((None, None, LANES, tm), col4)]
    return pl.pallas_call(
        _in_proj_kernel,
        grid=(N // tm,),
        in_specs=([pl.BlockSpec((tm, D), row),
                   pl.BlockSpec((None, 1, D), lambda i: (i // per_b, 0, 0)),
                   pl.BlockSpec((None, 1, D), lambda i: (i // per_b, 0, 0)),
                   pl.BlockSpec((1, D), full),
                   pl.BlockSpec(w_all.shape, full),
                   pl.BlockSpec(gains.shape, full),
                   pl.BlockSpec(g64.shape, full),
                   pl.BlockSpec(g32.shape, full)]
                  + [pl.BlockSpec(p.shape, full) for p in places]),
        out_specs=out_specs,
        out_shape=out_shape,
        scratch_shapes=[pltpu.VMEM((D, N_MAIN + LANES), BF16)],
        compiler_params=_params("arbitrary"),
        name="in_proj",
    )(xf, scale1, shift1, g1, w_all, gains, g64, g32, *places)


def _cum_kernel(misc_ref, bf_ref, tri_ref, sel_ref, o_ref, *, blk):
    seq = misc_ref.shape[0]
    carry = jnp.zeros((1, LANES), F32)
    tri = tri_ref[...]
    for j in range(seq // blk):
        rows = slice(j * blk, (j + 1) * blk)
        lf = _log_sigmoid(misc_ref[rows, :] + bf_ref[...])
        c = carry
        for part in _split_bf16(lf, 3):
            c = c + _dot(tri, part)
        carry = c[blk - 1:blk, :]
        parts = _split_bf16(c, 3)
        for h in range(HEADS):
            rep = _dot(parts[0], sel_ref[h]) + _dot(parts[1], sel_ref[h]) + _dot(parts[2], sel_ref[h])
            o_ref[h, rows, :] = rep * LOG2E


def _cum_call(misc, bf_row, tri, sel):
    B, S, _ = misc.shape
    blk = tri.shape[0]
    return pl.pallas_call(
        functools.partial(_cum_kernel, blk=blk),
        grid=(B,),
        in_specs=[pl.BlockSpec((None, S, LANES), lambda b: (b, 0, 0)),
                  pl.BlockSpec((1, LANES), lambda b: (0, 0)),
                  pl.BlockSpec((blk, blk), lambda b: (0, 0)),
                  pl.BlockSpec((HEADS, LANES, LANES), lambda b: (0, 0, 0))],
        out_specs=pl.BlockSpec((None, HEADS, S, LANES), lambda b: (b, 0, 0, 0)),
        out_shape=jax.ShapeDtypeStruct((B, HEADS, S, LANES), F32),
        compiler_params=_params("arbitrary"),
        name="forget_cumsum",
    )(misc, bf_row, tri, sel)


def _q_spec(width, t):
    return pl.BlockSpec((None, None, width, t), lambda b, i: (b, i, 0, 0))


def _seq_rows_spec(seq, width):
    return pl.BlockSpec((None, seq, width), lambda b, i: (b, 0, 0))


def _seq_cols_spec(nb, width, t):
    return pl.BlockSpec((None, nb, width, t), lambda b, i: (b, 0, 0, 0))


def _const_spec(shape):
    return pl.BlockSpec(shape, lambda b, i: (0,) * len(shape))


def _key_query_iota(t):
    key = lax.broadcasted_iota(I32, (t, t), 0)
    query = lax.broadcasted_iota(I32, (t, t), 1)
    return key, query


def _lane_repeat(x, t):
    return jnp.concatenate([x] * (t // LANES), axis=1)


def _softmax_steps(scores, values, m_refs, acc_refs):
    probs, alphas = [], []
    for s, m_ref in zip(scores, m_refs):
        m_old = m_ref[...]
        m_new = jnp.maximum(m_old, jnp.max(s, axis=0, keepdims=True))
        alphas.append(jnp.exp2(m_old - m_new))
        probs.append(jnp.exp2(s - m_new).astype(BF16))
        m_ref[...] = m_new
    for p, vx, alpha, acc_ref in zip(probs, values, alphas, acc_refs):
        acc_ref[...] = alpha * acc_ref[...] + _dot(vx, p)


def _softmax_init(m_scs, acc_scs):
    for m_sc, acc_sc in zip(m_scs, acc_scs):
        m_sc[...] = jnp.full(m_sc.shape, NEG_INF, F32)
        acc_sc[...] = jnp.zeros(acc_sc.shape, F32)


def _softmax_result(acc):
    return acc[:HEAD_DIM] / acc[HEAD_DIM:]


def _alibi(slope, j, qi, t, nblk=1):
    key = lax.broadcasted_iota(I32, (nblk * t, LANES), 0)
    return (slope * LOG2E) * (key + (j - qi) * t).astype(F32)


def _sweep_earlier_blocks(qi, step):
    def body(i, carry):
        step(2 * i, 2)
        return carry

    lax.fori_loop(0, qi // 2, body, 0)

    @pl.when(qi % 2 == 1)
    def _():
        step(qi - 1, 1)


def _key_rows(j, nblk, t):
    return pl.ds(pl.multiple_of(j * t, t), nblk * t)


def _value_cols(vx_ref, j, nblk, rows):
    tiles = [vx_ref[j + b, rows, :] for b in range(nblk)]
    return tiles[0] if nblk == 1 else jnp.concatenate(tiles, axis=1)


def _attn_a_kernel(q_ref, k_ref, vx_ref, cum_ref, o_ref, *scratch, t):
    m_sc, acc_sc = scratch[:HEADS], scratch[HEADS:]
    qi = pl.program_id(1)
    key, query = _key_query_iota(t)
    causal = key <= query
    _softmax_init(m_sc, acc_sc)

    def step(j, nblk, masked=False):
        rows = _key_rows(j, nblk, t)
        heads = [slice(LANES * h, LANES * (h + 1)) for h in range(HEADS)]
        scores = [_dot(k_ref[rows, hs], q_ref[hs, :]) for hs in heads]
        for h in range(HEADS):
            s = scores[h] - _lane_repeat(cum_ref[h, rows, :], t)
            scores[h] = jnp.where(causal, s, -jnp.inf) if masked else s
        _softmax_steps(scores, [_value_cols(vx_ref, j, nblk, hs) for hs in heads], m_sc, acc_sc)

    step(qi, 1, masked=True)
    _sweep_earlier_blocks(qi, step)
    o_ref[...] = jnp.concatenate([_softmax_result(acc_sc[h][...]) for h in range(HEADS)], axis=0).T


def _attn_scratch(n, t, rows=LANES):
    return [pltpu.VMEM((1, t), F32)] * n + [pltpu.VMEM((rows, t), F32)] * n


def _attn_a_call(qT, kp, vxT, cumrep):
    B, nb, _, t = qT.shape
    S = nb * t
    return pl.pallas_call(
        functools.partial(_attn_a_kernel, t=t),
        grid=(B, nb),
        in_specs=[_q_spec(HW, t), _seq_rows_spec(S, HW), _seq_cols_spec(nb, HW, t),
                  pl.BlockSpec((None, HEADS, S, LANES), lambda b, i: (b, 0, 0, 0))],
        out_specs=pl.BlockSpec((None, t, GROUP_W), lambda b, i: (b, i, 0)),
        out_shape=jax.ShapeDtypeStruct((B, S, GROUP_W), F32),
        scratch_shapes=_attn_scratch(HEADS, t),
        compiler_params=_params("arbitrary", "arbitrary"),
        name="attn_forget",
    )(qT, kp, vxT, cumrep)


def _attn_b_kernel(q_ref, k_ref, v_ref, after_ref, o_ref, *scratch, t):
    r_sc, acc_sc = scratch[:HEADS], scratch[HEADS:]
    qi = pl.program_id(1)
    key, query = _key_query_iota(t)
    strict = key < query
    for h in range(HEADS):
        r_sc[h][...] = jnp.zeros(r_sc[h].shape, F32)
        acc_sc[h][...] = jnp.zeros(acc_sc[h].shape, F32)

    def step(j, nblk, masked=False):
        after = after_ref[...]
        rows = _key_rows(j, nblk, t)
        heads = [slice(LANES * h, LANES * (h + 1)) for h in range(HEADS)]
        zs = [_dot(k_ref[rows, hs], q_ref[hs, :]) for hs in heads]
        lbs, splits, later_sums = [], [], []
        for h in range(HEADS):
            lb = _log2_sigmoid(zs[h])
            lm = lb - zs[h]
            if masked:
                lm = jnp.where(strict, lm, 0.0)
            blocks = [lm[b * t:(b + 1) * t] for b in range(nblk)]
            splits.append([_split_bf16(blk, 2) for blk in blocks])
            sums = [jnp.sum(blk, axis=0, keepdims=True) for blk in blocks]
            r_old = r_sc[h][...]
            total = sums[0]
            for s_ in sums[1:]:
                total = total + s_
            r_sc[h][...] = r_old + total
            lbs.append(lb + r_old)
            later, run = [], None
            for b in reversed(range(nblk)):
                later.append(run)
                run = sums[b] if run is None else run + sums[b]
            later_sums.append(later[::-1])
        suffixes = [[_dot(after, hi) + _dot(after, lo) for hi, lo in splits[h]]
                    for h in range(HEADS)]
        ws = []
        for h in range(HEADS):
            parts = [suffixes[h][b] if later_sums[h][b] is None else suffixes[h][b] + later_sums[h][b]
                     for b in range(nblk)]
            suffix = parts[0] if nblk == 1 else jnp.concatenate(parts, axis=0)
            w = jnp.exp2(lbs[h] + suffix)
            if masked:
                w = jnp.where(strict, w, 0.0)
            ws.append(w.astype(BF16))
        for h in range(HEADS):
            acc_sc[h][...] += _dot(_value_cols(v_ref, j, nblk, slice(HEAD_DIM * h, HEAD_DIM * (h + 1))),
                                   ws[h])

    step(qi, 1, masked=True)

    @pl.when(qi % 2 == 1)
    def _():
        step(qi - 1, 1)

    pairs = qi // 2

    def body(i, carry):
        step(2 * (pairs - 1 - i), 2)
        return carry

    lax.fori_loop(0, pairs, body, 0)
    o_ref[...] = jnp.concatenate([acc_sc[h][...] for h in range(HEADS)], axis=0).T


def _attn_b_call(qT, kp, vT, after):
    B, nb, _, t = qT.shape
    S = nb * t
    return pl.pallas_call(
        functools.partial(_attn_b_kernel, t=t),
        grid=(B, nb),
        in_specs=[_q_spec(HW, t), _seq_rows_spec(S, HW), _seq_cols_spec(nb, GROUP_W, t),
                  _const_spec((t, t))],
        out_specs=pl.BlockSpec((None, t, GROUP_W), lambda b, i: (b, i, 0)),
        out_shape=jax.ShapeDtypeStruct((B, S, GROUP_W), F32),
        scratch_shapes=_attn_scratch(HEADS, t, HEAD_DIM),
        compiler_params=_params("arbitrary", "arbitrary"),
        name="attn_stick",
    )(qT, kp, vT, after)


def _attn_c_kernel(q_ref, k_ref, vx_ref, lamv_ref, subg_ref, o_ref, *scratch,
                   t, slopes, lambda_init):
    m_sc, acc_sc = scratch[:2 * HEADS], scratch[2 * HEADS:]
    qi = pl.program_id(1)
    key, query = _key_query_iota(t)
    causal = key <= query
    _softmax_init(m_sc, acc_sc)

    def step(j, nblk, masked=False):
        rows = _key_rows(j, nblk, t)
        heads = [slice(LANES * h, LANES * (h + 1)) for h in range(HEADS)]
        scores = [_dot(k_ref[rows, heads[g // 2]], q_ref[LANES * g:LANES * (g + 1), :])
                  for g in range(2 * HEADS)]
        for g in range(2 * HEADS):
            s = scores[g] + _lane_repeat(_alibi(slopes[g // 2], j, qi, t, nblk), t)
            scores[g] = jnp.where(causal, s, -jnp.inf) if masked else s
        _softmax_steps(scores, [_value_cols(vx_ref, j, nblk, heads[g // 2])
                                for g in range(2 * HEADS)], m_sc, acc_sc)

    step(qi, 1, masked=True)
    _sweep_earlier_blocks(qi, step)

    lv = lamv_ref[...]
    lam = (jnp.exp(jnp.sum(lv[0:1] * lv[1:2], axis=-1, keepdims=True))
           - jnp.exp(jnp.sum(lv[2:3] * lv[3:4], axis=-1, keepdims=True)) + lambda_init)
    outs = []
    for h in range(HEADS):
        o = (_softmax_result(acc_sc[2 * h][...])
             - lam * _softmax_result(acc_sc[2 * h + 1][...]))
        ms = jnp.mean(o * o, axis=0, keepdims=True)
        outs.append(o * lax.rsqrt(ms + RMS_EPS) * subg_ref[...] * (1.0 - lambda_init))
    o_ref[...] = jnp.concatenate(outs, axis=0).T


def _attn_c_call(qT, kp, vxT, lamv, subg_col, slopes, lambda_init):
    B, nb, _, t = qT.shape
    S = nb * t
    return pl.pallas_call(
        functools.partial(_attn_c_kernel, t=t, slopes=slopes, lambda_init=lambda_init),
        grid=(B, nb),
        in_specs=[_q_spec(2 * HW, t), _seq_rows_spec(S, HW), _seq_cols_spec(nb, HW, t),
                  _const_spec(lamv.shape), _const_spec(subg_col.shape)],
        out_specs=pl.BlockSpec((None, t, GROUP_W), lambda b, i: (b, i, 0)),
        out_shape=jax.ShapeDtypeStruct((B, S, GROUP_W), F32),
        scratch_shapes=_attn_scratch(2 * HEADS, t),
        compiler_params=_params("arbitrary", "arbitrary"),
        name="attn_diff",
    )(qT, kp, vxT, lamv, subg_col)


def _fold_rows(x):
    return jnp.sum(x.reshape(x.shape[0] // SUBLANES, SUBLANES, x.shape[1]), axis=0)


def _attn_d_kernel(q_ref, iq_ref, iw_ref, dk_ref, dvx_ref, ik_ref, before_ref, o_ref,
                   keys_sc, tau_sc, *scratch, t, topk, slopes):
    m_sc, acc_sc = scratch[:HEADS], scratch[HEADS:]
    qi = pl.program_id(1)
    key, query = _key_query_iota(t)
    causal = key <= query

    w = iw_ref[...] * IDX_HEADS ** -0.5

    def index_step(j, nblk, masked=False):
        ikb = ik_ref[_key_rows(j, nblk, t), :]
        zs = [_dot(ikb, iq_ref[LANES * hh:LANES * (hh + 1), :]) for hh in range(IDX_HEADS)]
        sc = w[0:1, :] * jnp.maximum(zs[0], 0.0)
        for hh in range(1, IDX_HEADS):
            sc = sc + w[hh:hh + 1, :] * jnp.maximum(zs[hh], 0.0)
        sc = jnp.where(sc == 0.0, 0.0, sc)
        if masked:
            sc = jnp.where(causal, sc, NEG_INF)
        bits = pltpu.bitcast(sc, I32)
        keys = jnp.where(bits < 0, bits ^ 0x7FFFFFFF, bits)
        for b in range(nblk):
            keys_sc[j + b] = keys[b * t:(b + 1) * t]

    index_step(qi, 1, masked=True)
    _sweep_earlier_blocks(qi, index_step)

    qpos = qi * t + lax.broadcasted_iota(I32, (1, t), 1)
    kt = jnp.minimum(topk, qpos + 1).astype(F32)

    def count_ge(cand):
        def body(j, acc):
            return acc + _fold_rows((keys_sc[j] >= cand).astype(F32))
        acc = lax.fori_loop(0, qi + 1, body, jnp.zeros((SUBLANES, t), F32))
        return jnp.sum(acc, axis=0, keepdims=True)

    zero = jnp.zeros((1, t), I32)
    prefix = jnp.where(count_ge(zero) >= kt, zero, INT_MIN)

    def bit_body(i, prefix):
        cand = prefix + lax.shift_left(jnp.int32(1), 30 - i)
        return jnp.where(count_ge(cand) >= kt, cand, prefix)

    tau = lax.fori_loop(0, 31, bit_body, prefix)
    tau_sc[...] = tau
    excess = jnp.max(count_ge(tau) - kt)

    @pl.when(excess > 0.0)
    def _():
        need = kt - count_ge(tau + 1)

        def tie_body(j, seen):
            kj = keys_sc[j]
            eq = kj == tau
            eqb = eq.astype(BF16)
            earlier = _dot(before_ref[...], eqb) + seen
            keys_sc[j] = jnp.where(eq & (earlier >= need), INT_MIN, kj)
            return seen + jnp.sum(eqb.astype(F32), axis=0, keepdims=True)

        lax.fori_loop(0, qi + 1, tie_body, jnp.zeros((1, t), F32))

    _softmax_init(m_sc, acc_sc)

    def step(j, nblk):
        keys = [keys_sc[j + b] for b in range(nblk)]
        sel = (keys[0] if nblk == 1 else jnp.concatenate(keys, axis=0)) >= tau_sc[...]
        kb = dk_ref[_key_rows(j, nblk, t), :]
        vx = _value_cols(dvx_ref, j, nblk, slice(None))
        scores = [_dot(kb, q_ref[LANES * h:LANES * (h + 1), :]) for h in range(HEADS)]
        for h in range(HEADS):
            bias = _lane_repeat(_alibi(slopes[h], j, qi, t, nblk), t)
            scores[h] = jnp.where(sel, scores[h] + bias, -jnp.inf)
        _softmax_steps(scores, [vx] * HEADS, m_sc, acc_sc)

    _sweep_earlier_blocks(qi + 1, step)
    o_ref[...] = jnp.concatenate([_softmax_result(acc_sc[h][...]) for h in range(HEADS)], axis=0).T


def _attn_d_call(qT, iqT, miscT, dkp, dvxT, ikp, before, topk, slopes):
    B, nb, _, t = qT.shape
    S = nb * t
    iw_block = MISC_IW // IDX_HEADS
    return pl.pallas_call(
        functools.partial(_attn_d_kernel, t=t, topk=topk, slopes=slopes),
        grid=(B, nb),
        in_specs=[_q_spec(HW, t), _q_spec(IDX_HEADS * LANES, t),
                  pl.BlockSpec((None, None, IDX_HEADS, t), lambda b, i: (b, i, iw_block, 0)),
                  _seq_rows_spec(S, LANES), _seq_cols_spec(nb, LANES, t), _seq_rows_spec(S, LANES),
                  _const_spec((t, t))],
        out_specs=pl.BlockSpec((None, t, GROUP_W), lambda b, i: (b, i, 0)),
        out_shape=jax.ShapeDtypeStruct((B, S, GROUP_W), F32),
        scratch_shapes=([pltpu.VMEM((nb, t, t), I32), pltpu.VMEM((1, t), I32)]
                        + _attn_scratch(HEADS, t)),
        compiler_params=_params("arbitrary", "arbitrary"),
        name="attn_sparse",
    )(qT, iqT, miscT, dkp, dvxT, ikp, before)


def _out_proj_kernel(x_ref, oa_ref, ob_ref, oc_ref, od_ref, beta_ref, wo_ref, gate_ref,
                     g2_ref, scale_ref, shift_ref, wr_ref, br_ref, earlier_ref,
                     x1_ref, h2x_ref, route_ref, cnt_ref, cnt_sc):
    d_model = x_ref.shape[1]
    acc = None
    for i, o_ref in enumerate((oa_ref, ob_ref, oc_ref, od_ref)):
        sl = slice(GROUP_W * i, GROUP_W * (i + 1))
        mix = (o_ref[...] * beta_ref[:, sl]).astype(BF16)
        part = jnp.dot(mix, wo_ref[sl, :], preferred_element_type=F32)
        acc = part if acc is None else acc + part
    x1 = x_ref[...] + gate_ref[...] * acc
    x1_ref[...] = x1
    ms = jnp.mean(x1 * x1, axis=-1, keepdims=True)
    h2 = x1 * lax.rsqrt(ms + RMS_EPS) * g2_ref[...]
    h2 = h2 * (1.0 + scale_ref[...]) + shift_ref[...]
    h2x_ref[:, :d_model] = h2

    h_hi, h_lo = _split_bf16(h2, 2)
    w_hi, w_lo = _split_bf16(wr_ref[...], 2)
    logits = _dot(h_hi, w_hi) + _dot(h_hi, w_lo) + _dot(h_lo, w_hi) + br_ref[...]
    lt = logits.T
    tm = lt.shape[1]
    g = lt[0:N_GROUPS]
    gmax = jnp.max(g, axis=0, keepdims=True)
    gi = lax.broadcasted_iota(I32, g.shape, 0)
    gidx = jnp.min(jnp.where(g == gmax, gi, N_GROUPS), axis=0, keepdims=True)
    g_prob = 1.0 / jnp.sum(jnp.exp(g - gmax), axis=0, keepdims=True)
    e_sel = jnp.zeros((EXPERTS_PER_GROUP, tm), F32)
    for gg in range(N_GROUPS):
        lo = N_GROUPS + EXPERTS_PER_GROUP * gg
        e_sel = e_sel + jnp.where(gidx == gg, lt[lo:lo + EXPERTS_PER_GROUP], 0.0)
    ei = lax.broadcasted_iota(I32, e_sel.shape, 0)
    v1 = jnp.max(e_sel, axis=0, keepdims=True)
    i1 = jnp.min(jnp.where(e_sel == v1, ei, EXPERTS_PER_GROUP), axis=0, keepdims=True)
    rest = jnp.where(ei == i1, -jnp.inf, e_sel)
    v2 = jnp.max(rest, axis=0, keepdims=True)
    i2 = jnp.min(jnp.where(rest == v2, ei, EXPERTS_PER_GROUP), axis=0, keepdims=True)
    e2 = jnp.exp(v2 - v1)
    w1 = g_prob / (1.0 + e2)
    w2 = g_prob * e2 / (1.0 + e2)
    in_group = jnp.where(ei == i1, w1, 0.0) + jnp.where(ei == i2, w2, 0.0)
    cw = jnp.concatenate([in_group, jnp.zeros((LANES - EXPERTS_PER_GROUP, tm), F32)], axis=0)
    h2x_ref[:, d_model:] = cw.T

    @pl.when(pl.program_id(0) == 0)
    def _():
        cnt_sc[...] = jnp.zeros(cnt_sc.shape, F32)

    rows = lax.broadcasted_iota(I32, (SUBLANES, tm), 0)
    onehot = (rows == gidx).astype(F32)
    seen = _dot(onehot.astype(BF16), earlier_ref[...]) + cnt_sc[...]
    rank = jnp.sum(onehot * seen, axis=0, keepdims=True)
    route_ref[...] = jnp.concatenate(
        [gidx, rank.astype(I32), jnp.zeros((SUBLANES - 2, tm), I32)], axis=0)
    cnt_sc[...] = cnt_sc[...] + jnp.sum(onehot, axis=1, keepdims=True)
    cnt_ref[...] = cnt_sc[:, :LANES]


def _out_proj_call(xf, outs, beta, w_out, gate1, g2, scale2, shift2, w_r, b_r, earlier, seq, tm):
    N, D = xf.shape
    per_b = seq // tm
    row = lambda i: (i, 0)
    full = lambda i: (0, 0)
    per_batch = pl.BlockSpec((None, 1, D), lambda i: (i // per_b, 0, 0))
    return pl.pallas_call(
        _out_proj_kernel,
        grid=(N // tm,),
        in_specs=([pl.BlockSpec((tm, D), row)] + [pl.BlockSpec((tm, GROUP_W), row)] * 4
                  + [pl.BlockSpec((1, D), full), pl.BlockSpec((D, D), full), per_batch,
                     pl.BlockSpec((1, D), full), per_batch, per_batch,
                     pl.BlockSpec((D, LANES), full), pl.BlockSpec((1, LANES), full),
                     pl.BlockSpec((tm, tm), full)]),
        out_specs=[pl.BlockSpec((tm, D), row), pl.BlockSpec((tm, D + LANES), row),
                   pl.BlockSpec((SUBLANES, tm), lambda i: (0, i)),
                   pl.BlockSpec((SUBLANES, LANES), full)],
        out_shape=[jax.ShapeDtypeStruct((N, D), F32), jax.ShapeDtypeStruct((N, D + LANES), F32),
                   jax.ShapeDtypeStruct((SUBLANES, N), I32),
                   jax.ShapeDtypeStruct((SUBLANES, LANES), F32)],
        scratch_shapes=[pltpu.VMEM((SUBLANES, tm), F32)],
        compiler_params=_params("arbitrary"),
        name="out_proj_router",
    )(xf, *outs, beta, w_out, gate1, g2, scale2, shift2, w_r, b_r, earlier)


MOE_TILE = 256
DMA_UNROLL = 8


def _row_copy(src_ref, src_row, dst_ref, dst_row, sem):
    return pltpu.make_async_copy(src_ref.at[pl.ds(src_row, 1), :],
                                 dst_ref.at[pl.ds(dst_row, 1), :], sem)


def _dispatch_kernel(slot_ref, h2x_ref, zeros_ref, xs_ref, sem):
    del zeros_ref
    rows = h2x_ref.shape[0]

    def issue(r, carry):
        _row_copy(h2x_ref, r, xs_ref, slot_ref[0, r], sem).start()
        return carry

    lax.fori_loop(0, rows, issue, 0, unroll=DMA_UNROLL)

    def drain(r, carry):
        _row_copy(h2x_ref, r, xs_ref, slot_ref[0, r], sem).wait()
        return carry

    lax.fori_loop(0, rows, drain, 0, unroll=DMA_UNROLL)


def _dispatch_call(slot, h2x, xs_zero, tm):
    N, DX = h2x.shape
    return pl.pallas_call(
        _dispatch_kernel,
        grid=(N // tm,),
        in_specs=[pl.BlockSpec((None, 1, tm), lambda i: (i, 0, 0), memory_space=pltpu.SMEM),
                  pl.BlockSpec((tm, DX), lambda i: (i, 0)),
                  pl.BlockSpec(memory_space=pl.ANY)],
        out_specs=pl.BlockSpec(memory_space=pl.ANY),
        out_shape=jax.ShapeDtypeStruct(xs_zero.shape, F32),
        scratch_shapes=[pltpu.SemaphoreType.DMA(())],
        input_output_aliases={2: 0},
        compiler_params=_params("arbitrary"),
        name="moe_dispatch",
    )(slot.reshape(N // tm, 1, tm), h2x, xs_zero)


def _expert_kernel(group_ref, valid_ref, xs_ref, token_ref, w1_ref, w3_ref, w2_ref, expand_ref,
                   y_ref, y_sc, sem):
    del group_ref
    i = pl.program_id(0)
    valid = valid_ref[i]
    d_model = y_sc.shape[1]
    ff = w1_ref.shape[2]

    @pl.when(valid > 0)
    def _():
        x = xs_ref[:, :d_model].astype(BF16)
        cw_hi, cw_lo = _split_bf16(xs_ref[:, d_model:], 2)
        cwx = _dot(cw_hi, expand_ref[...]) + _dot(cw_lo, expand_ref[...])
        acc = None
        for e in range(EXPERTS_PER_GROUP):
            a = _dot(x, w1_ref[e])
            b = _dot(x, w3_ref[e])
            hid = a * jax.nn.sigmoid(a) * b * cwx[:, ff * e:ff * (e + 1)]
            part = _dot(hid.astype(BF16), w2_ref[e])
            acc = part if acc is None else acc + part
        y_sc[...] = acc

        def issue(r, carry):
            _row_copy(y_sc, r, y_ref, token_ref[0, r], sem).start()
            return carry

        def drain(r, carry):
            _row_copy(y_sc, r, y_ref, token_ref[0, r], sem).wait()
            return carry

        tile = y_sc.shape[0]

        @pl.when(valid == tile)
        def _():
            lax.fori_loop(0, tile, issue, 0, unroll=DMA_UNROLL)
            lax.fori_loop(0, tile, drain, 0, unroll=DMA_UNROLL)

        @pl.when(valid < tile)
        def _():
            lax.fori_loop(0, valid, issue, 0)
            lax.fori_loop(0, valid, drain, 0)


def _expert_call(tile_group, tile_valid, xs, tile_token, w1, w3, w2, expand, n_tokens):
    P, DX = xs.shape
    _, _, D, FF = w1.shape
    tr = MOE_TILE
    n_tiles = P // tr
    grid_spec = pltpu.PrefetchScalarGridSpec(
        num_scalar_prefetch=2,
        grid=(n_tiles,),
        in_specs=[pl.BlockSpec((tr, DX), lambda i, g, v: (i, 0)),
                  pl.BlockSpec((None, 1, tr), lambda i, g, v: (i, 0, 0), memory_space=pltpu.SMEM),
                  pl.BlockSpec((None, EXPERTS_PER_GROUP, D, FF), lambda i, g, v: (g[i], 0, 0, 0)),
                  pl.BlockSpec((None, EXPERTS_PER_GROUP, D, FF), lambda i, g, v: (g[i], 0, 0, 0)),
                  pl.BlockSpec((None, EXPERTS_PER_GROUP, FF, D), lambda i, g, v: (g[i], 0, 0, 0)),
                  pl.BlockSpec(expand.shape, lambda i, g, v: (0, 0))],
        out_specs=pl.BlockSpec(memory_space=pl.ANY),
        scratch_shapes=[pltpu.VMEM((tr, D), F32), pltpu.SemaphoreType.DMA(())],
    )
    return pl.pallas_call(
        _expert_kernel,
        grid_spec=grid_spec,
        out_shape=jax.ShapeDtypeStruct((n_tokens, D), F32),
        compiler_params=_params("arbitrary"),
        name="moe_experts",
    )(tile_group, tile_valid, xs, tile_token.reshape(n_tiles, 1, tr), w1, w3, w2, expand)


def _residual_kernel(x1_ref, gate_ref, y_ref, o_ref):
    o_ref[...] = x1_ref[...] + gate_ref[...] * y_ref[...]


def _residual_call(x1, gate2, y, seq, tm):
    N, D = x1.shape
    per_b = seq // tm
    row = pl.BlockSpec((tm, D), lambda i: (i, 0))
    return pl.pallas_call(
        _residual_kernel,
        grid=(N // tm,),
        in_specs=[row, pl.BlockSpec((None, 1, D), lambda i: (i // per_b, 0, 0)), row],
        out_specs=row,
        out_shape=jax.ShapeDtypeStruct((N, D), F32),
        compiler_params=_params("arbitrary"),
        name="moe_residual",
    )(x1, gate2, y)


def _moe_routing(route, cnt, n_tokens):
    tr = MOE_TILE
    n_tiles = n_tokens // tr + N_GROUPS
    counts = cnt[:N_GROUPS, 0].astype(I32)
    padded = (counts + tr - 1) // tr * tr
    ends = jnp.cumsum(padded)
    starts = ends - padded
    group, rank = route[0], route[1]
    slot = starts[group] + rank
    tile_start = jnp.arange(n_tiles, dtype=I32) * tr
    tile_group = jnp.minimum(jnp.sum((tile_start[:, None] >= ends[None, :]).astype(I32), axis=1),
                             N_GROUPS - 1)
    tile_valid = jnp.clip(starts[tile_group] + counts[tile_group] - tile_start, 0, tr)
    tile_valid = jnp.where(tile_start < ends[-1], tile_valid, 0)
    tile_token = jnp.zeros((n_tiles * tr,), I32).at[slot].set(jnp.arange(n_tokens, dtype=I32))
    return slot, tile_group, tile_valid, tile_token


def _block_diag_mean(width, group, valid_in_128=None):
    i = jnp.arange(width)
    same = (i[:, None] // group) == (i[None, :] // group)
    if valid_in_128 is not None:
        same = same & ((i[:, None] % LANES) < valid_in_128) & ((i[None, :] % LANES) < valid_in_128)
    return jnp.where(same, 1.0 / group, 0.0).astype(BF16)


def _placement(src_width, dst_width, pairs):
    src = jnp.array([p[0] for p in pairs], I32)
    dst = jnp.array([p[1] for p in pairs], I32)
    return jnp.zeros((src_width, dst_width), F32).at[src, dst].set(1.0).astype(BF16)


def _placements():
    per_head = [(HEAD_DIM * h + d, LANES * h + d) for h in range(HEADS) for d in range(HEAD_DIM)]
    diff_q = [(HEAD_DIM * h + DIFF_DIM * c + d, LANES * (2 * h + c) + DIFF_DIM * c + d)
              for h in range(HEADS) for c in range(2) for d in range(DIFF_DIM)]
    idx_q = [(IDX_DIM * hh + d, LANES * hh + d) for hh in range(IDX_HEADS) for d in range(IDX_DIM)]
    ident = [(d, d) for d in range(GROUP_W)]
    first64 = [(d, d) for d in range(HEAD_DIM)]
    second64 = [(HEAD_DIM + d, d) for d in range(HEAD_DIM)]
    idx_k = [(MISC_IK + d, d) for d in range(IDX_DIM)]
    table = {
        "a_qT": (GROUP_W, per_head), "a_kp": (GROUP_W, per_head), "a_vxT": (GROUP_W, per_head),
        "b_qT": (GROUP_W, per_head), "b_kp": (GROUP_W, per_head), "b_vT": (GROUP_W, ident),
        "c_qT": (GROUP_W, diff_q), "c_kp": (GROUP_W, per_head), "c_vxT": (GROUP_W, per_head),
        "d_qT": (GROUP_W, per_head), "i_qT": (GROUP_W, idx_q),
        "d_kp": (GROUP_W, first64), "d_vxT": (GROUP_W, second64), "i_kp": (LANES, idx_k),
    }
    out = []
    for name, src, width, orient, ones in OUTPUTS:
        src_width, pairs = table[name]
        p = _placement(src_width, width, pairs)
        out.append(p if orient == "rows" else p.T)
    return out


def kernel(x, c, ada_w, ada_b, norm1_g, norm2_g, w_in, b_f, qn_a, kn_a, qn_c, kn_c,
           lam_q1, lam_k1, lam_q2, lam_k2, subln_g, qn_d, kn_d, mix_beta, w_out,
           w_group, b_group, w_expert, b_expert, w1, w3, w2):
    B, S, D = x.shape
    L = ada_w.shape[0]
    N = B * S
    topk = min(TOPK_MAX, S // 4)
    t = ATTN_BLOCK
    tm = t
    slopes = [2.0 ** (-8.0 * i / (2 * HEADS)) for i in range(1, 2 * HEADS + 1)]
    slopes_c, slopes_d = tuple(slopes[0::2]), tuple(slopes[1::2])

    idx_t = jnp.arange(t)
    after = (idx_t[None, :] > idx_t[:, None]).astype(BF16)
    before = (idx_t[None, :] < idx_t[:, None]).astype(BF16)
    upto = (idx_t[None, :] <= idx_t[:, None]).astype(BF16)
    g64 = _block_diag_mean(GROUP_W, HEAD_DIM)
    g32 = _block_diag_mean(GROUP_W, DIFF_DIM)
    places = _placements()
    lane = jnp.arange(LANES)
    cum_sel = jnp.stack([jnp.broadcast_to((lane == MISC_AF + h)[:, None], (LANES, LANES))
                         for h in range(HEADS)]).astype(BF16)
    ff_col = jnp.arange(EXPERTS_PER_GROUP * EXPERT_FF)
    expand = (lane[:, None] == ff_col[None, :] // EXPERT_FF).astype(BF16)

    mod = _ada_call(c, ada_w, ada_b)
    xf = x.reshape(N, D)

    for l in range(L):
        m6 = mod[l].reshape(B, 6, 1, D)
        shift1, scale1, gate1, shift2, scale2, gate2 = (m6[:, i] for i in range(6))

        w_all = w_in[l].astype(BF16)
        ones = jnp.ones((GROUP_W - HEAD_DIM,), F32)
        gains = jnp.stack([jnp.tile(qn_a[l], HEADS) * (HEAD_DIM ** -0.5 * LOG2E),
                           jnp.tile(kn_a[l], HEADS),
                           jnp.tile(qn_c[l], 2 * HEADS) * (DIFF_DIM ** -0.5 * LOG2E),
                           jnp.tile(kn_c[l], 2 * HEADS),
                           jnp.tile(qn_d[l], HEADS) * (HEAD_DIM ** -0.5 * LOG2E),
                           jnp.concatenate([kn_d[l], ones])]).astype(F32)
        gains = jnp.concatenate([gains, jnp.zeros((2, GROUP_W), F32)], axis=0)

        outs = _in_proj_call(xf, scale1, shift1, norm1_g[l].reshape(1, D), w_all, gains,
                             g64, g32, places, B, S, tm)
        sec = {name: o for (name, _, _, _, _), o in zip(OUTPUTS, outs)}
        for name, src, width, orient, ones_ in OUTPUTS:
            if orient == "rows":
                sec[name] = sec[name].reshape(B, S, width)
        misc = outs[len(OUTPUTS)].reshape(B, S, LANES)
        miscT = outs[len(OUTPUTS) + 1]

        bf_row = jnp.zeros((1, LANES), F32).at[0, MISC_AF:MISC_AF + HEADS].set(b_f[l].astype(F32))
        cumrep = _cum_call(misc, bf_row, upto, cum_sel)

        o_a = _attn_a_call(sec["a_qT"], sec["a_kp"], sec["a_vxT"], cumrep)
        o_b = _attn_b_call(sec["b_qT"], sec["b_kp"], sec["b_vT"], after)
        lambda_init = 0.8 - 0.6 * math.exp(-0.3 * l)
        lamv = jnp.stack([lam_q1[l], lam_k1[l], lam_q2[l], lam_k2[l]]).astype(F32)
        o_c = _attn_c_call(sec["c_qT"], sec["c_kp"], sec["c_vxT"], lamv,
                           subln_g[l].reshape(HEAD_DIM, 1).astype(F32), slopes_c, lambda_init)
        o_d = _attn_d_call(sec["d_qT"], sec["i_qT"], miscT, sec["d_kp"], sec["d_vxT"], sec["i_kp"],
                           before, topk, slopes_d)

        w_r = jnp.concatenate([w_group[l], w_expert[l],
                               jnp.zeros((D, LANES - N_GROUPS - N_EXPERTS), F32)], axis=1)
        b_r = jnp.concatenate([b_group[l], b_expert[l],
                               jnp.zeros((LANES - N_GROUPS - N_EXPERTS,), F32)]).reshape(1, LANES)
        x1, h2x, route, cnt = _out_proj_call(
            xf, [o.reshape(N, GROUP_W) for o in (o_a, o_b, o_c, o_d)], mix_beta[l].reshape(1, D),
            w_out[l].astype(BF16), gate1, norm2_g[l].reshape(1, D), scale2, shift2, w_r, b_r,
            after, S, tm)

        slot, tile_group, tile_valid, tile_token = _moe_routing(route, cnt, N)
        xs = _dispatch_call(slot, h2x, jnp.zeros((tile_token.shape[0], D + LANES), F32), tm)
        y = _expert_call(tile_group, tile_valid, xs, tile_token,
                         w1[l].astype(BF16), w3[l].astype(BF16), w2[l].astype(BF16), expand, N)
        xf = _residual_call(x1, gate2, y, S, tm)

    return xf.reshape(B, S, D)
```

```python
import functools
import math

import jax
import jax.numpy as jnp
from jax import lax
from jax.experimental import pallas as pl
from jax.experimental.pallas import tpu as pltpu

F32 = jnp.float32
BF16 = jnp.bfloat16
I32 = jnp.int32

HEAD_DIM = 64
HEADS = 4
GROUP_W = HEADS * HEAD_DIM
DIFF_DIM = HEAD_DIM // 2
IDX_HEADS = 8
IDX_DIM = 32
TOPK_MAX = 256
N_GROUPS = 4
EXPERTS_PER_GROUP = 8
N_EXPERTS = N_GROUPS * EXPERTS_PER_GROUP
EXPERT_FF = 256
RMS_EPS = 1e-6
NEG_INF = -1e30
INT_MIN = -(2 ** 31)
LOG2E = math.log2(math.e)

LANES = 128
SUBLANES = 8
ATTN_BLOCK = 256
VMEM_LIMIT = 56 * 1024 * 1024

MISC_IK = 0
MISC_IW = IDX_DIM
MISC_AF = IDX_DIM + IDX_HEADS


def _params(*sem):
    return pltpu.CompilerParams(dimension_semantics=sem, vmem_limit_bytes=VMEM_LIMIT)


def _log_sigmoid(z):
    return jnp.minimum(z, 0.0) - jnp.log1p(jnp.exp(-jnp.abs(z)))


def _log2_sigmoid(z2):
    return jnp.minimum(z2, 0.0) - jnp.log2(1.0 + jnp.exp2(-jnp.abs(z2)))


def _split_bf16(x, parts):
    out = []
    rem = x
    for _ in range(parts):
        p = rem.astype(BF16)
        out.append(p)
        rem = rem - p.astype(F32)
    return out


def _dot(a, b):
    return jnp.dot(a, b, preferred_element_type=F32)


def _ada_kernel(c_ref, w_ref, b_ref, o_ref):
    c = c_ref[...]
    ca = c * jax.nn.sigmoid(c)
    o_ref[...] = jnp.dot(ca, w_ref[...], precision=lax.Precision.HIGHEST,
                         preferred_element_type=F32) + b_ref[...]


def _ada_call(c, ada_w, ada_b):
    L, D, E = ada_w.shape
    B = c.shape[0]
    tn = 1536
    return pl.pallas_call(
        _ada_kernel,
        grid=(L, E // tn),
        in_specs=[pl.BlockSpec((B, D), lambda l, j: (0, 0)),
                  pl.BlockSpec((None, D, tn), lambda l, j: (l, 0, j)),
                  pl.BlockSpec((None, 1, tn), lambda l, j: (l, 0, j))],
        out_specs=pl.BlockSpec((None, B, tn), lambda l, j: (l, 0, j)),
        out_shape=jax.ShapeDtypeStruct((L, B, E), F32),
        compiler_params=_params("arbitrary", "arbitrary"),
        name="ada_mod",
    )(c, ada_w, ada_b.reshape(L, 1, E))


SOURCES = ("a_q", "a_k", "a_v", "b_q", "b_k", "b_v", "c_q", "c_k", "c_v", "d_q", "i_q", "d_kv")
SRC_NORM = {"a_q": ("n64", 0), "a_k": ("n64", 1), "c_q": ("n32", 2), "c_k": ("n32", 3),
            "d_q": ("n64", 4), "d_kv": ("n64", 5)}
SRC_SCALE = {"b_q": HEAD_DIM ** -0.5 * LOG2E, "i_q": IDX_DIM ** -0.5}
HW = HEADS * LANES
OUTPUTS = (
    ("a_qT", "a_q", HW, "cols", False), ("a_kp", "a_k", HW, "rows", False),
    ("a_vxT", "a_v", HW, "cols", True),
    ("b_qT", "b_q", HW, "cols", False), ("b_kp", "b_k", HW, "rows", False),
    ("b_vT", "b_v", GROUP_W, "cols", False),
    ("c_qT", "c_q", 2 * HW, "cols", False), ("c_kp", "c_k", HW, "rows", False),
    ("c_vxT", "c_v", HW, "cols", True),
    ("d_qT", "d_q", HW, "cols", False), ("i_qT", "i_q", IDX_HEADS * LANES, "cols", False),
    ("d_kp", "d_kv", LANES, "rows", False), ("d_vxT", "d_kv", LANES, "cols", True),
    ("i_kp", "misc", LANES, "rows", False),
)
N_MAIN = len(SOURCES) * GROUP_W

IN_SPLITS = (GROUP_W, GROUP_W, GROUP_W, HEADS, GROUP_W, GROUP_W, GROUP_W, GROUP_W, GROUP_W,
             GROUP_W, GROUP_W, HEAD_DIM, HEAD_DIM, IDX_HEADS * IDX_DIM, IDX_DIM, IDX_HEADS)
IN_NAMES = ("a_q", "a_k", "a_v", "a_f", "b_q", "b_k", "b_v", "c_q", "c_k", "c_v",
            "d_q", "d_k", "d_v", "i_q", "i_k", "i_w")
IN_OFFSETS = {n: (sum(IN_SPLITS[:i]), IN_SPLITS[i]) for i, n in enumerate(IN_NAMES)}
P_IN = sum(IN_SPLITS)
SECTION_PARTS = {name: (name,) for name in SOURCES if name != "d_kv"}
SECTION_PARTS["d_kv"] = ("d_k", "d_v")
SECTION_PARTS["misc"] = ("i_k", "i_w", "a_f")
REALIGN_ROWS = 256


def _realign_weights(w_ref, w_sc):
    d_model = w_ref.shape[0]
    for r0 in range(0, d_model, REALIGN_ROWS):
        rows = slice(r0, r0 + REALIGN_ROWS)
        for i, name in enumerate(SOURCES + ("misc",)):
            width = LANES if name == "misc" else GROUP_W
            pieces, used = [], 0
            for part in SECTION_PARTS[name]:
                off, w = IN_OFFSETS[part]
                base = off // LANES * LANES
                end = min(-(-(off + w) // LANES) * LANES, P_IN)
                window = w_ref[rows, base:end]
                pieces.append(window[:, off - base:off - base + w])
                used += w
            if used < width:
                pieces.append(jnp.zeros((REALIGN_ROWS, width - used), BF16))
            block = pieces[0] if len(pieces) == 1 else jnp.concatenate(pieces, axis=1)
            w_sc[rows, i * GROUP_W:i * GROUP_W + width] = block


def _in_proj_kernel(*refs):
    (x_ref, scale_ref, shift_ref, g1_ref, w_in_ref, gains_ref, g64_ref, g32_ref) = refs[:8]
    place_refs = refs[8:8 + len(OUTPUTS)]
    out_refs = refs[8 + len(OUTPUTS):-1]
    w_ref = refs[-1]

    @pl.when(pl.program_id(0) == 0)
    def _():
        _realign_weights(w_in_ref, w_ref)

    x = x_ref[...]
    ms = jnp.mean(x * x, axis=-1, keepdims=True)
    h = x * lax.rsqrt(ms + RMS_EPS) * g1_ref[...]
    h = h * (1.0 + scale_ref[...]) + shift_ref[...]
    hb = h.astype(BF16)

    wm = w_ref[:, N_MAIN:N_MAIN + LANES]
    h_hi, h_lo = _split_bf16(h, 2)
    misc = _dot(h_hi, wm) + _dot(h_lo, wm)
    out_refs[len(OUTPUTS)][...] = misc
    out_refs[len(OUTPUTS) + 1][...] = misc.T

    compact = {"misc": misc.astype(BF16)}
    for i, name in enumerate(SOURCES):
        sec = _dot(hb, w_ref[:, i * GROUP_W:(i + 1) * GROUP_W])
        if name in SRC_NORM:
            kind, r = SRC_NORM[name]
            gmat = g64_ref if kind == "n64" else g32_ref
            msq = _dot((sec * sec).astype(BF16), gmat[...])
            fac = lax.rsqrt(msq + RMS_EPS)
            if name == "d_kv":
                lane = lax.broadcasted_iota(I32, sec.shape, 1)
                fac = jnp.where(lane < HEAD_DIM, fac, 1.0)
            sec = sec * fac * gains_ref[r:r + 1, :]
        elif name in SRC_SCALE:
            sec = sec * SRC_SCALE[name]
        compact[name] = sec.astype(BF16)

    for (name, src, width, orient, ones), p_ref, o_ref in zip(OUTPUTS, place_refs, out_refs):
        if orient == "rows":
            o_ref[...] = _dot(compact[src], p_ref[...]).astype(BF16)
        else:
            res = lax.dot_general(p_ref[...], compact[src], (((1,), (1,)), ((), ())),
                                  preferred_element_type=F32)
            if ones:
                r = lax.broadcasted_iota(I32, res.shape, 0)
                res = jnp.where((r & (LANES - 1)) >= HEAD_DIM, 1.0, res)
            o_ref[...] = res.astype(BF16)


def _in_proj_call(xf, scale1, shift1, g1, w_all, gains, g64, g32, places, batch, seq, tm):
    N, D = xf.shape
    per_b = seq // tm
    row = lambda i: (i, 0)
    full = lambda i: (0, 0)
    col4 = lambda i: (i // per_b, i % per_b, 0, 0)
    out_shape, out_specs = [], []
    for name, src, width, orient, ones in OUTPUTS:
        if orient == "rows":
            out_shape.append(jax.ShapeDtypeStruct((N, width), BF16))
            out_specs.append(pl.BlockSpec((tm, width), row))
        else:
            out_shape.append(jax.ShapeDtypeStruct((batch, per_b, width, tm), BF16))
            out_specs.append(pl.BlockSpec((None, None, width, tm), col4))
    out_shape += [jax.ShapeDtypeStruct((N, LANES), F32),
                  jax.ShapeDtypeStruct((batch, per_b, LANES, tm), F32)]
    out_specs += [pl.BlockSpec((tm, LANES), row), pl.BlockSpec((None, None, LANES, tm), col4)]
    return pl.pallas_call(
        _in_proj_kernel,
        grid=(N // tm,),
        in_specs=([pl.BlockSpec((tm, D), row),
                   pl.BlockSpec((None, 1, D), lambda i: (i // per_b, 0, 0)),
                   pl.BlockSpec((None, 1, D), lambda i: (i // per_b, 0, 0)),
                   pl.BlockSpec((1, D), full),
                   pl.BlockSpec(w_all.shape, full),
                   pl.BlockSpec(gains.shape, full),
                   pl.BlockSpec(g64.shape, full),
                   pl.BlockSpec(g32.shape, full)]
                  + [pl.BlockSpec(p.shape, full) for p in places]),
        out_specs=out_specs,
        out_shape=out_shape,
        scratch_shapes=[pltpu.VMEM((D, N_MAIN + LANES), BF16)],
        compiler_params=_params("arbitrary"),
        name="in_proj",
    )(xf, scale1, shift1, g1, w_all, gains, g64, g32, *places)


def _cum_kernel(misc_ref, bf_ref, tri_ref, sel_ref, o_ref, *, blk):
    seq = misc_ref.shape[0]
    carry = jnp.zeros((1, LANES), F32)
    tri = tri_ref[...]
    for j in range(seq // blk):
        rows = slice(j * blk, (j + 1) * blk)
        lf = _log_sigmoid(misc_ref[rows, :] + bf_ref[...])
        c = carry
        for part in _split_bf16(lf, 3):
            c = c + _dot(tri, part)
        carry = c[blk - 1:blk, :]
        parts = _split_bf16(c, 3)
        for h in range(HEADS):
            rep = _dot(parts[0], sel_ref[h]) + _dot(parts[1], sel_ref[h]) + _dot(parts[2], sel_ref[h])
            o_ref[h, rows, :] = rep * LOG2E


def _cum_call(misc, bf_row, tri, sel):
    B, S, _ = misc.shape
    blk = tri.shape[0]
    return pl.pallas_call(
        functools.partial(_cum_kernel, blk=blk),
        grid=(B,),
        in_specs=[pl.BlockSpec((None, S, LANES), lambda b: (b, 0, 0)),
                  pl.BlockSpec((1, LANES), lambda b: (0, 0)),
                  pl.BlockSpec((blk, blk), lambda b: (0, 0)),
                  pl.BlockSpec((HEADS, LANES, LANES), lambda b: (0, 0, 0))],
        out_specs=pl.BlockSpec((None, HEADS, S, LANES), lambda b: (b, 0, 0, 0)),
        out_shape=jax.ShapeDtypeStruct((B, HEADS, S, LANES), F32),
        compiler_params=_params("arbitrary"),
        name="forget_cumsum",
    )(misc, bf_row, tri, sel)


def _q_spec(width, t):
    return pl.BlockSpec((None, None, width, t), lambda b, i: (b, i, 0, 0))


def _seq_rows_spec(seq, width):
    return pl.BlockSpec((None, seq, width), lambda b, i: (b, 0, 0))


def _seq_cols_spec(nb, width, t):
    return pl.BlockSpec((None, nb, width, t), lambda b, i: (b, 0, 0, 0))


def _const_spec(shape):
    return pl.BlockSpec(shape, lambda b, i: (0,) * len(shape))


def _key_query_iota(t):
    key = lax.broadcasted_iota(I32, (t, t), 0)
    query = lax.broadcasted_iota(I32, (t, t), 1)
    return key, query


def _lane_repeat(x, t):
    return jnp.concatenate([x] * (t // LANES), axis=1)


def _softmax_steps(scores, values, m_refs, acc_refs):
    probs, alphas = [], []
    for s, m_ref in zip(scores, m_refs):
        m_old = m_ref[...]
        m_new = jnp.maximum(m_old, jnp.max(s, axis=0, keepdims=True))
        alphas.append(jnp.exp2(m_old - m_new))
        probs.append(jnp.exp2(s - m_new).astype(BF16))
        m_ref[...] = m_new
    for p, vx, alpha, acc_ref in zip(probs, values, alphas, acc_refs):
        acc_ref[...] = alpha * acc_ref[...] + _dot(vx, p)


def _softmax_init(m_scs, acc_scs):
    for m_sc, acc_sc in zip(m_scs, acc_scs):
        m_sc[...] = jnp.full(m_sc.shape, NEG_INF, F32)
        acc_sc[...] = jnp.zeros(acc_sc.shape, F32)


def _softmax_result(acc):
    return acc[:HEAD_DIM] / acc[HEAD_DIM:]


def _alibi(slope, j, qi, t, nblk=1):
    key = lax.broadcasted_iota(I32, (nblk * t, LANES), 0)
    return (slope * LOG2E) * (key + (j - qi) * t).astype(F32)


def _sweep_earlier_blocks(qi, step):
    def body(i, carry):
        step(2 * i, 2)
        return carry

    lax.fori_loop(0, qi // 2, body, 0)

    @pl.when(qi % 2 == 1)
    def _():
        step(qi - 1, 1)


def _key_rows(j, nblk, t):
    return pl.ds(pl.multiple_of(j * t, t), nblk * t)


def _value_cols(vx_ref, j, nblk, rows):
    tiles = [vx_ref[j + b, rows, :] for b in range(nblk)]
    return tiles[0] if nblk == 1 else jnp.concatenate(tiles, axis=1)


def _attn_a_kernel(q_ref, k_ref, vx_ref, cum_ref, o_ref, *scratch, t):
    m_sc, acc_sc = scratch[:HEADS], scratch[HEADS:]
    qi = pl.program_id(1)
    key, query = _key_query_iota(t)
    causal = key <= query
    _softmax_init(m_sc, acc_sc)

    def step(j, nblk, masked=False):
        rows = _key_rows(j, nblk, t)
        heads = [slice(LANES * h, LANES * (h + 1)) for h in range(HEADS)]
        scores = [_dot(k_ref[rows, hs], q_ref[hs, :]) for hs in heads]
        for h in range(HEADS):
            s = scores[h] - _lane_repeat(cum_ref[h, rows, :], t)
            scores[h] = jnp.where(causal, s, -jnp.inf) if masked else s
        _softmax_steps(scores, [_value_cols(vx_ref, j, nblk, hs) for hs in heads], m_sc, acc_sc)

    step(qi, 1, masked=True)
    _sweep_earlier_blocks(qi, step)
    o_ref[...] = jnp.concatenate([_softmax_result(acc_sc[h][...]) for h in range(HEADS)], axis=0).T


def _attn_scratch(n, t, rows=LANES):
    return [pltpu.VMEM((1, t), F32)] * n + [pltpu.VMEM((rows, t), F32)] * n


def _attn_a_call(qT, kp, vxT, cumrep):
    B, nb, _, t = qT.shape
    S = nb * t
    return pl.pallas_call(
        functools.partial(_attn_a_kernel, t=t),
        grid=(B, nb),
        in_specs=[_q_spec(HW, t), _seq_rows_spec(S, HW), _seq_cols_spec(nb, HW, t),
                  pl.BlockSpec((None, HEADS, S, LANES), lambda b, i: (b, 0, 0, 0))],
        out_specs=pl.BlockSpec((None, t, GROUP_W), lambda b, i: (b, i, 0)),
        out_shape=jax.ShapeDtypeStruct((B, S, GROUP_W), F32),
        scratch_shapes=_attn_scratch(HEADS, t),
        compiler_params=_params("arbitrary", "arbitrary"),
        name="attn_forget",
    )(qT, kp, vxT, cumrep)


def _attn_b_kernel(q_ref, k_ref, v_ref, after_ref, o_ref, *scratch, t):
    r_sc, acc_sc = scratch[:HEADS], scratch[HEADS:]
    qi = pl.program_id(1)
    key, query = _key_query_iota(t)
    strict = key < query
    for h in range(HEADS):
        r_sc[h][...] = jnp.zeros(r_sc[h].shape, F32)
        acc_sc[h][...] = jnp.zeros(acc_sc[h].shape, F32)

    def step(j, nblk, masked=False):
        after = after_ref[...]
        rows = _key_rows(j, nblk, t)
        heads = [slice(LANES * h, LANES * (h + 1)) for h in range(HEADS)]
        zs = [_dot(k_ref[rows, hs], q_ref[hs, :]) for hs in heads]
        lbs, splits, later_sums = [], [], []
        for h in range(HEADS):
            lb = _log2_sigmoid(zs[h])
            lm = lb - zs[h]
            if masked:
                lm = jnp.where(strict, lm, 0.0)
            blocks = [lm[b * t:(b + 1) * t] for b in range(nblk)]
            splits.append([_split_bf16(blk, 2) for blk in blocks])
            sums = [jnp.sum(blk, axis=0, keepdims=True) for blk in blocks]
            r_old = r_sc[h][...]
            total = sums[0]
            for s_ in sums[1:]:
                total = total + s_
            r_sc[h][...] = r_old + total
            lbs.append(lb + r_old)
            later, run = [], None
            for b in reversed(range(nblk)):
                later.append(run)
                run = sums[b] if run is None else run + sums[b]
            later_sums.append(later[::-1])
        suffixes = [[_dot(after, hi) + _dot(after, lo) for hi, lo in splits[h]]
                    for h in range(HEADS)]
        ws = []
        for h in range(HEADS):
            parts = [suffixes[h][b] if later_sums[h][b] is None else suffixes[h][b] + later_sums[h][b]
                     for b in range(nblk)]
            suffix = parts[0] if nblk == 1 else jnp.concatenate(parts, axis=0)
            w = jnp.exp2(lbs[h] + suffix)
            if masked:
                w = jnp.where(strict, w, 0.0)
            ws.append(w.astype(BF16))
        for h in range(HEADS):
            acc_sc[h][...] += _dot(_value_cols(v_ref, j, nblk, slice(HEAD_DIM * h, HEAD_DIM * (h + 1))),
                                   ws[h])

    step(qi, 1, masked=True)

    @pl.when(qi % 2 == 1)
    def _():
        step(qi - 1, 1)

    pairs = qi // 2

    def body(i, carry):
        step(2 * (pairs - 1 - i), 2)
        return carry

    lax.fori_loop(0, pairs, body, 0)
    o_ref[...] = jnp.concatenate([acc_sc[h][...] for h in range(HEADS)], axis=0).T


def _attn_b_call(qT, kp, vT, after):
    B, nb, _, t = qT.shape
    S = nb * t
    return pl.pallas_call(
        functools.partial(_attn_b_kernel, t=t),
        grid=(B, nb),
        in_specs=[_q_spec(HW, t), _seq_rows_spec(S, HW), _seq_cols_spec(nb, GROUP_W, t),
                  _const_spec((t, t))],
        out_specs=pl.BlockSpec((None, t, GROUP_W), lambda b, i: (b, i, 0)),
        out_shape=jax.ShapeDtypeStruct((B, S, GROUP_W), F32),
        scratch_shapes=_attn_scratch(HEADS, t, HEAD_DIM),
        compiler_params=_params("arbitrary", "arbitrary"),
        name="attn_stick",
    )(qT, kp, vT, after)


def _attn_c_kernel(q_ref, k_ref, vx_ref, lamv_ref, subg_ref, o_ref, *scratch,
                   t, slopes, lambda_init):
    m_sc, acc_sc = scratch[:2 * HEADS], scratch[2 * HEADS:]
    qi = pl.program_id(1)
    key, query = _key_query_iota(t)
    causal = key <= query
    _softmax_init(m_sc, acc_sc)

    def step(j, nblk, masked=False):
        rows = _key_rows(j, nblk, t)
        heads = [slice(LANES * h, LANES * (h + 1)) for h in range(HEADS)]
        scores = [_dot(k_ref[rows, heads[g // 2]], q_ref[LANES * g:LANES * (g + 1), :])
                  for g in range(2 * HEADS)]
        for g in range(2 * HEADS):
            s = scores[g] + _lane_repeat(_alibi(slopes[g // 2], j, qi, t, nblk), t)
            scores[g] = jnp.where(causal, s, -jnp.inf) if masked else s
        _softmax_steps(scores, [_value_cols(vx_ref, j, nblk, heads[g // 2])
                                for g in range(2 * HEADS)], m_sc, acc_sc)

    step(qi, 1, masked=True)
    _sweep_earlier_blocks(qi, step)

    lv = lamv_ref[...]
    lam = (jnp.exp(jnp.sum(lv[0:1] * lv[1:2], axis=-1, keepdims=True))
           - jnp.exp(jnp.sum(lv[2:3] * lv[3:4], axis=-1, keepdims=True)) + lambda_init)
    outs = []
    for h in range(HEADS):
        o = (_softmax_result(acc_sc[2 * h][...])
             - lam * _softmax_result(acc_sc[2 * h + 1][...]))
        ms = jnp.mean(o * o, axis=0, keepdims=True)
        outs.append(o * lax.rsqrt(ms + RMS_EPS) * subg_ref[...] * (1.0 - lambda_init))
    o_ref[...] = jnp.concatenate(outs, axis=0).T


def _attn_c_call(qT, kp, vxT, lamv, subg_col, slopes, lambda_init):
    B, nb, _, t = qT.shape
    S = nb * t
    return pl.pallas_call(
        functools.partial(_attn_c_kernel, t=t, slopes=slopes, lambda_init=lambda_init),
        grid=(B, nb),
        in_specs=[_q_spec(2 * HW, t), _seq_rows_spec(S, HW), _seq_cols_spec(nb, HW, t),
                  _const_spec(lamv.shape), _const_spec(subg_col.shape)],
        out_specs=pl.BlockSpec((None, t, GROUP_W), lambda b, i: (b, i, 0)),
        out_shape=jax.ShapeDtypeStruct((B, S, GROUP_W), F32),
        scratch_shapes=_attn_scratch(2 * HEADS, t),
        compiler_params=_params("arbitrary", "arbitrary"),
        name="attn_diff",
    )(qT, kp, vxT, lamv, subg_col)


def _fold_rows(x):
    return jnp.sum(x.reshape(x.shape[0] // SUBLANES, SUBLANES, x.shape[1]), axis=0)


def _attn_d_kernel(q_ref, iq_ref, iw_ref, dk_ref, dvx_ref, ik_ref, before_ref, o_ref,
                   keys_sc, tau_sc, *scratch, t, topk, slopes):
    m_sc, acc_sc = scratch[:HEADS], scratch[HEADS:]
    qi = pl.program_id(1)
    key, query = _key_query_iota(t)
    causal = key <= query

    w = iw_ref[...] * IDX_HEADS ** -0.5

    def index_step(j, nblk, masked=False):
        ikb = ik_ref[_key_rows(j, nblk, t), :]
        zs = [_dot(ikb, iq_ref[LANES * hh:LANES * (hh + 1), :]) for hh in range(IDX_HEADS)]
        sc = w[0:1, :] * jnp.maximum(zs[0], 0.0)
        for hh in range(1, IDX_HEADS):
            sc = sc + w[hh:hh + 1, :] * jnp.maximum(zs[hh], 0.0)
        sc = jnp.where(sc == 0.0, 0.0, sc)
        if masked:
            sc = jnp.where(causal, sc, NEG_INF)
        bits = pltpu.bitcast(sc, I32)
        keys = jnp.where(bits < 0, bits ^ 0x7FFFFFFF, bits)
        for b in range(nblk):
            keys_sc[j + b] = keys[b * t:(b + 1) * t]

    index_step(qi, 1, masked=True)
    _sweep_earlier_blocks(qi, index_step)

    qpos = qi * t + lax.broadcasted_iota(I32, (1, t), 1)
    kt = jnp.minimum(topk, qpos + 1).astype(F32)

    def count_ge(cand):
        def body(j, acc):
            return acc + _fold_rows((keys_sc[j] >= cand).astype(F32))
        acc = lax.fori_loop(0, qi + 1, body, jnp.zeros((SUBLANES, t), F32))
        return jnp.sum(acc, axis=0, keepdims=True)

    zero = jnp.zeros((1, t), I32)
    prefix = jnp.where(count_ge(zero) >= kt, zero, INT_MIN)

    def bit_body(i, prefix):
        cand = prefix + lax.shift_left(jnp.int32(1), 30 - i)
        return jnp.where(count_ge(cand) >= kt, cand, prefix)

    tau = lax.fori_loop(0, 31, bit_body, prefix)
    tau_sc[...] = tau
    excess = jnp.max(count_ge(tau) - kt)

    @pl.when(excess > 0.0)
    def _():
        need = kt - count_ge(tau + 1)

        def tie_body(j, seen):
            kj = keys_sc[j]
            eq = kj == tau
            eqb = eq.astype(BF16)
            earlier = _dot(before_ref[...], eqb) + seen
            keys_sc[j] = jnp.where(eq & (earlier >= need), INT_MIN, kj)
            return seen + jnp.sum(eqb.astype(F32), axis=0, keepdims=True)

        lax.fori_loop(0, qi + 1, tie_body, jnp.zeros((1, t), F32))

    _softmax_init(m_sc, acc_sc)

    def step(j, nblk):
        keys = [keys_sc[j + b] for b in range(nblk)]
        sel = (keys[0] if nblk == 1 else jnp.concatenate(keys, axis=0)) >= tau_sc[...]
        kb = dk_ref[_key_rows(j, nblk, t), :]
        vx = _value_cols(dvx_ref, j, nblk, slice(None))
        scores = [_dot(kb, q_ref[LANES * h:LANES * (h + 1), :]) for h in range(HEADS)]
        for h in range(HEADS):
            bias = _lane_repeat(_alibi(slopes[h], j, qi, t, nblk), t)
            scores[h] = jnp.where(sel, scores[h] + bias, -jnp.inf)
        _softmax_steps(scores, [vx] * HEADS, m_sc, acc_sc)

    _sweep_earlier_blocks(qi + 1, step)
    o_ref[...] = jnp.concatenate([_softmax_result(acc_sc[h][...]) for h in range(HEADS)], axis=0).T


def _attn_d_call(qT, iqT, miscT, dkp, dvxT, ikp, before, topk, slopes):
    B, nb, _, t = qT.shape
    S = nb * t
    iw_block = MISC_IW // IDX_HEADS
    return pl.pallas_call(
        functools.partial(_attn_d_kernel, t=t, topk=topk, slopes=slopes),
        grid=(B, nb),
        in_specs=[_q_spec(HW, t), _q_spec(IDX_HEADS * LANES, t),
                  pl.BlockSpec((None, None, IDX_HEADS, t), lambda b, i: (b, i, iw_block, 0)),
                  _seq_rows_spec(S, LANES), _seq_cols_spec(nb, LANES, t), _seq_rows_spec(S, LANES),
                  _const_spec((t, t))],
        out_specs=pl.BlockSpec((None, t, GROUP_W), lambda b, i: (b, i, 0)),
        out_shape=jax.ShapeDtypeStruct((B, S, GROUP_W), F32),
        scratch_shapes=([pltpu.VMEM((nb, t, t), I32), pltpu.VMEM((1, t), I32)]
                        + _attn_scratch(HEADS, t)),
        compiler_params=_params("arbitrary", "arbitrary"),
        name="attn_sparse",
    )(qT, iqT, miscT, dkp, dvxT, ikp, before)


def _out_proj_kernel(x_ref, oa_ref, ob_ref, oc_ref, od_ref, beta_ref, wo_ref, gate_ref,
                     g2_ref, scale_ref, shift_ref, wr_ref, br_ref, earlier_ref,
                     x1_ref, h2x_ref, route_ref, cnt_ref, cnt_sc):
    d_model = x_ref.shape[1]
    acc = None
    for i, o_ref in enumerate((oa_ref, ob_ref, oc_ref, od_ref)):
        sl = slice(GROUP_W * i, GROUP_W * (i + 1))
        mix = (o_ref[...] * beta_ref[:, sl]).astype(BF16)
        part = jnp.dot(mix, wo_ref[sl, :], preferred_element_type=F32)
        acc = part if acc is None else acc + part
    x1 = x_ref[...] + gate_ref[...] * acc
    x1_ref[...] = x1
    ms = jnp.mean(x1 * x1, axis=-1, keepdims=True)
    h2 = x1 * lax.rsqrt(ms + RMS_EPS) * g2_ref[...]
    h2 = h2 * (1.0 + scale_ref[...]) + shift_ref[...]
    h2x_ref[:, :d_model] = h2

    h_hi, h_lo = _split_bf16(h2, 2)
    w_hi, w_lo = _split_bf16(wr_ref[...], 2)
    logits = _dot(h_hi, w_hi) + _dot(h_hi, w_lo) + _dot(h_lo, w_hi) + br_ref[...]
    lt = logits.T
    tm = lt.shape[1]
    g = lt[0:N_GROUPS]
    gmax = jnp.max(g, axis=0, keepdims=True)
    gi = lax.broadcasted_iota(I32, g.shape, 0)
    gidx = jnp.min(jnp.where(g == gmax, gi, N_GROUPS), axis=0, keepdims=True)
    g_prob = 1.0 / jnp.sum(jnp.exp(g - gmax), axis=0, keepdims=True)
    e_sel = jnp.zeros((EXPERTS_PER_GROUP, tm), F32)
    for gg in range(N_GROUPS):
        lo = N_GROUPS + EXPERTS_PER_GROUP * gg
        e_sel = e_sel + jnp.where(gidx == gg, lt[lo:lo + EXPERTS_PER_GROUP], 0.0)
    ei = lax.broadcasted_iota(I32, e_sel.shape, 0)
    v1 = jnp.max(e_sel, axis=0, keepdims=True)
    i1 = jnp.min(jnp.where(e_sel == v1, ei, EXPERTS_PER_GROUP), axis=0, keepdims=True)
    rest = jnp.where(ei == i1, -jnp.inf, e_sel)
    v2 = jnp.max(rest, axis=0, keepdims=True)
    i2 = jnp.min(jnp.where(rest == v2, ei, EXPERTS_PER_GROUP), axis=0, keepdims=True)
    e2 = jnp.exp(v2 - v1)
    w1 = g_prob / (1.0 + e2)
    w2 = g_prob * e2 / (1.0 + e2)
    in_group = jnp.where(ei == i1, w1, 0.0) + jnp.where(ei == i2, w2, 0.0)
    cw = jnp.concatenate([in_group, jnp.zeros((LANES - EXPERTS_PER_GROUP, tm), F32)], axis=0)
    h2x_ref[:, d_model:] = cw.T

    @pl.when(pl.program_id(0) == 0)
    def _():
        cnt_sc[...] = jnp.zeros(cnt_sc.shape, F32)

    rows = lax.broadcasted_iota(I32, (SUBLANES, tm), 0)
    onehot = (rows == gidx).astype(F32)
    seen = _dot(onehot.astype(BF16), earlier_ref[...]) + cnt_sc[...]
    rank = jnp.sum(onehot * seen, axis=0, keepdims=True)
    route_ref[...] = jnp.concatenate(
        [gidx, rank.astype(I32), jnp.zeros((SUBLANES - 2, tm), I32)], axis=0)
    cnt_sc[...] = cnt_sc[...] + jnp.sum(onehot, axis=1, keepdims=True)
    cnt_ref[...] = cnt_sc[:, :LANES]


def _out_proj_call(xf, outs, beta, w_out, gate1, g2, scale2, shift2, w_r, b_r, earlier, seq, tm):
    N, D = xf.shape
    per_b = seq // tm
    row = lambda i: (i, 0)
    full = lambda i: (0, 0)
    per_batch = pl.BlockSpec((None, 1, D), lambda i: (i // per_b, 0, 0))
    return pl.pallas_call(
        _out_proj_kernel,
        grid=(N // tm,),
        in_specs=([pl.BlockSpec((tm, D), row)] + [pl.BlockSpec((tm, GROUP_W), row)] * 4
                  + [pl.BlockSpec((1, D), full), pl.BlockSpec((D, D), full), per_batch,
                     pl.BlockSpec((1, D), full), per_batch, per_batch,
                     pl.BlockSpec((D, LANES), full), pl.BlockSpec((1, LANES), full),
                     pl.BlockSpec((tm, tm), full)]),
        out_specs=[pl.BlockSpec((tm, D), row), pl.BlockSpec((tm, D + LANES), row),
                   pl.BlockSpec((SUBLANES, tm), lambda i: (0, i)),
                   pl.BlockSpec((SUBLANES, LANES), full)],
        out_shape=[jax.ShapeDtypeStruct((N, D), F32), jax.ShapeDtypeStruct((N, D + LANES), F32),
                   jax.ShapeDtypeStruct((SUBLANES, N), I32),
                   jax.ShapeDtypeStruct((SUBLANES, LANES), F32)],
        scratch_shapes=[pltpu.VMEM((SUBLANES, tm), F32)],
        compiler_params=_params("arbitrary"),
        name="out_proj_router",
    )(xf, *outs, beta, w_out, gate1, g2, scale2, shift2, w_r, b_r, earlier)


MOE_TILE = 256
DMA_UNROLL = 8


def _row_copy(src_ref, src_row, dst_ref, dst_row, sem):
    return pltpu.make_async_copy(src_ref.at[pl.ds(src_row, 1), :],
                                 dst_ref.at[pl.ds(dst_row, 1), :], sem)


def _dispatch_kernel(slot_ref, h2x_ref, zeros_ref, xs_ref, sem):
    del zeros_ref
    rows = h2x_ref.shape[0]

    def issue(r, carry):
        _row_copy(h2x_ref, r, xs_ref, slot_ref[0, r], sem).start()
        return carry

    lax.fori_loop(0, rows, issue, 0, unroll=DMA_UNROLL)

    def drain(r, carry):
        _row_copy(h2x_ref, r, xs_ref, slot_ref[0, r], sem).wait()
        return carry

    lax.fori_loop(0, rows, drain, 0, unroll=DMA_UNROLL)


def _dispatch_call(slot, h2x, xs_zero, tm):
    N, DX = h2x.shape
    return pl.pallas_call(
        _dispatch_kernel,
        grid=(N // tm,),
        in_specs=[pl.BlockSpec((None, 1, tm), lambda i: (i, 0, 0), memory_space=pltpu.SMEM),
                  pl.BlockSpec((tm, DX), lambda i: (i, 0)),
                  pl.BlockSpec(memory_space=pl.ANY)],
        out_specs=pl.BlockSpec(memory_space=pl.ANY),
        out_shape=jax.ShapeDtypeStruct(xs_zero.shape, F32),
        scratch_shapes=[pltpu.SemaphoreType.DMA(())],
        input_output_aliases={2: 0},
        compiler_params=_params("arbitrary"),
        name="moe_dispatch",
    )(slot.reshape(N // tm, 1, tm), h2x, xs_zero)


def _expert_kernel(group_ref, valid_ref, xs_ref, w1f_ref, w3f_ref, w2f_ref, expand_ref, y_ref,
                   w1_ref, w3_ref, w2_ref):
    i = pl.program_id(0)
    valid = valid_ref[i]
    d_model = y_ref.shape[1]
    ff = w1_ref.shape[2]
    new_group = jnp.logical_or(i == 0, group_ref[i] != group_ref[jnp.maximum(i - 1, 0)])

    @pl.when(valid == 0)
    def _():
        y_ref[...] = jnp.zeros(y_ref.shape, F32)

    @pl.when(jnp.logical_and(valid > 0, new_group))
    def _():
        for e in range(EXPERTS_PER_GROUP):
            w1_ref[e] = w1f_ref[e].astype(BF16)
            w3_ref[e] = w3f_ref[e].astype(BF16)
            w2_ref[e] = w2f_ref[e].astype(BF16)

    @pl.when(valid > 0)
    def _():
        x = xs_ref[:, :d_model].astype(BF16)
        cw_hi, cw_lo = _split_bf16(xs_ref[:, d_model:], 2)
        cwx = _dot(cw_hi, expand_ref[...]) + _dot(cw_lo, expand_ref[...])
        acc = None
        for e in range(EXPERTS_PER_GROUP):
            a = _dot(x, w1_ref[e])
            b = _dot(x, w3_ref[e])
            hid = a * jax.nn.sigmoid(a) * b * cwx[:, ff * e:ff * (e + 1)]
            part = _dot(hid.astype(BF16), w2_ref[e])
            acc = part if acc is None else acc + part
        y_ref[...] = acc


def _expert_call(tile_group, tile_valid, xs, w1, w3, w2, expand, layer):
    P, DX = xs.shape
    D, FF = w1.shape[-2:]
    tr = MOE_TILE

    def group_spec(rows, cols):
        return pl.BlockSpec((None, None, EXPERTS_PER_GROUP, rows, cols),
                            lambda i, g, v: (layer, g[i], 0, 0, 0),
                            pipeline_mode=pl.Buffered(1))

    grid_spec = pltpu.PrefetchScalarGridSpec(
        num_scalar_prefetch=2,
        grid=(P // tr,),
        in_specs=[pl.BlockSpec((tr, DX), lambda i, g, v: (i, 0)),
                  group_spec(D, FF), group_spec(D, FF), group_spec(FF, D),
                  pl.BlockSpec(expand.shape, lambda i, g, v: (0, 0))],
        out_specs=pl.BlockSpec((tr, D), lambda i, g, v: (i, 0)),
        scratch_shapes=[pltpu.VMEM((EXPERTS_PER_GROUP, D, FF), BF16),
                        pltpu.VMEM((EXPERTS_PER_GROUP, D, FF), BF16),
                        pltpu.VMEM((EXPERTS_PER_GROUP, FF, D), BF16)],
    )
    return pl.pallas_call(
        _expert_kernel,
        grid_spec=grid_spec,
        out_shape=jax.ShapeDtypeStruct((P, D), F32),
        compiler_params=_params("arbitrary"),
        name="moe_experts",
    )(tile_group, tile_valid, xs, w1, w3, w2, expand)


def _residual_kernel(slot_ref, slot_next_ref, x1_ref, gate_ref, ys_ref, o_ref, buf, sems):
    i = pl.program_id(0)
    rows = x1_ref.shape[0]

    def gather(slots, b, wait):
        def body(r, carry):
            cp = pltpu.make_async_copy(ys_ref.at[pl.ds(slots[0, r], 1), :],
                                       buf.at[b, pl.ds(r, 1), :], sems.at[b])
            cp.wait() if wait else cp.start()
            return carry
        lax.fori_loop(0, rows, body, 0, unroll=DMA_UNROLL)

    @pl.when(i == 0)
    def _():
        gather(slot_ref, 0, False)

    @pl.when(i + 1 < pl.num_programs(0))
    def _():
        gather(slot_next_ref, (i + 1) % 2, False)

    gather(slot_ref, i % 2, True)
    o_ref[...] = x1_ref[...] + gate_ref[...] * buf[i % 2]


def _residual_call(slot, x1, gate2, ys, seq, tm):
    N, D = x1.shape
    n = N // tm
    per_b = seq // tm
    row = pl.BlockSpec((tm, D), lambda i: (i, 0))
    slot3 = slot.reshape(n, 1, tm)
    return pl.pallas_call(
        _residual_kernel,
        grid=(n,),
        in_specs=[pl.BlockSpec((None, 1, tm), lambda i: (i, 0, 0), memory_space=pltpu.SMEM),
                  pl.BlockSpec((None, 1, tm), lambda i: (jnp.minimum(i + 1, n - 1), 0, 0),
                               memory_space=pltpu.SMEM),
                  row, pl.BlockSpec((None, 1, D), lambda i: (i // per_b, 0, 0)),
                  pl.BlockSpec(memory_space=pl.ANY)],
        out_specs=row,
        out_shape=jax.ShapeDtypeStruct((N, D), F32),
        scratch_shapes=[pltpu.VMEM((2, tm, D), F32), pltpu.SemaphoreType.DMA((2,))],
        compiler_params=_params("arbitrary"),
        name="moe_residual",
    )(slot3, slot3, x1, gate2, ys)


def _moe_routing(route, cnt, n_tokens):
    tr = MOE_TILE
    n_tiles = n_tokens // tr + N_GROUPS
    counts = cnt[:N_GROUPS, 0].astype(I32)
    padded = (counts + tr - 1) // tr * tr
    ends = jnp.cumsum(padded)
    starts = ends - padded
    group, rank = route[0], route[1]
    slot = starts[group] + rank
    tile_start = jnp.arange(n_tiles, dtype=I32) * tr
    tile_group = jnp.minimum(jnp.sum((tile_start[:, None] >= ends[None, :]).astype(I32), axis=1),
                             N_GROUPS - 1)
    tile_valid = jnp.clip(starts[tile_group] + counts[tile_group] - tile_start, 0, tr)
    tile_valid = jnp.where(tile_start < ends[-1], tile_valid, 0)
    return slot, tile_group, tile_valid


def _block_diag_mean(width, group, valid_in_128=None):
    i = jnp.arange(width)
    same = (i[:, None] // group) == (i[None, :] // group)
    if valid_in_128 is not None:
        same = same & ((i[:, None] % LANES) < valid_in_128) & ((i[None, :] % LANES) < valid_in_128)
    return jnp.where(same, 1.0 / group, 0.0).astype(BF16)


def _placement(src_width, dst_width, pairs):
    src = jnp.array([p[0] for p in pairs], I32)
    dst = jnp.array([p[1] for p in pairs], I32)
    return jnp.zeros((src_width, dst_width), F32).at[src, dst].set(1.0).astype(BF16)


def _placements():
    per_head = [(HEAD_DIM * h + d, LANES * h + d) for h in range(HEADS) for d in range(HEAD_DIM)]
    diff_q = [(HEAD_DIM * h + DIFF_DIM * c + d, LANES * (2 * h + c) + DIFF_DIM * c + d)
              for h in range(HEADS) for c in range(2) for d in range(DIFF_DIM)]
    idx_q = [(IDX_DIM * hh + d, LANES * hh + d) for hh in range(IDX_HEADS) for d in range(IDX_DIM)]
    ident = [(d, d) for d in range(GROUP_W)]
    first64 = [(d, d) for d in range(HEAD_DIM)]
    second64 = [(HEAD_DIM + d, d) for d in range(HEAD_DIM)]
    idx_k = [(MISC_IK + d, d) for d in range(IDX_DIM)]
    table = {
        "a_qT": (GROUP_W, per_head), "a_kp": (GROUP_W, per_head), "a_vxT": (GROUP_W, per_head),
        "b_qT": (GROUP_W, per_head), "b_kp": (GROUP_W, per_head), "b_vT": (GROUP_W, ident),
        "c_qT": (GROUP_W, diff_q), "c_kp": (GROUP_W, per_head), "c_vxT": (GROUP_W, per_head),
        "d_qT": (GROUP_W, per_head), "i_qT": (GROUP_W, idx_q),
        "d_kp": (GROUP_W, first64), "d_vxT": (GROUP_W, second64), "i_kp": (LANES, idx_k),
    }
    out = []
    for name, src, width, orient, ones in OUTPUTS:
        src_width, pairs = table[name]
        p = _placement(src_width, width, pairs)
        out.append(p if orient == "rows" else p.T)
    return out


def kernel(x, c, ada_w, ada_b, norm1_g, norm2_g, w_in, b_f, qn_a, kn_a, qn_c, kn_c,
           lam_q1, lam_k1, lam_q2, lam_k2, subln_g, qn_d, kn_d, mix_beta, w_out,
           w_group, b_group, w_expert, b_expert, w1, w3, w2):
    B, S, D = x.shape
    L = ada_w.shape[0]
    N = B * S
    topk = min(TOPK_MAX, S // 4)
    t = ATTN_BLOCK
    tm = t
    slopes = [2.0 ** (-8.0 * i / (2 * HEADS)) for i in range(1, 2 * HEADS + 1)]
    slopes_c, slopes_d = tuple(slopes[0::2]), tuple(slopes[1::2])

    idx_t = jnp.arange(t)
    after = (idx_t[None, :] > idx_t[:, None]).astype(BF16)
    before = (idx_t[None, :] < idx_t[:, None]).astype(BF16)
    upto = (idx_t[None, :] <= idx_t[:, None]).astype(BF16)
    g64 = _block_diag_mean(GROUP_W, HEAD_DIM)
    g32 = _block_diag_mean(GROUP_W, DIFF_DIM)
    places = _placements()
    lane = jnp.arange(LANES)
    cum_sel = jnp.stack([jnp.broadcast_to((lane == MISC_AF + h)[:, None], (LANES, LANES))
                         for h in range(HEADS)]).astype(BF16)
    ff_col = jnp.arange(EXPERTS_PER_GROUP * EXPERT_FF)
    expand = (lane[:, None] == ff_col[None, :] // EXPERT_FF).astype(BF16)

    mod = _ada_call(c, ada_w, ada_b)
    xf = x.reshape(N, D)

    for l in range(L):
        m6 = mod[l].reshape(B, 6, 1, D)
        shift1, scale1, gate1, shift2, scale2, gate2 = (m6[:, i] for i in range(6))

        w_all = w_in[l].astype(BF16)
        ones = jnp.ones((GROUP_W - HEAD_DIM,), F32)
        gains = jnp.stack([jnp.tile(qn_a[l], HEADS) * (HEAD_DIM ** -0.5 * LOG2E),
                           jnp.tile(kn_a[l], HEADS),
                           jnp.tile(qn_c[l], 2 * HEADS) * (DIFF_DIM ** -0.5 * LOG2E),
                           jnp.tile(kn_c[l], 2 * HEADS),
                           jnp.tile(qn_d[l], HEADS) * (HEAD_DIM ** -0.5 * LOG2E),
                           jnp.concatenate([kn_d[l], ones])]).astype(F32)
        gains = jnp.concatenate([gains, jnp.zeros((2, GROUP_W), F32)], axis=0)

        outs = _in_proj_call(xf, scale1, shift1, norm1_g[l].reshape(1, D), w_all, gains,
                             g64, g32, places, B, S, tm)
        sec = {name: o for (name, _, _, _, _), o in zip(OUTPUTS, outs)}
        for name, src, width, orient, ones_ in OUTPUTS:
            if orient == "rows":
                sec[name] = sec[name].reshape(B, S, width)
        misc = outs[len(OUTPUTS)].reshape(B, S, LANES)
        miscT = outs[len(OUTPUTS) + 1]

        bf_row = jnp.zeros((1, LANES), F32).at[0, MISC_AF:MISC_AF + HEADS].set(b_f[l].astype(F32))
        cumrep = _cum_call(misc, bf_row, upto, cum_sel)

        o_a = _attn_a_call(sec["a_qT"], sec["a_kp"], sec["a_vxT"], cumrep)
        o_b = _attn_b_call(sec["b_qT"], sec["b_kp"], sec["b_vT"], after)
        lambda_init = 0.8 - 0.6 * math.exp(-0.3 * l)
        lamv = jnp.stack([lam_q1[l], lam_k1[l], lam_q2[l], lam_k2[l]]).astype(F32)
        o_c = _attn_c_call(sec["c_qT"], sec["c_kp"], sec["c_vxT"], lamv,
                           subln_g[l].reshape(HEAD_DIM, 1).astype(F32), slopes_c, lambda_init)
        o_d = _attn_d_call(sec["d_qT"], sec["i_qT"], miscT, sec["d_kp"], sec["d_vxT"], sec["i_kp"],
                           before, topk, slopes_d)

        w_r = jnp.concatenate([w_group[l], w_expert[l],
                               jnp.zeros((D, LANES - N_GROUPS - N_EXPERTS), F32)], axis=1)
        b_r = jnp.concatenate([b_group[l], b_expert[l],
                               jnp.zeros((LANES - N_GROUPS - N_EXPERTS,), F32)]).reshape(1, LANES)
        x1, h2x, route, cnt = _out_proj_call(
            xf, [o.reshape(N, GROUP_W) for o in (o_a, o_b, o_c, o_d)], mix_beta[l].reshape(1, D),
            w_out[l].astype(BF16), gate1, norm2_g[l].reshape(1, D), scale2, shift2, w_r, b_r,
            after, S, tm)

        slot, tile_group, tile_valid = _moe_routing(route, cnt, N)
        xs = _dispatch_call(slot, h2x, jnp.zeros((tile_group.shape[0] * MOE_TILE, D + LANES), F32), tm)
        ys = _expert_call(tile_group, tile_valid, xs, w1, w3, w2, expand, l)
        xf = _residual_call(slot, x1, gate2, ys, S, tm)

    return xf.reshape(B, S, D)
```

```python
import functools
import math

import jax
import jax.numpy as jnp
from jax import lax
from jax.experimental import pallas as pl
from jax.experimental.pallas import tpu as pltpu

F32 = jnp.float32
BF16 = jnp.bfloat16
I32 = jnp.int32

HEAD_DIM = 64
HEADS = 4
GROUP_W = HEADS * HEAD_DIM
DIFF_DIM = HEAD_DIM // 2
IDX_HEADS = 8
IDX_DIM = 32
TOPK_MAX = 256
N_GROUPS = 4
EXPERTS_PER_GROUP = 8
N_EXPERTS = N_GROUPS * EXPERTS_PER_GROUP
EXPERT_FF = 256
RMS_EPS = 1e-6
NEG_INF = -1e30
INT_MIN = -(2 ** 31)
LOG2E = math.log2(math.e)

LANES = 128
SUBLANES = 8
ATTN_BLOCK = 256
VMEM_LIMIT = 56 * 1024 * 1024

MISC_IK = 0
MISC_IW = IDX_DIM
MISC_AF = IDX_DIM + IDX_HEADS


def _params(*sem):
    return pltpu.CompilerParams(dimension_semantics=sem, vmem_limit_bytes=VMEM_LIMIT)


def _log_sigmoid(z):
    return jnp.minimum(z, 0.0) - jnp.log1p(jnp.exp(-jnp.abs(z)))


def _log2_sigmoid(z2):
    return jnp.minimum(z2, 0.0) - jnp.log2(1.0 + jnp.exp2(-jnp.abs(z2)))


def _split_bf16(x, parts):
    out = []
    rem = x
    for _ in range(parts):
        p = rem.astype(BF16)
        out.append(p)
        rem = rem - p.astype(F32)
    return out


def _dot(a, b):
    return jnp.dot(a, b, preferred_element_type=F32)


def _ada_kernel(c_ref, w_ref, b_ref, o_ref):
    c = c_ref[...]
    ca = c * jax.nn.sigmoid(c)
    o_ref[...] = jnp.dot(ca, w_ref[...], precision=lax.Precision.HIGHEST,
                         preferred_element_type=F32) + b_ref[...]


def _ada_call(c, ada_w, ada_b):
    L, D, E = ada_w.shape
    B = c.shape[0]
    tn = 1536
    return pl.pallas_call(
        _ada_kernel,
        grid=(L, E // tn),
        in_specs=[pl.BlockSpec((B, D), lambda l, j: (0, 0)),
                  pl.BlockSpec((None, D, tn), lambda l, j: (l, 0, j)),
                  pl.BlockSpec((None, 1, tn), lambda l, j: (l, 0, j))],
        out_specs=pl.BlockSpec((None, B, tn), lambda l, j: (l, 0, j)),
        out_shape=jax.ShapeDtypeStruct((L, B, E), F32),
        compiler_params=_params("arbitrary", "arbitrary"),
        name="ada_mod",
    )(c, ada_w, ada_b.reshape(L, 1, E))


SOURCES = ("a_q", "a_k", "a_v", "b_q", "b_k", "b_v", "c_q", "c_k", "c_v", "d_q", "i_q", "d_kv")
SRC_NORM = {"a_q": ("n64", 0), "a_k": ("n64", 1), "c_q": ("n32", 2), "c_k": ("n32", 3),
            "d_q": ("n64", 4), "d_kv": ("n64", 5)}
SRC_SCALE = {"b_q": HEAD_DIM ** -0.5 * LOG2E, "i_q": IDX_DIM ** -0.5}
HW = HEADS * LANES
OUTPUTS = (
    ("a_qT", "a_q", HW, "cols", False), ("a_kp", "a_k", HW, "rows", False),
    ("a_vxT", "a_v", HW, "cols", True),
    ("b_qT", "b_q", HW, "cols", False), ("b_kp", "b_k", HW, "rows", False),
    ("b_vT", "b_v", GROUP_W, "cols", False),
    ("c_qT", "c_q", 2 * HW, "cols", False), ("c_kp", "c_k", HW, "rows", False),
    ("c_vxT", "c_v", HW, "cols", True),
    ("d_qT", "d_q", HW, "cols", False), ("i_qT", "i_q", IDX_HEADS * LANES, "cols", False),
    ("d_kp", "d_kv", LANES, "rows", False), ("d_vxT", "d_kv", LANES, "cols", True),
    ("i_kp", "misc", LANES, "rows", False),
)
N_MAIN = len(SOURCES) * GROUP_W

IN_SPLITS = (GROUP_W, GROUP_W, GROUP_W, HEADS, GROUP_W, GROUP_W, GROUP_W, GROUP_W, GROUP_W,
             GROUP_W, GROUP_W, HEAD_DIM, HEAD_DIM, IDX_HEADS * IDX_DIM, IDX_DIM, IDX_HEADS)
IN_NAMES = ("a_q", "a_k", "a_v", "a_f", "b_q", "b_k", "b_v", "c_q", "c_k", "c_v",
            "d_q", "d_k", "d_v", "i_q", "i_k", "i_w")
IN_OFFSETS = {n: (sum(IN_SPLITS[:i]), IN_SPLITS[i]) for i, n in enumerate(IN_NAMES)}
P_IN = sum(IN_SPLITS)
SECTION_PARTS = {name: (name,) for name in SOURCES if name != "d_kv"}
SECTION_PARTS["d_kv"] = ("d_k", "d_v")
SECTION_PARTS["misc"] = ("i_k", "i_w", "a_f")
REALIGN_ROWS = 256


def _realign_weights(w_ref, w_sc):
    d_model = w_ref.shape[0]
    for r0 in range(0, d_model, REALIGN_ROWS):
        rows = slice(r0, r0 + REALIGN_ROWS)
        for i, name in enumerate(SOURCES + ("misc",)):
            width = LANES if name == "misc" else GROUP_W
            pieces, used = [], 0
            for part in SECTION_PARTS[name]:
                off, w = IN_OFFSETS[part]
                base = off // LANES * LANES
                end = min(-(-(off + w) // LANES) * LANES, P_IN)
                window = w_ref[rows, base:end]
                pieces.append(window[:, off - base:off - base + w])
                used += w
            if used < width:
                pieces.append(jnp.zeros((REALIGN_ROWS, width - used), BF16))
            block = pieces[0] if len(pieces) == 1 else jnp.concatenate(pieces, axis=1)
            w_sc[rows, i * GROUP_W:i * GROUP_W + width] = block


def _in_proj_kernel(*refs):
    (x_ref, scale_ref, shift_ref, g1_ref, w_in_ref, gains_ref, g64_ref, g32_ref) = refs[:8]
    place_refs = refs[8:8 + len(OUTPUTS)]
    out_refs = refs[8 + len(OUTPUTS):-1]
    w_ref = refs[-1]

    @pl.when(pl.program_id(0) == 0)
    def _():
        _realign_weights(w_in_ref, w_ref)

    x = x_ref[...]
    ms = jnp.mean(x * x, axis=-1, keepdims=True)
    h = x * lax.rsqrt(ms + RMS_EPS) * g1_ref[...]
    h = h * (1.0 + scale_ref[...]) + shift_ref[...]
    hb = h.astype(BF16)

    wm = w_ref[:, N_MAIN:N_MAIN + LANES]
    h_hi, h_lo = _split_bf16(h, 2)
    misc = _dot(h_hi, wm) + _dot(h_lo, wm)
    out_refs[len(OUTPUTS)][...] = misc
    out_refs[len(OUTPUTS) + 1][...] = misc.T

    compact = {"misc": misc.astype(BF16)}
    for i, name in enumerate(SOURCES):
        sec = _dot(hb, w_ref[:, i * GROUP_W:(i + 1) * GROUP_W])
        if name in SRC_NORM:
            kind, r = SRC_NORM[name]
            gmat = g64_ref if kind == "n64" else g32_ref
            msq = _dot((sec * sec).astype(BF16), gmat[...])
            fac = lax.rsqrt(msq + RMS_EPS)
            if name == "d_kv":
                lane = lax.broadcasted_iota(I32, sec.shape, 1)
                fac = jnp.where(lane < HEAD_DIM, fac, 1.0)
            sec = sec * fac * gains_ref[r:r + 1, :]
        elif name in SRC_SCALE:
            sec = sec * SRC_SCALE[name]
        compact[name] = sec.astype(BF16)

    for (name, src, width, orient, ones), p_ref, o_ref in zip(OUTPUTS, place_refs, out_refs):
        if orient == "rows":
            o_ref[...] = _dot(compact[src], p_ref[...]).astype(BF16)
        else:
            res = lax.dot_general(p_ref[...], compact[src], (((1,), (1,)), ((), ())),
                                  preferred_element_type=F32)
            if ones:
                r = lax.broadcasted_iota(I32, res.shape, 0)
                res = jnp.where((r & (LANES - 1)) >= HEAD_DIM, 1.0, res)
            o_ref[...] = res.astype(BF16)


def _in_proj_call(xf, scale1, shift1, g1, w_all, gains, g64, g32, places, batch, seq, tm):
    N, D = xf.shape
    per_b = seq // tm
    row = lambda i: (i, 0)
    full = lambda i: (0, 0)
    col4 = lambda i: (i // per_b, i % per_b, 0, 0)
    out_shape, out_specs = [], []
    for name, src, width, orient, ones in OUTPUTS:
        if orient == "rows":
            out_shape.append(jax.ShapeDtypeStruct((N, width), BF16))
            out_specs.append(pl.BlockSpec((tm, width), row))
        else:
            out_shape.append(jax.ShapeDtypeStruct((batch, per_b, width, tm), BF16))
            out_specs.append(pl.BlockSpec((None, None, width, tm), col4))
    out_shape += [jax.ShapeDtypeStruct((N, LANES), F32),
                  jax.ShapeDtypeStruct((batch, per_b, LANES, tm), F32)]
    out_specs += [pl.BlockSpec((tm, LANES), row), pl.BlockSpec((None, None, LANES, tm), col4)]
    return pl.pallas_call(
        _in_proj_kernel,
        grid=(N // tm,),
        in_specs=([pl.BlockSpec((tm, D), row),
                   pl.BlockSpec((None, 1, D), lambda i: (i // per_b, 0, 0)),
                   pl.BlockSpec((None, 1, D), lambda i: (i // per_b, 0, 0)),
                   pl.BlockSpec((1, D), full),
                   pl.BlockSpec(w_all.shape, full),
                   pl.BlockSpec(gains.shape, full),
                   pl.BlockSpec(g64.shape, full),
                   pl.BlockSpec(g32.shape, full)]
                  + [pl.BlockSpec(p.shape, full) for p in places]),
        out_specs=out_specs,
        out_shape=out_shape,
        scratch_shapes=[pltpu.VMEM((D, N_MAIN + LANES), BF16)],
        compiler_params=_params("arbitrary"),
        name="in_proj",
    )(xf, scale1, shift1, g1, w_all, gains, g64, g32, *places)


def _cum_kernel(misc_ref, bf_ref, tri_ref, sel_ref, o_ref, *, blk):
    seq = misc_ref.shape[0]
    carry = jnp.zeros((1, LANES), F32)
    tri = tri_ref[...]
    for j in range(seq // blk):
        rows = slice(j * blk, (j + 1) * blk)
        lf = _log_sigmoid(misc_ref[rows, :] + bf_ref[...])
        c = carry
        for part in _split_bf16(lf, 3):
            c = c + _dot(tri, part)
        carry = c[blk - 1:blk, :]
        parts = _split_bf16(c, 3)
        for h in range(HEADS):
            rep = _dot(parts[0], sel_ref[h]) + _dot(parts[1], sel_ref[h]) + _dot(parts[2], sel_ref[h])
            o_ref[h, rows, :] = rep * LOG2E


def _cum_call(misc, bf_row, tri, sel):
    B, S, _ = misc.shape
    blk = tri.shape[0]
    return pl.pallas_call(
        functools.partial(_cum_kernel, blk=blk),
        grid=(B,),
        in_specs=[pl.BlockSpec((None, S, LANES), lambda b: (b, 0, 0)),
                  pl.BlockSpec((1, LANES), lambda b: (0, 0)),
                  pl.BlockSpec((blk, blk), lambda b: (0, 0)),
                  pl.BlockSpec((HEADS, LANES, LANES), lambda b: (0, 0, 0))],
        out_specs=pl.BlockSpec((None, HEADS, S, LANES), lambda b: (b, 0, 0, 0)),
        out_shape=jax.ShapeDtypeStruct((B, HEADS, S, LANES), F32),
        compiler_params=_params("arbitrary"),
        name="forget_cumsum",
    )(misc, bf_row, tri, sel)


def _q_spec(width, t):
    return pl.BlockSpec((None, None, width, t), lambda b, i: (b, i, 0, 0))


def _seq_rows_spec(seq, width):
    return pl.BlockSpec((None, seq, width), lambda b, i: (b, 0, 0))


def _seq_cols_spec(nb, width, t):
    return pl.BlockSpec((None, nb, width, t), lambda b, i: (b, 0, 0, 0))


def _const_spec(shape):
    return pl.BlockSpec(shape, lambda b, i: (0,) * len(shape))


def _key_query_iota(t):
    key = lax.broadcasted_iota(I32, (t, t), 0)
    query = lax.broadcasted_iota(I32, (t, t), 1)
    return key, query


def _lane_repeat(x, t):
    return jnp.concatenate([x] * (t // LANES), axis=1)


def _softmax_steps(scores, values, m_refs, acc_refs):
    probs, alphas = [], []
    for s, m_ref in zip(scores, m_refs):
        m_old = m_ref[...]
        m_new = jnp.maximum(m_old, jnp.max(s, axis=0, keepdims=True))
        alphas.append(jnp.exp2(m_old - m_new))
        probs.append(jnp.exp2(s - m_new).astype(BF16))
        m_ref[...] = m_new
    for p, vx, alpha, acc_ref in zip(probs, values, alphas, acc_refs):
        acc_ref[...] = alpha * acc_ref[...] + _dot(vx, p)


def _softmax_init(m_scs, acc_scs):
    for m_sc, acc_sc in zip(m_scs, acc_scs):
        m_sc[...] = jnp.full(m_sc.shape, NEG_INF, F32)
        acc_sc[...] = jnp.zeros(acc_sc.shape, F32)


def _softmax_result(acc):
    return acc[:HEAD_DIM] / acc[HEAD_DIM:]


def _alibi(slope, j, qi, t, nblk=1):
    key = lax.broadcasted_iota(I32, (nblk * t, LANES), 0)
    return (slope * LOG2E) * (key + (j - qi) * t).astype(F32)


def _sweep_earlier_blocks(qi, step):
    def body(i, carry):
        step(2 * i, 2)
        return carry

    lax.fori_loop(0, qi // 2, body, 0)

    @pl.when(qi % 2 == 1)
    def _():
        step(qi - 1, 1)


def _key_rows(j, nblk, t):
    return pl.ds(pl.multiple_of(j * t, t), nblk * t)


def _value_cols(vx_ref, j, nblk, rows):
    tiles = [vx_ref[j + b, rows, :] for b in range(nblk)]
    return tiles[0] if nblk == 1 else jnp.concatenate(tiles, axis=1)


def _attn_a_kernel(q_ref, k_ref, vx_ref, cum_ref, o_ref, *scratch, t):
    m_sc, acc_sc = scratch[:HEADS], scratch[HEADS:]
    qi = pl.program_id(1)
    key, query = _key_query_iota(t)
    causal = key <= query
    _softmax_init(m_sc, acc_sc)

    def step(j, nblk, masked=False):
        rows = _key_rows(j, nblk, t)
        heads = [slice(LANES * h, LANES * (h + 1)) for h in range(HEADS)]
        scores = [_dot(k_ref[rows, hs], q_ref[hs, :]) for hs in heads]
        for h in range(HEADS):
            s = scores[h] - _lane_repeat(cum_ref[h, rows, :], t)
            scores[h] = jnp.where(causal, s, -jnp.inf) if masked else s
        _softmax_steps(scores, [_value_cols(vx_ref, j, nblk, hs) for hs in heads], m_sc, acc_sc)

    step(qi, 1, masked=True)
    _sweep_earlier_blocks(qi, step)
    o_ref[...] = jnp.concatenate([_softmax_result(acc_sc[h][...]) for h in range(HEADS)], axis=0).T


def _attn_scratch(n, t, rows=LANES):
    return [pltpu.VMEM((1, t), F32)] * n + [pltpu.VMEM((rows, t), F32)] * n


def _attn_a_call(qT, kp, vxT, cumrep):
    B, nb, _, t = qT.shape
    S = nb * t
    return pl.pallas_call(
        functools.partial(_attn_a_kernel, t=t),
        grid=(B, nb),
        in_specs=[_q_spec(HW, t), _seq_rows_spec(S, HW), _seq_cols_spec(nb, HW, t),
                  pl.BlockSpec((None, HEADS, S, LANES), lambda b, i: (b, 0, 0, 0))],
        out_specs=pl.BlockSpec((None, t, GROUP_W), lambda b, i: (b, i, 0)),
        out_shape=jax.ShapeDtypeStruct((B, S, GROUP_W), F32),
        scratch_shapes=_attn_scratch(HEADS, t),
        compiler_params=_params("arbitrary", "arbitrary"),
        name="attn_forget",
    )(qT, kp, vxT, cumrep)


def _attn_b_kernel(q_ref, k_ref, v_ref, after_ref, o_ref, *scratch, t):
    r_sc, acc_sc = scratch[:HEADS], scratch[HEADS:]
    qi = pl.program_id(1)
    key, query = _key_query_iota(t)
    strict = key < query
    for h in range(HEADS):
        r_sc[h][...] = jnp.zeros(r_sc[h].shape, F32)
        acc_sc[h][...] = jnp.zeros(acc_sc[h].shape, F32)

    def step(j, nblk, masked=False):
        after = after_ref[...]
        rows = _key_rows(j, nblk, t)
        heads = [slice(LANES * h, LANES * (h + 1)) for h in range(HEADS)]
        zs = [_dot(k_ref[rows, hs], q_ref[hs, :]) for hs in heads]
        lbs, splits, later_sums = [], [], []
        for h in range(HEADS):
            lb = _log2_sigmoid(zs[h])
            lm = lb - zs[h]
            if masked:
                lm = jnp.where(strict, lm, 0.0)
            blocks = [lm[b * t:(b + 1) * t] for b in range(nblk)]
            splits.append([_split_bf16(blk, 2) for blk in blocks])
            sums = [jnp.sum(blk, axis=0, keepdims=True) for blk in blocks]
            r_old = r_sc[h][...]
            total = sums[0]
            for s_ in sums[1:]:
                total = total + s_
            r_sc[h][...] = r_old + total
            lbs.append(lb + r_old)
            later, run = [], None
            for b in reversed(range(nblk)):
                later.append(run)
                run = sums[b] if run is None else run + sums[b]
            later_sums.append(later[::-1])
        suffixes = [[_dot(after, hi) + _dot(after, lo) for hi, lo in splits[h]]
                    for h in range(HEADS)]
        ws = []
        for h in range(HEADS):
            parts = [suffixes[h][b] if later_sums[h][b] is None else suffixes[h][b] + later_sums[h][b]
                     for b in range(nblk)]
            suffix = parts[0] if nblk == 1 else jnp.concatenate(parts, axis=0)
            w = jnp.exp2(lbs[h] + suffix)
            if masked:
                w = jnp.where(strict, w, 0.0)
            ws.append(w.astype(BF16))
        for h in range(HEADS):
            acc_sc[h][...] += _dot(_value_cols(v_ref, j, nblk, slice(HEAD_DIM * h, HEAD_DIM * (h + 1))),
                                   ws[h])

    step(qi, 1, masked=True)

    @pl.when(qi % 2 == 1)
    def _():
        step(qi - 1, 1)

    pairs = qi // 2

    def body(i, carry):
        step(2 * (pairs - 1 - i), 2)
        return carry

    lax.fori_loop(0, pairs, body, 0)
    o_ref[...] = jnp.concatenate([acc_sc[h][...] for h in range(HEADS)], axis=0).T


def _attn_b_call(qT, kp, vT, after):
    B, nb, _, t = qT.shape
    S = nb * t
    return pl.pallas_call(
        functools.partial(_attn_b_kernel, t=t),
        grid=(B, nb),
        in_specs=[_q_spec(HW, t), _seq_rows_spec(S, HW), _seq_cols_spec(nb, GROUP_W, t),
                  _const_spec((t, t))],
        out_specs=pl.BlockSpec((None, t, GROUP_W), lambda b, i: (b, i, 0)),
        out_shape=jax.ShapeDtypeStruct((B, S, GROUP_W), F32),
        scratch_shapes=_attn_scratch(HEADS, t, HEAD_DIM),
        compiler_params=_params("arbitrary", "arbitrary"),
        name="attn_stick",
    )(qT, kp, vT, after)


def _attn_c_kernel(q_ref, k_ref, vx_ref, lamv_ref, subg_ref, o_ref, *scratch,
                   t, slopes, lambda_init):
    m_sc, acc_sc = scratch[:2 * HEADS], scratch[2 * HEADS:]
    qi = pl.program_id(1)
    key, query = _key_query_iota(t)
    causal = key <= query
    _softmax_init(m_sc, acc_sc)

    def step(j, nblk, masked=False):
        rows = _key_rows(j, nblk, t)
        heads = [slice(LANES * h, LANES * (h + 1)) for h in range(HEADS)]
        scores = [_dot(k_ref[rows, heads[g // 2]], q_ref[LANES * g:LANES * (g + 1), :])
                  for g in range(2 * HEADS)]
        for g in range(2 * HEADS):
            s = scores[g] + _lane_repeat(_alibi(slopes[g // 2], j, qi, t, nblk), t)
            scores[g] = jnp.where(causal, s, -jnp.inf) if masked else s
        _softmax_steps(scores, [_value_cols(vx_ref, j, nblk, heads[g // 2])
                                for g in range(2 * HEADS)], m_sc, acc_sc)

    step(qi, 1, masked=True)
    _sweep_earlier_blocks(qi, step)

    lv = lamv_ref[...]
    lam = (jnp.exp(jnp.sum(lv[0:1] * lv[1:2], axis=-1, keepdims=True))
           - jnp.exp(jnp.sum(lv[2:3] * lv[3:4], axis=-1, keepdims=True)) + lambda_init)
    outs = []
    for h in range(HEADS):
        o = (_softmax_result(acc_sc[2 * h][...])
             - lam * _softmax_result(acc_sc[2 * h + 1][...]))
        ms = jnp.mean(o * o, axis=0, keepdims=True)
        outs.append(o * lax.rsqrt(ms + RMS_EPS) * subg_ref[...] * (1.0 - lambda_init))
    o_ref[...] = jnp.concatenate(outs, axis=0).T


def _attn_c_call(qT, kp, vxT, lamv, subg_col, slopes, lambda_init):
    B, nb, _, t = qT.shape
    S = nb * t
    return pl.pallas_call(
        functools.partial(_attn_c_kernel, t=t, slopes=slopes, lambda_init=lambda_init),
        grid=(B, nb),
        in_specs=[_q_spec(2 * HW, t), _seq_rows_spec(S, HW), _seq_cols_spec(nb, HW, t),
                  _const_spec(lamv.shape), _const_spec(subg_col.shape)],
        out_specs=pl.BlockSpec((None, t, GROUP_W), lambda b, i: (b, i, 0)),
        out_shape=jax.ShapeDtypeStruct((B, S, GROUP_W), F32),
        scratch_shapes=_attn_scratch(2 * HEADS, t),
        compiler_params=_params("arbitrary", "arbitrary"),
        name="attn_diff",
    )(qT, kp, vxT, lamv, subg_col)


def _fold_rows(x, group=SUBLANES):
    return jnp.sum(x.reshape(x.shape[0] // group, group, x.shape[1]), axis=0)


PACKED_ROWS = 2 * SUBLANES
DIGIT_BITS = 8
N_DIGITS = 32 // DIGIT_BITS
DIGIT_MASK = (1 << DIGIT_BITS) - 1


def _fold_packed(x):
    slabs = [x[i * PACKED_ROWS:(i + 1) * PACKED_ROWS] for i in range(x.shape[0] // PACKED_ROWS)]
    while len(slabs) > 1:
        slabs = [a + b for a, b in zip(slabs[0::2], slabs[1::2])]
    return slabs[0]


def _attn_d_kernel(q_ref, iq_ref, iw_ref, dk_ref, dvx_ref, ik_ref, before_ref, o_ref,
                   keys_sc, tau_sc, *scratch, t, topk, slopes):
    digit_sc, scratch = scratch[:N_DIGITS], scratch[N_DIGITS:]
    m_sc, acc_sc = scratch[:HEADS], scratch[HEADS:]
    qi = pl.program_id(1)
    key, query = _key_query_iota(t)
    causal = key <= query

    w = iw_ref[...] * IDX_HEADS ** -0.5

    def index_step(j, nblk, masked=False):
        ikb = ik_ref[_key_rows(j, nblk, t), :]
        zs = [_dot(ikb, iq_ref[LANES * hh:LANES * (hh + 1), :]) for hh in range(IDX_HEADS)]
        sc = w[0:1, :] * jnp.maximum(zs[0], 0.0)
        for hh in range(1, IDX_HEADS):
            sc = sc + w[hh:hh + 1, :] * jnp.maximum(zs[hh], 0.0)
        sc = jnp.where(sc == 0.0, 0.0, sc)
        if masked:
            sc = jnp.where(causal, sc, NEG_INF)
        bits = pltpu.bitcast(sc, I32)
        keys = jnp.where(bits < 0, bits ^ 0x7FFFFFFF, bits)
        ukeys = keys ^ INT_MIN
        digits = [(lax.shift_right_logical(ukeys, DIGIT_BITS * (N_DIGITS - 1 - d)) & DIGIT_MASK)
                  .astype(F32).astype(BF16) for d in range(N_DIGITS)]
        for b in range(nblk):
            rows = slice(b * t, (b + 1) * t)
            keys_sc[j + b] = keys[rows]
            for d in range(N_DIGITS):
                digit_sc[d][j + b] = digits[d][rows]

    index_step(qi, 1, masked=True)
    _sweep_earlier_blocks(qi, index_step)

    qpos = qi * t + lax.broadcasted_iota(I32, (1, t), 1)
    kt = jnp.minimum(topk, qpos + 1).astype(F32)

    def count_ge(cand):
        def body(j, acc):
            return acc + _fold_rows((keys_sc[j] >= cand).astype(F32))
        acc = lax.fori_loop(0, qi + 1, body, jnp.zeros((SUBLANES, t), F32))
        return jnp.sum(acc, axis=0, keepdims=True)

    one_b = jnp.ones((), BF16)
    zero_b = jnp.zeros((), BF16)

    def count_digit_ge(vals_sc, cand):
        cand_b = cand.astype(F32).astype(BF16)

        def body(j, acc):
            ge = jnp.where(vals_sc[j] >= cand_b, one_b, zero_b)
            return acc + _fold_packed(ge).astype(F32)
        acc = lax.fori_loop(0, qi + 1, body, jnp.zeros((PACKED_ROWS, t), F32))
        return jnp.sum(acc, axis=0, keepdims=True)

    def keep_matching(vals_sc, match_sc, match):
        match_b = match.astype(F32).astype(BF16)

        def body(j, carry):
            vals_sc[j] = jnp.where(match_sc[j] == match_b, vals_sc[j], -one_b)
            return carry
        lax.fori_loop(0, qi + 1, body, 0)

    zero = jnp.zeros((1, t), I32)
    rank = kt
    above = jnp.zeros((1, t), F32)
    tau_u = zero
    digit = zero
    for d in range(N_DIGITS):
        if d > 0:
            keep_matching(digit_sc[d], digit_sc[d - 1], digit)

        def bit_body(i, prefix, d=d, rank=rank):
            cand = prefix + lax.shift_left(jnp.int32(1), DIGIT_BITS - 1 - i)
            return jnp.where(count_digit_ge(digit_sc[d], cand) >= rank, cand, prefix)

        digit = lax.fori_loop(0, DIGIT_BITS, bit_body, zero)
        tau_u = lax.shift_left(tau_u, DIGIT_BITS) | digit
        if d < N_DIGITS - 1:
            higher = count_digit_ge(digit_sc[d], digit + 1)
            above = above + higher
            rank = rank - higher
    tau = tau_u ^ INT_MIN
    tau_sc[...] = tau
    excess = jnp.max(above + count_digit_ge(digit_sc[N_DIGITS - 1], digit) - kt)

    @pl.when(excess > 0.0)
    def _():
        need = kt - count_ge(tau + 1)

        def tie_body(j, seen):
            kj = keys_sc[j]
            eq = kj == tau
            eqb = eq.astype(BF16)
            earlier = _dot(before_ref[...], eqb) + seen
            keys_sc[j] = jnp.where(eq & (earlier >= need), INT_MIN, kj)
            return seen + jnp.sum(eqb.astype(F32), axis=0, keepdims=True)

        lax.fori_loop(0, qi + 1, tie_body, jnp.zeros((1, t), F32))

    _softmax_init(m_sc, acc_sc)

    def step(j, nblk):
        keys = [keys_sc[j + b] for b in range(nblk)]
        sel = (keys[0] if nblk == 1 else jnp.concatenate(keys, axis=0)) >= tau_sc[...]
        kb = dk_ref[_key_rows(j, nblk, t), :]
        vx = _value_cols(dvx_ref, j, nblk, slice(None))
        scores = [_dot(kb, q_ref[LANES * h:LANES * (h + 1), :]) for h in range(HEADS)]
        for h in range(HEADS):
            bias = _lane_repeat(_alibi(slopes[h], j, qi, t, nblk), t)
            scores[h] = jnp.where(sel, scores[h] + bias, -jnp.inf)
        _softmax_steps(scores, [vx] * HEADS, m_sc, acc_sc)

    _sweep_earlier_blocks(qi + 1, step)
    o_ref[...] = jnp.concatenate([_softmax_result(acc_sc[h][...]) for h in range(HEADS)], axis=0).T


def _attn_d_call(qT, iqT, miscT, dkp, dvxT, ikp, before, topk, slopes):
    B, nb, _, t = qT.shape
    S = nb * t
    iw_block = MISC_IW // IDX_HEADS
    return pl.pallas_call(
        functools.partial(_attn_d_kernel, t=t, topk=topk, slopes=slopes),
        grid=(B, nb),
        in_specs=[_q_spec(HW, t), _q_spec(IDX_HEADS * LANES, t),
                  pl.BlockSpec((None, None, IDX_HEADS, t), lambda b, i: (b, i, iw_block, 0)),
                  _seq_rows_spec(S, LANES), _seq_cols_spec(nb, LANES, t), _seq_rows_spec(S, LANES),
                  _const_spec((t, t))],
        out_specs=pl.BlockSpec((None, t, GROUP_W), lambda b, i: (b, i, 0)),
        out_shape=jax.ShapeDtypeStruct((B, S, GROUP_W), F32),
        scratch_shapes=([pltpu.VMEM((nb, t, t), I32), pltpu.VMEM((1, t), I32)]
                        + [pltpu.VMEM((nb, t, t), BF16)] * N_DIGITS + _attn_scratch(HEADS, t)),
        compiler_params=_params("arbitrary", "arbitrary"),
        name="attn_sparse",
    )(qT, iqT, miscT, dkp, dvxT, ikp, before)


def _out_proj_kernel(x_ref, oa_ref, ob_ref, oc_ref, od_ref, beta_ref, wo_ref, gate_ref,
                     g2_ref, scale_ref, shift_ref, wr_ref, br_ref, earlier_ref,
                     x1_ref, h2x_ref, route_ref, cnt_ref, cnt_sc):
    d_model = x_ref.shape[1]
    acc = None
    for i, o_ref in enumerate((oa_ref, ob_ref, oc_ref, od_ref)):
        sl = slice(GROUP_W * i, GROUP_W * (i + 1))
        mix = (o_ref[...] * beta_ref[:, sl]).astype(BF16)
        part = jnp.dot(mix, wo_ref[sl, :], preferred_element_type=F32)
        acc = part if acc is None else acc + part
    x1 = x_ref[...] + gate_ref[...] * acc
    x1_ref[...] = x1
    ms = jnp.mean(x1 * x1, axis=-1, keepdims=True)
    h2 = x1 * lax.rsqrt(ms + RMS_EPS) * g2_ref[...]
    h2 = h2 * (1.0 + scale_ref[...]) + shift_ref[...]
    h2x_ref[:, :d_model] = h2

    h_hi, h_lo = _split_bf16(h2, 2)
    w_hi, w_lo = _split_bf16(wr_ref[...], 2)
    logits = _dot(h_hi, w_hi) + _dot(h_hi, w_lo) + _dot(h_lo, w_hi) + br_ref[...]
    lt = logits.T
    tm = lt.shape[1]
    g = lt[0:N_GROUPS]
    gmax = jnp.max(g, axis=0, keepdims=True)
    gi = lax.broadcasted_iota(I32, g.shape, 0)
    gidx = jnp.min(jnp.where(g == gmax, gi, N_GROUPS), axis=0, keepdims=True)
    g_prob = 1.0 / jnp.sum(jnp.exp(g - gmax), axis=0, keepdims=True)
    e_sel = jnp.zeros((EXPERTS_PER_GROUP, tm), F32)
    for gg in range(N_GROUPS):
        lo = N_GROUPS + EXPERTS_PER_GROUP * gg
        e_sel = e_sel + jnp.where(gidx == gg, lt[lo:lo + EXPERTS_PER_GROUP], 0.0)
    ei = lax.broadcasted_iota(I32, e_sel.shape, 0)
    v1 = jnp.max(e_sel, axis=0, keepdims=True)
    i1 = jnp.min(jnp.where(e_sel == v1, ei, EXPERTS_PER_GROUP), axis=0, keepdims=True)
    rest = jnp.where(ei == i1, -jnp.inf, e_sel)
    v2 = jnp.max(rest, axis=0, keepdims=True)
    i2 = jnp.min(jnp.where(rest == v2, ei, EXPERTS_PER_GROUP), axis=0, keepdims=True)
    e2 = jnp.exp(v2 - v1)
    w1 = g_prob / (1.0 + e2)
    w2 = g_prob * e2 / (1.0 + e2)
    in_group = jnp.where(ei == i1, w1, 0.0) + jnp.where(ei == i2, w2, 0.0)
    cw = jnp.concatenate([in_group, jnp.zeros((LANES - EXPERTS_PER_GROUP, tm), F32)], axis=0)
    h2x_ref[:, d_model:] = cw.T

    @pl.when(pl.program_id(0) == 0)
    def _():
        cnt_sc[...] = jnp.zeros(cnt_sc.shape, F32)

    rows = lax.broadcasted_iota(I32, (SUBLANES, tm), 0)
    onehot = (rows == gidx).astype(F32)
    seen = _dot(onehot.astype(BF16), earlier_ref[...]) + cnt_sc[...]
    rank = jnp.sum(onehot * seen, axis=0, keepdims=True)
    route_ref[...] = jnp.concatenate(
        [gidx, rank.astype(I32), jnp.zeros((SUBLANES - 2, tm), I32)], axis=0)
    cnt_sc[...] = cnt_sc[...] + jnp.sum(onehot, axis=1, keepdims=True)
    cnt_ref[...] = cnt_sc[:, :LANES]


def _out_proj_call(xf, outs, beta, w_out, gate1, g2, scale2, shift2, w_r, b_r, earlier, seq, tm):
    N, D = xf.shape
    per_b = seq // tm
    row = lambda i: (i, 0)
    full = lambda i: (0, 0)
    per_batch = pl.BlockSpec((None, 1, D), lambda i: (i // per_b, 0, 0))
    return pl.pallas_call(
        _out_proj_kernel,
        grid=(N // tm,),
        in_specs=([pl.BlockSpec((tm, D), row)] + [pl.BlockSpec((tm, GROUP_W), row)] * 4
                  + [pl.BlockSpec((1, D), full), pl.BlockSpec((D, D), full), per_batch,
                     pl.BlockSpec((1, D), full), per_batch, per_batch,
                     pl.BlockSpec((D, LANES), full), pl.BlockSpec((1, LANES), full),
                     pl.BlockSpec((tm, tm), full)]),
        out_specs=[pl.BlockSpec((tm, D), row), pl.BlockSpec((tm, D + LANES), row),
                   pl.BlockSpec((SUBLANES, tm), lambda i: (0, i)),
                   pl.BlockSpec((SUBLANES, LANES), full)],
        out_shape=[jax.ShapeDtypeStruct((N, D), F32), jax.ShapeDtypeStruct((N, D + LANES), F32),
                   jax.ShapeDtypeStruct((SUBLANES, N), I32),
                   jax.ShapeDtypeStruct((SUBLANES, LANES), F32)],
        scratch_shapes=[pltpu.VMEM((SUBLANES, tm), F32)],
        compiler_params=_params("arbitrary"),
        name="out_proj_router",
    )(xf, *outs, beta, w_out, gate1, g2, scale2, shift2, w_r, b_r, earlier)


MOE_TILE = 256
DMA_UNROLL = 8


def _row_copy(src_ref, src_row, dst_ref, dst_row, sem):
    return pltpu.make_async_copy(src_ref.at[pl.ds(src_row, 1), :],
                                 dst_ref.at[pl.ds(dst_row, 1), :], sem)


def _dispatch_kernel(slot_ref, h2x_ref, zeros_ref, xs_ref, sem):
    del zeros_ref
    rows = h2x_ref.shape[0]

    def issue(r, carry):
        _row_copy(h2x_ref, r, xs_ref, slot_ref[0, r], sem).start()
        return carry

    lax.fori_loop(0, rows, issue, 0, unroll=DMA_UNROLL)

    def drain(r, carry):
        _row_copy(h2x_ref, r, xs_ref, slot_ref[0, r], sem).wait()
        return carry

    lax.fori_loop(0, rows, drain, 0, unroll=DMA_UNROLL)


def _dispatch_call(slot, h2x, xs_zero, tm):
    N, DX = h2x.shape
    return pl.pallas_call(
        _dispatch_kernel,
        grid=(N // tm,),
        in_specs=[pl.BlockSpec((None, 1, tm), lambda i: (i, 0, 0), memory_space=pltpu.SMEM),
                  pl.BlockSpec((tm, DX), lambda i: (i, 0)),
                  pl.BlockSpec(memory_space=pl.ANY)],
        out_specs=pl.BlockSpec(memory_space=pl.ANY),
        out_shape=jax.ShapeDtypeStruct(xs_zero.shape, F32),
        scratch_shapes=[pltpu.SemaphoreType.DMA(())],
        input_output_aliases={2: 0},
        compiler_params=_params("arbitrary"),
        name="moe_dispatch",
    )(slot.reshape(N // tm, 1, tm), h2x, xs_zero)


def _expert_kernel(group_ref, valid_ref, xs_ref, w1f_ref, w3f_ref, w2f_ref, expand_ref, y_ref,
                   w1_ref, w3_ref, w2_ref):
    i = pl.program_id(0)
    valid = valid_ref[i]
    d_model = y_ref.shape[1]
    ff = w1_ref.shape[2]
    new_group = jnp.logical_or(i == 0, group_ref[i] != group_ref[jnp.maximum(i - 1, 0)])

    @pl.when(valid == 0)
    def _():
        y_ref[...] = jnp.zeros(y_ref.shape, F32)

    @pl.when(jnp.logical_and(valid > 0, new_group))
    def _():
        for e in range(EXPERTS_PER_GROUP):
            w1_ref[e] = w1f_ref[e].astype(BF16)
            w3_ref[e] = w3f_ref[e].astype(BF16)
            w2_ref[e] = w2f_ref[e].astype(BF16)

    @pl.when(valid > 0)
    def _():
        x = xs_ref[:, :d_model].astype(BF16)
        cw_hi, cw_lo = _split_bf16(xs_ref[:, d_model:], 2)
        cwx = _dot(cw_hi, expand_ref[...]) + _dot(cw_lo, expand_ref[...])
        acc = None
        for e in range(EXPERTS_PER_GROUP):
            a = _dot(x, w1_ref[e])
            b = _dot(x, w3_ref[e])
            hid = a * jax.nn.sigmoid(a) * b * cwx[:, ff * e:ff * (e + 1)]
            part = _dot(hid.astype(BF16), w2_ref[e])
            acc = part if acc is None else acc + part
        y_ref[...] = acc


def _expert_call(tile_group, tile_valid, xs, w1, w3, w2, expand, layer):
    P, DX = xs.shape
    D, FF = w1.shape[-2:]
    tr = MOE_TILE

    def group_spec(rows, cols):
        return pl.BlockSpec((None, None, EXPERTS_PER_GROUP, rows, cols),
                            lambda i, g, v: (layer, g[i], 0, 0, 0),
                            pipeline_mode=pl.Buffered(1))

    grid_spec = pltpu.PrefetchScalarGridSpec(
        num_scalar_prefetch=2,
        grid=(P // tr,),
        in_specs=[pl.BlockSpec((tr, DX), lambda i, g, v: (i, 0)),
                  group_spec(D, FF), group_spec(D, FF), group_spec(FF, D),
                  pl.BlockSpec(expand.shape, lambda i, g, v: (0, 0))],
        out_specs=pl.BlockSpec((tr, D), lambda i, g, v: (i, 0)),
        scratch_shapes=[pltpu.VMEM((EXPERTS_PER_GROUP, D, FF), BF16),
                        pltpu.VMEM((EXPERTS_PER_GROUP, D, FF), BF16),
                        pltpu.VMEM((EXPERTS_PER_GROUP, FF, D), BF16)],
    )
    return pl.pallas_call(
        _expert_kernel,
        grid_spec=grid_spec,
        out_shape=jax.ShapeDtypeStruct((P, D), F32),
        compiler_params=_params("arbitrary"),
        name="moe_experts",
    )(tile_group, tile_valid, xs, w1, w3, w2, expand)


def _residual_kernel(slot_ref, slot_next_ref, x1_ref, gate_ref, ys_ref, o_ref, buf, sems):
    i = pl.program_id(0)
    rows = x1_ref.shape[0]

    def gather(slots, b, wait):
        def body(r, carry):
            cp = pltpu.make_async_copy(ys_ref.at[pl.ds(slots[0, r], 1), :],
                                       buf.at[b, pl.ds(r, 1), :], sems.at[b])
            cp.wait() if wait else cp.start()
            return carry
        lax.fori_loop(0, rows, body, 0, unroll=DMA_UNROLL)

    @pl.when(i == 0)
    def _():
        gather(slot_ref, 0, False)

    @pl.when(i + 1 < pl.num_programs(0))
    def _():
        gather(slot_next_ref, (i + 1) % 2, False)

    gather(slot_ref, i % 2, True)
    o_ref[...] = x1_ref[...] + gate_ref[...] * buf[i % 2]


def _residual_call(slot, x1, gate2, ys, seq, tm):
    N, D = x1.shape
    n = N // tm
    per_b = seq // tm
    row = pl.BlockSpec((tm, D), lambda i: (i, 0))
    slot3 = slot.reshape(n, 1, tm)
    return pl.pallas_call(
        _residual_kernel,
        grid=(n,),
        in_specs=[pl.BlockSpec((None, 1, tm), lambda i: (i, 0, 0), memory_space=pltpu.SMEM),
                  pl.BlockSpec((None, 1, tm), lambda i: (jnp.minimum(i + 1, n - 1), 0, 0),
                               memory_space=pltpu.SMEM),
                  row, pl.BlockSpec((None, 1, D), lambda i: (i // per_b, 0, 0)),
                  pl.BlockSpec(memory_space=pl.ANY)],
        out_specs=row,
        out_shape=jax.ShapeDtypeStruct((N, D), F32),
        scratch_shapes=[pltpu.VMEM((2, tm, D), F32), pltpu.SemaphoreType.DMA((2,))],
        compiler_params=_params("arbitrary"),
        name="moe_residual",
    )(slot3, slot3, x1, gate2, ys)


def _moe_routing(route, cnt, n_tokens):
    tr = MOE_TILE
    n_tiles = n_tokens // tr + N_GROUPS
    counts = cnt[:N_GROUPS, 0].astype(I32)
    padded = (counts + tr - 1) // tr * tr
    ends = jnp.cumsum(padded)
    starts = ends - padded
    group, rank = route[0], route[1]
    slot = starts[group] + rank
    tile_start = jnp.arange(n_tiles, dtype=I32) * tr
    tile_group = jnp.minimum(jnp.sum((tile_start[:, None] >= ends[None, :]).astype(I32), axis=1),
                             N_GROUPS - 1)
    tile_valid = jnp.clip(starts[tile_group] + counts[tile_group] - tile_start, 0, tr)
    tile_valid = jnp.where(tile_start < ends[-1], tile_valid, 0)
    return slot, tile_group, tile_valid


def _block_diag_mean(width, group, valid_in_128=None):
    i = jnp.arange(width)
    same = (i[:, None] // group) == (i[None, :] // group)
    if valid_in_128 is not None:
        same = same & ((i[:, None] % LANES) < valid_in_128) & ((i[None, :] % LANES) < valid_in_128)
    return jnp.where(same, 1.0 / group, 0.0).astype(BF16)


def _placement(src_width, dst_width, pairs):
    src = jnp.array([p[0] for p in pairs], I32)
    dst = jnp.array([p[1] for p in pairs], I32)
    return jnp.zeros((src_width, dst_width), F32).at[src, dst].set(1.0).astype(BF16)


def _placements():
    per_head = [(HEAD_DIM * h + d, LANES * h + d) for h in range(HEADS) for d in range(HEAD_DIM)]
    diff_q = [(HEAD_DIM * h + DIFF_DIM * c + d, LANES * (2 * h + c) + DIFF_DIM * c + d)
              for h in range(HEADS) for c in range(2) for d in range(DIFF_DIM)]
    idx_q = [(IDX_DIM * hh + d, LANES * hh + d) for hh in range(IDX_HEADS) for d in range(IDX_DIM)]
    ident = [(d, d) for d in range(GROUP_W)]
    first64 = [(d, d) for d in range(HEAD_DIM)]
    second64 = [(HEAD_DIM + d, d) for d in range(HEAD_DIM)]
    idx_k = [(MISC_IK + d, d) for d in range(IDX_DIM)]
    table = {
        "a_qT": (GROUP_W, per_head), "a_kp": (GROUP_W, per_head), "a_vxT": (GROUP_W, per_head),
        "b_qT": (GROUP_W, per_head), "b_kp": (GROUP_W, per_head), "b_vT": (GROUP_W, ident),
        "c_qT": (GROUP_W, diff_q), "c_kp": (GROUP_W, per_head), "c_vxT": (GROUP_W, per_head),
        "d_qT": (GROUP_W, per_head), "i_qT": (GROUP_W, idx_q),
        "d_kp": (GROUP_W, first64), "d_vxT": (GROUP_W, second64), "i_kp": (LANES, idx_k),
    }
    out = []
    for name, src, width, orient, ones in OUTPUTS:
        src_width, pairs = table[name]
        p = _placement(src_width, width, pairs)
        out.append(p if orient == "rows" else p.T)
    return out


def kernel(x, c, ada_w, ada_b, norm1_g, norm2_g, w_in, b_f, qn_a, kn_a, qn_c, kn_c,
           lam_q1, lam_k1, lam_q2, lam_k2, subln_g, qn_d, kn_d, mix_beta, w_out,
           w_group, b_group, w_expert, b_expert, w1, w3, w2):
    B, S, D = x.shape
    L = ada_w.shape[0]
    N = B * S
    topk = min(TOPK_MAX, S // 4)
    t = ATTN_BLOCK
    tm = t
    slopes = [2.0 ** (-8.0 * i / (2 * HEADS)) for i in range(1, 2 * HEADS + 1)]
    slopes_c, slopes_d = tuple(slopes[0::2]), tuple(slopes[1::2])

    idx_t = jnp.arange(t)
    after = (idx_t[None, :] > idx_t[:, None]).astype(BF16)
    before = (idx_t[None, :] < idx_t[:, None]).astype(BF16)
    upto = (idx_t[None, :] <= idx_t[:, None]).astype(BF16)
    g64 = _block_diag_mean(GROUP_W, HEAD_DIM)
    g32 = _block_diag_mean(GROUP_W, DIFF_DIM)
    places = _placements()
    lane = jnp.arange(LANES)
    cum_sel = jnp.stack([jnp.broadcast_to((lane == MISC_AF + h)[:, None], (LANES, LANES))
                         for h in range(HEADS)]).astype(BF16)
    ff_col = jnp.arange(EXPERTS_PER_GROUP * EXPERT_FF)
    expand = (lane[:, None] == ff_col[None, :] // EXPERT_FF).astype(BF16)

    mod = _ada_call(c, ada_w, ada_b)
    xf = x.reshape(N, D)

    for l in range(L):
        m6 = mod[l].reshape(B, 6, 1, D)
        shift1, scale1, gate1, shift2, scale2, gate2 = (m6[:, i] for i in range(6))

        w_all = w_in[l].astype(BF16)
        ones = jnp.ones((GROUP_W - HEAD_DIM,), F32)
        gains = jnp.stack([jnp.tile(qn_a[l], HEADS) * (HEAD_DIM ** -0.5 * LOG2E),
                           jnp.tile(kn_a[l], HEADS),
                           jnp.tile(qn_c[l], 2 * HEADS) * (DIFF_DIM ** -0.5 * LOG2E),
                           jnp.tile(kn_c[l], 2 * HEADS),
                           jnp.tile(qn_d[l], HEADS) * (HEAD_DIM ** -0.5 * LOG2E),
                           jnp.concatenate([kn_d[l], ones])]).astype(F32)
        gains = jnp.concatenate([gains, jnp.zeros((2, GROUP_W), F32)], axis=0)

        outs = _in_proj_call(xf, scale1, shift1, norm1_g[l].reshape(1, D), w_all, gains,
                             g64, g32, places, B, S, tm)
        sec = {name: o for (name, _, _, _, _), o in zip(OUTPUTS, outs)}
        for name, src, width, orient, ones_ in OUTPUTS:
            if orient == "rows":
                sec[name] = sec[name].reshape(B, S, width)
        misc = outs[len(OUTPUTS)].reshape(B, S, LANES)
        miscT = outs[len(OUTPUTS) + 1]

        bf_row = jnp.zeros((1, LANES), F32).at[0, MISC_AF:MISC_AF + HEADS].set(b_f[l].astype(F32))
        cumrep = _cum_call(misc, bf_row, upto, cum_sel)

        o_a = _attn_a_call(sec["a_qT"], sec["a_kp"], sec["a_vxT"], cumrep)
        o_b = _attn_b_call(sec["b_qT"], sec["b_kp"], sec["b_vT"], after)
        lambda_init = 0.8 - 0.6 * math.exp(-0.3 * l)
        lamv = jnp.stack([lam_q1[l], lam_k1[l], lam_q2[l], lam_k2[l]]).astype(F32)
        o_c = _attn_c_call(sec["c_qT"], sec["c_kp"], sec["c_vxT"], lamv,
                           subln_g[l].reshape(HEAD_DIM, 1).astype(F32), slopes_c, lambda_init)
        o_d = _attn_d_call(sec["d_qT"], sec["i_qT"], miscT, sec["d_kp"], sec["d_vxT"], sec["i_kp"],
                           before, topk, slopes_d)

        w_r = jnp.concatenate([w_group[l], w_expert[l],
                               jnp.zeros((D, LANES - N_GROUPS - N_EXPERTS), F32)], axis=1)
        b_r = jnp.concatenate([b_group[l], b_expert[l],
                               jnp.zeros((LANES - N_GROUPS - N_EXPERTS,), F32)]).reshape(1, LANES)
        x1, h2x, route, cnt = _out_proj_call(
            xf, [o.reshape(N, GROUP_W) for o in (o_a, o_b, o_c, o_d)], mix_beta[l].reshape(1, D),
            w_out[l].astype(BF16), gate1, norm2_g[l].reshape(1, D), scale2, shift2, w_r, b_r,
            after, S, tm)

        slot, tile_group, tile_valid = _moe_routing(route, cnt, N)
        xs = _dispatch_call(slot, h2x, jnp.zeros((tile_group.shape[0] * MOE_TILE, D + LANES), F32), tm)
        ys = _expert_call(tile_group, tile_valid, xs, w1, w3, w2, expand, l)
        xf = _residual_call(slot, x1, gate2, ys, S, tm)

    return xf.reshape(B, S, D)
```

```python
import functools
import math

import jax
import jax.numpy as jnp
from jax import lax
from jax.experimental import pallas as pl
from jax.experimental.pallas import tpu as pltpu

F32 = jnp.float32
BF16 = jnp.bfloat16
I32 = jnp.int32

HEAD_DIM = 64
HEADS = 4
GROUP_W = HEADS * HEAD_DIM
DIFF_DIM = HEAD_DIM // 2
IDX_HEADS = 8
IDX_DIM = 32
TOPK_MAX = 256
N_GROUPS = 4
EXPERTS_PER_GROUP = 8
N_EXPERTS = N_GROUPS * EXPERTS_PER_GROUP
EXPERT_FF = 256
RMS_EPS = 1e-6
NEG_INF = -1e30
INT_MIN = -(2 ** 31)
LOG2E = math.log2(math.e)

LANES = 128
SUBLANES = 8
ATTN_BLOCK = 256
VMEM_LIMIT = 56 * 1024 * 1024

MISC_IK = 0
MISC_IW = IDX_DIM
MISC_AF = IDX_DIM + IDX_HEADS


def _params(*sem):
    return pltpu.CompilerParams(dimension_semantics=sem, vmem_limit_bytes=VMEM_LIMIT)


def _log_sigmoid(z):
    return jnp.minimum(z, 0.0) - jnp.log1p(jnp.exp(-jnp.abs(z)))


def _log2_sigmoid(z2):
    return jnp.minimum(z2, 0.0) - jnp.log2(1.0 + jnp.exp2(-jnp.abs(z2)))


def _split_bf16(x, parts):
    out = []
    rem = x
    for _ in range(parts):
        p = rem.astype(BF16)
        out.append(p)
        rem = rem - p.astype(F32)
    return out


def _dot(a, b):
    return jnp.dot(a, b, preferred_element_type=F32)


def _ada_kernel(c_ref, w_ref, b_ref, o_ref):
    c = c_ref[...]
    ca = c * jax.nn.sigmoid(c)
    o_ref[...] = jnp.dot(ca, w_ref[...], precision=lax.Precision.HIGHEST,
                         preferred_element_type=F32) + b_ref[...]


def _ada_call(c, ada_w, ada_b):
    L, D, E = ada_w.shape
    B = c.shape[0]
    tn = 1536
    return pl.pallas_call(
        _ada_kernel,
        grid=(L, E // tn),
        in_specs=[pl.BlockSpec((B, D), lambda l, j: (0, 0)),
                  pl.BlockSpec((None, D, tn), lambda l, j: (l, 0, j)),
                  pl.BlockSpec((None, 1, tn), lambda l, j: (l, 0, j))],
        out_specs=pl.BlockSpec((None, B, tn), lambda l, j: (l, 0, j)),
        out_shape=jax.ShapeDtypeStruct((L, B, E), F32),
        compiler_params=_params("arbitrary", "arbitrary"),
        name="ada_mod",
    )(c, ada_w, ada_b.reshape(L, 1, E))


SOURCES = ("a_q", "a_k", "a_v", "b_q", "b_k", "b_v", "c_q", "c_k", "c_v", "d_q", "i_q", "d_kv")
SRC_NORM = {"a_q": ("n64", 0), "a_k": ("n64", 1), "c_q": ("n32", 2), "c_k": ("n32", 3),
            "d_q": ("n64", 4), "d_kv": ("n64", 5)}
SRC_SCALE = {"b_q": HEAD_DIM ** -0.5 * LOG2E, "i_q": IDX_DIM ** -0.5}
HW = HEADS * LANES
OUTPUTS = (
    ("a_qT", "a_q", HW, "cols", False), ("a_kp", "a_k", HW, "rows", False),
    ("a_vxT", "a_v", HW, "cols", True),
    ("b_qT", "b_q", HW, "cols", False), ("b_kp", "b_k", HW, "rows", False),
    ("b_vT", "b_v", GROUP_W, "cols", False),
    ("c_qT", "c_q", 2 * HW, "cols", False), ("c_kp", "c_k", HW, "rows", False),
    ("c_vxT", "c_v", HW, "cols", True),
    ("d_qT", "d_q", HW, "cols", False), ("i_qT", "i_q", IDX_HEADS * LANES, "cols", False),
    ("d_kp", "d_kv", LANES, "rows", False), ("d_vxT", "d_kv", LANES, "cols", True),
    ("i_kp", "misc", LANES, "rows", False),
)
N_MAIN = len(SOURCES) * GROUP_W

IN_SPLITS = (GROUP_W, GROUP_W, GROUP_W, HEADS, GROUP_W, GROUP_W, GROUP_W, GROUP_W, GROUP_W,
             GROUP_W, GROUP_W, HEAD_DIM, HEAD_DIM, IDX_HEADS * IDX_DIM, IDX_DIM, IDX_HEADS)
IN_NAMES = ("a_q", "a_k", "a_v", "a_f", "b_q", "b_k", "b_v", "c_q", "c_k", "c_v",
            "d_q", "d_k", "d_v", "i_q", "i_k", "i_w")
IN_OFFSETS = {n: (sum(IN_SPLITS[:i]), IN_SPLITS[i]) for i, n in enumerate(IN_NAMES)}
P_IN = sum(IN_SPLITS)
SECTION_PARTS = {name: (name,) for name in SOURCES if name != "d_kv"}
SECTION_PARTS["d_kv"] = ("d_k", "d_v")
SECTION_PARTS["misc"] = ("i_k", "i_w", "a_f")
REALIGN_ROWS = 256


def _realign_weights(w_ref, w_sc):
    d_model = w_ref.shape[0]
    for r0 in range(0, d_model, REALIGN_ROWS):
        rows = slice(r0, r0 + REALIGN_ROWS)
        for i, name in enumerate(SOURCES + ("misc",)):
            width = LANES if name == "misc" else GROUP_W
            pieces, used = [], 0
            for part in SECTION_PARTS[name]:
                off, w = IN_OFFSETS[part]
                base = off // LANES * LANES
                end = min(-(-(off + w) // LANES) * LANES, P_IN)
                window = w_ref[rows, base:end]
                pieces.append(window[:, off - base:off - base + w])
                used += w
            if used < width:
                pieces.append(jnp.zeros((REALIGN_ROWS, width - used), BF16))
            block = pieces[0] if len(pieces) == 1 else jnp.concatenate(pieces, axis=1)
            w_sc[rows, i * GROUP_W:i * GROUP_W + width] = block


def _in_proj_kernel(*refs):
    (x_ref, scale_ref, shift_ref, g1_ref, w_in_ref, gains_ref, g64_ref, g32_ref) = refs[:8]
    place_refs = refs[8:8 + len(OUTPUTS)]
    out_refs = refs[8 + len(OUTPUTS):-1]
    w_ref = refs[-1]

    @pl.when(pl.program_id(0) == 0)
    def _():
        _realign_weights(w_in_ref, w_ref)

    x = x_ref[...]
    ms = jnp.mean(x * x, axis=-1, keepdims=True)
    h = x * lax.rsqrt(ms + RMS_EPS) * g1_ref[...]
    h = h * (1.0 + scale_ref[...]) + shift_ref[...]
    hb = h.astype(BF16)

    wm = w_ref[:, N_MAIN:N_MAIN + LANES]
    h_hi, h_lo = _split_bf16(h, 2)
    misc = _dot(h_hi, wm) + _dot(h_lo, wm)
    out_refs[len(OUTPUTS)][...] = misc
    out_refs[len(OUTPUTS) + 1][...] = misc.T

    compact = {"misc": misc.astype(BF16)}
    raw = {name: _dot(hb, w_ref[:, i * GROUP_W:(i + 1) * GROUP_W])
           for i, name in enumerate(SOURCES)}
    msqs = {name: _dot((raw[name] * raw[name]).astype(BF16),
                       (g64_ref if SRC_NORM[name][0] == "n64" else g32_ref)[...])
            for name in SOURCES if name in SRC_NORM}
    for name in SOURCES:
        sec = raw[name]
        if name in SRC_NORM:
            r = SRC_NORM[name][1]
            fac = lax.rsqrt(msqs[name] + RMS_EPS)
            if name == "d_kv":
                lane = lax.broadcasted_iota(I32, sec.shape, 1)
                fac = jnp.where(lane < HEAD_DIM, fac, 1.0)
            sec = sec * fac * gains_ref[r:r + 1, :]
        elif name in SRC_SCALE:
            sec = sec * SRC_SCALE[name]
        compact[name] = sec.astype(BF16)

    for (name, src, width, orient, ones), p_ref, o_ref in zip(OUTPUTS, place_refs, out_refs):
        if orient == "rows":
            o_ref[...] = _dot(compact[src], p_ref[...]).astype(BF16)
        else:
            res = lax.dot_general(p_ref[...], compact[src], (((1,), (1,)), ((), ())),
                                  preferred_element_type=F32)
            if ones:
                r = lax.broadcasted_iota(I32, res.shape, 0)
                res = jnp.where((r & (LANES - 1)) >= HEAD_DIM, 1.0, res)
            o_ref[...] = res.astype(BF16)


def _in_proj_call(xf, scale1, shift1, g1, w_all, gains, g64, g32, places, batch, seq, tm):
    N, D = xf.shape
    per_b = seq // tm
    row = lambda i: (i, 0)
    full = lambda i: (0, 0)
    col4 = lambda i: (i // per_b, i % per_b, 0, 0)
    out_shape, out_specs = [], []
    for name, src, width, orient, ones in OUTPUTS:
        if orient == "rows":
            out_shape.append(jax.ShapeDtypeStruct((N, width), BF16))
            out_specs.append(pl.BlockSpec((tm, width), row))
        else:
            out_shape.append(jax.ShapeDtypeStruct((batch, per_b, width, tm), BF16))
            out_specs.append(pl.BlockSpec((None, None, width, tm), col4))
    out_shape += [jax.ShapeDtypeStruct((N, LANES), F32),
                  jax.ShapeDtypeStruct((batch, per_b, LANES, tm), F32)]
    out_specs += [pl.BlockSpec((tm, LANES), row), pl.BlockSpec((None, None, LANES, tm), col4)]
    return pl.pallas_call(
        _in_proj_kernel,
        grid=(N // tm,),
        in_specs=([pl.BlockSpec((tm, D), row),
                   pl.BlockSpec((None, 1, D), lambda i: (i // per_b, 0, 0)),
                   pl.BlockSpec((None, 1, D), lambda i: (i // per_b, 0, 0)),
                   pl.BlockSpec((1, D), full),
                   pl.BlockSpec(w_all.shape, full),
                   pl.BlockSpec(gains.shape, full),
                   pl.BlockSpec(g64.shape, full),
                   pl.BlockSpec(g32.shape, full)]
                  + [pl.BlockSpec(p.shape, full) for p in places]),
        out_specs=out_specs,
        out_shape=out_shape,
        scratch_shapes=[pltpu.VMEM((D, N_MAIN + LANES), BF16)],
        compiler_params=_params("arbitrary"),
        name="in_proj",
    )(xf, scale1, shift1, g1, w_all, gains, g64, g32, *places)


def _cum_kernel(misc_ref, bf_ref, tri_ref, sel_ref, o_ref, *, blk):
    seq = misc_ref.shape[0]
    carry = jnp.zeros((1, LANES), F32)
    tri = tri_ref[...]
    for j in range(seq // blk):
        rows = slice(j * blk, (j + 1) * blk)
        lf = _log_sigmoid(misc_ref[rows, :] + bf_ref[...])
        c = carry
        for part in _split_bf16(lf, 3):
            c = c + _dot(tri, part)
        carry = c[blk - 1:blk, :]
        parts = _split_bf16(c, 3)
        for h in range(HEADS):
            rep = _dot(parts[0], sel_ref[h]) + _dot(parts[1], sel_ref[h]) + _dot(parts[2], sel_ref[h])
            o_ref[h, rows, :] = rep * LOG2E


def _cum_call(misc, bf_row, tri, sel):
    B, S, _ = misc.shape
    blk = tri.shape[0]
    return pl.pallas_call(
        functools.partial(_cum_kernel, blk=blk),
        grid=(B,),
        in_specs=[pl.BlockSpec((None, S, LANES), lambda b: (b, 0, 0)),
                  pl.BlockSpec((1, LANES), lambda b: (0, 0)),
                  pl.BlockSpec((blk, blk), lambda b: (0, 0)),
                  pl.BlockSpec((HEADS, LANES, LANES), lambda b: (0, 0, 0))],
        out_specs=pl.BlockSpec((None, HEADS, S, LANES), lambda b: (b, 0, 0, 0)),
        out_shape=jax.ShapeDtypeStruct((B, HEADS, S, LANES), F32),
        compiler_params=_params("arbitrary"),
        name="forget_cumsum",
    )(misc, bf_row, tri, sel)


def _q_spec(width, t):
    return pl.BlockSpec((None, None, width, t), lambda b, i: (b, i, 0, 0))


def _seq_rows_spec(seq, width):
    return pl.BlockSpec((None, seq, width), lambda b, i: (b, 0, 0))


def _seq_cols_spec(nb, width, t):
    return pl.BlockSpec((None, nb, width, t), lambda b, i: (b, 0, 0, 0))


def _const_spec(shape):
    return pl.BlockSpec(shape, lambda b, i: (0,) * len(shape))


def _lane_repeat(x, t):
    return jnp.concatenate([x] * (t // LANES), axis=1)


def _softmax_steps(scores, values, m_refs, acc_refs):
    probs, alphas = [], []
    for s, m_ref in zip(scores, m_refs):
        m_old = m_ref[...]
        m_new = jnp.maximum(m_old, jnp.max(s, axis=0, keepdims=True))
        alphas.append(jnp.exp2(m_old - m_new))
        probs.append(jnp.exp2(s - m_new).astype(BF16))
        m_ref[...] = m_new
    for p, vx, alpha, acc_ref in zip(probs, values, alphas, acc_refs):
        acc_ref[...] = alpha * acc_ref[...] + _dot(vx, p)


def _softmax_init(m_scs, acc_scs):
    for m_sc, acc_sc in zip(m_scs, acc_scs):
        m_sc[...] = jnp.full(m_sc.shape, NEG_INF, F32)
        acc_sc[...] = jnp.zeros(acc_sc.shape, F32)


def _softmax_result(acc):
    return acc[:HEAD_DIM] / acc[HEAD_DIM:]


def _alibi(slope, j, qi, t, nblk=1):
    key = lax.broadcasted_iota(I32, (nblk * t, LANES), 0)
    return (slope * LOG2E) * (key + (j - qi) * t).astype(F32)


def _sweep_earlier_blocks(qi, step):
    def body(i, carry):
        step(2 * i, 2)
        return carry

    lax.fori_loop(0, qi // 2, body, 0)

    @pl.when(qi % 2 == 1)
    def _():
        step(qi - 1, 1)


def _sweep_causal_blocks(qi, step):
    @pl.when(qi == 0)
    def _():
        step(0, 1, True)

    @pl.when(qi > 0)
    def _():
        step(qi - 1, 2, True)

    _sweep_earlier_blocks(jnp.maximum(qi - 1, 0), step)


def _causal_mask(t, nblk, strict=False):
    key = lax.broadcasted_iota(I32, (nblk * t, t), 0) - (nblk - 1) * t
    query = lax.broadcasted_iota(I32, (nblk * t, t), 1)
    return key < query if strict else key <= query


def _key_rows(j, nblk, t):
    return pl.ds(pl.multiple_of(j * t, t), nblk * t)


def _value_cols(vx_ref, j, nblk, rows):
    tiles = [vx_ref[j + b, rows, :] for b in range(nblk)]
    return tiles[0] if nblk == 1 else jnp.concatenate(tiles, axis=1)


def _attn_a_kernel(q_ref, k_ref, vx_ref, cum_ref, o_ref, *scratch, t):
    m_sc, acc_sc = scratch[:HEADS], scratch[HEADS:]
    qi = pl.program_id(1)
    _softmax_init(m_sc, acc_sc)

    def step(j, nblk, masked=False):
        rows = _key_rows(j, nblk, t)
        heads = [slice(LANES * h, LANES * (h + 1)) for h in range(HEADS)]
        scores = [_dot(k_ref[rows, hs], q_ref[hs, :]) for hs in heads]
        for h in range(HEADS):
            s = scores[h] - _lane_repeat(cum_ref[h, rows, :], t)
            scores[h] = jnp.where(_causal_mask(t, nblk), s, -jnp.inf) if masked else s
        _softmax_steps(scores, [_value_cols(vx_ref, j, nblk, hs) for hs in heads], m_sc, acc_sc)

    _sweep_causal_blocks(qi, step)
    o_ref[...] = jnp.concatenate([_softmax_result(acc_sc[h][...]) for h in range(HEADS)], axis=0).T


def _attn_scratch(n, t, rows=LANES):
    return [pltpu.VMEM((1, t), F32)] * n + [pltpu.VMEM((rows, t), F32)] * n


def _attn_a_call(qT, kp, vxT, cumrep):
    B, nb, _, t = qT.shape
    S = nb * t
    return pl.pallas_call(
        functools.partial(_attn_a_kernel, t=t),
        grid=(B, nb),
        in_specs=[_q_spec(HW, t), _seq_rows_spec(S, HW), _seq_cols_spec(nb, HW, t),
                  pl.BlockSpec((None, HEADS, S, LANES), lambda b, i: (b, 0, 0, 0))],
        out_specs=pl.BlockSpec((None, t, GROUP_W), lambda b, i: (b, i, 0)),
        out_shape=jax.ShapeDtypeStruct((B, S, GROUP_W), F32),
        scratch_shapes=_attn_scratch(HEADS, t),
        compiler_params=_params("arbitrary", "arbitrary"),
        name="attn_forget",
    )(qT, kp, vxT, cumrep)


def _attn_b_kernel(q_ref, k_ref, v_ref, after_ref, o_ref, *scratch, t):
    r_sc, acc_sc = scratch[:HEADS], scratch[HEADS:]
    qi = pl.program_id(1)
    for h in range(HEADS):
        r_sc[h][...] = jnp.zeros(r_sc[h].shape, F32)
        acc_sc[h][...] = jnp.zeros(acc_sc[h].shape, F32)

    def step(j, nblk, masked=False):
        after = after_ref[...]
        rows = _key_rows(j, nblk, t)
        heads = [slice(LANES * h, LANES * (h + 1)) for h in range(HEADS)]
        zs = [_dot(k_ref[rows, hs], q_ref[hs, :]) for hs in heads]
        lbs, splits, later_sums = [], [], []
        for h in range(HEADS):
            lb = _log2_sigmoid(zs[h])
            lm = lb - zs[h]
            if masked:
                lm = jnp.where(_causal_mask(t, nblk, strict=True), lm, 0.0)
            blocks = [lm[b * t:(b + 1) * t] for b in range(nblk)]
            splits.append([_split_bf16(blk, 2) for blk in blocks])
            sums = [jnp.sum(blk, axis=0, keepdims=True) for blk in blocks]
            r_old = r_sc[h][...]
            total = sums[0]
            for s_ in sums[1:]:
                total = total + s_
            r_sc[h][...] = r_old + total
            lbs.append(lb + r_old)
            later, run = [], None
            for b in reversed(range(nblk)):
                later.append(run)
                run = sums[b] if run is None else run + sums[b]
            later_sums.append(later[::-1])
        suffixes = [[_dot(after, hi) + _dot(after, lo) for hi, lo in splits[h]]
                    for h in range(HEADS)]
        ws = []
        for h in range(HEADS):
            parts = [suffixes[h][b] if later_sums[h][b] is None else suffixes[h][b] + later_sums[h][b]
                     for b in range(nblk)]
            suffix = parts[0] if nblk == 1 else jnp.concatenate(parts, axis=0)
            w = jnp.exp2(lbs[h] + suffix)
            if masked:
                w = jnp.where(_causal_mask(t, nblk, strict=True), w, 0.0)
            ws.append(w.astype(BF16))
        for h in range(HEADS):
            acc_sc[h][...] += _dot(_value_cols(v_ref, j, nblk, slice(HEAD_DIM * h, HEAD_DIM * (h + 1))),
                                   ws[h])

    @pl.when(qi == 0)
    def _():
        step(0, 1, True)

    @pl.when(qi > 0)
    def _():
        step(qi - 1, 2, True)

    rest = jnp.maximum(qi - 1, 0)

    @pl.when(rest % 2 == 1)
    def _():
        step(rest - 1, 1)

    pairs = rest // 2

    def body(i, carry):
        step(2 * (pairs - 1 - i), 2)
        return carry

    lax.fori_loop(0, pairs, body, 0)
    o_ref[...] = jnp.concatenate([acc_sc[h][...] for h in range(HEADS)], axis=0).T


def _attn_b_call(qT, kp, vT, after):
    B, nb, _, t = qT.shape
    S = nb * t
    return pl.pallas_call(
        functools.partial(_attn_b_kernel, t=t),
        grid=(B, nb),
        in_specs=[_q_spec(HW, t), _seq_rows_spec(S, HW), _seq_cols_spec(nb, GROUP_W, t),
                  _const_spec((t, t))],
        out_specs=pl.BlockSpec((None, t, GROUP_W), lambda b, i: (b, i, 0)),
        out_shape=jax.ShapeDtypeStruct((B, S, GROUP_W), F32),
        scratch_shapes=_attn_scratch(HEADS, t, HEAD_DIM),
        compiler_params=_params("arbitrary", "arbitrary"),
        name="attn_stick",
    )(qT, kp, vT, after)


def _attn_c_kernel(q_ref, k_ref, vx_ref, lamv_ref, subg_ref, o_ref, *scratch,
                   t, slopes, lambda_init):
    m_sc, acc_sc = scratch[:2 * HEADS], scratch[2 * HEADS:]
    qi = pl.program_id(1)
    _softmax_init(m_sc, acc_sc)

    def step(j, nblk, masked=False):
        rows = _key_rows(j, nblk, t)
        heads = [slice(LANES * h, LANES * (h + 1)) for h in range(HEADS)]
        scores = [_dot(k_ref[rows, heads[g // 2]], q_ref[LANES * g:LANES * (g + 1), :])
                  for g in range(2 * HEADS)]
        for g in range(2 * HEADS):
            s = scores[g] + _lane_repeat(_alibi(slopes[g // 2], j, qi, t, nblk), t)
            scores[g] = jnp.where(_causal_mask(t, nblk), s, -jnp.inf) if masked else s
        _softmax_steps(scores, [_value_cols(vx_ref, j, nblk, heads[g // 2])
                                for g in range(2 * HEADS)], m_sc, acc_sc)

    _sweep_causal_blocks(qi, step)

    lv = lamv_ref[...]
    lam = (jnp.exp(jnp.sum(lv[0:1] * lv[1:2], axis=-1, keepdims=True))
           - jnp.exp(jnp.sum(lv[2:3] * lv[3:4], axis=-1, keepdims=True)) + lambda_init)
    outs = []
    for h in range(HEADS):
        o = (_softmax_result(acc_sc[2 * h][...])
             - lam * _softmax_result(acc_sc[2 * h + 1][...]))
        ms = jnp.mean(o * o, axis=0, keepdims=True)
        outs.append(o * lax.rsqrt(ms + RMS_EPS) * subg_ref[...] * (1.0 - lambda_init))
    o_ref[...] = jnp.concatenate(outs, axis=0).T


def _attn_c_call(qT, kp, vxT, lamv, subg_col, slopes, lambda_init):
    B, nb, _, t = qT.shape
    S = nb * t
    return pl.pallas_call(
        functools.partial(_attn_c_kernel, t=t, slopes=slopes, lambda_init=lambda_init),
        grid=(B, nb),
        in_specs=[_q_spec(2 * HW, t), _seq_rows_spec(S, HW), _seq_cols_spec(nb, HW, t),
                  _const_spec(lamv.shape), _const_spec(subg_col.shape)],
        out_specs=pl.BlockSpec((None, t, GROUP_W), lambda b, i: (b, i, 0)),
        out_shape=jax.ShapeDtypeStruct((B, S, GROUP_W), F32),
        scratch_shapes=_attn_scratch(2 * HEADS, t),
        compiler_params=_params("arbitrary", "arbitrary"),
        name="attn_diff",
    )(qT, kp, vxT, lamv, subg_col)


def _fold_rows(x, group=SUBLANES):
    return jnp.sum(x.reshape(x.shape[0] // group, group, x.shape[1]), axis=0)


PACKED_ROWS = 2 * SUBLANES
DIGIT_BITS = 8
N_DIGITS = 32 // DIGIT_BITS
DIGIT_MASK = (1 << DIGIT_BITS) - 1


def _fold_packed(x):
    slabs = [x[i * PACKED_ROWS:(i + 1) * PACKED_ROWS] for i in range(x.shape[0] // PACKED_ROWS)]
    while len(slabs) > 1:
        slabs = [a + b for a, b in zip(slabs[0::2], slabs[1::2])]
    return slabs[0]


def _attn_d_kernel(q_ref, iq_ref, iw_ref, dk_ref, dvx_ref, ik_ref, before_ref, o_ref,
                   keys_sc, tau_sc, *scratch, t, topk, slopes):
    digit_sc, scratch = scratch[:N_DIGITS], scratch[N_DIGITS:]
    m_sc, acc_sc = scratch[:HEADS], scratch[HEADS:]
    qi = pl.program_id(1)

    w = iw_ref[...] * IDX_HEADS ** -0.5

    def index_step(j, nblk, masked=False):
        ikb = ik_ref[_key_rows(j, nblk, t), :]
        zs = [_dot(ikb, iq_ref[LANES * hh:LANES * (hh + 1), :]) for hh in range(IDX_HEADS)]
        sc = w[0:1, :] * jnp.maximum(zs[0], 0.0)
        for hh in range(1, IDX_HEADS):
            sc = sc + w[hh:hh + 1, :] * jnp.maximum(zs[hh], 0.0)
        sc = jnp.where(sc == 0.0, 0.0, sc)
        if masked:
            sc = jnp.where(_causal_mask(t, nblk), sc, NEG_INF)
        bits = pltpu.bitcast(sc, I32)
        keys = jnp.where(bits < 0, bits ^ 0x7FFFFFFF, bits)
        ukeys = keys ^ INT_MIN
        digits = [(lax.shift_right_logical(ukeys, DIGIT_BITS * (N_DIGITS - 1 - d)) & DIGIT_MASK)
                  .astype(F32).astype(BF16) for d in range(N_DIGITS)]
        for b in range(nblk):
            rows = slice(b * t, (b + 1) * t)
            keys_sc[j + b] = keys[rows]
            for d in range(N_DIGITS):
                digit_sc[d][j + b] = digits[d][rows]

    _sweep_causal_blocks(qi, index_step)

    qpos = qi * t + lax.broadcasted_iota(I32, (1, t), 1)
    kt = jnp.minimum(topk, qpos + 1).astype(F32)

    def count_ge(cand):
        def body(j, acc):
            return acc + _fold_rows((keys_sc[j] >= cand).astype(F32))
        acc = lax.fori_loop(0, qi + 1, body, jnp.zeros((SUBLANES, t), F32))
        return jnp.sum(acc, axis=0, keepdims=True)

    one_b = jnp.ones((), BF16)
    zero_b = jnp.zeros((), BF16)

    def count_digit_ge(vals_sc, cand):
        cand_b = cand.astype(F32).astype(BF16)

        def body(j, acc):
            ge = jnp.where(vals_sc[j] >= cand_b, one_b, zero_b)
            return acc + _fold_packed(ge).astype(F32)
        acc = lax.fori_loop(0, qi + 1, body, jnp.zeros((PACKED_ROWS, t), F32))
        return jnp.sum(acc, axis=0, keepdims=True)

    def keep_matching(vals_sc, match_sc, match):
        match_b = match.astype(F32).astype(BF16)

        def body(j, carry):
            vals_sc[j] = jnp.where(match_sc[j] == match_b, vals_sc[j], -one_b)
            return carry
        lax.fori_loop(0, qi + 1, body, 0)

    zero = jnp.zeros((1, t), I32)
    rank = kt
    above = jnp.zeros((1, t), F32)
    tau_u = zero
    digit = zero
    for d in range(N_DIGITS):
        if d > 0:
            keep_matching(digit_sc[d], digit_sc[d - 1], digit)

        def bit_body(i, prefix, d=d, rank=rank):
            cand = prefix + lax.shift_left(jnp.int32(1), DIGIT_BITS - 1 - i)
            return jnp.where(count_digit_ge(digit_sc[d], cand) >= rank, cand, prefix)

        digit = lax.fori_loop(0, DIGIT_BITS, bit_body, zero)
        tau_u = lax.shift_left(tau_u, DIGIT_BITS) | digit
        if d < N_DIGITS - 1:
            higher = count_digit_ge(digit_sc[d], digit + 1)
            above = above + higher
            rank = rank - higher
    tau = tau_u ^ INT_MIN
    tau_sc[...] = tau
    excess = jnp.max(above + count_digit_ge(digit_sc[N_DIGITS - 1], digit) - kt)

    @pl.when(excess > 0.0)
    def _():
        need = kt - count_ge(tau + 1)

        def tie_body(j, seen):
            kj = keys_sc[j]
            eq = kj == tau
            eqb = eq.astype(BF16)
            earlier = _dot(before_ref[...], eqb) + seen
            keys_sc[j] = jnp.where(eq & (earlier >= need), INT_MIN, kj)
            return seen + jnp.sum(eqb.astype(F32), axis=0, keepdims=True)

        lax.fori_loop(0, qi + 1, tie_body, jnp.zeros((1, t), F32))

    _softmax_init(m_sc, acc_sc)

    def step(j, nblk):
        keys = [keys_sc[j + b] for b in range(nblk)]
        sel = (keys[0] if nblk == 1 else jnp.concatenate(keys, axis=0)) >= tau_sc[...]
        kb = dk_ref[_key_rows(j, nblk, t), :]
        vx = _value_cols(dvx_ref, j, nblk, slice(None))
        scores = [_dot(kb, q_ref[LANES * h:LANES * (h + 1), :]) for h in range(HEADS)]
        for h in range(HEADS):
            bias = _lane_repeat(_alibi(slopes[h], j, qi, t, nblk), t)
            scores[h] = jnp.where(sel, scores[h] + bias, -jnp.inf)
        _softmax_steps(scores, [vx] * HEADS, m_sc, acc_sc)

    _sweep_earlier_blocks(qi + 1, step)
    o_ref[...] = jnp.concatenate([_softmax_result(acc_sc[h][...]) for h in range(HEADS)], axis=0).T


def _attn_d_call(qT, iqT, miscT, dkp, dvxT, ikp, before, topk, slopes):
    B, nb, _, t = qT.shape
    S = nb * t
    iw_block = MISC_IW // IDX_HEADS
    return pl.pallas_call(
        functools.partial(_attn_d_kernel, t=t, topk=topk, slopes=slopes),
        grid=(B, nb),
        in_specs=[_q_spec(HW, t), _q_spec(IDX_HEADS * LANES, t),
                  pl.BlockSpec((None, None, IDX_HEADS, t), lambda b, i: (b, i, iw_block, 0)),
                  _seq_rows_spec(S, LANES), _seq_cols_spec(nb, LANES, t), _seq_rows_spec(S, LANES),
                  _const_spec((t, t))],
        out_specs=pl.BlockSpec((None, t, GROUP_W), lambda b, i: (b, i, 0)),
        out_shape=jax.ShapeDtypeStruct((B, S, GROUP_W), F32),
        scratch_shapes=([pltpu.VMEM((nb, t, t), I32), pltpu.VMEM((1, t), I32)]
                        + [pltpu.VMEM((nb, t, t), BF16)] * N_DIGITS + _attn_scratch(HEADS, t)),
        compiler_params=_params("arbitrary", "arbitrary"),
        name="attn_sparse",
    )(qT, iqT, miscT, dkp, dvxT, ikp, before)


def _out_proj_kernel(x_ref, oa_ref, ob_ref, oc_ref, od_ref, beta_ref, wo_ref, gate_ref,
                     g2_ref, scale_ref, shift_ref, wr_ref, br_ref, earlier_ref,
                     x1_ref, h2x_ref, route_ref, cnt_ref, cnt_sc):
    d_model = x_ref.shape[1]
    acc = None
    for i, o_ref in enumerate((oa_ref, ob_ref, oc_ref, od_ref)):
        sl = slice(GROUP_W * i, GROUP_W * (i + 1))
        mix = (o_ref[...] * beta_ref[:, sl]).astype(BF16)
        part = jnp.dot(mix, wo_ref[sl, :], preferred_element_type=F32)
        acc = part if acc is None else acc + part
    x1 = x_ref[...] + gate_ref[...] * acc
    x1_ref[...] = x1
    ms = jnp.mean(x1 * x1, axis=-1, keepdims=True)
    h2 = x1 * lax.rsqrt(ms + RMS_EPS) * g2_ref[...]
    h2 = h2 * (1.0 + scale_ref[...]) + shift_ref[...]
    h2x_ref[:, :d_model] = h2

    h_hi, h_lo = _split_bf16(h2, 2)
    w_hi, w_lo = _split_bf16(wr_ref[...], 2)
    logits = _dot(h_hi, w_hi) + _dot(h_hi, w_lo) + _dot(h_lo, w_hi) + br_ref[...]
    lt = logits.T
    tm = lt.shape[1]
    g = lt[0:N_GROUPS]
    gmax = jnp.max(g, axis=0, keepdims=True)
    gi = lax.broadcasted_iota(I32, g.shape, 0)
    gidx = jnp.min(jnp.where(g == gmax, gi, N_GROUPS), axis=0, keepdims=True)
    g_prob = 1.0 / jnp.sum(jnp.exp(g - gmax), axis=0, keepdims=True)
    e_sel = jnp.zeros((EXPERTS_PER_GROUP, tm), F32)
    for gg in range(N_GROUPS):
        lo = N_GROUPS + EXPERTS_PER_GROUP * gg
        e_sel = e_sel + jnp.where(gidx == gg, lt[lo:lo + EXPERTS_PER_GROUP], 0.0)
    ei = lax.broadcasted_iota(I32, e_sel.shape, 0)
    v1 = jnp.max(e_sel, axis=0, keepdims=True)
    i1 = jnp.min(jnp.where(e_sel == v1, ei, EXPERTS_PER_GROUP), axis=0, keepdims=True)
    rest = jnp.where(ei == i1, -jnp.inf, e_sel)
    v2 = jnp.max(rest, axis=0, keepdims=True)
    i2 = jnp.min(jnp.where(rest == v2, ei, EXPERTS_PER_GROUP), axis=0, keepdims=True)
    e2 = jnp.exp(v2 - v1)
    w1 = g_prob / (1.0 + e2)
    w2 = g_prob * e2 / (1.0 + e2)
    in_group = jnp.where(ei == i1, w1, 0.0) + jnp.where(ei == i2, w2, 0.0)
    cw = jnp.concatenate([in_group, jnp.zeros((LANES - EXPERTS_PER_GROUP, tm), F32)], axis=0)
    h2x_ref[:, d_model:] = cw.T

    @pl.when(pl.program_id(0) == 0)
    def _():
        cnt_sc[...] = jnp.zeros(cnt_sc.shape, F32)

    rows = lax.broadcasted_iota(I32, (SUBLANES, tm), 0)
    onehot = (rows == gidx).astype(F32)
    seen = _dot(onehot.astype(BF16), earlier_ref[...]) + cnt_sc[...]
    rank = jnp.sum(onehot * seen, axis=0, keepdims=True)
    route_ref[...] = jnp.concatenate(
        [gidx, rank.astype(I32), jnp.zeros((SUBLANES - 2, tm), I32)], axis=0)
    cnt_sc[...] = cnt_sc[...] + jnp.sum(onehot, axis=1, keepdims=True)
    cnt_ref[...] = cnt_sc[:, :LANES]


def _out_proj_call(xf, outs, beta, w_out, gate1, g2, scale2, shift2, w_r, b_r, earlier, seq, tm):
    N, D = xf.shape
    per_b = seq // tm
    row = lambda i: (i, 0)
    full = lambda i: (0, 0)
    per_batch = pl.BlockSpec((None, 1, D), lambda i: (i // per_b, 0, 0))
    return pl.pallas_call(
        _out_proj_kernel,
        grid=(N // tm,),
        in_specs=([pl.BlockSpec((tm, D), row)] + [pl.BlockSpec((tm, GROUP_W), row)] * 4
                  + [pl.BlockSpec((1, D), full), pl.BlockSpec((D, D), full), per_batch,
                     pl.BlockSpec((1, D), full), per_batch, per_batch,
                     pl.BlockSpec((D, LANES), full), pl.BlockSpec((1, LANES), full),
                     pl.BlockSpec((tm, tm), full)]),
        out_specs=[pl.BlockSpec((tm, D), row), pl.BlockSpec((tm, D + LANES), row),
                   pl.BlockSpec((SUBLANES, tm), lambda i: (0, i)),
                   pl.BlockSpec((SUBLANES, LANES), full)],
        out_shape=[jax.ShapeDtypeStruct((N, D), F32), jax.ShapeDtypeStruct((N, D + LANES), F32),
                   jax.ShapeDtypeStruct((SUBLANES, N), I32),
                   jax.ShapeDtypeStruct((SUBLANES, LANES), F32)],
        scratch_shapes=[pltpu.VMEM((SUBLANES, tm), F32)],
        compiler_params=_params("arbitrary"),
        name="out_proj_router",
    )(xf, *outs, beta, w_out, gate1, g2, scale2, shift2, w_r, b_r, earlier)


MOE_TILE = 256

def _dispatch_kernel(slot_ref, h2x_ref, zeros_ref, xs_ref, sem):
    del zeros_ref

    def row_copy(g, u):
        return pltpu.make_async_copy(h2x_ref.at[g, pl.ds(u, 1), :],
                                     xs_ref.at[pl.ds(slot_ref[0, g * SUBLANES + u], 1), :], sem)

    def issue(g, carry):
        for u in range(SUBLANES):
            row_copy(g, u).start()
        return carry

    lax.fori_loop(0, h2x_ref.shape[0], issue, 0)

    def drain(g, carry):
        for u in range(SUBLANES):
            row_copy(g, u).wait()
        return carry

    lax.fori_loop(0, h2x_ref.shape[0], drain, 0)


def _dispatch_call(slot, h2x, xs_zero, tm):
    N, DX = h2x.shape
    return pl.pallas_call(
        _dispatch_kernel,
        grid=(N // tm,),
        in_specs=[pl.BlockSpec((None, 1, tm), lambda i: (i, 0, 0), memory_space=pltpu.SMEM),
                  pl.BlockSpec((tm // SUBLANES, SUBLANES, DX), lambda i: (i, 0, 0)),
                  pl.BlockSpec(memory_space=pl.ANY)],
        out_specs=pl.BlockSpec(memory_space=pl.ANY),
        out_shape=jax.ShapeDtypeStruct(xs_zero.shape, F32),
        scratch_shapes=[pltpu.SemaphoreType.DMA(())],
        input_output_aliases={2: 0},
        compiler_params=_params("arbitrary"),
        name="moe_dispatch",
    )(slot.reshape(N // tm, 1, tm), h2x.reshape(N // SUBLANES, SUBLANES, DX), xs_zero)


def _expert_kernel(group_ref, valid_ref, xs_ref, w1f_ref, w3f_ref, w2f_ref, expand_ref, y_ref,
                   w1_ref, w3_ref, w2_ref):
    i = pl.program_id(0)
    valid = valid_ref[i]
    d_model = y_ref.shape[1]
    ff = w1_ref.shape[2]
    new_group = jnp.logical_or(i == 0, group_ref[i] != group_ref[jnp.maximum(i - 1, 0)])

    @pl.when(valid == 0)
    def _():
        y_ref[...] = jnp.zeros(y_ref.shape, F32)

    @pl.when(jnp.logical_and(valid > 0, new_group))
    def _():
        for e in range(EXPERTS_PER_GROUP):
            w1_ref[e] = w1f_ref[e].astype(BF16)
            w3_ref[e] = w3f_ref[e].astype(BF16)
            w2_ref[e] = w2f_ref[e].astype(BF16)

    @pl.when(valid > 0)
    def _():
        x = xs_ref[:, :d_model].astype(BF16)
        cw_hi, cw_lo = _split_bf16(xs_ref[:, d_model:], 2)
        cwx = _dot(cw_hi, expand_ref[...]) + _dot(cw_lo, expand_ref[...])
        acc = None
        for e in range(EXPERTS_PER_GROUP):
            a = _dot(x, w1_ref[e])
            b = _dot(x, w3_ref[e])
            hid = a * jax.nn.sigmoid(a) * b * cwx[:, ff * e:ff * (e + 1)]
            part = _dot(hid.astype(BF16), w2_ref[e])
            acc = part if acc is None else acc + part
        y_ref[...] = acc


def _expert_call(tile_group, tile_valid, xs, w1, w3, w2, expand, layer):
    P, DX = xs.shape
    D, FF = w1.shape[-2:]
    tr = MOE_TILE

    def group_spec(rows, cols):
        return pl.BlockSpec((None, None, EXPERTS_PER_GROUP, rows, cols),
                            lambda i, g, v: (layer, g[i], 0, 0, 0),
                            pipeline_mode=pl.Buffered(1))

    grid_spec = pltpu.PrefetchScalarGridSpec(
        num_scalar_prefetch=2,
        grid=(P // tr,),
        in_specs=[pl.BlockSpec((tr, DX), lambda i, g, v: (i, 0)),
                  group_spec(D, FF), group_spec(D, FF), group_spec(FF, D),
                  pl.BlockSpec(expand.shape, lambda i, g, v: (0, 0))],
        out_specs=pl.BlockSpec((tr, D), lambda i, g, v: (i, 0)),
        scratch_shapes=[pltpu.VMEM((EXPERTS_PER_GROUP, D, FF), BF16),
                        pltpu.VMEM((EXPERTS_PER_GROUP, D, FF), BF16),
                        pltpu.VMEM((EXPERTS_PER_GROUP, FF, D), BF16)],
    )
    return pl.pallas_call(
        _expert_kernel,
        grid_spec=grid_spec,
        out_shape=jax.ShapeDtypeStruct((P, D), F32),
        compiler_params=_params("arbitrary"),
        name="moe_experts",
    )(tile_group, tile_valid, xs, w1, w3, w2, expand)


def _residual_kernel(slot_ref, slot_next_ref, x1_ref, gate_ref, ys_ref, o_ref, buf, sems):
    i = pl.program_id(0)

    def gather(slots, b, wait):
        def body(g, carry):
            for u in range(SUBLANES):
                cp = pltpu.make_async_copy(ys_ref.at[pl.ds(slots[0, g * SUBLANES + u], 1), :],
                                           buf.at[b, g, pl.ds(u, 1), :], sems.at[b])
                cp.wait() if wait else cp.start()
            return carry
        lax.fori_loop(0, buf.shape[1], body, 0)

    @pl.when(i == 0)
    def _():
        gather(slot_ref, 0, False)

    @pl.when(i + 1 < pl.num_programs(0))
    def _():
        gather(slot_next_ref, (i + 1) % 2, False)

    gather(slot_ref, i % 2, True)
    o_ref[...] = x1_ref[...] + gate_ref[...] * buf[i % 2].reshape(x1_ref.shape)


def _residual_call(slot, x1, gate2, ys, seq, tm):
    N, D = x1.shape
    n = N // tm
    per_b = seq // tm
    row = pl.BlockSpec((tm, D), lambda i: (i, 0))
    slot3 = slot.reshape(n, 1, tm)
    return pl.pallas_call(
        _residual_kernel,
        grid=(n,),
        in_specs=[pl.BlockSpec((None, 1, tm), lambda i: (i, 0, 0), memory_space=pltpu.SMEM),
                  pl.BlockSpec((None, 1, tm), lambda i: (jnp.minimum(i + 1, n - 1), 0, 0),
                               memory_space=pltpu.SMEM),
                  row, pl.BlockSpec((None, 1, D), lambda i: (i // per_b, 0, 0)),
                  pl.BlockSpec(memory_space=pl.ANY)],
        out_specs=row,
        out_shape=jax.ShapeDtypeStruct((N, D), F32),
        scratch_shapes=[pltpu.VMEM((2, tm // SUBLANES, SUBLANES, D), F32),
                        pltpu.SemaphoreType.DMA((2,))],
        compiler_params=_params("arbitrary"),
        name="moe_residual",
    )(slot3, slot3, x1, gate2, ys)


def _moe_routing(route, cnt, n_tokens):
    tr = MOE_TILE
    n_tiles = n_tokens // tr + N_GROUPS
    counts = cnt[:N_GROUPS, 0].astype(I32)
    padded = (counts + tr - 1) // tr * tr
    ends = jnp.cumsum(padded)
    starts = ends - padded
    group, rank = route[0], route[1]
    slot = starts[group] + rank
    tile_start = jnp.arange(n_tiles, dtype=I32) * tr
    tile_group = jnp.minimum(jnp.sum((tile_start[:, None] >= ends[None, :]).astype(I32), axis=1),
                             N_GROUPS - 1)
    tile_valid = jnp.clip(starts[tile_group] + counts[tile_group] - tile_start, 0, tr)
    tile_valid = jnp.where(tile_start < ends[-1], tile_valid, 0)
    return slot, tile_group, tile_valid


def _block_diag_mean(width, group, valid_in_128=None):
    i = jnp.arange(width)
    same = (i[:, None] // group) == (i[None, :] // group)
    if valid_in_128 is not None:
        same = same & ((i[:, None] % LANES) < valid_in_128) & ((i[None, :] % LANES) < valid_in_128)
    return jnp.where(same, 1.0 / group, 0.0).astype(BF16)


def _placement(src_width, dst_width, pairs):
    src = jnp.array([p[0] for p in pairs], I32)
    dst = jnp.array([p[1] for p in pairs], I32)
    return jnp.zeros((src_width, dst_width), F32).at[src, dst].set(1.0).astype(BF16)


def _placements():
    per_head = [(HEAD_DIM * h + d, LANES * h + d) for h in range(HEADS) for d in range(HEAD_DIM)]
    diff_q = [(HEAD_DIM * h + DIFF_DIM * c + d, LANES * (2 * h + c) + DIFF_DIM * c + d)
              for h in range(HEADS) for c in range(2) for d in range(DIFF_DIM)]
    idx_q = [(IDX_DIM * hh + d, LANES * hh + d) for hh in range(IDX_HEADS) for d in range(IDX_DIM)]
    ident = [(d, d) for d in range(GROUP_W)]
    first64 = [(d, d) for d in range(HEAD_DIM)]
    second64 = [(HEAD_DIM + d, d) for d in range(HEAD_DIM)]
    idx_k = [(MISC_IK + d, d) for d in range(IDX_DIM)]
    table = {
        "a_qT": (GROUP_W, per_head), "a_kp": (GROUP_W, per_head), "a_vxT": (GROUP_W, per_head),
        "b_qT": (GROUP_W, per_head), "b_kp": (GROUP_W, per_head), "b_vT": (GROUP_W, ident),
        "c_qT": (GROUP_W, diff_q), "c_kp": (GROUP_W, per_head), "c_vxT": (GROUP_W, per_head),
        "d_qT": (GROUP_W, per_head), "i_qT": (GROUP_W, idx_q),
        "d_kp": (GROUP_W, first64), "d_vxT": (GROUP_W, second64), "i_kp": (LANES, idx_k),
    }
    out = []
    for name, src, width, orient, ones in OUTPUTS:
        src_width, pairs = table[name]
        p = _placement(src_width, width, pairs)
        out.append(p if orient == "rows" else p.T)
    return out


def kernel(x, c, ada_w, ada_b, norm1_g, norm2_g, w_in, b_f, qn_a, kn_a, qn_c, kn_c,
           lam_q1, lam_k1, lam_q2, lam_k2, subln_g, qn_d, kn_d, mix_beta, w_out,
           w_group, b_group, w_expert, b_expert, w1, w3, w2):
    B, S, D = x.shape
    L = ada_w.shape[0]
    N = B * S
    topk = min(TOPK_MAX, S // 4)
    t = ATTN_BLOCK
    tm = t
    slopes = [2.0 ** (-8.0 * i / (2 * HEADS)) for i in range(1, 2 * HEADS + 1)]
    slopes_c, slopes_d = tuple(slopes[0::2]), tuple(slopes[1::2])

    idx_t = jnp.arange(t)
    after = (idx_t[None, :] > idx_t[:, None]).astype(BF16)
    before = (idx_t[None, :] < idx_t[:, None]).astype(BF16)
    upto = (idx_t[None, :] <= idx_t[:, None]).astype(BF16)
    g64 = _block_diag_mean(GROUP_W, HEAD_DIM)
    g32 = _block_diag_mean(GROUP_W, DIFF_DIM)
    places = _placements()
    lane = jnp.arange(LANES)
    cum_sel = jnp.stack([jnp.broadcast_to((lane == MISC_AF + h)[:, None], (LANES, LANES))
                         for h in range(HEADS)]).astype(BF16)
    ff_col = jnp.arange(EXPERTS_PER_GROUP * EXPERT_FF)
    expand = (lane[:, None] == ff_col[None, :] // EXPERT_FF).astype(BF16)

    mod = _ada_call(c, ada_w, ada_b)
    xf = x.reshape(N, D)

    for l in range(L):
        m6 = mod[l].reshape(B, 6, 1, D)
        shift1, scale1, gate1, shift2, scale2, gate2 = (m6[:, i] for i in range(6))

        w_all = w_in[l].astype(BF16)
        ones = jnp.ones((GROUP_W - HEAD_DIM,), F32)
        gains = jnp.stack([jnp.tile(qn_a[l], HEADS) * (HEAD_DIM ** -0.5 * LOG2E),
                           jnp.tile(kn_a[l], HEADS),
                           jnp.tile(qn_c[l], 2 * HEADS) * (DIFF_DIM ** -0.5 * LOG2E),
                           jnp.tile(kn_c[l], 2 * HEADS),
                           jnp.tile(qn_d[l], HEADS) * (HEAD_DIM ** -0.5 * LOG2E),
                           jnp.concatenate([kn_d[l], ones])]).astype(F32)
        gains = jnp.concatenate([gains, jnp.zeros((2, GROUP_W), F32)], axis=0)

        outs = _in_proj_call(xf, scale1, shift1, norm1_g[l].reshape(1, D), w_all, gains,
                             g64, g32, places, B, S, tm)
        sec = {name: o for (name, _, _, _, _), o in zip(OUTPUTS, outs)}
        for name, src, width, orient, ones_ in OUTPUTS:
            if orient == "rows":
                sec[name] = sec[name].reshape(B, S, width)
        misc = outs[len(OUTPUTS)].reshape(B, S, LANES)
        miscT = outs[len(OUTPUTS) + 1]

        bf_row = jnp.zeros((1, LANES), F32).at[0, MISC_AF:MISC_AF + HEADS].set(b_f[l].astype(F32))
        cumrep = _cum_call(misc, bf_row, upto, cum_sel)

        o_a = _attn_a_call(sec["a_qT"], sec["a_kp"], sec["a_vxT"], cumrep)
        o_b = _attn_b_call(sec["b_qT"], sec["b_kp"], sec["b_vT"], after)
        lambda_init = 0.8 - 0.6 * math.exp(-0.3 * l)
        lamv = jnp.stack([lam_q1[l], lam_k1[l], lam_q2[l], lam_k2[l]]).astype(F32)
        o_c = _attn_c_call(sec["c_qT"], sec["c_kp"], sec["c_vxT"], lamv,
                           subln_g[l].reshape(HEAD_DIM, 1).astype(F32), slopes_c, lambda_init)
        o_d = _attn_d_call(sec["d_qT"], sec["i_qT"], miscT, sec["d_kp"], sec["d_vxT"], sec["i_kp"],
                           before, topk, slopes_d)

        w_r = jnp.concatenate([w_group[l], w_expert[l],
                               jnp.zeros((D, LANES - N_GROUPS - N_EXPERTS), F32)], axis=1)
        b_r = jnp.concatenate([b_group[l], b_expert[l],
                               jnp.zeros((LANES - N_GROUPS - N_EXPERTS,), F32)]).reshape(1, LANES)
        x1, h2x, route, cnt = _out_proj_call(
            xf, [o.reshape(N, GROUP_W) for o in (o_a, o_b, o_c, o_d)], mix_beta[l].reshape(1, D),
            w_out[l].astype(BF16), gate1, norm2_g[l].reshape(1, D), scale2, shift2, w_r, b_r,
            after, S, tm)

        slot, tile_group, tile_valid = _moe_routing(route, cnt, N)
        xs = _dispatch_call(slot, h2x, jnp.zeros((tile_group.shape[0] * MOE_TILE, D + LANES), F32), tm)
        ys = _expert_call(tile_group, tile_valid, xs, w1, w3, w2, expand, l)
        xf = _residual_call(slot, x1, gate2, ys, S, tm)

    return xf.reshape(B, S, D)
```

```python
import functools
import math

import jax
import jax.numpy as jnp
from jax import lax
from jax.experimental import pallas as pl
from jax.experimental.pallas import tpu as pltpu

F32 = jnp.float32
BF16 = jnp.bfloat16
I32 = jnp.int32

HEAD_DIM = 64
HEADS = 4
GROUP_W = HEADS * HEAD_DIM
DIFF_DIM = HEAD_DIM // 2
IDX_HEADS = 8
IDX_DIM = 32
TOPK_MAX = 256
N_GROUPS = 4
EXPERTS_PER_GROUP = 8
N_EXPERTS = N_GROUPS * EXPERTS_PER_GROUP
EXPERT_FF = 256
RMS_EPS = 1e-6
NEG_INF = -1e30
INT_MIN = -(2 ** 31)
LOG2E = math.log2(math.e)

LANES = 128
SUBLANES = 8
ATTN_BLOCK = 256
VMEM_LIMIT = 56 * 1024 * 1024

MISC_IK = 0
MISC_IW = IDX_DIM
MISC_AF = IDX_DIM + IDX_HEADS


def _params(*sem):
    return pltpu.CompilerParams(dimension_semantics=sem, vmem_limit_bytes=VMEM_LIMIT)


def _log_sigmoid(z):
    return jnp.minimum(z, 0.0) - jnp.log1p(jnp.exp(-jnp.abs(z)))


def _log2_sigmoid(z2):
    return jnp.minimum(z2, 0.0) - jnp.log2(1.0 + jnp.exp2(-jnp.abs(z2)))


def _split_bf16(x, parts):
    out = []
    rem = x
    for _ in range(parts):
        p = rem.astype(BF16)
        out.append(p)
        rem = rem - p.astype(F32)
    return out


def _dot(a, b):
    return jnp.dot(a, b, preferred_element_type=F32)


def _ada_kernel(c_ref, w_ref, b_ref, o_ref):
    c = c_ref[...]
    ca = c * jax.nn.sigmoid(c)
    o_ref[...] = jnp.dot(ca, w_ref[...], precision=lax.Precision.HIGHEST,
                         preferred_element_type=F32) + b_ref[...]


def _ada_call(c, ada_w, ada_b):
    L, D, E = ada_w.shape
    B = c.shape[0]
    tn = 1536
    return pl.pallas_call(
        _ada_kernel,
        grid=(L, E // tn),
        in_specs=[pl.BlockSpec((B, D), lambda l, j: (0, 0)),
                  pl.BlockSpec((None, D, tn), lambda l, j: (l, 0, j)),
                  pl.BlockSpec((None, 1, tn), lambda l, j: (l, 0, j))],
        out_specs=pl.BlockSpec((None, B, tn), lambda l, j: (l, 0, j)),
        out_shape=jax.ShapeDtypeStruct((L, B, E), F32),
        compiler_params=_params("arbitrary", "arbitrary"),
        name="ada_mod",
    )(c, ada_w, ada_b.reshape(L, 1, E))


SOURCES = ("a_q", "a_k", "a_v", "b_q", "b_k", "b_v", "c_q", "c_k", "c_v", "d_q", "i_q", "d_kv")
SRC_NORM = {"a_q": ("n64", 0), "a_k": ("n64", 1), "c_q": ("n32", 2), "c_k": ("n32", 3),
            "d_q": ("n64", 4), "d_kv": ("n64", 5)}
SRC_SCALE = {"b_q": HEAD_DIM ** -0.5 * LOG2E, "i_q": IDX_DIM ** -0.5}
HW = HEADS * LANES
OUTPUTS = (
    ("a_qT", "a_q", HW, "cols", False), ("a_kp", "a_k", HW, "rows", False),
    ("a_vxT", "a_v", HW, "cols", True),
    ("b_qT", "b_q", HW, "cols", False), ("b_kp", "b_k", HW, "rows", False),
    ("b_vT", "b_v", GROUP_W, "cols", False),
    ("c_qT", "c_q", 2 * HW, "cols", False), ("c_kp", "c_k", HW, "rows", False),
    ("c_vxT", "c_v", HW, "cols", True),
    ("d_qT", "d_q", HW, "cols", False), ("i_qT", "i_q", IDX_HEADS * LANES, "cols", False),
    ("d_kp", "d_kv", LANES, "rows", False), ("d_vxT", "d_kv", LANES, "cols", True),
    ("i_kp", "misc", LANES, "rows", False),
)
N_MAIN = len(SOURCES) * GROUP_W

IN_SPLITS = (GROUP_W, GROUP_W, GROUP_W, HEADS, GROUP_W, GROUP_W, GROUP_W, GROUP_W, GROUP_W,
             GROUP_W, GROUP_W, HEAD_DIM, HEAD_DIM, IDX_HEADS * IDX_DIM, IDX_DIM, IDX_HEADS)
IN_NAMES = ("a_q", "a_k", "a_v", "a_f", "b_q", "b_k", "b_v", "c_q", "c_k", "c_v",
            "d_q", "d_k", "d_v", "i_q", "i_k", "i_w")
IN_OFFSETS = {n: (sum(IN_SPLITS[:i]), IN_SPLITS[i]) for i, n in enumerate(IN_NAMES)}
P_IN = sum(IN_SPLITS)
SECTION_PARTS = {name: (name,) for name in SOURCES if name != "d_kv"}
SECTION_PARTS["d_kv"] = ("d_k", "d_v")
SECTION_PARTS["misc"] = ("i_k", "i_w", "a_f")
REALIGN_ROWS = 256


def _realign_weights(w_ref, w_sc):
    d_model = w_ref.shape[0]
    for r0 in range(0, d_model, REALIGN_ROWS):
        rows = slice(r0, r0 + REALIGN_ROWS)
        for i, name in enumerate(SOURCES + ("misc",)):
            width = LANES if name == "misc" else GROUP_W
            pieces, used = [], 0
            for part in SECTION_PARTS[name]:
                off, w = IN_OFFSETS[part]
                base = off // LANES * LANES
                end = min(-(-(off + w) // LANES) * LANES, P_IN)
                window = w_ref[rows, base:end]
                pieces.append(window[:, off - base:off - base + w])
                used += w
            if used < width:
                pieces.append(jnp.zeros((REALIGN_ROWS, width - used), BF16))
            block = pieces[0] if len(pieces) == 1 else jnp.concatenate(pieces, axis=1)
            w_sc[rows, i * GROUP_W:i * GROUP_W + width] = block


def _in_proj_kernel(*refs):
    (x_ref, scale_ref, shift_ref, g1_ref, w_in_ref, gains_ref, g64_ref, g32_ref) = refs[:8]
    place_refs = refs[8:8 + len(OUTPUTS)]
    out_refs = refs[8 + len(OUTPUTS):-1]
    w_ref = refs[-1]

    @pl.when(pl.program_id(0) == 0)
    def _():
        _realign_weights(w_in_ref, w_ref)

    x = x_ref[...]
    ms = jnp.mean(x * x, axis=-1, keepdims=True)
    h = x * lax.rsqrt(ms + RMS_EPS) * g1_ref[...]
    h = h * (1.0 + scale_ref[...]) + shift_ref[...]
    hb = h.astype(BF16)

    wm = w_ref[:, N_MAIN:N_MAIN + LANES]
    h_hi, h_lo = _split_bf16(h, 2)
    misc = _dot(h_hi, wm) + _dot(h_lo, wm)
    out_refs[len(OUTPUTS)][...] = misc
    out_refs[len(OUTPUTS) + 1][...] = misc.T

    compact = {"misc": misc.astype(BF16)}
    raw = {name: _dot(hb, w_ref[:, i * GROUP_W:(i + 1) * GROUP_W])
           for i, name in enumerate(SOURCES)}
    msqs = {name: _dot((raw[name] * raw[name]).astype(BF16),
                       (g64_ref if SRC_NORM[name][0] == "n64" else g32_ref)[...])
            for name in SOURCES if name in SRC_NORM}
    for name in SOURCES:
        sec = raw[name]
        if name in SRC_NORM:
            r = SRC_NORM[name][1]
            fac = lax.rsqrt(msqs[name] + RMS_EPS)
            if name == "d_kv":
                lane = lax.broadcasted_iota(I32, sec.shape, 1)
                fac = jnp.where(lane < HEAD_DIM, fac, 1.0)
            sec = sec * fac * gains_ref[r:r + 1, :]
        elif name in SRC_SCALE:
            sec = sec * SRC_SCALE[name]
        compact[name] = sec.astype(BF16)

    for (name, src, width, orient, ones), p_ref, o_ref in zip(OUTPUTS, place_refs, out_refs):
        if orient == "rows":
            o_ref[...] = _dot(compact[src], p_ref[...]).astype(BF16)
        else:
            res = lax.dot_general(p_ref[...], compact[src], (((1,), (1,)), ((), ())),
                                  preferred_element_type=F32)
            if ones:
                r = lax.broadcasted_iota(I32, res.shape, 0)
                res = jnp.where((r & (LANES - 1)) >= HEAD_DIM, 1.0, res)
            o_ref[...] = res.astype(BF16)


def _in_proj_call(xf, scale1, shift1, g1, w_all, gains, g64, g32, places, batch, seq, tm):
    N, D = xf.shape
    per_b = seq // tm
    row = lambda i: (i, 0)
    full = lambda i: (0, 0)
    col4 = lambda i: (i // per_b, i % per_b, 0, 0)
    out_shape, out_specs = [], []
    for name, src, width, orient, ones in OUTPUTS:
        if orient == "rows":
            out_shape.append(jax.ShapeDtypeStruct((N, width), BF16))
            out_specs.append(pl.BlockSpec((tm, width), row))
        else:
            out_shape.append(jax.ShapeDtypeStruct((batch, per_b, width, tm), BF16))
            out_specs.append(pl.BlockSpec((None, None, width, tm), col4))
    out_shape += [jax.ShapeDtypeStruct((N, LANES), F32),
                  jax.ShapeDtypeStruct((batch, per_b, LANES, tm), F32)]
    out_specs += [pl.BlockSpec((tm, LANES), row), pl.BlockSpec((None, None, LANES, tm), col4)]
    return pl.pallas_call(
        _in_proj_kernel,
        grid=(N // tm,),
        in_specs=([pl.BlockSpec((tm, D), row),
                   pl.BlockSpec((None, 1, D), lambda i: (i // per_b, 0, 0)),
                   pl.BlockSpec((None, 1, D), lambda i: (i // per_b, 0, 0)),
                   pl.BlockSpec((1, D), full),
                   pl.BlockSpec(w_all.shape, full),
                   pl.BlockSpec(gains.shape, full),
                   pl.BlockSpec(g64.shape, full),
                   pl.BlockSpec(g32.shape, full)]
                  + [pl.BlockSpec(p.shape, full) for p in places]),
        out_specs=out_specs,
        out_shape=out_shape,
        scratch_shapes=[pltpu.VMEM((D, N_MAIN + LANES), BF16)],
        compiler_params=_params("arbitrary"),
        name="in_proj",
    )(xf, scale1, shift1, g1, w_all, gains, g64, g32, *places)


def _cum_kernel(misc_ref, bf_ref, tri_ref, sel_ref, o_ref, *, blk):
    seq = misc_ref.shape[0]
    carry = jnp.zeros((1, LANES), F32)
    tri = tri_ref[...]
    for j in range(seq // blk):
        rows = slice(j * blk, (j + 1) * blk)
        lf = _log_sigmoid(misc_ref[rows, :] + bf_ref[...])
        c = carry
        for part in _split_bf16(lf, 3):
            c = c + _dot(tri, part)
        carry = c[blk - 1:blk, :]
        parts = _split_bf16(c, 3)
        for h in range(HEADS):
            rep = _dot(parts[0], sel_ref[h]) + _dot(parts[1], sel_ref[h]) + _dot(parts[2], sel_ref[h])
            o_ref[h, rows, :] = rep * LOG2E


def _cum_call(misc, bf_row, tri, sel):
    B, S, _ = misc.shape
    blk = tri.shape[0]
    return pl.pallas_call(
        functools.partial(_cum_kernel, blk=blk),
        grid=(B,),
        in_specs=[pl.BlockSpec((None, S, LANES), lambda b: (b, 0, 0)),
                  pl.BlockSpec((1, LANES), lambda b: (0, 0)),
                  pl.BlockSpec((blk, blk), lambda b: (0, 0)),
                  pl.BlockSpec((HEADS, LANES, LANES), lambda b: (0, 0, 0))],
        out_specs=pl.BlockSpec((None, HEADS, S, LANES), lambda b: (b, 0, 0, 0)),
        out_shape=jax.ShapeDtypeStruct((B, HEADS, S, LANES), F32),
        compiler_params=_params("arbitrary"),
        name="forget_cumsum",
    )(misc, bf_row, tri, sel)


def _seq_rows_spec(seq, width):
    return pl.BlockSpec((None, seq, width), lambda b: (b, 0, 0))


def _seq_cols_spec(nb, width, t):
    return pl.BlockSpec((None, nb, width, t), lambda b: (b, 0, 0, 0))


def _const_spec(shape):
    return pl.BlockSpec(shape, lambda b: (0,) * len(shape))


def _for_each_query_tile(q_ref, o_ref, t, tile_fn):
    def body(qi, carry):
        o_ref[pl.ds(pl.multiple_of(qi * t, t), t), :] = tile_fn(qi, q_ref.at[qi])
        return carry

    lax.fori_loop(0, q_ref.shape[0], body, 0)


def _lane_repeat(x, t):
    return jnp.concatenate([x] * (t // LANES), axis=1)


def _softmax_steps(scores, values, m_refs, acc_refs):
    probs, alphas = [], []
    for s, m_ref in zip(scores, m_refs):
        m_old = m_ref[...]
        m_new = jnp.maximum(m_old, jnp.max(s, axis=0, keepdims=True))
        alphas.append(jnp.exp2(m_old - m_new))
        probs.append(jnp.exp2(s - m_new).astype(BF16))
        m_ref[...] = m_new
    for p, vx, alpha, acc_ref in zip(probs, values, alphas, acc_refs):
        acc_ref[...] = alpha * acc_ref[...] + _dot(vx, p)


def _softmax_init(m_scs, acc_scs):
    for m_sc, acc_sc in zip(m_scs, acc_scs):
        m_sc[...] = jnp.full(m_sc.shape, NEG_INF, F32)
        acc_sc[...] = jnp.zeros(acc_sc.shape, F32)


def _softmax_result(acc):
    return acc[:HEAD_DIM] / acc[HEAD_DIM:]


def _alibi(slope, j, qi, t, nblk=1):
    key = lax.broadcasted_iota(I32, (nblk * t, LANES), 0)
    return (slope * LOG2E) * (key + (j - qi) * t).astype(F32)


def _sweep_earlier_blocks(qi, step):
    def body(i, carry):
        step(2 * i, 2)
        return carry

    lax.fori_loop(0, qi // 2, body, 0)

    @pl.when(qi % 2 == 1)
    def _():
        step(qi - 1, 1)


def _sweep_causal_blocks(qi, step):
    @pl.when(qi == 0)
    def _():
        step(0, 1, True)

    @pl.when(qi > 0)
    def _():
        step(qi - 1, 2, True)

    _sweep_earlier_blocks(jnp.maximum(qi - 1, 0), step)


def _causal_mask(t, nblk, strict=False):
    key = lax.broadcasted_iota(I32, (nblk * t, t), 0) - (nblk - 1) * t
    query = lax.broadcasted_iota(I32, (nblk * t, t), 1)
    return key < query if strict else key <= query


def _key_rows(j, nblk, t):
    return pl.ds(pl.multiple_of(j * t, t), nblk * t)


def _value_cols(vx_ref, j, nblk, rows):
    tiles = [vx_ref[j + b, rows, :] for b in range(nblk)]
    return tiles[0] if nblk == 1 else jnp.concatenate(tiles, axis=1)


def _attn_a_kernel(q_ref, k_ref, vx_ref, cum_ref, o_ref, *scratch, t):
    m_sc, acc_sc = scratch[:HEADS], scratch[HEADS:]

    def tile(qi, q):
        _softmax_init(m_sc, acc_sc)

        def step(j, nblk, masked=False):
            rows = _key_rows(j, nblk, t)
            heads = [slice(LANES * h, LANES * (h + 1)) for h in range(HEADS)]
            scores = [_dot(k_ref[rows, hs], q[hs, :]) for hs in heads]
            for h in range(HEADS):
                s = scores[h] - _lane_repeat(cum_ref[h, rows, :], t)
                scores[h] = jnp.where(_causal_mask(t, nblk), s, -jnp.inf) if masked else s
            _softmax_steps(scores, [_value_cols(vx_ref, j, nblk, hs) for hs in heads],
                           m_sc, acc_sc)

        _sweep_causal_blocks(qi, step)
        return jnp.concatenate([_softmax_result(acc_sc[h][...]) for h in range(HEADS)], axis=0).T

    _for_each_query_tile(q_ref, o_ref, t, tile)


def _attn_scratch(n, t, rows=LANES):
    return [pltpu.VMEM((1, t), F32)] * n + [pltpu.VMEM((rows, t), F32)] * n


def _attn_a_call(qT, kp, vxT, cumrep):
    B, nb, _, t = qT.shape
    S = nb * t
    return pl.pallas_call(
        functools.partial(_attn_a_kernel, t=t),
        grid=(B,),
        in_specs=[_seq_cols_spec(nb, HW, t), _seq_rows_spec(S, HW), _seq_cols_spec(nb, HW, t),
                  pl.BlockSpec((None, HEADS, S, LANES), lambda b: (b, 0, 0, 0))],
        out_specs=_seq_rows_spec(S, GROUP_W),
        out_shape=jax.ShapeDtypeStruct((B, S, GROUP_W), F32),
        scratch_shapes=_attn_scratch(HEADS, t),
        compiler_params=_params("arbitrary"),
        name="attn_forget",
    )(qT, kp, vxT, cumrep)


STICK_SPLIT_TERMS = 2


def _attn_b_kernel(q_ref, k_ref, v_ref, after_ref, o_ref, *scratch, t):
    tile = functools.partial(_attn_b_tile, k_ref=k_ref, v_ref=v_ref, after_ref=after_ref,
                             r_sc=scratch[:HEADS], acc_sc=scratch[HEADS:], t=t)
    _for_each_query_tile(q_ref, o_ref, t, tile)


def _attn_b_tile(qi, q_ref, *, k_ref, v_ref, after_ref, r_sc, acc_sc, t):
    for h in range(HEADS):
        r_sc[h][...] = jnp.zeros(r_sc[h].shape, F32)
        acc_sc[h][...] = jnp.zeros(acc_sc[h].shape, F32)

    def step(j, nblk, masked=False):
        after = after_ref[...]
        rows = _key_rows(j, nblk, t)
        heads = [slice(LANES * h, LANES * (h + 1)) for h in range(HEADS)]
        zs = [_dot(k_ref[rows, hs], q_ref[hs, :]) for hs in heads]
        lbs, splits, later_sums = [], [], []
        for h in range(HEADS):
            lb = _log2_sigmoid(zs[h])
            lm = lb - zs[h]
            if masked:
                lm = jnp.where(_causal_mask(t, nblk, strict=True), lm, 0.0)
            blocks = [lm[b * t:(b + 1) * t] for b in range(nblk)]
            splits.append([_split_bf16(blk, STICK_SPLIT_TERMS) for blk in blocks])
            sums = [jnp.sum(blk, axis=0, keepdims=True) for blk in blocks]
            r_old = r_sc[h][...]
            total = sums[0]
            for s_ in sums[1:]:
                total = total + s_
            r_sc[h][...] = r_old + total
            lbs.append(lb + r_old)
            later, run = [], None
            for b in reversed(range(nblk)):
                later.append(run)
                run = sums[b] if run is None else run + sums[b]
            later_sums.append(later[::-1])
        suffixes = [[functools.reduce(lambda a, b: a + b, [_dot(after, term) for term in terms])
                     for terms in splits[h]] for h in range(HEADS)]
        ws = []
        for h in range(HEADS):
            parts = [suffixes[h][b] if later_sums[h][b] is None else suffixes[h][b] + later_sums[h][b]
                     for b in range(nblk)]
            suffix = parts[0] if nblk == 1 else jnp.concatenate(parts, axis=0)
            w = jnp.exp2(lbs[h] + suffix)
            if masked:
                w = jnp.where(_causal_mask(t, nblk, strict=True), w, 0.0)
            ws.append(w.astype(BF16))
        for h in range(HEADS):
            acc_sc[h][...] += _dot(_value_cols(v_ref, j, nblk, slice(HEAD_DIM * h, HEAD_DIM * (h + 1))),
                                   ws[h])

    @pl.when(qi == 0)
    def _():
        step(0, 1, True)

    @pl.when(qi > 0)
    def _():
        step(qi - 1, 2, True)

    rest = jnp.maximum(qi - 1, 0)

    @pl.when(rest % 2 == 1)
    def _():
        step(rest - 1, 1)

    pairs = rest // 2

    def body(i, carry):
        step(2 * (pairs - 1 - i), 2)
        return carry

    lax.fori_loop(0, pairs, body, 0)
    return jnp.concatenate([acc_sc[h][...] for h in range(HEADS)], axis=0).T


def _attn_b_call(qT, kp, vT, after):
    B, nb, _, t = qT.shape
    S = nb * t
    return pl.pallas_call(
        functools.partial(_attn_b_kernel, t=t),
        grid=(B,),
        in_specs=[_seq_cols_spec(nb, HW, t), _seq_rows_spec(S, HW), _seq_cols_spec(nb, GROUP_W, t),
                  _const_spec((t, t))],
        out_specs=_seq_rows_spec(S, GROUP_W),
        out_shape=jax.ShapeDtypeStruct((B, S, GROUP_W), F32),
        scratch_shapes=_attn_scratch(HEADS, t, HEAD_DIM),
        compiler_params=_params("arbitrary"),
        name="attn_stick",
    )(qT, kp, vT, after)


def _attn_c_kernel(q_ref, k_ref, vx_ref, lamv_ref, subg_ref, o_ref, *scratch,
                   t, slopes, lambda_init):
    tile = functools.partial(_attn_c_tile, k_ref=k_ref, vx_ref=vx_ref, lamv_ref=lamv_ref,
                             subg_ref=subg_ref, m_sc=scratch[:2 * HEADS], acc_sc=scratch[2 * HEADS:],
                             t=t, slopes=slopes, lambda_init=lambda_init)
    _for_each_query_tile(q_ref, o_ref, t, tile)


def _attn_c_tile(qi, q_ref, *, k_ref, vx_ref, lamv_ref, subg_ref, m_sc, acc_sc,
                 t, slopes, lambda_init):
    _softmax_init(m_sc, acc_sc)

    def step(j, nblk, masked=False):
        rows = _key_rows(j, nblk, t)
        heads = [slice(LANES * h, LANES * (h + 1)) for h in range(HEADS)]
        scores = [_dot(k_ref[rows, heads[g // 2]], q_ref[LANES * g:LANES * (g + 1), :])
                  for g in range(2 * HEADS)]
        for g in range(2 * HEADS):
            s = scores[g] + _lane_repeat(_alibi(slopes[g // 2], j, qi, t, nblk), t)
            scores[g] = jnp.where(_causal_mask(t, nblk), s, -jnp.inf) if masked else s
        _softmax_steps(scores, [_value_cols(vx_ref, j, nblk, heads[g // 2])
                                for g in range(2 * HEADS)], m_sc, acc_sc)

    _sweep_causal_blocks(qi, step)

    lv = lamv_ref[...]
    lam = (jnp.exp(jnp.sum(lv[0:1] * lv[1:2], axis=-1, keepdims=True))
           - jnp.exp(jnp.sum(lv[2:3] * lv[3:4], axis=-1, keepdims=True)) + lambda_init)
    outs = []
    for h in range(HEADS):
        o = (_softmax_result(acc_sc[2 * h][...])
             - lam * _softmax_result(acc_sc[2 * h + 1][...]))
        ms = jnp.mean(o * o, axis=0, keepdims=True)
        outs.append(o * lax.rsqrt(ms + RMS_EPS) * subg_ref[...] * (1.0 - lambda_init))
    return jnp.concatenate(outs, axis=0).T


def _attn_c_call(qT, kp, vxT, lamv, subg_col, slopes, lambda_init):
    B, nb, _, t = qT.shape
    S = nb * t
    return pl.pallas_call(
        functools.partial(_attn_c_kernel, t=t, slopes=slopes, lambda_init=lambda_init),
        grid=(B,),
        in_specs=[_seq_cols_spec(nb, 2 * HW, t), _seq_rows_spec(S, HW), _seq_cols_spec(nb, HW, t),
                  _const_spec(lamv.shape), _const_spec(subg_col.shape)],
        out_specs=_seq_rows_spec(S, GROUP_W),
        out_shape=jax.ShapeDtypeStruct((B, S, GROUP_W), F32),
        scratch_shapes=_attn_scratch(2 * HEADS, t),
        compiler_params=_params("arbitrary"),
        name="attn_diff",
    )(qT, kp, vxT, lamv, subg_col)


def _fold_rows(x, group=SUBLANES):
    return jnp.sum(x.reshape(x.shape[0] // group, group, x.shape[1]), axis=0)


PACKED_ROWS = 2 * SUBLANES
DIGIT_BITS = 8
N_DIGITS = 32 // DIGIT_BITS
DIGIT_MASK = (1 << DIGIT_BITS) - 1


def _fold_packed(x):
    slabs = [x[i * PACKED_ROWS:(i + 1) * PACKED_ROWS] for i in range(x.shape[0] // PACKED_ROWS)]
    while len(slabs) > 1:
        slabs = [a + b for a, b in zip(slabs[0::2], slabs[1::2])]
    return slabs[0]


def _attn_d_kernel(q_ref, iq_all_ref, iw_all_ref, dk_ref, dvx_ref, ik_ref, before_ref, o_ref,
                   keys_sc, tau_sc, *scratch, t, topk, slopes):
    def tile(qi, q):
        return _attn_d_tile(qi, q, iq_all_ref.at[qi], iw_all_ref.at[qi], dk_ref, dvx_ref, ik_ref,
                            before_ref, keys_sc, tau_sc, scratch, t=t, topk=topk, slopes=slopes)

    _for_each_query_tile(q_ref, o_ref, t, tile)


def _attn_d_tile(qi, q_ref, iq_ref, iw_ref, dk_ref, dvx_ref, ik_ref, before_ref,
                 keys_sc, tau_sc, scratch, *, t, topk, slopes):
    digit_sc, scratch = scratch[:N_DIGITS], scratch[N_DIGITS:]
    m_sc, acc_sc = scratch[:HEADS], scratch[HEADS:]

    w = iw_ref[...] * IDX_HEADS ** -0.5

    def index_step(j, nblk, masked=False):
        ikb = ik_ref[_key_rows(j, nblk, t), :]
        zs = [_dot(ikb, iq_ref[LANES * hh:LANES * (hh + 1), :]) for hh in range(IDX_HEADS)]
        sc = w[0:1, :] * jnp.maximum(zs[0], 0.0)
        for hh in range(1, IDX_HEADS):
            sc = sc + w[hh:hh + 1, :] * jnp.maximum(zs[hh], 0.0)
        sc = jnp.where(sc == 0.0, 0.0, sc)
        if masked:
            sc = jnp.where(_causal_mask(t, nblk), sc, NEG_INF)
        bits = pltpu.bitcast(sc, I32)
        keys = jnp.where(bits < 0, bits ^ 0x7FFFFFFF, bits)
        ukeys = keys ^ INT_MIN
        digits = [(lax.shift_right_logical(ukeys, DIGIT_BITS * (N_DIGITS - 1 - d)) & DIGIT_MASK)
                  .astype(F32).astype(BF16) for d in range(N_DIGITS)]
        for b in range(nblk):
            rows = slice(b * t, (b + 1) * t)
            keys_sc[j + b] = keys[rows]
            for d in range(N_DIGITS):
                digit_sc[d][j + b] = digits[d][rows]

    _sweep_causal_blocks(qi, index_step)

    qpos = qi * t + lax.broadcasted_iota(I32, (1, t), 1)
    kt = jnp.minimum(topk, qpos + 1).astype(F32)

    def count_ge(cand):
        def body(j, acc):
            return acc + _fold_rows((keys_sc[j] >= cand).astype(F32))
        acc = lax.fori_loop(0, qi + 1, body, jnp.zeros((SUBLANES, t), F32))
        return jnp.sum(acc, axis=0, keepdims=True)

    one_b = jnp.ones((), BF16)
    zero_b = jnp.zeros((), BF16)

    def count_digit_ge(vals_sc, cand):
        cand_b = cand.astype(F32).astype(BF16)

        def block_count(j):
            return _fold_packed(jnp.where(vals_sc[j] >= cand_b, one_b, zero_b))

        def pair(i, acc):
            return acc + (block_count(2 * i) + block_count(2 * i + 1)).astype(F32)

        acc = lax.fori_loop(0, (qi + 1) // 2, pair, jnp.zeros((PACKED_ROWS, t), F32))
        acc = lax.cond(qi % 2 == 0, lambda a: a + block_count(qi).astype(F32), lambda a: a, acc)
        return jnp.sum(acc, axis=0, keepdims=True)

    def keep_matching(vals_sc, match_sc, match):
        match_b = match.astype(F32).astype(BF16)

        def body(j, carry):
            vals_sc[j] = jnp.where(match_sc[j] == match_b, vals_sc[j], -one_b)
            return carry
        lax.fori_loop(0, qi + 1, body, 0)

    zero = jnp.zeros((1, t), I32)
    rank = kt
    above = jnp.zeros((1, t), F32)
    tau_u = zero
    digit = zero
    for d in range(N_DIGITS):
        if d > 0:
            keep_matching(digit_sc[d], digit_sc[d - 1], digit)

        def bit_body(i, prefix, d=d, rank=rank):
            cand = prefix + lax.shift_left(jnp.int32(1), DIGIT_BITS - 1 - i)
            return jnp.where(count_digit_ge(digit_sc[d], cand) >= rank, cand, prefix)

        digit = lax.fori_loop(0, DIGIT_BITS, bit_body, zero)
        tau_u = lax.shift_left(tau_u, DIGIT_BITS) | digit
        if d < N_DIGITS - 1:
            higher = count_digit_ge(digit_sc[d], digit + 1)
            above = above + higher
            rank = rank - higher
    tau = tau_u ^ INT_MIN
    tau_sc[...] = tau
    excess = jnp.max(above + count_digit_ge(digit_sc[N_DIGITS - 1], digit) - kt)

    @pl.when(excess > 0.0)
    def _():
        need = kt - count_ge(tau + 1)

        def tie_body(j, seen):
            kj = keys_sc[j]
            eq = kj == tau
            eqb = eq.astype(BF16)
            earlier = _dot(before_ref[...], eqb) + seen
            keys_sc[j] = jnp.where(eq & (earlier >= need), INT_MIN, kj)
            return seen + jnp.sum(eqb.astype(F32), axis=0, keepdims=True)

        lax.fori_loop(0, qi + 1, tie_body, jnp.zeros((1, t), F32))

    _softmax_init(m_sc, acc_sc)

    def step(j, nblk):
        keys = [keys_sc[j + b] for b in range(nblk)]
        sel = (keys[0] if nblk == 1 else jnp.concatenate(keys, axis=0)) >= tau_sc[...]
        kb = dk_ref[_key_rows(j, nblk, t), :]
        vx = _value_cols(dvx_ref, j, nblk, slice(None))
        scores = [_dot(kb, q_ref[LANES * h:LANES * (h + 1), :]) for h in range(HEADS)]
        for h in range(HEADS):
            bias = _lane_repeat(_alibi(slopes[h], j, qi, t, nblk), t)
            scores[h] = jnp.where(sel, scores[h] + bias, -jnp.inf)
        _softmax_steps(scores, [vx] * HEADS, m_sc, acc_sc)

    _sweep_earlier_blocks(qi + 1, step)
    return jnp.concatenate([_softmax_result(acc_sc[h][...]) for h in range(HEADS)], axis=0).T


def _attn_d_call(qT, iqT, miscT, dkp, dvxT, ikp, before, topk, slopes):
    B, nb, _, t = qT.shape
    S = nb * t
    iw_block = MISC_IW // IDX_HEADS
    return pl.pallas_call(
        functools.partial(_attn_d_kernel, t=t, topk=topk, slopes=slopes),
        grid=(B,),
        in_specs=[_seq_cols_spec(nb, HW, t), _seq_cols_spec(nb, IDX_HEADS * LANES, t),
                  pl.BlockSpec((None, nb, IDX_HEADS, t), lambda b: (b, 0, iw_block, 0)),
                  _seq_rows_spec(S, LANES), _seq_cols_spec(nb, LANES, t), _seq_rows_spec(S, LANES),
                  _const_spec((t, t))],
        out_specs=_seq_rows_spec(S, GROUP_W),
        out_shape=jax.ShapeDtypeStruct((B, S, GROUP_W), F32),
        scratch_shapes=([pltpu.VMEM((nb, t, t), I32), pltpu.VMEM((1, t), I32)]
                        + [pltpu.VMEM((nb, t, t), BF16)] * N_DIGITS + _attn_scratch(HEADS, t)),
        compiler_params=_params("arbitrary"),
        name="attn_sparse",
    )(qT, iqT, miscT, dkp, dvxT, ikp, before)


def _out_proj_kernel(x_ref, oa_ref, ob_ref, oc_ref, od_ref, beta_ref, wo_ref, gate_ref,
                     g2_ref, scale_ref, shift_ref, wr_ref, br_ref, earlier_ref,
                     x1_ref, h2x_ref, route_ref, cnt_ref, cnt_sc):
    d_model = x_ref.shape[1]
    acc = None
    for i, o_ref in enumerate((oa_ref, ob_ref, oc_ref, od_ref)):
        sl = slice(GROUP_W * i, GROUP_W * (i + 1))
        mix = (o_ref[...] * beta_ref[:, sl]).astype(BF16)
        part = jnp.dot(mix, wo_ref[sl, :], preferred_element_type=F32)
        acc = part if acc is None else acc + part
    x1 = x_ref[...] + gate_ref[...] * acc
    x1_ref[...] = x1
    ms = jnp.mean(x1 * x1, axis=-1, keepdims=True)
    h2 = x1 * lax.rsqrt(ms + RMS_EPS) * g2_ref[...]
    h2 = h2 * (1.0 + scale_ref[...]) + shift_ref[...]
    h2x_ref[:, :d_model] = h2

    h_hi, h_lo = _split_bf16(h2, 2)
    w_hi, w_lo = _split_bf16(wr_ref[...], 2)
    logits = _dot(h_hi, w_hi) + _dot(h_hi, w_lo) + _dot(h_lo, w_hi) + br_ref[...]
    lt = logits.T
    tm = lt.shape[1]
    g = lt[0:N_GROUPS]
    gmax = jnp.max(g, axis=0, keepdims=True)
    gi = lax.broadcasted_iota(I32, g.shape, 0)
    gidx = jnp.min(jnp.where(g == gmax, gi, N_GROUPS), axis=0, keepdims=True)
    g_prob = 1.0 / jnp.sum(jnp.exp(g - gmax), axis=0, keepdims=True)
    e_sel = jnp.zeros((EXPERTS_PER_GROUP, tm), F32)
    for gg in range(N_GROUPS):
        lo = N_GROUPS + EXPERTS_PER_GROUP * gg
        e_sel = e_sel + jnp.where(gidx == gg, lt[lo:lo + EXPERTS_PER_GROUP], 0.0)
    ei = lax.broadcasted_iota(I32, e_sel.shape, 0)
    v1 = jnp.max(e_sel, axis=0, keepdims=True)
    i1 = jnp.min(jnp.where(e_sel == v1, ei, EXPERTS_PER_GROUP), axis=0, keepdims=True)
    rest = jnp.where(ei == i1, -jnp.inf, e_sel)
    v2 = jnp.max(rest, axis=0, keepdims=True)
    i2 = jnp.min(jnp.where(rest == v2, ei, EXPERTS_PER_GROUP), axis=0, keepdims=True)
    e2 = jnp.exp(v2 - v1)
    w1 = g_prob / (1.0 + e2)
    w2 = g_prob * e2 / (1.0 + e2)
    in_group = jnp.where(ei == i1, w1, 0.0) + jnp.where(ei == i2, w2, 0.0)
    cw = jnp.concatenate([in_group, jnp.zeros((LANES - EXPERTS_PER_GROUP, tm), F32)], axis=0)
    h2x_ref[:, d_model:] = cw.T

    @pl.when(pl.program_id(0) == 0)
    def _():
        cnt_sc[...] = jnp.zeros(cnt_sc.shape, F32)

    rows = lax.broadcasted_iota(I32, (SUBLANES, tm), 0)
    onehot = (rows == gidx).astype(F32)
    seen = _dot(onehot.astype(BF16), earlier_ref[...]) + cnt_sc[...]
    rank = jnp.sum(onehot * seen, axis=0, keepdims=True)
    route_ref[...] = jnp.concatenate(
        [gidx, rank.astype(I32), jnp.zeros((SUBLANES - 2, tm), I32)], axis=0)
    cnt_sc[...] = cnt_sc[...] + jnp.sum(onehot, axis=1, keepdims=True)
    cnt_ref[...] = cnt_sc[:, :LANES]


def _out_proj_call(xf, outs, beta, w_out, gate1, g2, scale2, shift2, w_r, b_r, earlier, seq, tm):
    N, D = xf.shape
    per_b = seq // tm
    row = lambda i: (i, 0)
    full = lambda i: (0, 0)
    per_batch = pl.BlockSpec((None, 1, D), lambda i: (i // per_b, 0, 0))
    return pl.pallas_call(
        _out_proj_kernel,
        grid=(N // tm,),
        in_specs=([pl.BlockSpec((tm, D), row)] + [pl.BlockSpec((tm, GROUP_W), row)] * 4
                  + [pl.BlockSpec((1, D), full), pl.BlockSpec((D, D), full), per_batch,
                     pl.BlockSpec((1, D), full), per_batch, per_batch,
                     pl.BlockSpec((D, LANES), full), pl.BlockSpec((1, LANES), full),
                     pl.BlockSpec((tm, tm), full)]),
        out_specs=[pl.BlockSpec((tm, D), row), pl.BlockSpec((tm, D + LANES), row),
                   pl.BlockSpec((SUBLANES, tm), lambda i: (0, i)),
                   pl.BlockSpec((SUBLANES, LANES), full)],
        out_shape=[jax.ShapeDtypeStruct((N, D), F32), jax.ShapeDtypeStruct((N, D + LANES), F32),
                   jax.ShapeDtypeStruct((SUBLANES, N), I32),
                   jax.ShapeDtypeStruct((SUBLANES, LANES), F32)],
        scratch_shapes=[pltpu.VMEM((SUBLANES, tm), F32)],
        compiler_params=_params("arbitrary"),
        name="out_proj_router",
    )(xf, *outs, beta, w_out, gate1, g2, scale2, shift2, w_r, b_r, earlier)


MOE_TILE = 256

def _dispatch_kernel(slot_ref, h2x_ref, zeros_ref, xs_ref, sem):
    del zeros_ref

    def row_copy(g, u):
        return pltpu.make_async_copy(h2x_ref.at[g, pl.ds(u, 1), :],
                                     xs_ref.at[pl.ds(slot_ref[0, g * SUBLANES + u], 1), :], sem)

    def issue(g, carry):
        for u in range(SUBLANES):
            row_copy(g, u).start()
        return carry

    lax.fori_loop(0, h2x_ref.shape[0], issue, 0)

    def drain(g, carry):
        for u in range(SUBLANES):
            row_copy(g, u).wait()
        return carry

    lax.fori_loop(0, h2x_ref.shape[0], drain, 0)


def _dispatch_call(slot, h2x, xs_zero, tm):
    N, DX = h2x.shape
    return pl.pallas_call(
        _dispatch_kernel,
        grid=(N // tm,),
        in_specs=[pl.BlockSpec((None, 1, tm), lambda i: (i, 0, 0), memory_space=pltpu.SMEM),
                  pl.BlockSpec((tm // SUBLANES, SUBLANES, DX), lambda i: (i, 0, 0)),
                  pl.BlockSpec(memory_space=pl.ANY)],
        out_specs=pl.BlockSpec(memory_space=pl.ANY),
        out_shape=jax.ShapeDtypeStruct(xs_zero.shape, F32),
        scratch_shapes=[pltpu.SemaphoreType.DMA(())],
        input_output_aliases={2: 0},
        compiler_params=_params("arbitrary"),
        name="moe_dispatch",
    )(slot.reshape(N // tm, 1, tm), h2x.reshape(N // SUBLANES, SUBLANES, DX), xs_zero)


def _expert_kernel(group_ref, valid_ref, xs_ref, w1f_ref, w3f_ref, w2f_ref, expand_ref, y_ref,
                   w1_ref, w3_ref, w2_ref):
    i = pl.program_id(0)
    valid = valid_ref[i]
    d_model = y_ref.shape[1]
    ff = w1_ref.shape[2]
    new_group = jnp.logical_or(i == 0, group_ref[i] != group_ref[jnp.maximum(i - 1, 0)])

    @pl.when(valid == 0)
    def _():
        y_ref[...] = jnp.zeros(y_ref.shape, F32)

    @pl.when(jnp.logical_and(valid > 0, new_group))
    def _():
        for e in range(EXPERTS_PER_GROUP):
            w1_ref[e] = w1f_ref[e].astype(BF16)
            w3_ref[e] = w3f_ref[e].astype(BF16)
            w2_ref[e] = w2f_ref[e].astype(BF16)

    @pl.when(valid > 0)
    def _():
        x = xs_ref[:, :d_model].astype(BF16)
        cw_hi, cw_lo = _split_bf16(xs_ref[:, d_model:], 2)
        cwx = _dot(cw_hi, expand_ref[...]) + _dot(cw_lo, expand_ref[...])
        acc = None
        for e in range(EXPERTS_PER_GROUP):
            a = _dot(x, w1_ref[e])
            b = _dot(x, w3_ref[e])
            hid = a * jax.nn.sigmoid(a) * b * _lane_repeat(cwx[:, LANES * e:LANES * (e + 1)], ff)
            part = _dot(hid.astype(BF16), w2_ref[e])
            acc = part if acc is None else acc + part
        y_ref[...] = acc


def _expert_call(tile_group, tile_valid, xs, w1, w3, w2, expand, layer):
    P, DX = xs.shape
    D, FF = w1.shape[-2:]
    tr = MOE_TILE

    def group_spec(rows, cols):
        return pl.BlockSpec((None, None, EXPERTS_PER_GROUP, rows, cols),
                            lambda i, g, v: (layer, g[i], 0, 0, 0),
                            pipeline_mode=pl.Buffered(1))

    grid_spec = pltpu.PrefetchScalarGridSpec(
        num_scalar_prefetch=2,
        grid=(P // tr,),
        in_specs=[pl.BlockSpec((tr, DX), lambda i, g, v: (i, 0)),
                  group_spec(D, FF), group_spec(D, FF), group_spec(FF, D),
                  pl.BlockSpec(expand.shape, lambda i, g, v: (0, 0))],
        out_specs=pl.BlockSpec((tr, D), lambda i, g, v: (i, 0)),
        scratch_shapes=[pltpu.VMEM((EXPERTS_PER_GROUP, D, FF), BF16),
                        pltpu.VMEM((EXPERTS_PER_GROUP, D, FF), BF16),
                        pltpu.VMEM((EXPERTS_PER_GROUP, FF, D), BF16)],
    )
    return pl.pallas_call(
        _expert_kernel,
        grid_spec=grid_spec,
        out_shape=jax.ShapeDtypeStruct((P, D), F32),
        compiler_params=_params("arbitrary"),
        name="moe_experts",
    )(tile_group, tile_valid, xs, w1, w3, w2, expand)


def _residual_kernel(slot_ref, slot_next_ref, x1_ref, gate_ref, ys_ref, o_ref, buf, sems):
    i = pl.program_id(0)

    def gather(slots, b, wait):
        def body(g, carry):
            for u in range(SUBLANES):
                cp = pltpu.make_async_copy(ys_ref.at[pl.ds(slots[0, g * SUBLANES + u], 1), :],
                                           buf.at[b, g, pl.ds(u, 1), :], sems.at[b])
                cp.wait() if wait else cp.start()
            return carry
        lax.fori_loop(0, buf.shape[1], body, 0)

    @pl.when(i == 0)
    def _():
        gather(slot_ref, 0, False)

    @pl.when(i + 1 < pl.num_programs(0))
    def _():
        gather(slot_next_ref, (i + 1) % 2, False)

    gather(slot_ref, i % 2, True)
    o_ref[...] = x1_ref[...] + gate_ref[...] * buf[i % 2].reshape(x1_ref.shape)


def _residual_call(slot, x1, gate2, ys, seq, tm):
    N, D = x1.shape
    n = N // tm
    per_b = seq // tm
    row = pl.BlockSpec((tm, D), lambda i: (i, 0))
    slot3 = slot.reshape(n, 1, tm)
    return pl.pallas_call(
        _residual_kernel,
        grid=(n,),
        in_specs=[pl.BlockSpec((None, 1, tm), lambda i: (i, 0, 0), memory_space=pltpu.SMEM),
                  pl.BlockSpec((None, 1, tm), lambda i: (jnp.minimum(i + 1, n - 1), 0, 0),
                               memory_space=pltpu.SMEM),
                  row, pl.BlockSpec((None, 1, D), lambda i: (i // per_b, 0, 0)),
                  pl.BlockSpec(memory_space=pl.ANY)],
        out_specs=row,
        out_shape=jax.ShapeDtypeStruct((N, D), F32),
        scratch_shapes=[pltpu.VMEM((2, tm // SUBLANES, SUBLANES, D), F32),
                        pltpu.SemaphoreType.DMA((2,))],
        compiler_params=_params("arbitrary"),
        name="moe_residual",
    )(slot3, slot3, x1, gate2, ys)


def _moe_routing(route, cnt, n_tokens):
    tr = MOE_TILE
    n_tiles = n_tokens // tr + N_GROUPS
    counts = cnt[:N_GROUPS, 0].astype(I32)
    padded = (counts + tr - 1) // tr * tr
    ends = jnp.cumsum(padded)
    starts = ends - padded
    group, rank = route[0], route[1]
    slot = starts[group] + rank
    tile_start = jnp.arange(n_tiles, dtype=I32) * tr
    tile_group = jnp.minimum(jnp.sum((tile_start[:, None] >= ends[None, :]).astype(I32), axis=1),
                             N_GROUPS - 1)
    tile_valid = jnp.clip(starts[tile_group] + counts[tile_group] - tile_start, 0, tr)
    tile_valid = jnp.where(tile_start < ends[-1], tile_valid, 0)
    return slot, tile_group, tile_valid


def _block_diag_mean(width, group, valid_in_128=None):
    i = jnp.arange(width)
    same = (i[:, None] // group) == (i[None, :] // group)
    if valid_in_128 is not None:
        same = same & ((i[:, None] % LANES) < valid_in_128) & ((i[None, :] % LANES) < valid_in_128)
    return jnp.where(same, 1.0 / group, 0.0).astype(BF16)


def _placement(src_width, dst_width, pairs):
    src = jnp.array([p[0] for p in pairs], I32)
    dst = jnp.array([p[1] for p in pairs], I32)
    return jnp.zeros((src_width, dst_width), F32).at[src, dst].set(1.0).astype(BF16)


def _placements():
    per_head = [(HEAD_DIM * h + d, LANES * h + d) for h in range(HEADS) for d in range(HEAD_DIM)]
    diff_q = [(HEAD_DIM * h + DIFF_DIM * c + d, LANES * (2 * h + c) + DIFF_DIM * c + d)
              for h in range(HEADS) for c in range(2) for d in range(DIFF_DIM)]
    idx_q = [(IDX_DIM * hh + d, LANES * hh + d) for hh in range(IDX_HEADS) for d in range(IDX_DIM)]
    ident = [(d, d) for d in range(GROUP_W)]
    first64 = [(d, d) for d in range(HEAD_DIM)]
    second64 = [(HEAD_DIM + d, d) for d in range(HEAD_DIM)]
    idx_k = [(MISC_IK + d, d) for d in range(IDX_DIM)]
    table = {
        "a_qT": (GROUP_W, per_head), "a_kp": (GROUP_W, per_head), "a_vxT": (GROUP_W, per_head),
        "b_qT": (GROUP_W, per_head), "b_kp": (GROUP_W, per_head), "b_vT": (GROUP_W, ident),
        "c_qT": (GROUP_W, diff_q), "c_kp": (GROUP_W, per_head), "c_vxT": (GROUP_W, per_head),
        "d_qT": (GROUP_W, per_head), "i_qT": (GROUP_W, idx_q),
        "d_kp": (GROUP_W, first64), "d_vxT": (GROUP_W, second64), "i_kp": (LANES, idx_k),
    }
    out = []
    for name, src, width, orient, ones in OUTPUTS:
        src_width, pairs = table[name]
        p = _placement(src_width, width, pairs)
        out.append(p if orient == "rows" else p.T)
    return out


def kernel(x, c, ada_w, ada_b, norm1_g, norm2_g, w_in, b_f, qn_a, kn_a, qn_c, kn_c,
           lam_q1, lam_k1, lam_q2, lam_k2, subln_g, qn_d, kn_d, mix_beta, w_out,
           w_group, b_group, w_expert, b_expert, w1, w3, w2):
    B, S, D = x.shape
    L = ada_w.shape[0]
    N = B * S
    topk = min(TOPK_MAX, S // 4)
    t = ATTN_BLOCK
    tm = t
    slopes = [2.0 ** (-8.0 * i / (2 * HEADS)) for i in range(1, 2 * HEADS + 1)]
    slopes_c, slopes_d = tuple(slopes[0::2]), tuple(slopes[1::2])

    idx_t = jnp.arange(t)
    after = (idx_t[None, :] > idx_t[:, None]).astype(BF16)
    before = (idx_t[None, :] < idx_t[:, None]).astype(BF16)
    upto = (idx_t[None, :] <= idx_t[:, None]).astype(BF16)
    g64 = _block_diag_mean(GROUP_W, HEAD_DIM)
    g32 = _block_diag_mean(GROUP_W, DIFF_DIM)
    places = _placements()
    lane = jnp.arange(LANES)
    cum_sel = jnp.stack([jnp.broadcast_to((lane == MISC_AF + h)[:, None], (LANES, LANES))
                         for h in range(HEADS)]).astype(BF16)
    wide = jnp.arange(EXPERTS_PER_GROUP * LANES)
    expand = (lane[:, None] == wide[None, :] // LANES).astype(BF16)

    mod = _ada_call(c, ada_w, ada_b)
    xf = x.reshape(N, D)

    for l in range(L):
        m6 = mod[l].reshape(B, 6, 1, D)
        shift1, scale1, gate1, shift2, scale2, gate2 = (m6[:, i] for i in range(6))

        w_all = w_in[l].astype(BF16)
        ones = jnp.ones((GROUP_W - HEAD_DIM,), F32)
        gains = jnp.stack([jnp.tile(qn_a[l], HEADS) * (HEAD_DIM ** -0.5 * LOG2E),
                           jnp.tile(kn_a[l], HEADS),
                           jnp.tile(qn_c[l], 2 * HEADS) * (DIFF_DIM ** -0.5 * LOG2E),
                           jnp.tile(kn_c[l], 2 * HEADS),
                           jnp.tile(qn_d[l], HEADS) * (HEAD_DIM ** -0.5 * LOG2E),
                           jnp.concatenate([kn_d[l], ones])]).astype(F32)
        gains = jnp.concatenate([gains, jnp.zeros((2, GROUP_W), F32)], axis=0)

        outs = _in_proj_call(xf, scale1, shift1, norm1_g[l].reshape(1, D), w_all, gains,
                             g64, g32, places, B, S, tm)
        sec = {name: o for (name, _, _, _, _), o in zip(OUTPUTS, outs)}
        for name, src, width, orient, ones_ in OUTPUTS:
            if orient == "rows":
                sec[name] = sec[name].reshape(B, S, width)
        misc = outs[len(OUTPUTS)].reshape(B, S, LANES)
        miscT = outs[len(OUTPUTS) + 1]

        bf_row = jnp.zeros((1, LANES), F32).at[0, MISC_AF:MISC_AF + HEADS].set(b_f[l].astype(F32))
        cumrep = _cum_call(misc, bf_row, upto, cum_sel)

        o_a = _attn_a_call(sec["a_qT"], sec["a_kp"], sec["a_vxT"], cumrep)
        o_b = _attn_b_call(sec["b_qT"], sec["b_kp"], sec["b_vT"], after)
        lambda_init = 0.8 - 0.6 * math.exp(-0.3 * l)
        lamv = jnp.stack([lam_q1[l], lam_k1[l], lam_q2[l], lam_k2[l]]).astype(F32)
        o_c = _attn_c_call(sec["c_qT"], sec["c_kp"], sec["c_vxT"], lamv,
                           subln_g[l].reshape(HEAD_DIM, 1).astype(F32), slopes_c, lambda_init)
        o_d = _attn_d_call(sec["d_qT"], sec["i_qT"], miscT, sec["d_kp"], sec["d_vxT"], sec["i_kp"],
                           before, topk, slopes_d)

        w_r = jnp.concatenate([w_group[l], w_expert[l],
                               jnp.zeros((D, LANES - N_GROUPS - N_EXPERTS), F32)], axis=1)
        b_r = jnp.concatenate([b_group[l], b_expert[l],
                               jnp.zeros((LANES - N_GROUPS - N_EXPERTS,), F32)]).reshape(1, LANES)
        x1, h2x, route, cnt = _out_proj_call(
            xf, [o.reshape(N, GROUP_W) for o in (o_a, o_b, o_c, o_d)], mix_beta[l].reshape(1, D),
            w_out[l].astype(BF16), gate1, norm2_g[l].reshape(1, D), scale2, shift2, w_r, b_r,
            after, S, tm)

        slot, tile_group, tile_valid = _moe_routing(route, cnt, N)
        xs = _dispatch_call(slot, h2x, jnp.zeros((tile_group.shape[0] * MOE_TILE, D + LANES), F32), tm)
        ys = _expert_call(tile_group, tile_valid, xs, w1, w3, w2, expand, l)
        xf = _residual_call(slot, x1, gate2, ys, S, tm)

    return xf.reshape(B, S, D)
```

```python
import functools
import math

import jax
import jax.numpy as jnp
from jax import lax
from jax.experimental import pallas as pl
from jax.experimental.pallas import tpu as pltpu

F32 = jnp.float32
BF16 = jnp.bfloat16
I32 = jnp.int32

HEAD_DIM = 64
HEADS = 4
GROUP_W = HEADS * HEAD_DIM
DIFF_DIM = HEAD_DIM // 2
IDX_HEADS = 8
IDX_DIM = 32
TOPK_MAX = 256
N_GROUPS = 4
EXPERTS_PER_GROUP = 8
N_EXPERTS = N_GROUPS * EXPERTS_PER_GROUP
EXPERT_FF = 256
RMS_EPS = 1e-6
NEG_INF = -1e30
INT_MIN = -(2 ** 31)
LOG2E = math.log2(math.e)

LANES = 128
SUBLANES = 8
ATTN_BLOCK = 256
VMEM_LIMIT = 56 * 1024 * 1024

MISC_IK = 0
MISC_IW = IDX_DIM
MISC_AF = IDX_DIM + IDX_HEADS


def _params(*sem):
    return pltpu.CompilerParams(dimension_semantics=sem, vmem_limit_bytes=VMEM_LIMIT)


def _log_sigmoid(z):
    return jnp.minimum(z, 0.0) - jnp.log1p(jnp.exp(-jnp.abs(z)))


def _log2_sigmoid(z2):
    return jnp.minimum(z2, 0.0) - jnp.log2(1.0 + jnp.exp2(-jnp.abs(z2)))


def _split_bf16(x, parts):
    out = []
    rem = x
    for _ in range(parts):
        p = rem.astype(BF16)
        out.append(p)
        rem = rem - p.astype(F32)
    return out


def _dot(a, b):
    return jnp.dot(a, b, preferred_element_type=F32)


def _ada_kernel(c_ref, w_ref, b_ref, o_ref):
    c = c_ref[...]
    ca = c * jax.nn.sigmoid(c)
    o_ref[...] = jnp.dot(ca, w_ref[...], precision=lax.Precision.HIGHEST,
                         preferred_element_type=F32) + b_ref[...]


def _ada_call(c, ada_w, ada_b):
    L, D, E = ada_w.shape
    B = c.shape[0]
    tn = 1536
    return pl.pallas_call(
        _ada_kernel,
        grid=(L, E // tn),
        in_specs=[pl.BlockSpec((B, D), lambda l, j: (0, 0)),
                  pl.BlockSpec((None, D, tn), lambda l, j: (l, 0, j)),
                  pl.BlockSpec((None, 1, tn), lambda l, j: (l, 0, j))],
        out_specs=pl.BlockSpec((None, B, tn), lambda l, j: (l, 0, j)),
        out_shape=jax.ShapeDtypeStruct((L, B, E), F32),
        compiler_params=_params("arbitrary", "arbitrary"),
        name="ada_mod",
    )(c, ada_w, ada_b.reshape(L, 1, E))


SOURCES = ("a_q", "a_k", "a_v", "b_q", "b_k", "b_v", "c_q", "c_k", "c_v", "d_q", "i_q", "d_kv")
SRC_NORM = {"a_q": ("n64", 0), "a_k": ("n64", 1), "c_q": ("n32", 2), "c_k": ("n32", 3),
            "d_q": ("n64", 4), "d_kv": ("n64", 5)}
SRC_SCALE = {"b_q": HEAD_DIM ** -0.5 * LOG2E, "i_q": IDX_DIM ** -0.5}
HW = HEADS * LANES
OUTPUTS = (
    ("a_qT", "a_q", HW, "cols", False), ("a_kp", "a_k", HW, "rows", False),
    ("a_vxT", "a_v", HW, "cols", True),
    ("b_qT", "b_q", HW, "cols", False), ("b_kp", "b_k", HW, "rows", False),
    ("b_vT", "b_v", GROUP_W, "cols", False),
    ("c_qT", "c_q", 2 * HW, "cols", False), ("c_kp", "c_k", HW, "rows", False),
    ("c_vxT", "c_v", HW, "cols", True),
    ("d_qT", "d_q", HW, "cols", False), ("i_qT", "i_q", IDX_HEADS * LANES, "cols", False),
    ("d_kp", "d_kv", LANES, "rows", False), ("d_vxT", "d_kv", LANES, "cols", True),
    ("i_kp", "misc", LANES, "rows", False),
)
N_MAIN = len(SOURCES) * GROUP_W
N_ROW_OUTPUTS = sum(1 for o in OUTPUTS if o[3] == "rows")


def _col_plans():
    per_head = [p for h in range(HEADS) for p in ((HEAD_DIM * h, HEAD_DIM), (None, HEAD_DIM))]
    diff_q = []
    for h in range(HEADS):
        for c in range(2):
            lead = DIFF_DIM * c
            if lead:
                diff_q.append((None, lead))
            diff_q.append((HEAD_DIM * h + DIFF_DIM * c, DIFF_DIM))
            diff_q.append((None, LANES - lead - DIFF_DIM))
    idx_q = [p for hh in range(IDX_HEADS) for p in ((IDX_DIM * hh, IDX_DIM), (None, LANES - IDX_DIM))]
    return {"a_qT": per_head, "a_vxT": per_head, "b_qT": per_head, "b_vT": [(0, GROUP_W)],
            "c_qT": diff_q, "c_vxT": per_head, "d_qT": per_head, "i_qT": idx_q,
            "d_vxT": [(HEAD_DIM, HEAD_DIM), (None, HEAD_DIM)]}


COL_PLANS = _col_plans()

IN_SPLITS = (GROUP_W, GROUP_W, GROUP_W, HEADS, GROUP_W, GROUP_W, GROUP_W, GROUP_W, GROUP_W,
             GROUP_W, GROUP_W, HEAD_DIM, HEAD_DIM, IDX_HEADS * IDX_DIM, IDX_DIM, IDX_HEADS)
IN_NAMES = ("a_q", "a_k", "a_v", "a_f", "b_q", "b_k", "b_v", "c_q", "c_k", "c_v",
            "d_q", "d_k", "d_v", "i_q", "i_k", "i_w")
IN_OFFSETS = {n: (sum(IN_SPLITS[:i]), IN_SPLITS[i]) for i, n in enumerate(IN_NAMES)}
P_IN = sum(IN_SPLITS)
SECTION_PARTS = {name: (name,) for name in SOURCES if name != "d_kv"}
SECTION_PARTS["d_kv"] = ("d_k", "d_v")
SECTION_PARTS["misc"] = ("i_k", "i_w", "a_f")
REALIGN_ROWS = 256


def _realign_weights(w_ref, w_sc):
    d_model = w_ref.shape[0]
    for r0 in range(0, d_model, REALIGN_ROWS):
        rows = slice(r0, r0 + REALIGN_ROWS)
        for i, name in enumerate(SOURCES + ("misc",)):
            width = LANES if name == "misc" else GROUP_W
            pieces, used = [], 0
            for part in SECTION_PARTS[name]:
                off, w = IN_OFFSETS[part]
                base = off // LANES * LANES
                end = min(-(-(off + w) // LANES) * LANES, P_IN)
                window = w_ref[rows, base:end]
                pieces.append(window[:, off - base:off - base + w])
                used += w
            if used < width:
                pieces.append(jnp.zeros((REALIGN_ROWS, width - used), BF16))
            block = pieces[0] if len(pieces) == 1 else jnp.concatenate(pieces, axis=1)
            w_sc[rows, i * GROUP_W:i * GROUP_W + width] = block


def _in_proj_kernel(*refs):
    (x_ref, scale_ref, shift_ref, g1_ref, w_in_ref, gains_ref, g64_ref, g32_ref) = refs[:8]
    place_refs = refs[8:8 + N_ROW_OUTPUTS]
    out_refs = refs[8 + N_ROW_OUTPUTS:-1]
    w_ref = refs[-1]

    @pl.when(pl.program_id(0) == 0)
    def _():
        _realign_weights(w_in_ref, w_ref)

    x = x_ref[...]
    ms = jnp.mean(x * x, axis=-1, keepdims=True)
    h = x * lax.rsqrt(ms + RMS_EPS) * g1_ref[...]
    h = h * (1.0 + scale_ref[...]) + shift_ref[...]
    hb = h.astype(BF16)

    wm = w_ref[:, N_MAIN:N_MAIN + LANES]
    h_hi, h_lo = _split_bf16(h, 2)
    misc = _dot(h_hi, wm) + _dot(h_lo, wm)
    out_refs[len(OUTPUTS)][...] = misc
    out_refs[len(OUTPUTS) + 1][...] = misc.T

    compact = {"misc": misc.astype(BF16)}
    final = {}
    raw = {name: _dot(hb, w_ref[:, i * GROUP_W:(i + 1) * GROUP_W])
           for i, name in enumerate(SOURCES)}
    msqs = {name: _dot((raw[name] * raw[name]).astype(BF16),
                       (g64_ref if SRC_NORM[name][0] == "n64" else g32_ref)[...])
            for name in SOURCES if name in SRC_NORM}
    for name in SOURCES:
        sec = raw[name]
        if name in SRC_NORM:
            r = SRC_NORM[name][1]
            fac = lax.rsqrt(msqs[name] + RMS_EPS)
            if name == "d_kv":
                lane = lax.broadcasted_iota(I32, sec.shape, 1)
                fac = jnp.where(lane < HEAD_DIM, fac, 1.0)
            sec = sec * fac * gains_ref[r:r + 1, :]
        elif name in SRC_SCALE:
            sec = sec * SRC_SCALE[name]
        final[name] = sec
        compact[name] = sec.astype(BF16)

    tokens = x.shape[0]
    transposed = {}
    place_iter = iter(place_refs)
    for (name, src, width, orient, ones), o_ref in zip(OUTPUTS, out_refs):
        if orient == "rows":
            o_ref[...] = _dot(compact[src], next(place_iter)[...]).astype(BF16)
            continue
        if src not in transposed:
            transposed[src] = final[src].T
        slabs = []
        for start, rows in COL_PLANS[name]:
            if start is None:
                slabs.append(jnp.full((rows, tokens), 1.0 if ones else 0.0, F32))
            else:
                slabs.append(transposed[src][start:start + rows])
        o_ref[...] = jnp.concatenate(slabs, axis=0).astype(BF16)


def _in_proj_call(xf, scale1, shift1, g1, w_all, gains, g64, g32, places, batch, seq, tm):
    N, D = xf.shape
    per_b = seq // tm
    row = lambda i: (i, 0)
    full = lambda i: (0, 0)
    col4 = lambda i: (i // per_b, i % per_b, 0, 0)
    out_shape, out_specs = [], []
    for name, src, width, orient, ones in OUTPUTS:
        if orient == "rows":
            out_shape.append(jax.ShapeDtypeStruct((N, width), BF16))
            out_specs.append(pl.BlockSpec((tm, width), row))
        else:
            out_shape.append(jax.ShapeDtypeStruct((batch, per_b, width, tm), BF16))
            out_specs.append(pl.BlockSpec((None, None, width, tm), col4))
    out_shape += [jax.ShapeDtypeStruct((N, LANES), F32),
                  jax.ShapeDtypeStruct((batch, per_b, LANES, tm), F32)]
    out_specs += [pl.BlockSpec((tm, LANES), row), pl.BlockSpec((None, None, LANES, tm), col4)]
    return pl.pallas_call(
        _in_proj_kernel,
        grid=(N // tm,),
        in_specs=([pl.BlockSpec((tm, D), row),
                   pl.BlockSpec((None, 1, D), lambda i: (i // per_b, 0, 0)),
                   pl.BlockSpec((None, 1, D), lambda i: (i // per_b, 0, 0)),
                   pl.BlockSpec((1, D), full),
                   pl.BlockSpec(w_all.shape, full),
                   pl.BlockSpec(gains.shape, full),
                   pl.BlockSpec(g64.shape, full),
                   pl.BlockSpec(g32.shape, full)]
                  + [pl.BlockSpec(p.shape, full) for p in places]),
        out_specs=out_specs,
        out_shape=out_shape,
        scratch_shapes=[pltpu.VMEM((D, N_MAIN + LANES), BF16)],
        compiler_params=_params("arbitrary"),
        name="in_proj",
    )(xf, scale1, shift1, g1, w_all, gains, g64, g32, *places)


def _cum_kernel(misc_ref, bf_ref, tri_ref, sel_ref, o_ref, *, blk):
    seq = misc_ref.shape[0]
    carry = jnp.zeros((1, LANES), F32)
    tri = tri_ref[...]
    for j in range(seq // blk):
        rows = slice(j * blk, (j + 1) * blk)
        lf = _log_sigmoid(misc_ref[rows, :] + bf_ref[...])
        c = carry
        for part in _split_bf16(lf, 3):
            c = c + _dot(tri, part)
        carry = c[blk - 1:blk, :]
        parts = _split_bf16(c, 3)
        for h in range(HEADS):
            rep = _dot(parts[0], sel_ref[h]) + _dot(parts[1], sel_ref[h]) + _dot(parts[2], sel_ref[h])
            o_ref[h, rows, :] = rep * LOG2E


def _cum_call(misc, bf_row, tri, sel):
    B, S, _ = misc.shape
    blk = tri.shape[0]
    return pl.pallas_call(
        functools.partial(_cum_kernel, blk=blk),
        grid=(B,),
        in_specs=[pl.BlockSpec((None, S, LANES), lambda b: (b, 0, 0)),
                  pl.BlockSpec((1, LANES), lambda b: (0, 0)),
                  pl.BlockSpec((blk, blk), lambda b: (0, 0)),
                  pl.BlockSpec((HEADS, LANES, LANES), lambda b: (0, 0, 0))],
        out_specs=pl.BlockSpec((None, HEADS, S, LANES), lambda b: (b, 0, 0, 0)),
        out_shape=jax.ShapeDtypeStruct((B, HEADS, S, LANES), F32),
        compiler_params=_params("arbitrary"),
        name="forget_cumsum",
    )(misc, bf_row, tri, sel)


def _seq_rows_spec(seq, width):
    return pl.BlockSpec((None, seq, width), lambda b: (b, 0, 0))


def _seq_cols_spec(nb, width, t):
    return pl.BlockSpec((None, nb, width, t), lambda b: (b, 0, 0, 0))


def _const_spec(shape):
    return pl.BlockSpec(shape, lambda b: (0,) * len(shape))


def _for_each_query_tile(q_ref, o_ref, t, tile_fn):
    def body(qi, carry):
        o_ref[pl.ds(pl.multiple_of(qi * t, t), t), :] = tile_fn(qi, q_ref.at[qi])
        return carry

    lax.fori_loop(0, q_ref.shape[0], body, 0)


def _lane_repeat(x, t):
    return jnp.concatenate([x] * (t // LANES), axis=1)


def _softmax_steps(scores, values, m_refs, acc_refs):
    probs, alphas = [], []
    for s, m_ref in zip(scores, m_refs):
        m_old = m_ref[...]
        m_new = jnp.maximum(m_old, jnp.max(s, axis=0, keepdims=True))
        alphas.append(jnp.exp2(m_old - m_new))
        probs.append(jnp.exp2(s - m_new).astype(BF16))
        m_ref[...] = m_new
    for p, vx, alpha, acc_ref in zip(probs, values, alphas, acc_refs):
        acc_ref[...] = alpha * acc_ref[...] + _dot(vx, p)


def _softmax_init(m_scs, acc_scs):
    for m_sc, acc_sc in zip(m_scs, acc_scs):
        m_sc[...] = jnp.full(m_sc.shape, NEG_INF, F32)
        acc_sc[...] = jnp.zeros(acc_sc.shape, F32)


def _softmax_result(acc):
    return acc[:HEAD_DIM] / acc[HEAD_DIM:]


def _alibi(slope, j, qi, t, nblk=1):
    key = lax.broadcasted_iota(I32, (nblk * t, LANES), 0)
    return (slope * LOG2E) * (key + (j - qi) * t).astype(F32)


def _sweep_earlier_blocks(qi, step):
    def body(i, carry):
        step(2 * i, 2)
        return carry

    lax.fori_loop(0, qi // 2, body, 0)

    @pl.when(qi % 2 == 1)
    def _():
        step(qi - 1, 1)


def _sweep_causal_blocks(qi, step):
    @pl.when(qi == 0)
    def _():
        step(0, 1, True)

    @pl.when(qi > 0)
    def _():
        step(qi - 1, 2, True)

    _sweep_earlier_blocks(jnp.maximum(qi - 1, 0), step)


def _causal_mask(t, nblk, strict=False):
    key = lax.broadcasted_iota(I32, (nblk * t, t), 0) - (nblk - 1) * t
    query = lax.broadcasted_iota(I32, (nblk * t, t), 1)
    return key < query if strict else key <= query


def _key_rows(j, nblk, t):
    return pl.ds(pl.multiple_of(j * t, t), nblk * t)


def _value_cols(vx_ref, j, nblk, rows):
    tiles = [vx_ref[j + b, rows, :] for b in range(nblk)]
    return tiles[0] if nblk == 1 else jnp.concatenate(tiles, axis=1)


def _attn_a_kernel(q_ref, k_ref, vx_ref, cum_ref, o_ref, *scratch, t):
    m_sc, acc_sc = scratch[:HEADS], scratch[HEADS:]

    def tile(qi, q):
        _softmax_init(m_sc, acc_sc)

        def step(j, nblk, masked=False):
            rows = _key_rows(j, nblk, t)
            heads = [slice(LANES * h, LANES * (h + 1)) for h in range(HEADS)]
            scores = [_dot(k_ref[rows, hs], q[hs, :]) for hs in heads]
            for h in range(HEADS):
                s = scores[h] - _lane_repeat(cum_ref[h, rows, :], t)
                scores[h] = jnp.where(_causal_mask(t, nblk), s, -jnp.inf) if masked else s
            _softmax_steps(scores, [_value_cols(vx_ref, j, nblk, hs) for hs in heads],
                           m_sc, acc_sc)

        _sweep_causal_blocks(qi, step)
        return jnp.concatenate([_softmax_result(acc_sc[h][...]) for h in range(HEADS)], axis=0).T

    _for_each_query_tile(q_ref, o_ref, t, tile)


def _attn_scratch(n, t, rows=LANES):
    return [pltpu.VMEM((1, t), F32)] * n + [pltpu.VMEM((rows, t), F32)] * n


def _attn_a_call(qT, kp, vxT, cumrep):
    B, nb, _, t = qT.shape
    S = nb * t
    return pl.pallas_call(
        functools.partial(_attn_a_kernel, t=t),
        grid=(B,),
        in_specs=[_seq_cols_spec(nb, HW, t), _seq_rows_spec(S, HW), _seq_cols_spec(nb, HW, t),
                  pl.BlockSpec((None, HEADS, S, LANES), lambda b: (b, 0, 0, 0))],
        out_specs=_seq_rows_spec(S, GROUP_W),
        out_shape=jax.ShapeDtypeStruct((B, S, GROUP_W), F32),
        scratch_shapes=_attn_scratch(HEADS, t),
        compiler_params=_params("arbitrary"),
        name="attn_forget",
    )(qT, kp, vxT, cumrep)


STICK_SPLIT_TERMS = 2


def _attn_b_kernel(q_ref, k_ref, v_ref, after_ref, o_ref, *scratch, t):
    tile = functools.partial(_attn_b_tile, k_ref=k_ref, v_ref=v_ref, after_ref=after_ref,
                             r_sc=scratch[:HEADS], acc_sc=scratch[HEADS:], t=t)
    _for_each_query_tile(q_ref, o_ref, t, tile)


def _attn_b_tile(qi, q_ref, *, k_ref, v_ref, after_ref, r_sc, acc_sc, t):
    for h in range(HEADS):
        r_sc[h][...] = jnp.zeros(r_sc[h].shape, F32)
        acc_sc[h][...] = jnp.zeros(acc_sc[h].shape, F32)

    def step(j, nblk, masked=False):
        after = after_ref[...]
        rows = _key_rows(j, nblk, t)
        heads = [slice(LANES * h, LANES * (h + 1)) for h in range(HEADS)]
        zs = [_dot(k_ref[rows, hs], q_ref[hs, :]) for hs in heads]
        lbs, splits, later_sums = [], [], []
        for h in range(HEADS):
            lb = _log2_sigmoid(zs[h])
            lm = lb - zs[h]
            if masked:
                lm = jnp.where(_causal_mask(t, nblk, strict=True), lm, 0.0)
            blocks = [lm[b * t:(b + 1) * t] for b in range(nblk)]
            splits.append([_split_bf16(blk, STICK_SPLIT_TERMS) for blk in blocks])
            sums = [jnp.sum(blk, axis=0, keepdims=True) for blk in blocks]
            r_old = r_sc[h][...]
            total = sums[0]
            for s_ in sums[1:]:
                total = total + s_
            r_sc[h][...] = r_old + total
            lbs.append(lb + r_old)
            later, run = [], None
            for b in reversed(range(nblk)):
                later.append(run)
                run = sums[b] if run is None else run + sums[b]
            later_sums.append(later[::-1])
        suffixes = [[functools.reduce(lambda a, b: a + b, [_dot(after, term) for term in terms])
                     for terms in splits[h]] for h in range(HEADS)]
        ws = []
        for h in range(HEADS):
            parts = [suffixes[h][b] if later_sums[h][b] is None else suffixes[h][b] + later_sums[h][b]
                     for b in range(nblk)]
            suffix = parts[0] if nblk == 1 else jnp.concatenate(parts, axis=0)
            w = jnp.exp2(lbs[h] + suffix)
            if masked:
                w = jnp.where(_causal_mask(t, nblk, strict=True), w, 0.0)
            ws.append(w.astype(BF16))
        for h in range(HEADS):
            acc_sc[h][...] += _dot(_value_cols(v_ref, j, nblk, slice(HEAD_DIM * h, HEAD_DIM * (h + 1))),
                                   ws[h])

    @pl.when(qi == 0)
    def _():
        step(0, 1, True)

    @pl.when(qi > 0)
    def _():
        step(qi - 1, 2, True)

    rest = jnp.maximum(qi - 1, 0)

    @pl.when(rest % 2 == 1)
    def _():
        step(rest - 1, 1)

    pairs = rest // 2

    def body(i, carry):
        step(2 * (pairs - 1 - i), 2)
        return carry

    lax.fori_loop(0, pairs, body, 0)
    return jnp.concatenate([acc_sc[h][...] for h in range(HEADS)], axis=0).T


def _attn_b_call(qT, kp, vT, after):
    B, nb, _, t = qT.shape
    S = nb * t
    return pl.pallas_call(
        functools.partial(_attn_b_kernel, t=t),
        grid=(B,),
        in_specs=[_seq_cols_spec(nb, HW, t), _seq_rows_spec(S, HW), _seq_cols_spec(nb, GROUP_W, t),
                  _const_spec((t, t))],
        out_specs=_seq_rows_spec(S, GROUP_W),
        out_shape=jax.ShapeDtypeStruct((B, S, GROUP_W), F32),
        scratch_shapes=_attn_scratch(HEADS, t, HEAD_DIM),
        compiler_params=_params("arbitrary"),
        name="attn_stick",
    )(qT, kp, vT, after)


def _attn_c_kernel(q_ref, k_ref, vx_ref, lamv_ref, subg_ref, o_ref, *scratch,
                   t, slopes, lambda_init):
    tile = functools.partial(_attn_c_tile, k_ref=k_ref, vx_ref=vx_ref, lamv_ref=lamv_ref,
                             subg_ref=subg_ref, m_sc=scratch[:2 * HEADS], acc_sc=scratch[2 * HEADS:],
                             t=t, slopes=slopes, lambda_init=lambda_init)
    _for_each_query_tile(q_ref, o_ref, t, tile)


def _attn_c_tile(qi, q_ref, *, k_ref, vx_ref, lamv_ref, subg_ref, m_sc, acc_sc,
                 t, slopes, lambda_init):
    _softmax_init(m_sc, acc_sc)

    def step(j, nblk, masked=False):
        rows = _key_rows(j, nblk, t)
        heads = [slice(LANES * h, LANES * (h + 1)) for h in range(HEADS)]
        scores = [_dot(k_ref[rows, heads[g // 2]], q_ref[LANES * g:LANES * (g + 1), :])
                  for g in range(2 * HEADS)]
        for g in range(2 * HEADS):
            s = scores[g] + _lane_repeat(_alibi(slopes[g // 2], j, qi, t, nblk), t)
            scores[g] = jnp.where(_causal_mask(t, nblk), s, -jnp.inf) if masked else s
        _softmax_steps(scores, [_value_cols(vx_ref, j, nblk, heads[g // 2])
                                for g in range(2 * HEADS)], m_sc, acc_sc)

    _sweep_causal_blocks(qi, step)

    lv = lamv_ref[...]
    lam = (jnp.exp(jnp.sum(lv[0:1] * lv[1:2], axis=-1, keepdims=True))
           - jnp.exp(jnp.sum(lv[2:3] * lv[3:4], axis=-1, keepdims=True)) + lambda_init)
    outs = []
    for h in range(HEADS):
        o = (_softmax_result(acc_sc[2 * h][...])
             - lam * _softmax_result(acc_sc[2 * h + 1][...]))
        ms = jnp.mean(o * o, axis=0, keepdims=True)
        outs.append(o * lax.rsqrt(ms + RMS_EPS) * subg_ref[...] * (1.0 - lambda_init))
    return jnp.concatenate(outs, axis=0).T


def _attn_c_call(qT, kp, vxT, lamv, subg_col, slopes, lambda_init):
    B, nb, _, t = qT.shape
    S = nb * t
    return pl.pallas_call(
        functools.partial(_attn_c_kernel, t=t, slopes=slopes, lambda_init=lambda_init),
        grid=(B,),
        in_specs=[_seq_cols_spec(nb, 2 * HW, t), _seq_rows_spec(S, HW), _seq_cols_spec(nb, HW, t),
                  _const_spec(lamv.shape), _const_spec(subg_col.shape)],
        out_specs=_seq_rows_spec(S, GROUP_W),
        out_shape=jax.ShapeDtypeStruct((B, S, GROUP_W), F32),
        scratch_shapes=_attn_scratch(2 * HEADS, t),
        compiler_params=_params("arbitrary"),
        name="attn_diff",
    )(qT, kp, vxT, lamv, subg_col)


def _fold_rows(x, group=SUBLANES):
    return jnp.sum(x.reshape(x.shape[0] // group, group, x.shape[1]), axis=0)


PACKED_ROWS = 2 * SUBLANES
DIGIT_BITS = 8
N_DIGITS = 32 // DIGIT_BITS
DIGIT_MASK = (1 << DIGIT_BITS) - 1


def _fold_packed(x):
    slabs = [x[i * PACKED_ROWS:(i + 1) * PACKED_ROWS] for i in range(x.shape[0] // PACKED_ROWS)]
    while len(slabs) > 1:
        slabs = [a + b for a, b in zip(slabs[0::2], slabs[1::2])]
    return slabs[0]


def _attn_d_kernel(q_ref, iq_all_ref, iw_all_ref, dk_ref, dvx_ref, ik_ref, before_ref, o_ref,
                   keys_sc, tau_sc, *scratch, t, topk, slopes):
    def tile(qi, q):
        return _attn_d_tile(qi, q, iq_all_ref.at[qi], iw_all_ref.at[qi], dk_ref, dvx_ref, ik_ref,
                            before_ref, keys_sc, tau_sc, scratch, t=t, topk=topk, slopes=slopes)

    _for_each_query_tile(q_ref, o_ref, t, tile)


def _attn_d_tile(qi, q_ref, iq_ref, iw_ref, dk_ref, dvx_ref, ik_ref, before_ref,
                 keys_sc, tau_sc, scratch, *, t, topk, slopes):
    digit_sc, scratch = scratch[:N_DIGITS], scratch[N_DIGITS:]
    m_sc, acc_sc = scratch[:HEADS], scratch[HEADS:]

    w = iw_ref[...] * IDX_HEADS ** -0.5

    def index_step(j, nblk, masked=False):
        ikb = ik_ref[_key_rows(j, nblk, t), :]
        zs = [_dot(ikb, iq_ref[LANES * hh:LANES * (hh + 1), :]) for hh in range(IDX_HEADS)]
        sc = w[0:1, :] * jnp.maximum(zs[0], 0.0)
        for hh in range(1, IDX_HEADS):
            sc = sc + w[hh:hh + 1, :] * jnp.maximum(zs[hh], 0.0)
        sc = jnp.where(sc == 0.0, 0.0, sc)
        if masked:
            sc = jnp.where(_causal_mask(t, nblk), sc, NEG_INF)
        bits = pltpu.bitcast(sc, I32)
        keys = jnp.where(bits < 0, bits ^ 0x7FFFFFFF, bits)
        ukeys = keys ^ INT_MIN
        digits = [(lax.shift_right_logical(ukeys, DIGIT_BITS * (N_DIGITS - 1 - d)) & DIGIT_MASK)
                  .astype(F32).astype(BF16) for d in range(N_DIGITS)]
        for b in range(nblk):
            rows = slice(b * t, (b + 1) * t)
            keys_sc[j + b] = keys[rows]
            for d in range(N_DIGITS):
                digit_sc[d][j + b] = digits[d][rows]

    _sweep_causal_blocks(qi, index_step)

    qpos = qi * t + lax.broadcasted_iota(I32, (1, t), 1)
    kt = jnp.minimum(topk, qpos + 1).astype(F32)

    def count_ge(cand):
        def body(j, acc):
            return acc + _fold_rows((keys_sc[j] >= cand).astype(F32))
        acc = lax.fori_loop(0, qi + 1, body, jnp.zeros((SUBLANES, t), F32))
        return jnp.sum(acc, axis=0, keepdims=True)

    one_b = jnp.ones((), BF16)
    zero_b = jnp.zeros((), BF16)

    def count_digit_ge(vals_sc, cand):
        cand_b = cand.astype(F32).astype(BF16)

        def block_count(j):
            return _fold_packed(jnp.where(vals_sc[j] >= cand_b, one_b, zero_b))

        def pair(i, acc):
            return acc + (block_count(2 * i) + block_count(2 * i + 1)).astype(F32)

        acc = lax.fori_loop(0, (qi + 1) // 2, pair, jnp.zeros((PACKED_ROWS, t), F32))
        acc = lax.cond(qi % 2 == 0, lambda a: a + block_count(qi).astype(F32), lambda a: a, acc)
        return jnp.sum(acc, axis=0, keepdims=True)

    def keep_matching(vals_sc, match_sc, match):
        match_b = match.astype(F32).astype(BF16)

        def body(j, carry):
            vals_sc[j] = jnp.where(match_sc[j] == match_b, vals_sc[j], -one_b)
            return carry
        lax.fori_loop(0, qi + 1, body, 0)

    zero = jnp.zeros((1, t), I32)
    rank = kt
    above = jnp.zeros((1, t), F32)
    tau_u = zero
    digit = zero
    for d in range(N_DIGITS):
        if d > 0:
            keep_matching(digit_sc[d], digit_sc[d - 1], digit)

        def bit_body(i, prefix, d=d, rank=rank):
            cand = prefix + lax.shift_left(jnp.int32(1), DIGIT_BITS - 1 - i)
            return jnp.where(count_digit_ge(digit_sc[d], cand) >= rank, cand, prefix)

        digit = lax.fori_loop(0, DIGIT_BITS, bit_body, zero)
        tau_u = lax.shift_left(tau_u, DIGIT_BITS) | digit
        if d < N_DIGITS - 1:
            higher = count_digit_ge(digit_sc[d], digit + 1)
            above = above + higher
            rank = rank - higher
    tau = tau_u ^ INT_MIN
    tau_sc[...] = tau
    excess = jnp.max(above + count_digit_ge(digit_sc[N_DIGITS - 1], digit) - kt)

    @pl.when(excess > 0.0)
    def _():
        need = kt - count_ge(tau + 1)

        def tie_body(j, seen):
            kj = keys_sc[j]
            eq = kj == tau
            eqb = eq.astype(BF16)
            earlier = _dot(before_ref[...], eqb) + seen
            keys_sc[j] = jnp.where(eq & (earlier >= need), INT_MIN, kj)
            return seen + jnp.sum(eqb.astype(F32), axis=0, keepdims=True)

        lax.fori_loop(0, qi + 1, tie_body, jnp.zeros((1, t), F32))

    _softmax_init(m_sc, acc_sc)

    def step(j, nblk):
        keys = [keys_sc[j + b] for b in range(nblk)]
        sel = (keys[0] if nblk == 1 else jnp.concatenate(keys, axis=0)) >= tau_sc[...]
        kb = dk_ref[_key_rows(j, nblk, t), :]
        vx = _value_cols(dvx_ref, j, nblk, slice(None))
        scores = [_dot(kb, q_ref[LANES * h:LANES * (h + 1), :]) for h in range(HEADS)]
        for h in range(HEADS):
            bias = _lane_repeat(_alibi(slopes[h], j, qi, t, nblk), t)
            scores[h] = jnp.where(sel, scores[h] + bias, -jnp.inf)
        _softmax_steps(scores, [vx] * HEADS, m_sc, acc_sc)

    _sweep_earlier_blocks(qi + 1, step)
    return jnp.concatenate([_softmax_result(acc_sc[h][...]) for h in range(HEADS)], axis=0).T


def _attn_d_call(qT, iqT, miscT, dkp, dvxT, ikp, before, topk, slopes):
    B, nb, _, t = qT.shape
    S = nb * t
    iw_block = MISC_IW // IDX_HEADS
    return pl.pallas_call(
        functools.partial(_attn_d_kernel, t=t, topk=topk, slopes=slopes),
        grid=(B,),
        in_specs=[_seq_cols_spec(nb, HW, t), _seq_cols_spec(nb, IDX_HEADS * LANES, t),
                  pl.BlockSpec((None, nb, IDX_HEADS, t), lambda b: (b, 0, iw_block, 0)),
                  _seq_rows_spec(S, LANES), _seq_cols_spec(nb, LANES, t), _seq_rows_spec(S, LANES),
                  _const_spec((t, t))],
        out_specs=_seq_rows_spec(S, GROUP_W),
        out_shape=jax.ShapeDtypeStruct((B, S, GROUP_W), F32),
        scratch_shapes=([pltpu.VMEM((nb, t, t), I32), pltpu.VMEM((1, t), I32)]
                        + [pltpu.VMEM((nb, t, t), BF16)] * N_DIGITS + _attn_scratch(HEADS, t)),
        compiler_params=_params("arbitrary"),
        name="attn_sparse",
    )(qT, iqT, miscT, dkp, dvxT, ikp, before)


def _out_proj_kernel(x_ref, oa_ref, ob_ref, oc_ref, od_ref, beta_ref, wo_ref, gate_ref,
                     g2_ref, scale_ref, shift_ref, wr_ref, br_ref, earlier_ref,
                     x1_ref, h2x_ref, route_ref, cnt_ref, cnt_sc):
    d_model = x_ref.shape[1]
    acc = None
    for i, o_ref in enumerate((oa_ref, ob_ref, oc_ref, od_ref)):
        sl = slice(GROUP_W * i, GROUP_W * (i + 1))
        mix = (o_ref[...] * beta_ref[:, sl]).astype(BF16)
        part = jnp.dot(mix, wo_ref[sl, :], preferred_element_type=F32)
        acc = part if acc is None else acc + part
    x1 = x_ref[...] + gate_ref[...] * acc
    x1_ref[...] = x1
    ms = jnp.mean(x1 * x1, axis=-1, keepdims=True)
    h2 = x1 * lax.rsqrt(ms + RMS_EPS) * g2_ref[...]
    h2 = h2 * (1.0 + scale_ref[...]) + shift_ref[...]
    h2x_ref[:, :d_model] = h2

    h_hi, h_lo = _split_bf16(h2, 2)
    w_hi, w_lo = _split_bf16(wr_ref[...], 2)
    logits = _dot(h_hi, w_hi) + _dot(h_hi, w_lo) + _dot(h_lo, w_hi) + br_ref[...]
    lt = logits.T
    tm = lt.shape[1]
    g = lt[0:N_GROUPS]
    gmax = jnp.max(g, axis=0, keepdims=True)
    gi = lax.broadcasted_iota(I32, g.shape, 0)
    gidx = jnp.min(jnp.where(g == gmax, gi, N_GROUPS), axis=0, keepdims=True)
    g_prob = 1.0 / jnp.sum(jnp.exp(g - gmax), axis=0, keepdims=True)
    e_sel = jnp.zeros((EXPERTS_PER_GROUP, tm), F32)
    for gg in range(N_GROUPS):
        lo = N_GROUPS + EXPERTS_PER_GROUP * gg
        e_sel = e_sel + jnp.where(gidx == gg, lt[lo:lo + EXPERTS_PER_GROUP], 0.0)
    ei = lax.broadcasted_iota(I32, e_sel.shape, 0)
    v1 = jnp.max(e_sel, axis=0, keepdims=True)
    i1 = jnp.min(jnp.where(e_sel == v1, ei, EXPERTS_PER_GROUP), axis=0, keepdims=True)
    rest = jnp.where(ei == i1, -jnp.inf, e_sel)
    v2 = jnp.max(rest, axis=0, keepdims=True)
    i2 = jnp.min(jnp.where(rest == v2, ei, EXPERTS_PER_GROUP), axis=0, keepdims=True)
    e2 = jnp.exp(v2 - v1)
    w1 = g_prob / (1.0 + e2)
    w2 = g_prob * e2 / (1.0 + e2)
    in_group = jnp.where(ei == i1, w1, 0.0) + jnp.where(ei == i2, w2, 0.0)
    cw = jnp.concatenate([in_group, jnp.zeros((LANES - EXPERTS_PER_GROUP, tm), F32)], axis=0)
    h2x_ref[:, d_model:] = cw.T

    @pl.when(pl.program_id(0) == 0)
    def _():
        cnt_sc[...] = jnp.zeros(cnt_sc.shape, F32)

    rows = lax.broadcasted_iota(I32, (SUBLANES, tm), 0)
    onehot = (rows == gidx).astype(F32)
    seen = _dot(onehot.astype(BF16), earlier_ref[...]) + cnt_sc[...]
    rank = jnp.sum(onehot * seen, axis=0, keepdims=True)
    route_ref[...] = jnp.concatenate(
        [gidx, rank.astype(I32), jnp.zeros((SUBLANES - 2, tm), I32)], axis=0)
    cnt_sc[...] = cnt_sc[...] + jnp.sum(onehot, axis=1, keepdims=True)
    cnt_ref[...] = cnt_sc[:, :LANES]


def _out_proj_call(xf, outs, beta, w_out, gate1, g2, scale2, shift2, w_r, b_r, earlier, seq, tm):
    N, D = xf.shape
    per_b = seq // tm
    row = lambda i: (i, 0)
    full = lambda i: (0, 0)
    per_batch = pl.BlockSpec((None, 1, D), lambda i: (i // per_b, 0, 0))
    return pl.pallas_call(
        _out_proj_kernel,
        grid=(N // tm,),
        in_specs=([pl.BlockSpec((tm, D), row)] + [pl.BlockSpec((tm, GROUP_W), row)] * 4
                  + [pl.BlockSpec((1, D), full), pl.BlockSpec((D, D), full), per_batch,
                     pl.BlockSpec((1, D), full), per_batch, per_batch,
                     pl.BlockSpec((D, LANES), full), pl.BlockSpec((1, LANES), full),
                     pl.BlockSpec((tm, tm), full)]),
        out_specs=[pl.BlockSpec((tm, D), row), pl.BlockSpec((tm, D + LANES), row),
                   pl.BlockSpec((SUBLANES, tm), lambda i: (0, i)),
                   pl.BlockSpec((SUBLANES, LANES), full)],
        out_shape=[jax.ShapeDtypeStruct((N, D), F32), jax.ShapeDtypeStruct((N, D + LANES), F32),
                   jax.ShapeDtypeStruct((SUBLANES, N), I32),
                   jax.ShapeDtypeStruct((SUBLANES, LANES), F32)],
        scratch_shapes=[pltpu.VMEM((SUBLANES, tm), F32)],
        compiler_params=_params("arbitrary"),
        name="out_proj_router",
    )(xf, *outs, beta, w_out, gate1, g2, scale2, shift2, w_r, b_r, earlier)


MOE_TILE = 256

def _dispatch_kernel(slot_ref, h2x_ref, zeros_ref, xs_ref, sem):
    del zeros_ref

    def row_copy(g, u):
        return pltpu.make_async_copy(h2x_ref.at[g, pl.ds(u, 1), :],
                                     xs_ref.at[pl.ds(slot_ref[0, g * SUBLANES + u], 1), :], sem)

    def issue(g, carry):
        for u in range(SUBLANES):
            row_copy(g, u).start()
        return carry

    lax.fori_loop(0, h2x_ref.shape[0], issue, 0)

    def drain(g, carry):
        for u in range(SUBLANES):
            row_copy(g, u).wait()
        return carry

    lax.fori_loop(0, h2x_ref.shape[0], drain, 0)


def _dispatch_call(slot, h2x, xs_zero, tm):
    N, DX = h2x.shape
    return pl.pallas_call(
        _dispatch_kernel,
        grid=(N // tm,),
        in_specs=[pl.BlockSpec((None, 1, tm), lambda i: (i, 0, 0), memory_space=pltpu.SMEM),
                  pl.BlockSpec((tm // SUBLANES, SUBLANES, DX), lambda i: (i, 0, 0)),
                  pl.BlockSpec(memory_space=pl.ANY)],
        out_specs=pl.BlockSpec(memory_space=pl.ANY),
        out_shape=jax.ShapeDtypeStruct(xs_zero.shape, F32),
        scratch_shapes=[pltpu.SemaphoreType.DMA(())],
        input_output_aliases={2: 0},
        compiler_params=_params("arbitrary"),
        name="moe_dispatch",
    )(slot.reshape(N // tm, 1, tm), h2x.reshape(N // SUBLANES, SUBLANES, DX), xs_zero)


def _expert_kernel(group_ref, valid_ref, xs_ref, w1f_ref, w3f_ref, w2f_ref, y_ref,
                   w1_ref, w3_ref, w2_ref):
    i = pl.program_id(0)
    valid = valid_ref[i]
    d_model = y_ref.shape[1]
    ff = w1_ref.shape[2]
    new_group = jnp.logical_or(i == 0, group_ref[i] != group_ref[jnp.maximum(i - 1, 0)])

    @pl.when(valid == 0)
    def _():
        y_ref[...] = jnp.zeros(y_ref.shape, F32)

    @pl.when(jnp.logical_and(valid > 0, new_group))
    def _():
        for e in range(EXPERTS_PER_GROUP):
            w1_ref[e] = w1f_ref[e].astype(BF16)
            w3_ref[e] = w3f_ref[e].astype(BF16)
            w2_ref[e] = w2f_ref[e].astype(BF16)

    @pl.when(valid > 0)
    def _():
        x = xs_ref[:, :d_model].astype(BF16)
        cw = xs_ref[:, d_model:]
        acc = None
        for e in range(EXPERTS_PER_GROUP):
            a = _dot(x, w1_ref[e])
            b = _dot(x, w3_ref[e])
            hid = a * jax.nn.sigmoid(a) * b * jnp.broadcast_to(cw[:, e:e + 1], a.shape)
            part = _dot(hid.astype(BF16), w2_ref[e])
            acc = part if acc is None else acc + part
        y_ref[...] = acc


def _expert_call(tile_group, tile_valid, xs, w1, w3, w2, layer):
    P, DX = xs.shape
    D, FF = w1.shape[-2:]
    tr = MOE_TILE

    def group_spec(rows, cols):
        return pl.BlockSpec((None, None, EXPERTS_PER_GROUP, rows, cols),
                            lambda i, g, v: (layer, g[i], 0, 0, 0),
                            pipeline_mode=pl.Buffered(1))

    grid_spec = pltpu.PrefetchScalarGridSpec(
        num_scalar_prefetch=2,
        grid=(P // tr,),
        in_specs=[pl.BlockSpec((tr, DX), lambda i, g, v: (i, 0)),
                  group_spec(D, FF), group_spec(D, FF), group_spec(FF, D)],
        out_specs=pl.BlockSpec((tr, D), lambda i, g, v: (i, 0)),
        scratch_shapes=[pltpu.VMEM((EXPERTS_PER_GROUP, D, FF), BF16),
                        pltpu.VMEM((EXPERTS_PER_GROUP, D, FF), BF16),
                        pltpu.VMEM((EXPERTS_PER_GROUP, FF, D), BF16)],
    )
    return pl.pallas_call(
        _expert_kernel,
        grid_spec=grid_spec,
        out_shape=jax.ShapeDtypeStruct((P, D), F32),
        compiler_params=_params("arbitrary"),
        name="moe_experts",
    )(tile_group, tile_valid, xs, w1, w3, w2)


def _residual_kernel(slot_ref, slot_next_ref, x1_ref, gate_ref, ys_ref, o_ref, buf, sems):
    i = pl.program_id(0)

    def gather(slots, b, wait):
        def body(g, carry):
            for u in range(SUBLANES):
                cp = pltpu.make_async_copy(ys_ref.at[pl.ds(slots[0, g * SUBLANES + u], 1), :],
                                           buf.at[b, g, pl.ds(u, 1), :], sems.at[b])
                cp.wait() if wait else cp.start()
            return carry
        lax.fori_loop(0, buf.shape[1], body, 0)

    @pl.when(i == 0)
    def _():
        gather(slot_ref, 0, False)

    @pl.when(i + 1 < pl.num_programs(0))
    def _():
        gather(slot_next_ref, (i + 1) % 2, False)

    gather(slot_ref, i % 2, True)
    o_ref[...] = x1_ref[...] + gate_ref[...] * buf[i % 2].reshape(x1_ref.shape)


def _residual_call(slot, x1, gate2, ys, seq, tm):
    N, D = x1.shape
    n = N // tm
    per_b = seq // tm
    row = pl.BlockSpec((tm, D), lambda i: (i, 0))
    slot3 = slot.reshape(n, 1, tm)
    return pl.pallas_call(
        _residual_kernel,
        grid=(n,),
        in_specs=[pl.BlockSpec((None, 1, tm), lambda i: (i, 0, 0), memory_space=pltpu.SMEM),
                  pl.BlockSpec((None, 1, tm), lambda i: (jnp.minimum(i + 1, n - 1), 0, 0),
                               memory_space=pltpu.SMEM),
                  row, pl.BlockSpec((None, 1, D), lambda i: (i // per_b, 0, 0)),
                  pl.BlockSpec(memory_space=pl.ANY)],
        out_specs=row,
        out_shape=jax.ShapeDtypeStruct((N, D), F32),
        scratch_shapes=[pltpu.VMEM((2, tm // SUBLANES, SUBLANES, D), F32),
                        pltpu.SemaphoreType.DMA((2,))],
        compiler_params=_params("arbitrary"),
        name="moe_residual",
    )(slot3, slot3, x1, gate2, ys)


def _moe_routing(route, cnt, n_tokens):
    tr = MOE_TILE
    n_tiles = n_tokens // tr + N_GROUPS
    counts = cnt[:N_GROUPS, 0].astype(I32)
    padded = (counts + tr - 1) // tr * tr
    ends = jnp.cumsum(padded)
    starts = ends - padded
    group, rank = route[0], route[1]
    slot = starts[group] + rank
    tile_start = jnp.arange(n_tiles, dtype=I32) * tr
    tile_group = jnp.minimum(jnp.sum((tile_start[:, None] >= ends[None, :]).astype(I32), axis=1),
                             N_GROUPS - 1)
    tile_valid = jnp.clip(starts[tile_group] + counts[tile_group] - tile_start, 0, tr)
    tile_valid = jnp.where(tile_start < ends[-1], tile_valid, 0)
    return slot, tile_group, tile_valid


def _block_diag_mean(width, group, valid_in_128=None):
    i = jnp.arange(width)
    same = (i[:, None] // group) == (i[None, :] // group)
    if valid_in_128 is not None:
        same = same & ((i[:, None] % LANES) < valid_in_128) & ((i[None, :] % LANES) < valid_in_128)
    return jnp.where(same, 1.0 / group, 0.0).astype(BF16)


def _placement(src_width, dst_width, pairs):
    src = jnp.array([p[0] for p in pairs], I32)
    dst = jnp.array([p[1] for p in pairs], I32)
    return jnp.zeros((src_width, dst_width), F32).at[src, dst].set(1.0).astype(BF16)


def _placements():
    per_head = [(HEAD_DIM * h + d, LANES * h + d) for h in range(HEADS) for d in range(HEAD_DIM)]
    first64 = [(d, d) for d in range(HEAD_DIM)]
    idx_k = [(MISC_IK + d, d) for d in range(IDX_DIM)]
    table = {"a_kp": (GROUP_W, per_head), "b_kp": (GROUP_W, per_head), "c_kp": (GROUP_W, per_head),
             "d_kp": (GROUP_W, first64), "i_kp": (LANES, idx_k)}
    return [_placement(table[name][0], width, table[name][1])
            for name, src, width, orient, ones in OUTPUTS if orient == "rows"]


def kernel(x, c, ada_w, ada_b, norm1_g, norm2_g, w_in, b_f, qn_a, kn_a, qn_c, kn_c,
           lam_q1, lam_k1, lam_q2, lam_k2, subln_g, qn_d, kn_d, mix_beta, w_out,
           w_group, b_group, w_expert, b_expert, w1, w3, w2):
    B, S, D = x.shape
    L = ada_w.shape[0]
    N = B * S
    topk = min(TOPK_MAX, S // 4)
    t = ATTN_BLOCK
    tm = t
    slopes = [2.0 ** (-8.0 * i / (2 * HEADS)) for i in range(1, 2 * HEADS + 1)]
    slopes_c, slopes_d = tuple(slopes[0::2]), tuple(slopes[1::2])

    idx_t = jnp.arange(t)
    after = (idx_t[None, :] > idx_t[:, None]).astype(BF16)
    before = (idx_t[None, :] < idx_t[:, None]).astype(BF16)
    upto = (idx_t[None, :] <= idx_t[:, None]).astype(BF16)
    g64 = _block_diag_mean(GROUP_W, HEAD_DIM)
    g32 = _block_diag_mean(GROUP_W, DIFF_DIM)
    places = _placements()
    lane = jnp.arange(LANES)
    cum_sel = jnp.stack([jnp.broadcast_to((lane == MISC_AF + h)[:, None], (LANES, LANES))
                         for h in range(HEADS)]).astype(BF16)

    mod = _ada_call(c, ada_w, ada_b)
    xf = x.reshape(N, D)

    for l in range(L):
        m6 = mod[l].reshape(B, 6, 1, D)
        shift1, scale1, gate1, shift2, scale2, gate2 = (m6[:, i] for i in range(6))

        w_all = w_in[l].astype(BF16)
        ones = jnp.ones((GROUP_W - HEAD_DIM,), F32)
        gains = jnp.stack([jnp.tile(qn_a[l], HEADS) * (HEAD_DIM ** -0.5 * LOG2E),
                           jnp.tile(kn_a[l], HEADS),
                           jnp.tile(qn_c[l], 2 * HEADS) * (DIFF_DIM ** -0.5 * LOG2E),
                           jnp.tile(kn_c[l], 2 * HEADS),
                           jnp.tile(qn_d[l], HEADS) * (HEAD_DIM ** -0.5 * LOG2E),
                           jnp.concatenate([kn_d[l], ones])]).astype(F32)
        gains = jnp.concatenate([gains, jnp.zeros((2, GROUP_W), F32)], axis=0)

        outs = _in_proj_call(xf, scale1, shift1, norm1_g[l].reshape(1, D), w_all, gains,
                             g64, g32, places, B, S, tm)
        sec = {name: o for (name, _, _, _, _), o in zip(OUTPUTS, outs)}
        for name, src, width, orient, ones_ in OUTPUTS:
            if orient == "rows":
                sec[name] = sec[name].reshape(B, S, width)
        misc = outs[len(OUTPUTS)].reshape(B, S, LANES)
        miscT = outs[len(OUTPUTS) + 1]

        bf_row = jnp.zeros((1, LANES), F32).at[0, MISC_AF:MISC_AF + HEADS].set(b_f[l].astype(F32))
        cumrep = _cum_call(misc, bf_row, upto, cum_sel)

        o_a = _attn_a_call(sec["a_qT"], sec["a_kp"], sec["a_vxT"], cumrep)
        o_b = _attn_b_call(sec["b_qT"], sec["b_kp"], sec["b_vT"], after)
        lambda_init = 0.8 - 0.6 * math.exp(-0.3 * l)
        lamv = jnp.stack([lam_q1[l], lam_k1[l], lam_q2[l], lam_k2[l]]).astype(F32)
        o_c = _attn_c_call(sec["c_qT"], sec["c_kp"], sec["c_vxT"], lamv,
                           subln_g[l].reshape(HEAD_DIM, 1).astype(F32), slopes_c, lambda_init)
        o_d = _attn_d_call(sec["d_qT"], sec["i_qT"], miscT, sec["d_kp"], sec["d_vxT"], sec["i_kp"],
                           before, topk, slopes_d)

        w_r = jnp.concatenate([w_group[l], w_expert[l],
                               jnp.zeros((D, LANES - N_GROUPS - N_EXPERTS), F32)], axis=1)
        b_r = jnp.concatenate([b_group[l], b_expert[l],
                               jnp.zeros((LANES - N_GROUPS - N_EXPERTS,), F32)]).reshape(1, LANES)
        x1, h2x, route, cnt = _out_proj_call(
            xf, [o.reshape(N, GROUP_W) for o in (o_a, o_b, o_c, o_d)], mix_beta[l].reshape(1, D),
            w_out[l].astype(BF16), gate1, norm2_g[l].reshape(1, D), scale2, shift2, w_r, b_r,
            after, S, tm)

        slot, tile_group, tile_valid = _moe_routing(route, cnt, N)
        xs = _dispatch_call(slot, h2x, jnp.zeros((tile_group.shape[0] * MOE_TILE, D + LANES), F32), tm)
        ys = _expert_call(tile_group, tile_valid, xs, w1, w3, w2, l)
        xf = _residual_call(slot, x1, gate2, ys, S, tm)

    return xf.reshape(B, S, D)
```

```python
import functools
import math

import jax
import jax.numpy as jnp
from jax import lax
from jax.experimental import pallas as pl
from jax.experimental.pallas import tpu as pltpu

F32 = jnp.float32
BF16 = jnp.bfloat16
I32 = jnp.int32

HEAD_DIM = 64
HEADS = 4
GROUP_W = HEADS * HEAD_DIM
DIFF_DIM = HEAD_DIM // 2
IDX_HEADS = 8
IDX_DIM = 32
TOPK_MAX = 256
N_GROUPS = 4
EXPERTS_PER_GROUP = 8
N_EXPERTS = N_GROUPS * EXPERTS_PER_GROUP
EXPERT_FF = 256
RMS_EPS = 1e-6
NEG_INF = -1e30
INT_MIN = -(2 ** 31)
LOG2E = math.log2(math.e)

LANES = 128
SUBLANES = 8
ATTN_BLOCK = 256
VMEM_LIMIT = 56 * 1024 * 1024

MISC_IK = 0
MISC_IW = IDX_DIM
MISC_AF = IDX_DIM + IDX_HEADS


def _params(*sem):
    return pltpu.CompilerParams(dimension_semantics=sem, vmem_limit_bytes=VMEM_LIMIT)


def _log_sigmoid(z):
    return jnp.minimum(z, 0.0) - jnp.log1p(jnp.exp(-jnp.abs(z)))


def _log2_sigmoid(z2):
    return jnp.minimum(z2, 0.0) - jnp.log2(1.0 + jnp.exp2(-jnp.abs(z2)))


def _split_bf16(x, parts):
    out = []
    rem = x
    for _ in range(parts):
        p = rem.astype(BF16)
        out.append(p)
        rem = rem - p.astype(F32)
    return out


def _dot(a, b):
    return jnp.dot(a, b, preferred_element_type=F32)


def _ada_kernel(c_ref, w_ref, b_ref, o_ref):
    c = c_ref[...]
    ca = c * jax.nn.sigmoid(c)
    o_ref[...] = jnp.dot(ca, w_ref[...], precision=lax.Precision.HIGHEST,
                         preferred_element_type=F32) + b_ref[...]


def _ada_call(c, ada_w, ada_b):
    L, D, E = ada_w.shape
    B = c.shape[0]
    tn = 1536
    return pl.pallas_call(
        _ada_kernel,
        grid=(L, E // tn),
        in_specs=[pl.BlockSpec((B, D), lambda l, j: (0, 0)),
                  pl.BlockSpec((None, D, tn), lambda l, j: (l, 0, j)),
                  pl.BlockSpec((None, 1, tn), lambda l, j: (l, 0, j))],
        out_specs=pl.BlockSpec((None, B, tn), lambda l, j: (l, 0, j)),
        out_shape=jax.ShapeDtypeStruct((L, B, E), F32),
        compiler_params=_params("arbitrary", "arbitrary"),
        name="ada_mod",
    )(c, ada_w, ada_b.reshape(L, 1, E))


SOURCES = ("a_q", "a_k", "a_v", "b_q", "b_k", "b_v", "c_q", "c_k", "c_v", "d_q", "i_q", "d_kv")
SRC_NORM = {"a_q": ("n64", 0), "a_k": ("n64", 1), "c_q": ("n32", 2), "c_k": ("n32", 3),
            "d_q": ("n64", 4), "d_kv": ("n64", 5)}
SRC_SCALE = {"b_q": HEAD_DIM ** -0.5 * LOG2E, "i_q": IDX_DIM ** -0.5}
HW = HEADS * LANES
OUTPUTS = (
    ("a_qT", "a_q", HW, "cols", False), ("a_kp", "a_k", HW, "rows", False),
    ("a_vxT", "a_v", HW, "cols", True),
    ("b_qT", "b_q", HW, "cols", False), ("b_kp", "b_k", HW, "rows", False),
    ("b_vT", "b_v", GROUP_W, "cols", False),
    ("c_qT", "c_q", 2 * HW, "cols", False), ("c_kp", "c_k", HW, "rows", False),
    ("c_vxT", "c_v", HW, "cols", True),
    ("d_qT", "d_q", HW, "cols", False), ("i_qT", "i_q", IDX_HEADS * LANES, "cols", False),
    ("d_kp", "d_kv", LANES, "rows", False), ("d_vxT", "d_kv", LANES, "cols", True),
    ("i_kp", "misc", LANES, "rows", False),
)
N_MAIN = len(SOURCES) * GROUP_W
N_ROW_OUTPUTS = sum(1 for o in OUTPUTS if o[3] == "rows")


def _col_plans():
    per_head = [p for h in range(HEADS) for p in ((HEAD_DIM * h, HEAD_DIM), (None, HEAD_DIM))]
    diff_q = []
    for h in range(HEADS):
        for c in range(2):
            lead = DIFF_DIM * c
            if lead:
                diff_q.append((None, lead))
            diff_q.append((HEAD_DIM * h + DIFF_DIM * c, DIFF_DIM))
            diff_q.append((None, LANES - lead - DIFF_DIM))
    idx_q = [p for hh in range(IDX_HEADS) for p in ((IDX_DIM * hh, IDX_DIM), (None, LANES - IDX_DIM))]
    return {"a_qT": per_head, "a_vxT": per_head, "b_qT": per_head, "b_vT": [(0, GROUP_W)],
            "c_qT": diff_q, "c_vxT": per_head, "d_qT": per_head, "i_qT": idx_q,
            "d_vxT": [(HEAD_DIM, HEAD_DIM), (None, HEAD_DIM)]}


COL_PLANS = _col_plans()

IN_SPLITS = (GROUP_W, GROUP_W, GROUP_W, HEADS, GROUP_W, GROUP_W, GROUP_W, GROUP_W, GROUP_W,
             GROUP_W, GROUP_W, HEAD_DIM, HEAD_DIM, IDX_HEADS * IDX_DIM, IDX_DIM, IDX_HEADS)
IN_NAMES = ("a_q", "a_k", "a_v", "a_f", "b_q", "b_k", "b_v", "c_q", "c_k", "c_v",
            "d_q", "d_k", "d_v", "i_q", "i_k", "i_w")
IN_OFFSETS = {n: (sum(IN_SPLITS[:i]), IN_SPLITS[i]) for i, n in enumerate(IN_NAMES)}
P_IN = sum(IN_SPLITS)
SECTION_PARTS = {name: (name,) for name in SOURCES if name != "d_kv"}
SECTION_PARTS["d_kv"] = ("d_k", "d_v")
SECTION_PARTS["misc"] = ("i_k", "i_w", "a_f")
REALIGN_ROWS = 256


def _realign_weights(w_ref, w_sc):
    d_model = w_ref.shape[0]
    for r0 in range(0, d_model, REALIGN_ROWS):
        rows = slice(r0, r0 + REALIGN_ROWS)
        for i, name in enumerate(SOURCES + ("misc",)):
            width = LANES if name == "misc" else GROUP_W
            pieces, used = [], 0
            for part in SECTION_PARTS[name]:
                off, w = IN_OFFSETS[part]
                base = off // LANES * LANES
                end = min(-(-(off + w) // LANES) * LANES, P_IN)
                window = w_ref[rows, base:end]
                pieces.append(window[:, off - base:off - base + w])
                used += w
            if used < width:
                pieces.append(jnp.zeros((REALIGN_ROWS, width - used), w_sc.dtype))
            block = pieces[0] if len(pieces) == 1 else jnp.concatenate(pieces, axis=1)
            w_sc[rows, i * GROUP_W:i * GROUP_W + width] = block


def _in_proj_kernel(*refs):
    (x_ref, scale_ref, shift_ref, g1_ref, w_in_ref, gains_ref, g64_ref, g32_ref) = refs[:8]
    place_refs = refs[8:8 + N_ROW_OUTPUTS]
    out_refs = refs[8 + N_ROW_OUTPUTS:-1]
    w_ref = refs[-1]

    @pl.when(pl.program_id(0) == 0)
    def _():
        _realign_weights(w_in_ref, w_ref)

    x = x_ref[...]
    ms = jnp.mean(x * x, axis=-1, keepdims=True)
    h = x * lax.rsqrt(ms + RMS_EPS) * g1_ref[...]
    h = h * (1.0 + scale_ref[...]) + shift_ref[...]

    wm = w_ref[:, N_MAIN:N_MAIN + LANES]
    misc = functools.reduce(lambda a, b: a + b, [_dot(term, wm) for term in _split_bf16(h, 3)])
    out_refs[len(OUTPUTS)][...] = misc
    out_refs[len(OUTPUTS) + 1][...] = misc.T

    final = {"misc": misc}
    raw = {name: _dot(h, w_ref[:, i * GROUP_W:(i + 1) * GROUP_W])
           for i, name in enumerate(SOURCES)}
    msqs = {name: _dot(raw[name] * raw[name],
                       (g64_ref if SRC_NORM[name][0] == "n64" else g32_ref)[...])
            for name in SOURCES if name in SRC_NORM}
    for name in SOURCES:
        sec = raw[name]
        if name in SRC_NORM:
            r = SRC_NORM[name][1]
            fac = lax.rsqrt(msqs[name] + RMS_EPS)
            if name == "d_kv":
                lane = lax.broadcasted_iota(I32, sec.shape, 1)
                fac = jnp.where(lane < HEAD_DIM, fac, 1.0)
            sec = sec * fac * gains_ref[r:r + 1, :]
        elif name in SRC_SCALE:
            sec = sec * SRC_SCALE[name]
        final[name] = sec

    tokens = x.shape[0]
    transposed = {}
    place_iter = iter(place_refs)
    for (name, src, width, orient, ones), o_ref in zip(OUTPUTS, out_refs):
        if orient == "rows":
            o_ref[...] = _dot(final[src], next(place_iter)[...])
            continue
        if src not in transposed:
            transposed[src] = final[src].T
        slabs = []
        for start, rows in COL_PLANS[name]:
            if start is None:
                slabs.append(jnp.full((rows, tokens), 1.0 if ones else 0.0, F32))
            else:
                slabs.append(transposed[src][start:start + rows])
        o_ref[...] = jnp.concatenate(slabs, axis=0)


def _in_proj_call(xf, scale1, shift1, g1, w_in, layer, gains, g64, g32, places, batch, seq, tm):
    N, D = xf.shape
    per_b = seq // tm
    row = lambda i: (i, 0)
    full = lambda i: (0, 0)
    col4 = lambda i: (i // per_b, i % per_b, 0, 0)
    out_shape, out_specs = [], []
    for name, src, width, orient, ones in OUTPUTS:
        if orient == "rows":
            out_shape.append(jax.ShapeDtypeStruct((N, width), F32))
            out_specs.append(pl.BlockSpec((tm, width), row))
        else:
            out_shape.append(jax.ShapeDtypeStruct((batch, per_b, width, tm), F32))
            out_specs.append(pl.BlockSpec((None, None, width, tm), col4))
    out_shape += [jax.ShapeDtypeStruct((N, LANES), F32),
                  jax.ShapeDtypeStruct((batch, per_b, LANES, tm), F32)]
    out_specs += [pl.BlockSpec((tm, LANES), row), pl.BlockSpec((None, None, LANES, tm), col4)]
    return pl.pallas_call(
        _in_proj_kernel,
        grid=(N // tm,),
        in_specs=([pl.BlockSpec((tm, D), row),
                   pl.BlockSpec((None, 1, D), lambda i: (i // per_b, 0, 0)),
                   pl.BlockSpec((None, 1, D), lambda i: (i // per_b, 0, 0)),
                   pl.BlockSpec((1, D), full),
                   pl.BlockSpec((None,) + w_in.shape[1:], lambda i: (layer, 0, 0),
                                pipeline_mode=pl.Buffered(1)),
                   pl.BlockSpec(gains.shape, full),
                   pl.BlockSpec(g64.shape, full),
                   pl.BlockSpec(g32.shape, full)]
                  + [pl.BlockSpec(p.shape, full) for p in places]),
        out_specs=out_specs,
        out_shape=out_shape,
        scratch_shapes=[pltpu.VMEM((D, N_MAIN + LANES), F32)],
        compiler_params=_params("arbitrary"),
        name="in_proj",
    )(xf, scale1, shift1, g1, w_in, gains, g64, g32, *places)


def _cum_kernel(misc_ref, bf_ref, tri_ref, sel_ref, o_ref, *, blk):
    seq = misc_ref.shape[0]
    carry = jnp.zeros((1, LANES), F32)
    tri = tri_ref[...]
    for j in range(seq // blk):
        rows = slice(j * blk, (j + 1) * blk)
        lf = _log_sigmoid(misc_ref[rows, :] + bf_ref[...])
        c = carry
        for part in _split_bf16(lf, 3):
            c = c + _dot(tri, part)
        carry = c[blk - 1:blk, :]
        parts = _split_bf16(c, 3)
        for h in range(HEADS):
            rep = _dot(parts[0], sel_ref[h]) + _dot(parts[1], sel_ref[h]) + _dot(parts[2], sel_ref[h])
            o_ref[h, rows, :] = rep * LOG2E


def _cum_call(misc, bf_row, tri, sel):
    B, S, _ = misc.shape
    blk = tri.shape[0]
    return pl.pallas_call(
        functools.partial(_cum_kernel, blk=blk),
        grid=(B,),
        in_specs=[pl.BlockSpec((None, S, LANES), lambda b: (b, 0, 0)),
                  pl.BlockSpec((1, LANES), lambda b: (0, 0)),
                  pl.BlockSpec((blk, blk), lambda b: (0, 0)),
                  pl.BlockSpec((HEADS, LANES, LANES), lambda b: (0, 0, 0))],
        out_specs=pl.BlockSpec((None, HEADS, S, LANES), lambda b: (b, 0, 0, 0)),
        out_shape=jax.ShapeDtypeStruct((B, HEADS, S, LANES), F32),
        compiler_params=_params("arbitrary"),
        name="forget_cumsum",
    )(misc, bf_row, tri, sel)


def _seq_rows_spec(seq, width):
    return pl.BlockSpec((None, seq, width), lambda b: (b, 0, 0))


def _seq_cols_spec(nb, width, t):
    return pl.BlockSpec((None, nb, width, t), lambda b: (b, 0, 0, 0))


def _const_spec(shape):
    return pl.BlockSpec(shape, lambda b: (0,) * len(shape))


def _for_each_query_tile(q_ref, o_ref, t, tile_fn):
    def body(qi, carry):
        o_ref[pl.ds(pl.multiple_of(qi * t, t), t), :] = tile_fn(qi, q_ref.at[qi])
        return carry

    lax.fori_loop(0, q_ref.shape[0], body, 0)


def _lane_repeat(x, t):
    return jnp.concatenate([x] * (t // LANES), axis=1)


def _softmax_steps(scores, values, m_refs, acc_refs):
    probs, alphas = [], []
    for s, m_ref in zip(scores, m_refs):
        m_old = m_ref[...]
        m_new = jnp.maximum(m_old, jnp.max(s, axis=0, keepdims=True))
        alphas.append(jnp.exp2(m_old - m_new))
        probs.append(jnp.exp2(s - m_new))
        m_ref[...] = m_new
    for p, vx, alpha, acc_ref in zip(probs, values, alphas, acc_refs):
        acc_ref[...] = alpha * acc_ref[...] + _dot(vx, p)


def _softmax_init(m_scs, acc_scs):
    for m_sc, acc_sc in zip(m_scs, acc_scs):
        m_sc[...] = jnp.full(m_sc.shape, NEG_INF, F32)
        acc_sc[...] = jnp.zeros(acc_sc.shape, F32)


def _softmax_result(acc):
    return acc[:HEAD_DIM] / acc[HEAD_DIM:]


def _alibi(slope, j, qi, t, nblk=1):
    key = lax.broadcasted_iota(I32, (nblk * t, LANES), 0)
    return (slope * LOG2E) * (key + (j - qi) * t).astype(F32)


def _sweep_earlier_blocks(qi, step):
    def body(i, carry):
        step(2 * i, 2)
        return carry

    lax.fori_loop(0, qi // 2, body, 0)

    @pl.when(qi % 2 == 1)
    def _():
        step(qi - 1, 1)


def _sweep_causal_blocks(qi, step):
    @pl.when(qi == 0)
    def _():
        step(0, 1, True)

    @pl.when(qi > 0)
    def _():
        step(qi - 1, 2, True)

    _sweep_earlier_blocks(jnp.maximum(qi - 1, 0), step)


def _causal_mask(t, nblk, strict=False):
    key = lax.broadcasted_iota(I32, (nblk * t, t), 0) - (nblk - 1) * t
    query = lax.broadcasted_iota(I32, (nblk * t, t), 1)
    return key < query if strict else key <= query


def _key_rows(j, nblk, t):
    return pl.ds(pl.multiple_of(j * t, t), nblk * t)


def _value_cols(vx_ref, j, nblk, rows):
    tiles = [vx_ref[j + b, rows, :] for b in range(nblk)]
    return tiles[0] if nblk == 1 else jnp.concatenate(tiles, axis=1)


def _attn_a_kernel(q_ref, k_ref, vx_ref, cum_ref, o_ref, *scratch, t):
    m_sc, acc_sc = scratch[:HEADS], scratch[HEADS:]

    def tile(qi, q):
        _softmax_init(m_sc, acc_sc)

        def step(j, nblk, masked=False):
            rows = _key_rows(j, nblk, t)
            heads = [slice(LANES * h, LANES * (h + 1)) for h in range(HEADS)]
            scores = [_dot(k_ref[rows, hs], q[hs, :]) for hs in heads]
            for h in range(HEADS):
                s = scores[h] - _lane_repeat(cum_ref[h, rows, :], t)
                scores[h] = jnp.where(_causal_mask(t, nblk), s, -jnp.inf) if masked else s
            _softmax_steps(scores, [_value_cols(vx_ref, j, nblk, hs) for hs in heads],
                           m_sc, acc_sc)

        _sweep_causal_blocks(qi, step)
        return jnp.concatenate([_softmax_result(acc_sc[h][...]) for h in range(HEADS)], axis=0).T

    _for_each_query_tile(q_ref, o_ref, t, tile)


def _attn_scratch(n, t, rows=LANES):
    return [pltpu.VMEM((1, t), F32)] * n + [pltpu.VMEM((rows, t), F32)] * n


def _attn_a_call(qT, kp, vxT, cumrep):
    B, nb, _, t = qT.shape
    S = nb * t
    return pl.pallas_call(
        functools.partial(_attn_a_kernel, t=t),
        grid=(B,),
        in_specs=[_seq_cols_spec(nb, HW, t), _seq_rows_spec(S, HW), _seq_cols_spec(nb, HW, t),
                  pl.BlockSpec((None, HEADS, S, LANES), lambda b: (b, 0, 0, 0))],
        out_specs=_seq_rows_spec(S, GROUP_W),
        out_shape=jax.ShapeDtypeStruct((B, S, GROUP_W), F32),
        scratch_shapes=_attn_scratch(HEADS, t),
        compiler_params=_params("arbitrary"),
        name="attn_forget",
    )(qT, kp, vxT, cumrep)


STICK_SPLIT_TERMS = 3


def _attn_b_kernel(q_ref, k_ref, v_ref, after_ref, o_ref, *scratch, t):
    tile = functools.partial(_attn_b_tile, k_ref=k_ref, v_ref=v_ref, after_ref=after_ref,
                             r_sc=scratch[:HEADS], acc_sc=scratch[HEADS:], t=t)
    _for_each_query_tile(q_ref, o_ref, t, tile)


def _attn_b_tile(qi, q_ref, *, k_ref, v_ref, after_ref, r_sc, acc_sc, t):
    for h in range(HEADS):
        r_sc[h][...] = jnp.zeros(r_sc[h].shape, F32)
        acc_sc[h][...] = jnp.zeros(acc_sc[h].shape, F32)

    def step(j, nblk, masked=False):
        after = after_ref[...]
        rows = _key_rows(j, nblk, t)
        heads = [slice(LANES * h, LANES * (h + 1)) for h in range(HEADS)]
        zs = [_dot(k_ref[rows, hs], q_ref[hs, :]) for hs in heads]
        lbs, splits, later_sums = [], [], []
        for h in range(HEADS):
            lb = _log2_sigmoid(zs[h])
            lm = lb - zs[h]
            if masked:
                lm = jnp.where(_causal_mask(t, nblk, strict=True), lm, 0.0)
            blocks = [lm[b * t:(b + 1) * t] for b in range(nblk)]
            splits.append([_split_bf16(blk, STICK_SPLIT_TERMS) for blk in blocks])
            sums = [jnp.sum(blk, axis=0, keepdims=True) for blk in blocks]
            r_old = r_sc[h][...]
            total = sums[0]
            for s_ in sums[1:]:
                total = total + s_
            r_sc[h][...] = r_old + total
            lbs.append(lb + r_old)
            later, run = [], None
            for b in reversed(range(nblk)):
                later.append(run)
                run = sums[b] if run is None else run + sums[b]
            later_sums.append(later[::-1])
        suffixes = [[functools.reduce(lambda a, b: a + b, [_dot(after, term) for term in terms])
                     for terms in splits[h]] for h in range(HEADS)]
        ws = []
        for h in range(HEADS):
            parts = [suffixes[h][b] if later_sums[h][b] is None else suffixes[h][b] + later_sums[h][b]
                     for b in range(nblk)]
            suffix = parts[0] if nblk == 1 else jnp.concatenate(parts, axis=0)
            w = jnp.exp2(lbs[h] + suffix)
            if masked:
                w = jnp.where(_causal_mask(t, nblk, strict=True), w, 0.0)
            ws.append(w)
        for h in range(HEADS):
            acc_sc[h][...] += _dot(_value_cols(v_ref, j, nblk, slice(HEAD_DIM * h, HEAD_DIM * (h + 1))),
                                   ws[h])

    @pl.when(qi == 0)
    def _():
        step(0, 1, True)

    @pl.when(qi > 0)
    def _():
        step(qi - 1, 2, True)

    rest = jnp.maximum(qi - 1, 0)

    @pl.when(rest % 2 == 1)
    def _():
        step(rest - 1, 1)

    pairs = rest // 2

    def body(i, carry):
        step(2 * (pairs - 1 - i), 2)
        return carry

    lax.fori_loop(0, pairs, body, 0)
    return jnp.concatenate([acc_sc[h][...] for h in range(HEADS)], axis=0).T


def _attn_b_call(qT, kp, vT, after):
    B, nb, _, t = qT.shape
    S = nb * t
    return pl.pallas_call(
        functools.partial(_attn_b_kernel, t=t),
        grid=(B,),
        in_specs=[_seq_cols_spec(nb, HW, t), _seq_rows_spec(S, HW), _seq_cols_spec(nb, GROUP_W, t),
                  _const_spec((t, t))],
        out_specs=_seq_rows_spec(S, GROUP_W),
        out_shape=jax.ShapeDtypeStruct((B, S, GROUP_W), F32),
        scratch_shapes=_attn_scratch(HEADS, t, HEAD_DIM),
        compiler_params=_params("arbitrary"),
        name="attn_stick",
    )(qT, kp, vT, after)


def _attn_c_kernel(q_ref, k_ref, vx_ref, lamv_ref, subg_ref, o_ref, *scratch,
                   t, slopes, lambda_init):
    tile = functools.partial(_attn_c_tile, k_ref=k_ref, vx_ref=vx_ref, lamv_ref=lamv_ref,
                             subg_ref=subg_ref, m_sc=scratch[:2 * HEADS], acc_sc=scratch[2 * HEADS:],
                             t=t, slopes=slopes, lambda_init=lambda_init)
    _for_each_query_tile(q_ref, o_ref, t, tile)


def _attn_c_tile(qi, q_ref, *, k_ref, vx_ref, lamv_ref, subg_ref, m_sc, acc_sc,
                 t, slopes, lambda_init):
    _softmax_init(m_sc, acc_sc)

    def step(j, nblk, masked=False):
        rows = _key_rows(j, nblk, t)
        heads = [slice(LANES * h, LANES * (h + 1)) for h in range(HEADS)]
        scores = [_dot(k_ref[rows, heads[g // 2]], q_ref[LANES * g:LANES * (g + 1), :])
                  for g in range(2 * HEADS)]
        for g in range(2 * HEADS):
            s = scores[g] + _lane_repeat(_alibi(slopes[g // 2], j, qi, t, nblk), t)
            scores[g] = jnp.where(_causal_mask(t, nblk), s, -jnp.inf) if masked else s
        _softmax_steps(scores, [_value_cols(vx_ref, j, nblk, heads[g // 2])
                                for g in range(2 * HEADS)], m_sc, acc_sc)

    _sweep_causal_blocks(qi, step)

    lv = lamv_ref[...]
    lam = (jnp.exp(jnp.sum(lv[0:1] * lv[1:2], axis=-1, keepdims=True))
           - jnp.exp(jnp.sum(lv[2:3] * lv[3:4], axis=-1, keepdims=True)) + lambda_init)
    outs = []
    for h in range(HEADS):
        o = (_softmax_result(acc_sc[2 * h][...])
             - lam * _softmax_result(acc_sc[2 * h + 1][...]))
        ms = jnp.mean(o * o, axis=0, keepdims=True)
        outs.append(o * lax.rsqrt(ms + RMS_EPS) * subg_ref[...] * (1.0 - lambda_init))
    return jnp.concatenate(outs, axis=0).T


def _attn_c_call(qT, kp, vxT, lamv, subg_col, slopes, lambda_init):
    B, nb, _, t = qT.shape
    S = nb * t
    return pl.pallas_call(
        functools.partial(_attn_c_kernel, t=t, slopes=slopes, lambda_init=lambda_init),
        grid=(B,),
        in_specs=[_seq_cols_spec(nb, 2 * HW, t), _seq_rows_spec(S, HW), _seq_cols_spec(nb, HW, t),
                  _const_spec(lamv.shape), _const_spec(subg_col.shape)],
        out_specs=_seq_rows_spec(S, GROUP_W),
        out_shape=jax.ShapeDtypeStruct((B, S, GROUP_W), F32),
        scratch_shapes=_attn_scratch(2 * HEADS, t),
        compiler_params=_params("arbitrary"),
        name="attn_diff",
    )(qT, kp, vxT, lamv, subg_col)


def _fold_rows(x, group=SUBLANES):
    return jnp.sum(x.reshape(x.shape[0] // group, group, x.shape[1]), axis=0)


PACKED_ROWS = 2 * SUBLANES
DIGIT_BITS = 8
N_DIGITS = 32 // DIGIT_BITS
DIGIT_MASK = (1 << DIGIT_BITS) - 1


def _fold_packed(x):
    slabs = [x[i * PACKED_ROWS:(i + 1) * PACKED_ROWS] for i in range(x.shape[0] // PACKED_ROWS)]
    while len(slabs) > 1:
        slabs = [a + b for a, b in zip(slabs[0::2], slabs[1::2])]
    return slabs[0]


def _attn_d_kernel(q_ref, iq_all_ref, iw_all_ref, dk_ref, dvx_ref, ik_ref, before_ref, o_ref,
                   keys_sc, tau_sc, *scratch, t, topk, slopes):
    def tile(qi, q):
        return _attn_d_tile(qi, q, iq_all_ref.at[qi], iw_all_ref.at[qi], dk_ref, dvx_ref, ik_ref,
                            before_ref, keys_sc, tau_sc, scratch, t=t, topk=topk, slopes=slopes)

    _for_each_query_tile(q_ref, o_ref, t, tile)


def _attn_d_tile(qi, q_ref, iq_ref, iw_ref, dk_ref, dvx_ref, ik_ref, before_ref,
                 keys_sc, tau_sc, scratch, *, t, topk, slopes):
    digit_sc, scratch = scratch[:N_DIGITS], scratch[N_DIGITS:]
    m_sc, acc_sc = scratch[:HEADS], scratch[HEADS:]

    w = iw_ref[...] * IDX_HEADS ** -0.5

    def index_step(j, nblk, masked=False):
        ikb = ik_ref[_key_rows(j, nblk, t), :]
        zs = [_dot(ikb, iq_ref[LANES * hh:LANES * (hh + 1), :]) for hh in range(IDX_HEADS)]
        sc = w[0:1, :] * jnp.maximum(zs[0], 0.0)
        for hh in range(1, IDX_HEADS):
            sc = sc + w[hh:hh + 1, :] * jnp.maximum(zs[hh], 0.0)
        sc = jnp.where(sc == 0.0, 0.0, sc)
        if masked:
            sc = jnp.where(_causal_mask(t, nblk), sc, NEG_INF)
        bits = pltpu.bitcast(sc, I32)
        keys = jnp.where(bits < 0, bits ^ 0x7FFFFFFF, bits)
        ukeys = keys ^ INT_MIN
        digits = [(lax.shift_right_logical(ukeys, DIGIT_BITS * (N_DIGITS - 1 - d)) & DIGIT_MASK)
                  .astype(F32).astype(BF16) for d in range(N_DIGITS)]
        for b in range(nblk):
            rows = slice(b * t, (b + 1) * t)
            keys_sc[j + b] = keys[rows]
            for d in range(N_DIGITS):
                digit_sc[d][j + b] = digits[d][rows]

    _sweep_causal_blocks(qi, index_step)

    qpos = qi * t + lax.broadcasted_iota(I32, (1, t), 1)
    kt = jnp.minimum(topk, qpos + 1).astype(F32)

    def count_ge(cand):
        def body(j, acc):
            return acc + _fold_rows((keys_sc[j] >= cand).astype(F32))
        acc = lax.fori_loop(0, qi + 1, body, jnp.zeros((SUBLANES, t), F32))
        return jnp.sum(acc, axis=0, keepdims=True)

    one_b = jnp.ones((), BF16)
    zero_b = jnp.zeros((), BF16)

    def count_digit_ge(vals_sc, cand):
        cand_b = cand.astype(F32).astype(BF16)

        def block_count(j):
            return _fold_packed(jnp.where(vals_sc[j] >= cand_b, one_b, zero_b))

        def pair(i, acc):
            return acc + (block_count(2 * i) + block_count(2 * i + 1)).astype(F32)

        acc = lax.fori_loop(0, (qi + 1) // 2, pair, jnp.zeros((PACKED_ROWS, t), F32))
        acc = lax.cond(qi % 2 == 0, lambda a: a + block_count(qi).astype(F32), lambda a: a, acc)
        return jnp.sum(acc, axis=0, keepdims=True)

    def keep_matching(vals_sc, match_sc, match):
        match_b = match.astype(F32).astype(BF16)

        def body(j, carry):
            vals_sc[j] = jnp.where(match_sc[j] == match_b, vals_sc[j], -one_b)
            return carry
        lax.fori_loop(0, qi + 1, body, 0)

    zero = jnp.zeros((1, t), I32)
    rank = kt
    above = jnp.zeros((1, t), F32)
    tau_u = zero
    digit = zero
    for d in range(N_DIGITS):
        if d > 0:
            keep_matching(digit_sc[d], digit_sc[d - 1], digit)

        def bit_body(i, prefix, d=d, rank=rank):
            cand = prefix + lax.shift_left(jnp.int32(1), DIGIT_BITS - 1 - i)
            return jnp.where(count_digit_ge(digit_sc[d], cand) >= rank, cand, prefix)

        digit = lax.fori_loop(0, DIGIT_BITS, bit_body, zero)
        tau_u = lax.shift_left(tau_u, DIGIT_BITS) | digit
        if d < N_DIGITS - 1:
            higher = count_digit_ge(digit_sc[d], digit + 1)
            above = above + higher
            rank = rank - higher
    tau = tau_u ^ INT_MIN
    tau_sc[...] = tau
    excess = jnp.max(above + count_digit_ge(digit_sc[N_DIGITS - 1], digit) - kt)

    @pl.when(excess > 0.0)
    def _():
        need = kt - count_ge(tau + 1)

        def tie_body(j, seen):
            kj = keys_sc[j]
            eq = kj == tau
            eqb = eq.astype(BF16)
            earlier = _dot(before_ref[...], eqb) + seen
            keys_sc[j] = jnp.where(eq & (earlier >= need), INT_MIN, kj)
            return seen + jnp.sum(eqb.astype(F32), axis=0, keepdims=True)

        lax.fori_loop(0, qi + 1, tie_body, jnp.zeros((1, t), F32))

    _softmax_init(m_sc, acc_sc)

    def step(j, nblk):
        keys = [keys_sc[j + b] for b in range(nblk)]
        sel = (keys[0] if nblk == 1 else jnp.concatenate(keys, axis=0)) >= tau_sc[...]
        kb = dk_ref[_key_rows(j, nblk, t), :]
        vx = _value_cols(dvx_ref, j, nblk, slice(None))
        scores = [_dot(kb, q_ref[LANES * h:LANES * (h + 1), :]) for h in range(HEADS)]
        for h in range(HEADS):
            bias = _lane_repeat(_alibi(slopes[h], j, qi, t, nblk), t)
            scores[h] = jnp.where(sel, scores[h] + bias, -jnp.inf)
        _softmax_steps(scores, [vx] * HEADS, m_sc, acc_sc)

    _sweep_earlier_blocks(qi + 1, step)
    return jnp.concatenate([_softmax_result(acc_sc[h][...]) for h in range(HEADS)], axis=0).T


def _attn_d_call(qT, iqT, miscT, dkp, dvxT, ikp, before, topk, slopes):
    B, nb, _, t = qT.shape
    S = nb * t
    iw_block = MISC_IW // IDX_HEADS
    return pl.pallas_call(
        functools.partial(_attn_d_kernel, t=t, topk=topk, slopes=slopes),
        grid=(B,),
        in_specs=[_seq_cols_spec(nb, HW, t), _seq_cols_spec(nb, IDX_HEADS * LANES, t),
                  pl.BlockSpec((None, nb, IDX_HEADS, t), lambda b: (b, 0, iw_block, 0)),
                  _seq_rows_spec(S, LANES), _seq_cols_spec(nb, LANES, t), _seq_rows_spec(S, LANES),
                  _const_spec((t, t))],
        out_specs=_seq_rows_spec(S, GROUP_W),
        out_shape=jax.ShapeDtypeStruct((B, S, GROUP_W), F32),
        scratch_shapes=([pltpu.VMEM((nb, t, t), I32), pltpu.VMEM((1, t), I32)]
                        + [pltpu.VMEM((nb, t, t), BF16)] * N_DIGITS + _attn_scratch(HEADS, t)),
        compiler_params=_params("arbitrary"),
        name="attn_sparse",
    )(qT, iqT, miscT, dkp, dvxT, ikp, before)


def _out_proj_kernel(x_ref, oa_ref, ob_ref, oc_ref, od_ref, beta_ref, wo_ref, gate_ref,
                     g2_ref, scale_ref, shift_ref, wr_ref, br_ref, earlier_ref,
                     x1_ref, h2x_ref, route_ref, cnt_ref, cnt_sc):
    d_model = x_ref.shape[1]
    acc = None
    for i, o_ref in enumerate((oa_ref, ob_ref, oc_ref, od_ref)):
        sl = slice(GROUP_W * i, GROUP_W * (i + 1))
        mix = o_ref[...] * beta_ref[:, sl]
        part = jnp.dot(mix, wo_ref[sl, :], preferred_element_type=F32)
        acc = part if acc is None else acc + part
    x1 = x_ref[...] + gate_ref[...] * acc
    x1_ref[...] = x1
    ms = jnp.mean(x1 * x1, axis=-1, keepdims=True)
    h2 = x1 * lax.rsqrt(ms + RMS_EPS) * g2_ref[...]
    h2 = h2 * (1.0 + scale_ref[...]) + shift_ref[...]
    h2x_ref[:, :d_model] = h2

    logits = jnp.dot(h2, wr_ref[...], precision=lax.Precision.HIGHEST,
                     preferred_element_type=F32) + br_ref[...]
    lt = logits.T
    tm = lt.shape[1]
    g = lt[0:N_GROUPS]
    gmax = jnp.max(g, axis=0, keepdims=True)
    gi = lax.broadcasted_iota(I32, g.shape, 0)
    gidx = jnp.min(jnp.where(g == gmax, gi, N_GROUPS), axis=0, keepdims=True)
    g_prob = 1.0 / jnp.sum(jnp.exp(g - gmax), axis=0, keepdims=True)
    e_sel = jnp.zeros((EXPERTS_PER_GROUP, tm), F32)
    for gg in range(N_GROUPS):
        lo = N_GROUPS + EXPERTS_PER_GROUP * gg
        e_sel = e_sel + jnp.where(gidx == gg, lt[lo:lo + EXPERTS_PER_GROUP], 0.0)
    ei = lax.broadcasted_iota(I32, e_sel.shape, 0)
    v1 = jnp.max(e_sel, axis=0, keepdims=True)
    i1 = jnp.min(jnp.where(e_sel == v1, ei, EXPERTS_PER_GROUP), axis=0, keepdims=True)
    rest = jnp.where(ei == i1, -jnp.inf, e_sel)
    v2 = jnp.max(rest, axis=0, keepdims=True)
    i2 = jnp.min(jnp.where(rest == v2, ei, EXPERTS_PER_GROUP), axis=0, keepdims=True)
    e2 = jnp.exp(v2 - v1)
    w1 = g_prob / (1.0 + e2)
    w2 = g_prob * e2 / (1.0 + e2)
    in_group = jnp.where(ei == i1, w1, 0.0) + jnp.where(ei == i2, w2, 0.0)
    cw = jnp.concatenate([in_group, jnp.zeros((LANES - EXPERTS_PER_GROUP, tm), F32)], axis=0)
    h2x_ref[:, d_model:] = cw.T

    @pl.when(pl.program_id(0) == 0)
    def _():
        cnt_sc[...] = jnp.zeros(cnt_sc.shape, F32)

    rows = lax.broadcasted_iota(I32, (SUBLANES, tm), 0)
    onehot = (rows == gidx).astype(F32)
    seen = _dot(onehot.astype(BF16), earlier_ref[...]) + cnt_sc[...]
    rank = jnp.sum(onehot * seen, axis=0, keepdims=True)
    route_ref[...] = jnp.concatenate(
        [gidx, rank.astype(I32), jnp.zeros((SUBLANES - 2, tm), I32)], axis=0)
    cnt_sc[...] = cnt_sc[...] + jnp.sum(onehot, axis=1, keepdims=True)
    cnt_ref[...] = cnt_sc[:, :LANES]


def _out_proj_call(xf, outs, beta, w_out, gate1, g2, scale2, shift2, w_r, b_r, earlier, seq, tm):
    N, D = xf.shape
    per_b = seq // tm
    row = lambda i: (i, 0)
    full = lambda i: (0, 0)
    per_batch = pl.BlockSpec((None, 1, D), lambda i: (i // per_b, 0, 0))
    return pl.pallas_call(
        _out_proj_kernel,
        grid=(N // tm,),
        in_specs=([pl.BlockSpec((tm, D), row)] + [pl.BlockSpec((tm, GROUP_W), row)] * 4
                  + [pl.BlockSpec((1, D), full), pl.BlockSpec((D, D), full), per_batch,
                     pl.BlockSpec((1, D), full), per_batch, per_batch,
                     pl.BlockSpec((D, LANES), full), pl.BlockSpec((1, LANES), full),
                     pl.BlockSpec((tm, tm), full)]),
        out_specs=[pl.BlockSpec((tm, D), row), pl.BlockSpec((tm, D + LANES), row),
                   pl.BlockSpec((SUBLANES, tm), lambda i: (0, i)),
                   pl.BlockSpec((SUBLANES, LANES), full)],
        out_shape=[jax.ShapeDtypeStruct((N, D), F32), jax.ShapeDtypeStruct((N, D + LANES), F32),
                   jax.ShapeDtypeStruct((SUBLANES, N), I32),
                   jax.ShapeDtypeStruct((SUBLANES, LANES), F32)],
        scratch_shapes=[pltpu.VMEM((SUBLANES, tm), F32)],
        compiler_params=_params("arbitrary"),
        name="out_proj_router",
    )(xf, *outs, beta, w_out, gate1, g2, scale2, shift2, w_r, b_r, earlier)


MOE_TILE = 256

def _dispatch_kernel(slot_ref, h2x_ref, zeros_ref, xs_ref, sem):
    del zeros_ref

    def row_copy(g, u):
        return pltpu.make_async_copy(h2x_ref.at[g, pl.ds(u, 1), :],
                                     xs_ref.at[pl.ds(slot_ref[0, g * SUBLANES + u], 1), :], sem)

    def issue(g, carry):
        for u in range(SUBLANES):
            row_copy(g, u).start()
        return carry

    lax.fori_loop(0, h2x_ref.shape[0], issue, 0)

    def drain(g, carry):
        for u in range(SUBLANES):
            row_copy(g, u).wait()
        return carry

    lax.fori_loop(0, h2x_ref.shape[0], drain, 0)


def _dispatch_call(slot, h2x, xs_zero, tm):
    N, DX = h2x.shape
    return pl.pallas_call(
        _dispatch_kernel,
        grid=(N // tm,),
        in_specs=[pl.BlockSpec((None, 1, tm), lambda i: (i, 0, 0), memory_space=pltpu.SMEM),
                  pl.BlockSpec((tm // SUBLANES, SUBLANES, DX), lambda i: (i, 0, 0)),
                  pl.BlockSpec(memory_space=pl.ANY)],
        out_specs=pl.BlockSpec(memory_space=pl.ANY),
        out_shape=jax.ShapeDtypeStruct(xs_zero.shape, F32),
        scratch_shapes=[pltpu.SemaphoreType.DMA(())],
        input_output_aliases={2: 0},
        compiler_params=_params("arbitrary"),
        name="moe_dispatch",
    )(slot.reshape(N // tm, 1, tm), h2x.reshape(N // SUBLANES, SUBLANES, DX), xs_zero)


def _expert_kernel(group_ref, valid_ref, xs_ref, w1_ref, w3_ref, w2_ref, y_ref):
    del group_ref
    valid = valid_ref[pl.program_id(0)]
    d_model = y_ref.shape[1]

    @pl.when(valid == 0)
    def _():
        y_ref[...] = jnp.zeros(y_ref.shape, F32)

    @pl.when(valid > 0)
    def _():
        x = xs_ref[:, :d_model]
        cw = xs_ref[:, d_model:]
        acc = None
        for e in range(EXPERTS_PER_GROUP):
            a = _dot(x, w1_ref[e])
            b = _dot(x, w3_ref[e])
            hid = a * jax.nn.sigmoid(a) * b * jnp.broadcast_to(cw[:, e:e + 1], a.shape)
            part = _dot(hid, w2_ref[e])
            acc = part if acc is None else acc + part
        y_ref[...] = acc


def _expert_call(tile_group, tile_valid, xs, w1, w3, w2, layer):
    P, DX = xs.shape
    D, FF = w1.shape[-2:]
    tr = MOE_TILE

    def group_spec(rows, cols):
        return pl.BlockSpec((None, None, EXPERTS_PER_GROUP, rows, cols),
                            lambda i, g, v: (layer, g[i], 0, 0, 0),
                            pipeline_mode=pl.Buffered(1))

    grid_spec = pltpu.PrefetchScalarGridSpec(
        num_scalar_prefetch=2,
        grid=(P // tr,),
        in_specs=[pl.BlockSpec((tr, DX), lambda i, g, v: (i, 0)),
                  group_spec(D, FF), group_spec(D, FF), group_spec(FF, D)],
        out_specs=pl.BlockSpec((tr, D), lambda i, g, v: (i, 0)),
    )
    return pl.pallas_call(
        _expert_kernel,
        grid_spec=grid_spec,
        out_shape=jax.ShapeDtypeStruct((P, D), F32),
        compiler_params=_params("arbitrary"),
        name="moe_experts",
    )(tile_group, tile_valid, xs, w1, w3, w2)


def _residual_kernel(slot_ref, slot_next_ref, x1_ref, gate_ref, ys_ref, o_ref, buf, sems):
    i = pl.program_id(0)

    def gather(slots, b, wait):
        def body(g, carry):
            for u in range(SUBLANES):
                cp = pltpu.make_async_copy(ys_ref.at[pl.ds(slots[0, g * SUBLANES + u], 1), :],
                                           buf.at[b, g, pl.ds(u, 1), :], sems.at[b])
                cp.wait() if wait else cp.start()
            return carry
        lax.fori_loop(0, buf.shape[1], body, 0)

    @pl.when(i == 0)
    def _():
        gather(slot_ref, 0, False)

    @pl.when(i + 1 < pl.num_programs(0))
    def _():
        gather(slot_next_ref, (i + 1) % 2, False)

    gather(slot_ref, i % 2, True)
    o_ref[...] = x1_ref[...] + gate_ref[...] * buf[i % 2].reshape(x1_ref.shape)


def _residual_call(slot, x1, gate2, ys, seq, tm):
    N, D = x1.shape
    n = N // tm
    per_b = seq // tm
    row = pl.BlockSpec((tm, D), lambda i: (i, 0))
    slot3 = slot.reshape(n, 1, tm)
    return pl.pallas_call(
        _residual_kernel,
        grid=(n,),
        in_specs=[pl.BlockSpec((None, 1, tm), lambda i: (i, 0, 0), memory_space=pltpu.SMEM),
                  pl.BlockSpec((None, 1, tm), lambda i: (jnp.minimum(i + 1, n - 1), 0, 0),
                               memory_space=pltpu.SMEM),
                  row, pl.BlockSpec((None, 1, D), lambda i: (i // per_b, 0, 0)),
                  pl.BlockSpec(memory_space=pl.ANY)],
        out_specs=row,
        out_shape=jax.ShapeDtypeStruct((N, D), F32),
        scratch_shapes=[pltpu.VMEM((2, tm // SUBLANES, SUBLANES, D), F32),
                        pltpu.SemaphoreType.DMA((2,))],
        compiler_params=_params("arbitrary"),
        name="moe_residual",
    )(slot3, slot3, x1, gate2, ys)


def _moe_routing(route, cnt, n_tokens):
    tr = MOE_TILE
    n_tiles = n_tokens // tr + N_GROUPS
    counts = cnt[:N_GROUPS, 0].astype(I32)
    padded = (counts + tr - 1) // tr * tr
    ends = jnp.cumsum(padded)
    starts = ends - padded
    group, rank = route[0], route[1]
    slot = starts[group] + rank
    tile_start = jnp.arange(n_tiles, dtype=I32) * tr
    tile_group = jnp.minimum(jnp.sum((tile_start[:, None] >= ends[None, :]).astype(I32), axis=1),
                             N_GROUPS - 1)
    tile_valid = jnp.clip(starts[tile_group] + counts[tile_group] - tile_start, 0, tr)
    tile_valid = jnp.where(tile_start < ends[-1], tile_valid, 0)
    return slot, tile_group, tile_valid


def _block_diag_mean(width, group, valid_in_128=None):
    i = jnp.arange(width)
    same = (i[:, None] // group) == (i[None, :] // group)
    if valid_in_128 is not None:
        same = same & ((i[:, None] % LANES) < valid_in_128) & ((i[None, :] % LANES) < valid_in_128)
    return jnp.where(same, 1.0 / group, 0.0).astype(F32)


def _placement(src_width, dst_width, pairs):
    src = jnp.array([p[0] for p in pairs], I32)
    dst = jnp.array([p[1] for p in pairs], I32)
    return jnp.zeros((src_width, dst_width), F32).at[src, dst].set(1.0)


def _placements():
    per_head = [(HEAD_DIM * h + d, LANES * h + d) for h in range(HEADS) for d in range(HEAD_DIM)]
    first64 = [(d, d) for d in range(HEAD_DIM)]
    idx_k = [(MISC_IK + d, d) for d in range(IDX_DIM)]
    table = {"a_kp": (GROUP_W, per_head), "b_kp": (GROUP_W, per_head), "c_kp": (GROUP_W, per_head),
             "d_kp": (GROUP_W, first64), "i_kp": (LANES, idx_k)}
    return [_placement(table[name][0], width, table[name][1])
            for name, src, width, orient, ones in OUTPUTS if orient == "rows"]


def kernel(x, c, ada_w, ada_b, norm1_g, norm2_g, w_in, b_f, qn_a, kn_a, qn_c, kn_c,
           lam_q1, lam_k1, lam_q2, lam_k2, subln_g, qn_d, kn_d, mix_beta, w_out,
           w_group, b_group, w_expert, b_expert, w1, w3, w2):
    B, S, D = x.shape
    L = ada_w.shape[0]
    N = B * S
    topk = min(TOPK_MAX, S // 4)
    t = ATTN_BLOCK
    tm = t
    slopes = [2.0 ** (-8.0 * i / (2 * HEADS)) for i in range(1, 2 * HEADS + 1)]
    slopes_c, slopes_d = tuple(slopes[0::2]), tuple(slopes[1::2])

    idx_t = jnp.arange(t)
    after = (idx_t[None, :] > idx_t[:, None]).astype(BF16)
    before = (idx_t[None, :] < idx_t[:, None]).astype(BF16)
    upto = (idx_t[None, :] <= idx_t[:, None]).astype(BF16)
    g64 = _block_diag_mean(GROUP_W, HEAD_DIM)
    g32 = _block_diag_mean(GROUP_W, DIFF_DIM)
    places = _placements()
    lane = jnp.arange(LANES)
    cum_sel = jnp.stack([jnp.broadcast_to((lane == MISC_AF + h)[:, None], (LANES, LANES))
                         for h in range(HEADS)]).astype(BF16)

    mod = _ada_call(c, ada_w, ada_b)
    xf = x.reshape(N, D)

    for l in range(L):
        m6 = mod[l].reshape(B, 6, 1, D)
        shift1, scale1, gate1, shift2, scale2, gate2 = (m6[:, i] for i in range(6))

        ones = jnp.ones((GROUP_W - HEAD_DIM,), F32)
        gains = jnp.stack([jnp.tile(qn_a[l], HEADS) * (HEAD_DIM ** -0.5 * LOG2E),
                           jnp.tile(kn_a[l], HEADS),
                           jnp.tile(qn_c[l], 2 * HEADS) * (DIFF_DIM ** -0.5 * LOG2E),
                           jnp.tile(kn_c[l], 2 * HEADS),
                           jnp.tile(qn_d[l], HEADS) * (HEAD_DIM ** -0.5 * LOG2E),
                           jnp.concatenate([kn_d[l], ones])]).astype(F32)
        gains = jnp.concatenate([gains, jnp.zeros((2, GROUP_W), F32)], axis=0)

        outs = _in_proj_call(xf, scale1, shift1, norm1_g[l].reshape(1, D), w_in, l, gains,
                             g64, g32, places, B, S, tm)
        sec = {name: o for (name, _, _, _, _), o in zip(OUTPUTS, outs)}
        for name, src, width, orient, ones_ in OUTPUTS:
            if orient == "rows":
                sec[name] = sec[name].reshape(B, S, width)
        misc = outs[len(OUTPUTS)].reshape(B, S, LANES)
        miscT = outs[len(OUTPUTS) + 1]

        bf_row = jnp.zeros((1, LANES), F32).at[0, MISC_AF:MISC_AF + HEADS].set(b_f[l].astype(F32))
        cumrep = _cum_call(misc, bf_row, upto, cum_sel)

        o_a = _attn_a_call(sec["a_qT"], sec["a_kp"], sec["a_vxT"], cumrep)
        o_b = _attn_b_call(sec["b_qT"], sec["b_kp"], sec["b_vT"], after)
        lambda_init = 0.8 - 0.6 * math.exp(-0.3 * l)
        lamv = jnp.stack([lam_q1[l], lam_k1[l], lam_q2[l], lam_k2[l]]).astype(F32)
        o_c = _attn_c_call(sec["c_qT"], sec["c_kp"], sec["c_vxT"], lamv,
                           subln_g[l].reshape(HEAD_DIM, 1).astype(F32), slopes_c, lambda_init)
        o_d = _attn_d_call(sec["d_qT"], sec["i_qT"], miscT, sec["d_kp"], sec["d_vxT"], sec["i_kp"],
                           before, topk, slopes_d)

        w_r = jnp.concatenate([w_group[l], w_expert[l],
                               jnp.zeros((D, LANES - N_GROUPS - N_EXPERTS), F32)], axis=1)
        b_r = jnp.concatenate([b_group[l], b_expert[l],
                               jnp.zeros((LANES - N_GROUPS - N_EXPERTS,), F32)]).reshape(1, LANES)
        x1, h2x, route, cnt = _out_proj_call(
            xf, [o.reshape(N, GROUP_W) for o in (o_a, o_b, o_c, o_d)], mix_beta[l].reshape(1, D),
            w_out[l], gate1, norm2_g[l].reshape(1, D), scale2, shift2, w_r, b_r,
            after, S, tm)

        slot, tile_group, tile_valid = _moe_routing(route, cnt, N)
        xs = _dispatch_call(slot, h2x, jnp.zeros((tile_group.shape[0] * MOE_TILE, D + LANES), F32), tm)
        ys = _expert_call(tile_group, tile_valid, xs, w1, w3, w2, l)
        xf = _residual_call(slot, x1, gate2, ys, S, tm)

    return xf.reshape(B, S, D)
```

```python
import functools
import math

import jax
import jax.numpy as jnp
from jax import lax
from jax.experimental import pallas as pl
from jax.experimental.pallas import tpu as pltpu

F32 = jnp.float32
BF16 = jnp.bfloat16
I32 = jnp.int32

HEAD_DIM = 64
HEADS = 4
GROUP_W = HEADS * HEAD_DIM
DIFF_DIM = HEAD_DIM // 2
IDX_HEADS = 8
IDX_DIM = 32
TOPK_MAX = 256
N_GROUPS = 4
EXPERTS_PER_GROUP = 8
N_EXPERTS = N_GROUPS * EXPERTS_PER_GROUP
EXPERT_FF = 256
RMS_EPS = 1e-6
NEG_INF = -1e30
INT_MIN = -(2 ** 31)
LOG2E = math.log2(math.e)

LANES = 128
SUBLANES = 8
ATTN_BLOCK = 256
VMEM_LIMIT = 56 * 1024 * 1024

MISC_IK = 0
MISC_IW = IDX_DIM
MISC_AF = IDX_DIM + IDX_HEADS


def _params(*sem):
    return pltpu.CompilerParams(dimension_semantics=sem, vmem_limit_bytes=VMEM_LIMIT)


def _log_sigmoid(z):
    return jnp.minimum(z, 0.0) - jnp.log1p(jnp.exp(-jnp.abs(z)))


def _log2_sigmoid(z2):
    return jnp.minimum(z2, 0.0) - jnp.log2(1.0 + jnp.exp2(-jnp.abs(z2)))


def _split_bf16(x, parts):
    out = []
    rem = x
    for _ in range(parts):
        p = rem.astype(BF16)
        out.append(p)
        rem = rem - p.astype(F32)
    return out


def _dot(a, b):
    return jnp.dot(a, b, preferred_element_type=F32)


def _ada_kernel(c_ref, w_ref, b_ref, o_ref):
    c = c_ref[...]
    ca = c * jax.nn.sigmoid(c)
    o_ref[...] = jnp.dot(ca, w_ref[...], precision=lax.Precision.HIGHEST,
                         preferred_element_type=F32) + b_ref[...]


def _ada_call(c, ada_w, ada_b):
    L, D, E = ada_w.shape
    B = c.shape[0]
    tn = 1536
    return pl.pallas_call(
        _ada_kernel,
        grid=(L, E // tn),
        in_specs=[pl.BlockSpec((B, D), lambda l, j: (0, 0)),
                  pl.BlockSpec((None, D, tn), lambda l, j: (l, 0, j)),
                  pl.BlockSpec((None, 1, tn), lambda l, j: (l, 0, j))],
        out_specs=pl.BlockSpec((None, B, tn), lambda l, j: (l, 0, j)),
        out_shape=jax.ShapeDtypeStruct((L, B, E), F32),
        compiler_params=_params("arbitrary", "arbitrary"),
        name="ada_mod",
    )(c, ada_w, ada_b.reshape(L, 1, E))


SOURCES = ("a_q", "a_k", "a_v", "b_q", "b_k", "b_v", "c_q", "c_k", "c_v", "d_q", "i_q", "d_kv")
SRC_NORM = {"a_q": ("n64", 0), "a_k": ("n64", 1), "c_q": ("n32", 2), "c_k": ("n32", 3),
            "d_q": ("n64", 4), "d_kv": ("n64", 5)}
SRC_SCALE = {"b_q": HEAD_DIM ** -0.5 * LOG2E, "i_q": IDX_DIM ** -0.5}
HW = HEADS * LANES
OUTPUTS = (
    ("a_qT", "a_q", HW, "cols", False), ("a_kp", "a_k", HW, "rows", False),
    ("a_vxT", "a_v", HW, "cols", True),
    ("b_qT", "b_q", HW, "cols", False), ("b_kp", "b_k", HW, "rows", False),
    ("b_vT", "b_v", GROUP_W, "cols", False),
    ("c_qT", "c_q", 2 * HW, "cols", False), ("c_kp", "c_k", HW, "rows", False),
    ("c_vxT", "c_v", HW, "cols", True),
    ("d_qT", "d_q", HW, "cols", False), ("i_qT", "i_q", IDX_HEADS * LANES, "cols", False),
    ("d_kp", "d_kv", LANES, "rows", False), ("d_vxT", "d_kv", LANES, "cols", True),
    ("i_kp", "misc", LANES, "rows", False),
)
N_MAIN = len(SOURCES) * GROUP_W


def _layout_plans():
    per_head = [p for h in range(HEADS) for p in ((HEAD_DIM * h, HEAD_DIM), (None, HEAD_DIM))]
    diff_q = []
    for h in range(HEADS):
        for c in range(2):
            lead = DIFF_DIM * c
            if lead:
                diff_q.append((None, lead))
            diff_q.append((HEAD_DIM * h + DIFF_DIM * c, DIFF_DIM))
            diff_q.append((None, LANES - lead - DIFF_DIM))
    idx_q = [p for hh in range(IDX_HEADS) for p in ((IDX_DIM * hh, IDX_DIM), (None, LANES - IDX_DIM))]
    return {"a_qT": per_head, "a_vxT": per_head, "b_qT": per_head, "b_vT": [(0, GROUP_W)],
            "c_qT": diff_q, "c_vxT": per_head, "d_qT": per_head, "i_qT": idx_q,
            "d_vxT": [(HEAD_DIM, HEAD_DIM), (None, HEAD_DIM)],
            "a_kp": per_head, "b_kp": per_head, "c_kp": per_head,
            "d_kp": [(0, HEAD_DIM), (None, HEAD_DIM)],
            "i_kp": [(MISC_IK, IDX_DIM), (None, LANES - IDX_DIM)]}


LAYOUT_PLANS = _layout_plans()

IN_SPLITS = (GROUP_W, GROUP_W, GROUP_W, HEADS, GROUP_W, GROUP_W, GROUP_W, GROUP_W, GROUP_W,
             GROUP_W, GROUP_W, HEAD_DIM, HEAD_DIM, IDX_HEADS * IDX_DIM, IDX_DIM, IDX_HEADS)
IN_NAMES = ("a_q", "a_k", "a_v", "a_f", "b_q", "b_k", "b_v", "c_q", "c_k", "c_v",
            "d_q", "d_k", "d_v", "i_q", "i_k", "i_w")
IN_OFFSETS = {n: (sum(IN_SPLITS[:i]), IN_SPLITS[i]) for i, n in enumerate(IN_NAMES)}
P_IN = sum(IN_SPLITS)
SECTION_PARTS = {name: (name,) for name in SOURCES if name != "d_kv"}
SECTION_PARTS["d_kv"] = ("d_k", "d_v")
SECTION_PARTS["misc"] = ("i_k", "i_w", "a_f")
REALIGN_ROWS = 256


def _realign_weights(w_ref, w_sc):
    d_model = w_ref.shape[0]
    for r0 in range(0, d_model, REALIGN_ROWS):
        rows = slice(r0, r0 + REALIGN_ROWS)
        for i, name in enumerate(SOURCES + ("misc",)):
            width = LANES if name == "misc" else GROUP_W
            pieces, used = [], 0
            for part in SECTION_PARTS[name]:
                off, w = IN_OFFSETS[part]
                base = off // LANES * LANES
                end = min(-(-(off + w) // LANES) * LANES, P_IN)
                window = w_ref[rows, base:end]
                pieces.append(window[:, off - base:off - base + w])
                used += w
            if used < width:
                pieces.append(jnp.zeros((REALIGN_ROWS, width - used), w_sc.dtype))
            block = pieces[0] if len(pieces) == 1 else jnp.concatenate(pieces, axis=1)
            w_sc[rows, i * GROUP_W:i * GROUP_W + width] = block


def _in_proj_kernel(*refs):
    (x_ref, scale_ref, shift_ref, g1_ref, w_in_ref, gains_ref, g64_ref, g32_ref) = refs[:8]
    out_refs = refs[8:-1]
    w_ref = refs[-1]

    @pl.when(pl.program_id(0) == 0)
    def _():
        _realign_weights(w_in_ref, w_ref)

    x = x_ref[...]
    ms = jnp.mean(x * x, axis=-1, keepdims=True)
    h = x * lax.rsqrt(ms + RMS_EPS) * g1_ref[...]
    h = h * (1.0 + scale_ref[...]) + shift_ref[...]

    wm = w_ref[:, N_MAIN:N_MAIN + LANES]
    misc = functools.reduce(lambda a, b: a + b, [_dot(term, wm) for term in _split_bf16(h, 3)])
    out_refs[len(OUTPUTS)][...] = misc
    out_refs[len(OUTPUTS) + 1][...] = misc.T

    final = {"misc": misc}
    raw = {name: _dot(h, w_ref[:, i * GROUP_W:(i + 1) * GROUP_W])
           for i, name in enumerate(SOURCES)}
    msqs = {name: _dot(raw[name] * raw[name],
                       (g64_ref if SRC_NORM[name][0] == "n64" else g32_ref)[...])
            for name in SOURCES if name in SRC_NORM}
    for name in SOURCES:
        sec = raw[name]
        if name in SRC_NORM:
            r = SRC_NORM[name][1]
            fac = lax.rsqrt(msqs[name] + RMS_EPS)
            if name == "d_kv":
                lane = lax.broadcasted_iota(I32, sec.shape, 1)
                fac = jnp.where(lane < HEAD_DIM, fac, 1.0)
            sec = sec * fac * gains_ref[r:r + 1, :]
        elif name in SRC_SCALE:
            sec = sec * SRC_SCALE[name]
        final[name] = sec

    tokens = x.shape[0]
    transposed = {}
    for (name, src, width, orient, ones), o_ref in zip(OUTPUTS, out_refs):
        if orient == "cols" and src not in transposed:
            transposed[src] = final[src].T
        pieces = []
        for start, size in LAYOUT_PLANS[name]:
            if orient == "rows":
                pieces.append(jnp.zeros((tokens, size), F32) if start is None
                              else final[src][:, start:start + size])
            else:
                pieces.append(jnp.full((size, tokens), 1.0 if ones else 0.0, F32) if start is None
                              else transposed[src][start:start + size])
        o_ref[...] = jnp.concatenate(pieces, axis=1 if orient == "rows" else 0)


def _in_proj_call(xf, scale1, shift1, g1, w_in, layer, gains, g64, g32, batch, seq, tm):
    N, D = xf.shape
    per_b = seq // tm
    row = lambda i: (i, 0)
    full = lambda i: (0, 0)
    col4 = lambda i: (i // per_b, i % per_b, 0, 0)
    out_shape, out_specs = [], []
    for name, src, width, orient, ones in OUTPUTS:
        if orient == "rows":
            out_shape.append(jax.ShapeDtypeStruct((N, width), F32))
            out_specs.append(pl.BlockSpec((tm, width), row))
        else:
            out_shape.append(jax.ShapeDtypeStruct((batch, per_b, width, tm), F32))
            out_specs.append(pl.BlockSpec((None, None, width, tm), col4))
    out_shape += [jax.ShapeDtypeStruct((N, LANES), F32),
                  jax.ShapeDtypeStruct((batch, per_b, LANES, tm), F32)]
    out_specs += [pl.BlockSpec((tm, LANES), row), pl.BlockSpec((None, None, LANES, tm), col4)]
    return pl.pallas_call(
        _in_proj_kernel,
        grid=(N // tm,),
        in_specs=([pl.BlockSpec((tm, D), row),
                   pl.BlockSpec((None, 1, D), lambda i: (i // per_b, 0, 0)),
                   pl.BlockSpec((None, 1, D), lambda i: (i // per_b, 0, 0)),
                   pl.BlockSpec((1, D), full),
                   pl.BlockSpec((None,) + w_in.shape[1:], lambda i: (layer, 0, 0),
                                pipeline_mode=pl.Buffered(1)),
                   pl.BlockSpec(gains.shape, full),
                   pl.BlockSpec(g64.shape, full),
                   pl.BlockSpec(g32.shape, full)]),
        out_specs=out_specs,
        out_shape=out_shape,
        scratch_shapes=[pltpu.VMEM((D, N_MAIN + LANES), F32)],
        compiler_params=_params("arbitrary"),
        name="in_proj",
    )(xf, scale1, shift1, g1, w_in, gains, g64, g32)


def _cum_kernel(misc_ref, bf_ref, tri_ref, sel_ref, o_ref, *, blk):
    seq = misc_ref.shape[0]
    carry = jnp.zeros((1, LANES), F32)
    tri = tri_ref[...]
    for j in range(seq // blk):
        rows = slice(j * blk, (j + 1) * blk)
        lf = _log_sigmoid(misc_ref[rows, :] + bf_ref[...])
        c = carry
        for part in _split_bf16(lf, 3):
            c = c + _dot(tri, part)
        carry = c[blk - 1:blk, :]
        parts = _split_bf16(c, 3)
        for h in range(HEADS):
            rep = _dot(parts[0], sel_ref[h]) + _dot(parts[1], sel_ref[h]) + _dot(parts[2], sel_ref[h])
            o_ref[h, rows, :] = rep * LOG2E


def _cum_call(misc, bf_row, tri, sel):
    B, S, _ = misc.shape
    blk = tri.shape[0]
    return pl.pallas_call(
        functools.partial(_cum_kernel, blk=blk),
        grid=(B,),
        in_specs=[pl.BlockSpec((None, S, LANES), lambda b: (b, 0, 0)),
                  pl.BlockSpec((1, LANES), lambda b: (0, 0)),
                  pl.BlockSpec((blk, blk), lambda b: (0, 0)),
                  pl.BlockSpec((HEADS, LANES, LANES), lambda b: (0, 0, 0))],
        out_specs=pl.BlockSpec((None, HEADS, S, LANES), lambda b: (b, 0, 0, 0)),
        out_shape=jax.ShapeDtypeStruct((B, HEADS, S, LANES), F32),
        compiler_params=_params("arbitrary"),
        name="forget_cumsum",
    )(misc, bf_row, tri, sel)


def _seq_rows_spec(seq, width):
    return pl.BlockSpec((None, seq, width), lambda b: (b, 0, 0))


def _seq_cols_spec(nb, width, t):
    return pl.BlockSpec((None, nb, width, t), lambda b: (b, 0, 0, 0))


def _const_spec(shape):
    return pl.BlockSpec(shape, lambda b: (0,) * len(shape))


def _for_each_query_tile(q_ref, o_ref, t, tile_fn):
    def body(qi, carry):
        o_ref[pl.ds(pl.multiple_of(qi * t, t), t), :] = tile_fn(qi, q_ref.at[qi])
        return carry

    lax.fori_loop(0, q_ref.shape[0], body, 0)


def _lane_repeat(x, t):
    return jnp.concatenate([x] * (t // LANES), axis=1)


def _softmax_steps(scores, values, m_refs, acc_refs):
    probs, alphas = [], []
    for s, m_ref in zip(scores, m_refs):
        m_old = m_ref[...]
        m_new = jnp.maximum(m_old, jnp.max(s, axis=0, keepdims=True))
        alphas.append(jnp.exp2(m_old - m_new))
        probs.append(jnp.exp2(s - m_new))
        m_ref[...] = m_new
    for p, vx, alpha, acc_ref in zip(probs, values, alphas, acc_refs):
        acc_ref[...] = alpha * acc_ref[...] + _dot(vx, p)


def _softmax_init(m_scs, acc_scs):
    for m_sc, acc_sc in zip(m_scs, acc_scs):
        m_sc[...] = jnp.full(m_sc.shape, NEG_INF, F32)
        acc_sc[...] = jnp.zeros(acc_sc.shape, F32)


def _softmax_result(acc):
    return acc[:HEAD_DIM] / acc[HEAD_DIM:]


def _alibi(slope, j, qi, t, nblk=1):
    key = lax.broadcasted_iota(I32, (nblk * t, LANES), 0)
    return (slope * LOG2E) * (key + (j - qi) * t).astype(F32)


def _sweep_earlier_blocks(qi, step):
    def body(i, carry):
        step(2 * i, 2)
        return carry

    lax.fori_loop(0, qi // 2, body, 0)

    @pl.when(qi % 2 == 1)
    def _():
        step(qi - 1, 1)


def _sweep_causal_blocks(qi, step):
    @pl.when(qi == 0)
    def _():
        step(0, 1, True)

    @pl.when(qi > 0)
    def _():
        step(qi - 1, 2, True)

    _sweep_earlier_blocks(jnp.maximum(qi - 1, 0), step)


def _causal_mask(t, nblk, strict=False):
    key = lax.broadcasted_iota(I32, (nblk * t, t), 0) - (nblk - 1) * t
    query = lax.broadcasted_iota(I32, (nblk * t, t), 1)
    return key < query if strict else key <= query


def _key_rows(j, nblk, t):
    return pl.ds(pl.multiple_of(j * t, t), nblk * t)


def _value_cols(vx_ref, j, nblk, rows):
    tiles = [vx_ref[j + b, rows, :] for b in range(nblk)]
    return tiles[0] if nblk == 1 else jnp.concatenate(tiles, axis=1)


def _attn_a_kernel(q_ref, k_ref, vx_ref, cum_ref, o_ref, *scratch, t):
    m_sc, acc_sc = scratch[:HEADS], scratch[HEADS:]

    def tile(qi, q):
        _softmax_init(m_sc, acc_sc)

        def step(j, nblk, masked=False):
            rows = _key_rows(j, nblk, t)
            heads = [slice(LANES * h, LANES * (h + 1)) for h in range(HEADS)]
            scores = [_dot(k_ref[rows, hs], q[hs, :]) for hs in heads]
            for h in range(HEADS):
                s = scores[h] - _lane_repeat(cum_ref[h, rows, :], t)
                scores[h] = jnp.where(_causal_mask(t, nblk), s, -jnp.inf) if masked else s
            _softmax_steps(scores, [_value_cols(vx_ref, j, nblk, hs) for hs in heads],
                           m_sc, acc_sc)

        _sweep_causal_blocks(qi, step)
        return jnp.concatenate([_softmax_result(acc_sc[h][...]) for h in range(HEADS)], axis=0).T

    _for_each_query_tile(q_ref, o_ref, t, tile)


def _attn_scratch(n, t, rows=LANES):
    return [pltpu.VMEM((1, t), F32)] * n + [pltpu.VMEM((rows, t), F32)] * n


def _attn_a_call(qT, kp, vxT, cumrep):
    B, nb, _, t = qT.shape
    S = nb * t
    return pl.pallas_call(
        functools.partial(_attn_a_kernel, t=t),
        grid=(B,),
        in_specs=[_seq_cols_spec(nb, HW, t), _seq_rows_spec(S, HW), _seq_cols_spec(nb, HW, t),
                  pl.BlockSpec((None, HEADS, S, LANES), lambda b: (b, 0, 0, 0))],
        out_specs=_seq_rows_spec(S, GROUP_W),
        out_shape=jax.ShapeDtypeStruct((B, S, GROUP_W), F32),
        scratch_shapes=_attn_scratch(HEADS, t),
        compiler_params=_params("arbitrary"),
        name="attn_forget",
    )(qT, kp, vxT, cumrep)


STICK_SPLIT_TERMS = 3


def _attn_b_kernel(q_ref, k_ref, v_ref, after_ref, o_ref, *scratch, t):
    tile = functools.partial(_attn_b_tile, k_ref=k_ref, v_ref=v_ref, after_ref=after_ref,
                             r_sc=scratch[:HEADS], acc_sc=scratch[HEADS:], t=t)
    _for_each_query_tile(q_ref, o_ref, t, tile)


def _attn_b_tile(qi, q_ref, *, k_ref, v_ref, after_ref, r_sc, acc_sc, t):
    for h in range(HEADS):
        r_sc[h][...] = jnp.zeros(r_sc[h].shape, F32)
        acc_sc[h][...] = jnp.zeros(acc_sc[h].shape, F32)

    def step(j, nblk, masked=False):
        after = after_ref[...]
        rows = _key_rows(j, nblk, t)
        heads = [slice(LANES * h, LANES * (h + 1)) for h in range(HEADS)]
        zs = [_dot(k_ref[rows, hs], q_ref[hs, :]) for hs in heads]
        lbs, splits, later_sums = [], [], []
        for h in range(HEADS):
            lb = _log2_sigmoid(zs[h])
            lm = lb - zs[h]
            if masked:
                lm = jnp.where(_causal_mask(t, nblk, strict=True), lm, 0.0)
            blocks = [lm[b * t:(b + 1) * t] for b in range(nblk)]
            splits.append([_split_bf16(blk, STICK_SPLIT_TERMS) for blk in blocks])
            sums = [jnp.sum(blk, axis=0, keepdims=True) for blk in blocks]
            r_old = r_sc[h][...]
            total = sums[0]
            for s_ in sums[1:]:
                total = total + s_
            r_sc[h][...] = r_old + total
            lbs.append(lb + r_old)
            later, run = [], None
            for b in reversed(range(nblk)):
                later.append(run)
                run = sums[b] if run is None else run + sums[b]
            later_sums.append(later[::-1])
        suffixes = [[functools.reduce(lambda a, b: a + b, [_dot(after, term) for term in terms])
                     for terms in splits[h]] for h in range(HEADS)]
        ws = []
        for h in range(HEADS):
            parts = [suffixes[h][b] if later_sums[h][b] is None else suffixes[h][b] + later_sums[h][b]
                     for b in range(nblk)]
            suffix = parts[0] if nblk == 1 else jnp.concatenate(parts, axis=0)
            w = jnp.exp2(lbs[h] + suffix)
            if masked:
                w = jnp.where(_causal_mask(t, nblk, strict=True), w, 0.0)
            ws.append(w)
        for h in range(HEADS):
            acc_sc[h][...] += _dot(_value_cols(v_ref, j, nblk, slice(HEAD_DIM * h, HEAD_DIM * (h + 1))),
                                   ws[h])

    @pl.when(qi == 0)
    def _():
        step(0, 1, True)

    @pl.when(qi > 0)
    def _():
        step(qi - 1, 2, True)

    rest = jnp.maximum(qi - 1, 0)

    @pl.when(rest % 2 == 1)
    def _():
        step(rest - 1, 1)

    pairs = rest // 2

    def body(i, carry):
        step(2 * (pairs - 1 - i), 2)
        return carry

    lax.fori_loop(0, pairs, body, 0)
    return jnp.concatenate([acc_sc[h][...] for h in range(HEADS)], axis=0).T


def _attn_b_call(qT, kp, vT, after):
    B, nb, _, t = qT.shape
    S = nb * t
    return pl.pallas_call(
        functools.partial(_attn_b_kernel, t=t),
        grid=(B,),
        in_specs=[_seq_cols_spec(nb, HW, t), _seq_rows_spec(S, HW), _seq_cols_spec(nb, GROUP_W, t),
                  _const_spec((t, t))],
        out_specs=_seq_rows_spec(S, GROUP_W),
        out_shape=jax.ShapeDtypeStruct((B, S, GROUP_W), F32),
        scratch_shapes=_attn_scratch(HEADS, t, HEAD_DIM),
        compiler_params=_params("arbitrary"),
        name="attn_stick",
    )(qT, kp, vT, after)


def _attn_c_kernel(q_ref, k_ref, vx_ref, lamv_ref, subg_ref, o_ref, *scratch,
                   t, slopes, lambda_init):
    tile = functools.partial(_attn_c_tile, k_ref=k_ref, vx_ref=vx_ref, lamv_ref=lamv_ref,
                             subg_ref=subg_ref, m_sc=scratch[:2 * HEADS], acc_sc=scratch[2 * HEADS:],
                             t=t, slopes=slopes, lambda_init=lambda_init)
    _for_each_query_tile(q_ref, o_ref, t, tile)


def _attn_c_tile(qi, q_ref, *, k_ref, vx_ref, lamv_ref, subg_ref, m_sc, acc_sc,
                 t, slopes, lambda_init):
    _softmax_init(m_sc, acc_sc)

    def step(j, nblk, masked=False):
        rows = _key_rows(j, nblk, t)
        heads = [slice(LANES * h, LANES * (h + 1)) for h in range(HEADS)]
        scores = [_dot(k_ref[rows, heads[g // 2]], q_ref[LANES * g:LANES * (g + 1), :])
                  for g in range(2 * HEADS)]
        for g in range(2 * HEADS):
            s = scores[g] + _lane_repeat(_alibi(slopes[g // 2], j, qi, t, nblk), t)
            scores[g] = jnp.where(_causal_mask(t, nblk), s, -jnp.inf) if masked else s
        _softmax_steps(scores, [_value_cols(vx_ref, j, nblk, heads[g // 2])
                                for g in range(2 * HEADS)], m_sc, acc_sc)

    _sweep_causal_blocks(qi, step)

    lv = lamv_ref[...]
    lam = (jnp.exp(jnp.sum(lv[0:1] * lv[1:2], axis=-1, keepdims=True))
           - jnp.exp(jnp.sum(lv[2:3] * lv[3:4], axis=-1, keepdims=True)) + lambda_init)
    outs = []
    for h in range(HEADS):
        o = (_softmax_result(acc_sc[2 * h][...])
             - lam * _softmax_result(acc_sc[2 * h + 1][...]))
        ms = jnp.mean(o * o, axis=0, keepdims=True)
        outs.append(o * lax.rsqrt(ms + RMS_EPS) * subg_ref[...] * (1.0 - lambda_init))
    return jnp.concatenate(outs, axis=0).T


def _attn_c_call(qT, kp, vxT, lamv, subg_col, slopes, lambda_init):
    B, nb, _, t = qT.shape
    S = nb * t
    return pl.pallas_call(
        functools.partial(_attn_c_kernel, t=t, slopes=slopes, lambda_init=lambda_init),
        grid=(B,),
        in_specs=[_seq_cols_spec(nb, 2 * HW, t), _seq_rows_spec(S, HW), _seq_cols_spec(nb, HW, t),
                  _const_spec(lamv.shape), _const_spec(subg_col.shape)],
        out_specs=_seq_rows_spec(S, GROUP_W),
        out_shape=jax.ShapeDtypeStruct((B, S, GROUP_W), F32),
        scratch_shapes=_attn_scratch(2 * HEADS, t),
        compiler_params=_params("arbitrary"),
        name="attn_diff",
    )(qT, kp, vxT, lamv, subg_col)


def _fold_rows(x, group=SUBLANES):
    return jnp.sum(x.reshape(x.shape[0] // group, group, x.shape[1]), axis=0)


PACKED_ROWS = 2 * SUBLANES
DIGIT_BITS = 8
N_DIGITS = 32 // DIGIT_BITS
DIGIT_MASK = (1 << DIGIT_BITS) - 1


def _fold_packed(x):
    slabs = [x[i * PACKED_ROWS:(i + 1) * PACKED_ROWS] for i in range(x.shape[0] // PACKED_ROWS)]
    while len(slabs) > 1:
        slabs = [a + b for a, b in zip(slabs[0::2], slabs[1::2])]
    return slabs[0]


def _attn_d_kernel(q_ref, iq_all_ref, iw_all_ref, dk_ref, dvx_ref, ik_ref, before_ref, o_ref,
                   keys_sc, tau_sc, *scratch, t, topk, slopes):
    def tile(qi, q):
        return _attn_d_tile(qi, q, iq_all_ref.at[qi], iw_all_ref.at[qi], dk_ref, dvx_ref, ik_ref,
                            before_ref, keys_sc, tau_sc, scratch, t=t, topk=topk, slopes=slopes)

    _for_each_query_tile(q_ref, o_ref, t, tile)


def _attn_d_tile(qi, q_ref, iq_ref, iw_ref, dk_ref, dvx_ref, ik_ref, before_ref,
                 keys_sc, tau_sc, scratch, *, t, topk, slopes):
    digit_sc, scratch = scratch[:N_DIGITS], scratch[N_DIGITS:]
    m_sc, acc_sc = scratch[:HEADS], scratch[HEADS:]

    w = iw_ref[...] * IDX_HEADS ** -0.5

    def index_step(j, nblk, masked=False):
        ikb = ik_ref[_key_rows(j, nblk, t), :]
        zs = [_dot(ikb, iq_ref[LANES * hh:LANES * (hh + 1), :]) for hh in range(IDX_HEADS)]
        sc = w[0:1, :] * jnp.maximum(zs[0], 0.0)
        for hh in range(1, IDX_HEADS):
            sc = sc + w[hh:hh + 1, :] * jnp.maximum(zs[hh], 0.0)
        sc = jnp.where(sc == 0.0, 0.0, sc)
        if masked:
            sc = jnp.where(_causal_mask(t, nblk), sc, NEG_INF)
        bits = pltpu.bitcast(sc, I32)
        keys = jnp.where(bits < 0, bits ^ 0x7FFFFFFF, bits)
        ukeys = keys ^ INT_MIN
        digits = [(lax.shift_right_logical(ukeys, DIGIT_BITS * (N_DIGITS - 1 - d)) & DIGIT_MASK)
                  .astype(F32).astype(BF16) for d in range(N_DIGITS)]
        for b in range(nblk):
            rows = slice(b * t, (b + 1) * t)
            keys_sc[j + b] = keys[rows]
            for d in range(N_DIGITS):
                digit_sc[d][j + b] = digits[d][rows]

    _sweep_causal_blocks(qi, index_step)

    qpos = qi * t + lax.broadcasted_iota(I32, (1, t), 1)
    kt = jnp.minimum(topk, qpos + 1).astype(F32)

    def count_ge(cand):
        def body(j, acc):
            return acc + _fold_rows((keys_sc[j] >= cand).astype(F32))
        acc = lax.fori_loop(0, qi + 1, body, jnp.zeros((SUBLANES, t), F32))
        return jnp.sum(acc, axis=0, keepdims=True)

    one_b = jnp.ones((), BF16)
    zero_b = jnp.zeros((), BF16)

    def count_digit_ge(vals_sc, cand):
        cand_b = cand.astype(F32).astype(BF16)

        def block_count(j):
            return _fold_packed(jnp.where(vals_sc[j] >= cand_b, one_b, zero_b))

        def pair(i, acc):
            return acc + (block_count(2 * i) + block_count(2 * i + 1)).astype(F32)

        acc = lax.fori_loop(0, (qi + 1) // 2, pair, jnp.zeros((PACKED_ROWS, t), F32))
        acc = lax.cond(qi % 2 == 0, lambda a: a + block_count(qi).astype(F32), lambda a: a, acc)
        return jnp.sum(acc, axis=0, keepdims=True)

    def keep_matching(vals_sc, match_sc, match):
        match_b = match.astype(F32).astype(BF16)

        def body(j, carry):
            vals_sc[j] = jnp.where(match_sc[j] == match_b, vals_sc[j], -one_b)
            return carry
        lax.fori_loop(0, qi + 1, body, 0)

    zero = jnp.zeros((1, t), I32)
    rank = kt
    above = jnp.zeros((1, t), F32)
    tau_u = zero
    digit = zero
    for d in range(N_DIGITS):
        if d > 0:
            keep_matching(digit_sc[d], digit_sc[d - 1], digit)

        def bit_body(i, prefix, d=d, rank=rank):
            cand = prefix + lax.shift_left(jnp.int32(1), DIGIT_BITS - 1 - i)
            return jnp.where(count_digit_ge(digit_sc[d], cand) >= rank, cand, prefix)

        digit = lax.fori_loop(0, DIGIT_BITS, bit_body, zero)
        tau_u = lax.shift_left(tau_u, DIGIT_BITS) | digit
        if d < N_DIGITS - 1:
            higher = count_digit_ge(digit_sc[d], digit + 1)
            above = above + higher
            rank = rank - higher
    tau = tau_u ^ INT_MIN
    tau_sc[...] = tau
    excess = jnp.max(above + count_digit_ge(digit_sc[N_DIGITS - 1], digit) - kt)

    @pl.when(excess > 0.0)
    def _():
        need = kt - count_ge(tau + 1)

        def tie_body(j, seen):
            kj = keys_sc[j]
            eq = kj == tau
            eqb = eq.astype(BF16)
            earlier = _dot(before_ref[...], eqb) + seen
            keys_sc[j] = jnp.where(eq & (earlier >= need), INT_MIN, kj)
            return seen + jnp.sum(eqb.astype(F32), axis=0, keepdims=True)

        lax.fori_loop(0, qi + 1, tie_body, jnp.zeros((1, t), F32))

    _softmax_init(m_sc, acc_sc)

    def step(j, nblk):
        keys = [keys_sc[j + b] for b in range(nblk)]
        sel = (keys[0] if nblk == 1 else jnp.concatenate(keys, axis=0)) >= tau_sc[...]
        kb = dk_ref[_key_rows(j, nblk, t), :]
        vx = _value_cols(dvx_ref, j, nblk, slice(None))
        scores = [_dot(kb, q_ref[LANES * h:LANES * (h + 1), :]) for h in range(HEADS)]
        for h in range(HEADS):
            bias = _lane_repeat(_alibi(slopes[h], j, qi, t, nblk), t)
            scores[h] = jnp.where(sel, scores[h] + bias, -jnp.inf)
        _softmax_steps(scores, [vx] * HEADS, m_sc, acc_sc)

    _sweep_earlier_blocks(qi + 1, step)
    return jnp.concatenate([_softmax_result(acc_sc[h][...]) for h in range(HEADS)], axis=0).T


def _attn_d_call(qT, iqT, miscT, dkp, dvxT, ikp, before, topk, slopes):
    B, nb, _, t = qT.shape
    S = nb * t
    iw_block = MISC_IW // IDX_HEADS
    return pl.pallas_call(
        functools.partial(_attn_d_kernel, t=t, topk=topk, slopes=slopes),
        grid=(B,),
        in_specs=[_seq_cols_spec(nb, HW, t), _seq_cols_spec(nb, IDX_HEADS * LANES, t),
                  pl.BlockSpec((None, nb, IDX_HEADS, t), lambda b: (b, 0, iw_block, 0)),
                  _seq_rows_spec(S, LANES), _seq_cols_spec(nb, LANES, t), _seq_rows_spec(S, LANES),
                  _const_spec((t, t))],
        out_specs=_seq_rows_spec(S, GROUP_W),
        out_shape=jax.ShapeDtypeStruct((B, S, GROUP_W), F32),
        scratch_shapes=([pltpu.VMEM((nb, t, t), I32), pltpu.VMEM((1, t), I32)]
                        + [pltpu.VMEM((nb, t, t), BF16)] * N_DIGITS + _attn_scratch(HEADS, t)),
        compiler_params=_params("arbitrary"),
        name="attn_sparse",
    )(qT, iqT, miscT, dkp, dvxT, ikp, before)


def _out_proj_kernel(x_ref, oa_ref, ob_ref, oc_ref, od_ref, beta_ref, wo_ref, gate_ref,
                     g2_ref, scale_ref, shift_ref, wr_ref, br_ref, earlier_ref,
                     x1_ref, h2x_ref, route_ref, cnt_ref, cnt_sc):
    d_model = x_ref.shape[1]
    acc = None
    for i, o_ref in enumerate((oa_ref, ob_ref, oc_ref, od_ref)):
        sl = slice(GROUP_W * i, GROUP_W * (i + 1))
        mix = o_ref[...] * beta_ref[:, sl]
        part = jnp.dot(mix, wo_ref[sl, :], preferred_element_type=F32)
        acc = part if acc is None else acc + part
    x1 = x_ref[...] + gate_ref[...] * acc
    x1_ref[...] = x1
    ms = jnp.mean(x1 * x1, axis=-1, keepdims=True)
    h2 = x1 * lax.rsqrt(ms + RMS_EPS) * g2_ref[...]
    h2 = h2 * (1.0 + scale_ref[...]) + shift_ref[...]
    h2x_ref[:, :d_model] = h2

    logits = jnp.dot(h2, wr_ref[...], precision=lax.Precision.HIGHEST,
                     preferred_element_type=F32) + br_ref[...]
    lt = logits.T
    tm = lt.shape[1]
    g = lt[0:N_GROUPS]
    gmax = jnp.max(g, axis=0, keepdims=True)
    gi = lax.broadcasted_iota(I32, g.shape, 0)
    gidx = jnp.min(jnp.where(g == gmax, gi, N_GROUPS), axis=0, keepdims=True)
    g_prob = 1.0 / jnp.sum(jnp.exp(g - gmax), axis=0, keepdims=True)
    e_sel = jnp.zeros((EXPERTS_PER_GROUP, tm), F32)
    for gg in range(N_GROUPS):
        lo = N_GROUPS + EXPERTS_PER_GROUP * gg
        e_sel = e_sel + jnp.where(gidx == gg, lt[lo:lo + EXPERTS_PER_GROUP], 0.0)
    ei = lax.broadcasted_iota(I32, e_sel.shape, 0)
    v1 = jnp.max(e_sel, axis=0, keepdims=True)
    i1 = jnp.min(jnp.where(e_sel == v1, ei, EXPERTS_PER_GROUP), axis=0, keepdims=True)
    rest = jnp.where(ei == i1, -jnp.inf, e_sel)
    v2 = jnp.max(rest, axis=0, keepdims=True)
    i2 = jnp.min(jnp.where(rest == v2, ei, EXPERTS_PER_GROUP), axis=0, keepdims=True)
    e2 = jnp.exp(v2 - v1)
    w1 = g_prob / (1.0 + e2)
    w2 = g_prob * e2 / (1.0 + e2)
    in_group = jnp.where(ei == i1, w1, 0.0) + jnp.where(ei == i2, w2, 0.0)
    cw = jnp.concatenate([in_group, jnp.zeros((LANES - EXPERTS_PER_GROUP, tm), F32)], axis=0)
    h2x_ref[:, d_model:] = cw.T

    @pl.when(pl.program_id(0) == 0)
    def _():
        cnt_sc[...] = jnp.zeros(cnt_sc.shape, F32)

    rows = lax.broadcasted_iota(I32, (SUBLANES, tm), 0)
    onehot = (rows == gidx).astype(F32)
    seen = _dot(onehot.astype(BF16), earlier_ref[...]) + cnt_sc[...]
    rank = jnp.sum(onehot * seen, axis=0, keepdims=True)
    route_ref[...] = jnp.concatenate(
        [gidx, rank.astype(I32), jnp.zeros((SUBLANES - 2, tm), I32)], axis=0)
    cnt_sc[...] = cnt_sc[...] + jnp.sum(onehot, axis=1, keepdims=True)
    cnt_ref[...] = cnt_sc[:, :LANES]


def _out_proj_call(xf, outs, beta, w_out, gate1, g2, scale2, shift2, w_r, b_r, earlier, seq, tm):
    N, D = xf.shape
    per_b = seq // tm
    row = lambda i: (i, 0)
    full = lambda i: (0, 0)
    per_batch = pl.BlockSpec((None, 1, D), lambda i: (i // per_b, 0, 0))
    return pl.pallas_call(
        _out_proj_kernel,
        grid=(N // tm,),
        in_specs=([pl.BlockSpec((tm, D), row)] + [pl.BlockSpec((tm, GROUP_W), row)] * 4
                  + [pl.BlockSpec((1, D), full), pl.BlockSpec((D, D), full), per_batch,
                     pl.BlockSpec((1, D), full), per_batch, per_batch,
                     pl.BlockSpec((D, LANES), full), pl.BlockSpec((1, LANES), full),
                     pl.BlockSpec((tm, tm), full)]),
        out_specs=[pl.BlockSpec((tm, D), row), pl.BlockSpec((tm, D + LANES), row),
                   pl.BlockSpec((SUBLANES, tm), lambda i: (0, i)),
                   pl.BlockSpec((SUBLANES, LANES), full)],
        out_shape=[jax.ShapeDtypeStruct((N, D), F32), jax.ShapeDtypeStruct((N, D + LANES), F32),
                   jax.ShapeDtypeStruct((SUBLANES, N), I32),
                   jax.ShapeDtypeStruct((SUBLANES, LANES), F32)],
        scratch_shapes=[pltpu.VMEM((SUBLANES, tm), F32)],
        compiler_params=_params("arbitrary"),
        name="out_proj_router",
    )(xf, *outs, beta, w_out, gate1, g2, scale2, shift2, w_r, b_r, earlier)


MOE_TILE = 256

def _dispatch_kernel(slot_ref, h2x_ref, zeros_ref, xs_ref, sem):
    del zeros_ref

    def row_copy(g, u):
        return pltpu.make_async_copy(h2x_ref.at[g, pl.ds(u, 1), :],
                                     xs_ref.at[pl.ds(slot_ref[0, g * SUBLANES + u], 1), :], sem)

    def issue(g, carry):
        for u in range(SUBLANES):
            row_copy(g, u).start()
        return carry

    lax.fori_loop(0, h2x_ref.shape[0], issue, 0)

    def drain(g, carry):
        for u in range(SUBLANES):
            row_copy(g, u).wait()
        return carry

    lax.fori_loop(0, h2x_ref.shape[0], drain, 0)


def _dispatch_call(slot, h2x, xs_zero, tm):
    N, DX = h2x.shape
    return pl.pallas_call(
        _dispatch_kernel,
        grid=(N // tm,),
        in_specs=[pl.BlockSpec((None, 1, tm), lambda i: (i, 0, 0), memory_space=pltpu.SMEM),
                  pl.BlockSpec((tm // SUBLANES, SUBLANES, DX), lambda i: (i, 0, 0)),
                  pl.BlockSpec(memory_space=pl.ANY)],
        out_specs=pl.BlockSpec(memory_space=pl.ANY),
        out_shape=jax.ShapeDtypeStruct(xs_zero.shape, F32),
        scratch_shapes=[pltpu.SemaphoreType.DMA(())],
        input_output_aliases={2: 0},
        compiler_params=_params("arbitrary"),
        name="moe_dispatch",
    )(slot.reshape(N // tm, 1, tm), h2x.reshape(N // SUBLANES, SUBLANES, DX), xs_zero)


def _expert_kernel(group_ref, valid_ref, xs_ref, w1_ref, w3_ref, w2_ref, y_ref):
    del group_ref
    valid = valid_ref[pl.program_id(0)]
    d_model = y_ref.shape[1]

    @pl.when(valid == 0)
    def _():
        y_ref[...] = jnp.zeros(y_ref.shape, F32)

    @pl.when(valid > 0)
    def _():
        x = xs_ref[:, :d_model]
        cw = xs_ref[:, d_model:]
        acc = None
        for e in range(EXPERTS_PER_GROUP):
            a = _dot(x, w1_ref[e])
            b = _dot(x, w3_ref[e])
            hid = a * jax.nn.sigmoid(a) * b * jnp.broadcast_to(cw[:, e:e + 1], a.shape)
            part = _dot(hid, w2_ref[e])
            acc = part if acc is None else acc + part
        y_ref[...] = acc


def _expert_call(tile_group, tile_valid, xs, w1, w3, w2, layer):
    P, DX = xs.shape
    D, FF = w1.shape[-2:]
    tr = MOE_TILE

    def group_spec(rows, cols):
        return pl.BlockSpec((None, None, EXPERTS_PER_GROUP, rows, cols),
                            lambda i, g, v: (layer, g[i], 0, 0, 0),
                            pipeline_mode=pl.Buffered(1))

    grid_spec = pltpu.PrefetchScalarGridSpec(
        num_scalar_prefetch=2,
        grid=(P // tr,),
        in_specs=[pl.BlockSpec((tr, DX), lambda i, g, v: (i, 0)),
                  group_spec(D, FF), group_spec(D, FF), group_spec(FF, D)],
        out_specs=pl.BlockSpec((tr, D), lambda i, g, v: (i, 0)),
    )
    return pl.pallas_call(
        _expert_kernel,
        grid_spec=grid_spec,
        out_shape=jax.ShapeDtypeStruct((P, D), F32),
        compiler_params=_params("arbitrary"),
        name="moe_experts",
    )(tile_group, tile_valid, xs, w1, w3, w2)


def _residual_kernel(slot_ref, slot_next_ref, x1_ref, gate_ref, ys_ref, o_ref, buf, sems):
    i = pl.program_id(0)

    def gather(slots, b, wait):
        def body(g, carry):
            for u in range(SUBLANES):
                cp = pltpu.make_async_copy(ys_ref.at[pl.ds(slots[0, g * SUBLANES + u], 1), :],
                                           buf.at[b, g, pl.ds(u, 1), :], sems.at[b])
                cp.wait() if wait else cp.start()
            return carry
        lax.fori_loop(0, buf.shape[1], body, 0)

    @pl.when(i == 0)
    def _():
        gather(slot_ref, 0, False)

    @pl.when(i + 1 < pl.num_programs(0))
    def _():
        gather(slot_next_ref, (i + 1) % 2, False)

    gather(slot_ref, i % 2, True)
    o_ref[...] = x1_ref[...] + gate_ref[...] * buf[i % 2].reshape(x1_ref.shape)


def _residual_call(slot, x1, gate2, ys, seq, tm):
    N, D = x1.shape
    n = N // tm
    per_b = seq // tm
    row = pl.BlockSpec((tm, D), lambda i: (i, 0))
    slot3 = slot.reshape(n, 1, tm)
    return pl.pallas_call(
        _residual_kernel,
        grid=(n,),
        in_specs=[pl.BlockSpec((None, 1, tm), lambda i: (i, 0, 0), memory_space=pltpu.SMEM),
                  pl.BlockSpec((None, 1, tm), lambda i: (jnp.minimum(i + 1, n - 1), 0, 0),
                               memory_space=pltpu.SMEM),
                  row, pl.BlockSpec((None, 1, D), lambda i: (i // per_b, 0, 0)),
                  pl.BlockSpec(memory_space=pl.ANY)],
        out_specs=row,
        out_shape=jax.ShapeDtypeStruct((N, D), F32),
        scratch_shapes=[pltpu.VMEM((2, tm // SUBLANES, SUBLANES, D), F32),
                        pltpu.SemaphoreType.DMA((2,))],
        compiler_params=_params("arbitrary"),
        name="moe_residual",
    )(slot3, slot3, x1, gate2, ys)


def _moe_routing(route, cnt, n_tokens):
    tr = MOE_TILE
    n_tiles = n_tokens // tr + N_GROUPS
    counts = cnt[:N_GROUPS, 0].astype(I32)
    padded = (counts + tr - 1) // tr * tr
    ends = jnp.cumsum(padded)
    starts = ends - padded
    group, rank = route[0], route[1]
    slot = starts[group] + rank
    tile_start = jnp.arange(n_tiles, dtype=I32) * tr
    tile_group = jnp.minimum(jnp.sum((tile_start[:, None] >= ends[None, :]).astype(I32), axis=1),
                             N_GROUPS - 1)
    tile_valid = jnp.clip(starts[tile_group] + counts[tile_group] - tile_start, 0, tr)
    tile_valid = jnp.where(tile_start < ends[-1], tile_valid, 0)
    return slot, tile_group, tile_valid


def _block_diag_mean(width, group, valid_in_128=None):
    i = jnp.arange(width)
    same = (i[:, None] // group) == (i[None, :] // group)
    if valid_in_128 is not None:
        same = same & ((i[:, None] % LANES) < valid_in_128) & ((i[None, :] % LANES) < valid_in_128)
    return jnp.where(same, 1.0 / group, 0.0).astype(F32)


def kernel(x, c, ada_w, ada_b, norm1_g, norm2_g, w_in, b_f, qn_a, kn_a, qn_c, kn_c,
           lam_q1, lam_k1, lam_q2, lam_k2, subln_g, qn_d, kn_d, mix_beta, w_out,
           w_group, b_group, w_expert, b_expert, w1, w3, w2):
    B, S, D = x.shape
    L = ada_w.shape[0]
    N = B * S
    topk = min(TOPK_MAX, S // 4)
    t = ATTN_BLOCK
    tm = t
    slopes = [2.0 ** (-8.0 * i / (2 * HEADS)) for i in range(1, 2 * HEADS + 1)]
    slopes_c, slopes_d = tuple(slopes[0::2]), tuple(slopes[1::2])

    idx_t = jnp.arange(t)
    after = (idx_t[None, :] > idx_t[:, None]).astype(BF16)
    before = (idx_t[None, :] < idx_t[:, None]).astype(BF16)
    upto = (idx_t[None, :] <= idx_t[:, None]).astype(BF16)
    g64 = _block_diag_mean(GROUP_W, HEAD_DIM)
    g32 = _block_diag_mean(GROUP_W, DIFF_DIM)
    lane = jnp.arange(LANES)
    cum_sel = jnp.stack([jnp.broadcast_to((lane == MISC_AF + h)[:, None], (LANES, LANES))
                         for h in range(HEADS)]).astype(BF16)

    mod = _ada_call(c, ada_w, ada_b)
    xf = x.reshape(N, D)

    for l in range(L):
        m6 = mod[l].reshape(B, 6, 1, D)
        shift1, scale1, gate1, shift2, scale2, gate2 = (m6[:, i] for i in range(6))

        ones = jnp.ones((GROUP_W - HEAD_DIM,), F32)
        gains = jnp.stack([jnp.tile(qn_a[l], HEADS) * (HEAD_DIM ** -0.5 * LOG2E),
                           jnp.tile(kn_a[l], HEADS),
                           jnp.tile(qn_c[l], 2 * HEADS) * (DIFF_DIM ** -0.5 * LOG2E),
                           jnp.tile(kn_c[l], 2 * HEADS),
                           jnp.tile(qn_d[l], HEADS) * (HEAD_DIM ** -0.5 * LOG2E),
                           jnp.concatenate([kn_d[l], ones])]).astype(F32)
        gains = jnp.concatenate([gains, jnp.zeros((2, GROUP_W), F32)], axis=0)

        outs = _in_proj_call(xf, scale1, shift1, norm1_g[l].reshape(1, D), w_in, l, gains,
                             g64, g32, B, S, tm)
        sec = {name: o for (name, _, _, _, _), o in zip(OUTPUTS, outs)}
        for name, src, width, orient, ones_ in OUTPUTS:
            if orient == "rows":
                sec[name] = sec[name].reshape(B, S, width)
        misc = outs[len(OUTPUTS)].reshape(B, S, LANES)
        miscT = outs[len(OUTPUTS) + 1]

        bf_row = jnp.zeros((1, LANES), F32).at[0, MISC_AF:MISC_AF + HEADS].set(b_f[l].astype(F32))
        cumrep = _cum_call(misc, bf_row, upto, cum_sel)

        o_a = _attn_a_call(sec["a_qT"], sec["a_kp"], sec["a_vxT"], cumrep)
        o_b = _attn_b_call(sec["b_qT"], sec["b_kp"], sec["b_vT"], after)
        lambda_init = 0.8 - 0.6 * math.exp(-0.3 * l)
        lamv = jnp.stack([lam_q1[l], lam_k1[l], lam_q2[l], lam_k2[l]]).astype(F32)
        o_c = _attn_c_call(sec["c_qT"], sec["c_kp"], sec["c_vxT"], lamv,
                           subln_g[l].reshape(HEAD_DIM, 1).astype(F32), slopes_c, lambda_init)
        o_d = _attn_d_call(sec["d_qT"], sec["i_qT"], miscT, sec["d_kp"], sec["d_vxT"], sec["i_kp"],
                           before, topk, slopes_d)

        w_r = jnp.concatenate([w_group[l], w_expert[l],
                               jnp.zeros((D, LANES - N_GROUPS - N_EXPERTS), F32)], axis=1)
        b_r = jnp.concatenate([b_group[l], b_expert[l],
                               jnp.zeros((LANES - N_GROUPS - N_EXPERTS,), F32)]).reshape(1, LANES)
        x1, h2x, route, cnt = _out_proj_call(
            xf, [o.reshape(N, GROUP_W) for o in (o_a, o_b, o_c, o_d)], mix_beta[l].reshape(1, D),
            w_out[l], gate1, norm2_g[l].reshape(1, D), scale2, shift2, w_r, b_r,
            after, S, tm)

        slot, tile_group, tile_valid = _moe_routing(route, cnt, N)
        xs = _dispatch_call(slot, h2x, jnp.zeros((tile_group.shape[0] * MOE_TILE, D + LANES), F32), tm)
        ys = _expert_call(tile_group, tile_valid, xs, w1, w3, w2, l)
        xf = _residual_call(slot, x1, gate2, ys, S, tm)

    return xf.reshape(B, S, D)
```

```python
import functools
import math

import jax
import jax.numpy as jnp
from jax import lax
from jax.experimental import pallas as pl
from jax.experimental.pallas import tpu as pltpu

F32 = jnp.float32
BF16 = jnp.bfloat16
I32 = jnp.int32

HEAD_DIM = 64
HEADS = 4
GROUP_W = HEADS * HEAD_DIM
DIFF_DIM = HEAD_DIM // 2
IDX_HEADS = 8
IDX_DIM = 32
TOPK_MAX = 256
N_GROUPS = 4
EXPERTS_PER_GROUP = 8
N_EXPERTS = N_GROUPS * EXPERTS_PER_GROUP
EXPERT_FF = 256
RMS_EPS = 1e-6
NEG_INF = -1e30
INT_MIN = -(2 ** 31)
LOG2E = math.log2(math.e)

LANES = 128
SUBLANES = 8
ATTN_BLOCK = 256
VMEM_LIMIT = 56 * 1024 * 1024

MISC_IK = 0
MISC_IW = IDX_DIM
MISC_AF = IDX_DIM + IDX_HEADS


def _params(*sem):
    return pltpu.CompilerParams(dimension_semantics=sem, vmem_limit_bytes=VMEM_LIMIT)


def _log_sigmoid(z):
    return jnp.minimum(z, 0.0) - jnp.log1p(jnp.exp(-jnp.abs(z)))


def _log2_sigmoid(z2):
    return jnp.minimum(z2, 0.0) - jnp.log2(1.0 + jnp.exp2(-jnp.abs(z2)))


def _split_bf16(x, parts):
    out = []
    rem = x
    for _ in range(parts):
        p = rem.astype(BF16)
        out.append(p)
        rem = rem - p.astype(F32)
    return out


def _dot(a, b):
    return jnp.dot(a, b, preferred_element_type=F32)


def _ada_kernel(c_ref, w_ref, b_ref, o_ref):
    c = c_ref[...]
    ca = c * jax.nn.sigmoid(c)
    o_ref[...] = jnp.dot(ca, w_ref[...], precision=lax.Precision.HIGHEST,
                         preferred_element_type=F32) + b_ref[...]


def _ada_call(c, ada_w, ada_b):
    L, D, E = ada_w.shape
    B = c.shape[0]
    tn = 1536
    return pl.pallas_call(
        _ada_kernel,
        grid=(L, E // tn),
        in_specs=[pl.BlockSpec((B, D), lambda l, j: (0, 0)),
                  pl.BlockSpec((None, D, tn), lambda l, j: (l, 0, j)),
                  pl.BlockSpec((None, 1, tn), lambda l, j: (l, 0, j))],
        out_specs=pl.BlockSpec((None, B, tn), lambda l, j: (l, 0, j)),
        out_shape=jax.ShapeDtypeStruct((L, B, E), F32),
        compiler_params=_params("arbitrary", "arbitrary"),
        name="ada_mod",
    )(c, ada_w, ada_b.reshape(L, 1, E))


SOURCES = ("a_q", "a_k", "a_v", "b_q", "b_k", "b_v", "c_q", "c_k", "c_v", "d_q", "i_q", "d_kv")
SRC_NORM = {"a_q": ("n64", 0), "a_k": ("n64", 1), "c_q": ("n32", 2), "c_k": ("n32", 3),
            "d_q": ("n64", 4), "d_kv": ("n64", 5)}
SRC_SCALE = {"b_q": HEAD_DIM ** -0.5 * LOG2E, "i_q": IDX_DIM ** -0.5}
HW = HEADS * LANES
OUTPUTS = (
    ("a_qT", "a_q", HW, "cols", False), ("a_kp", "a_k", HW, "rows", False),
    ("a_vxT", "a_v", HW, "cols", True),
    ("b_qT", "b_q", HW, "cols", False), ("b_kp", "b_k", HW, "rows", False),
    ("b_vT", "b_v", GROUP_W, "cols", False),
    ("c_qT", "c_q", 2 * HW, "cols", False), ("c_kp", "c_k", HW, "rows", False),
    ("c_vxT", "c_v", HW, "cols", True),
    ("d_qT", "d_q", HW, "cols", False), ("i_qT", "i_q", IDX_HEADS * LANES, "cols", False),
    ("d_kp", "d_kv", LANES, "rows", False), ("d_vxT", "d_kv", LANES, "cols", True),
    ("i_kp", "misc", LANES, "rows", False),
)
N_MAIN = len(SOURCES) * GROUP_W


def _layout_plans():
    per_head = [p for h in range(HEADS) for p in ((HEAD_DIM * h, HEAD_DIM), (None, HEAD_DIM))]
    diff_q = []
    for h in range(HEADS):
        for c in range(2):
            lead = DIFF_DIM * c
            if lead:
                diff_q.append((None, lead))
            diff_q.append((HEAD_DIM * h + DIFF_DIM * c, DIFF_DIM))
            diff_q.append((None, LANES - lead - DIFF_DIM))
    idx_q = [p for hh in range(IDX_HEADS) for p in ((IDX_DIM * hh, IDX_DIM), (None, LANES - IDX_DIM))]
    return {"a_qT": per_head, "a_vxT": per_head, "b_qT": per_head, "b_vT": [(0, GROUP_W)],
            "c_qT": diff_q, "c_vxT": per_head, "d_qT": per_head, "i_qT": idx_q,
            "d_vxT": [(HEAD_DIM, HEAD_DIM), (None, HEAD_DIM)],
            "a_kp": per_head, "b_kp": per_head, "c_kp": per_head,
            "d_kp": [(0, HEAD_DIM), (None, HEAD_DIM)],
            "i_kp": [(MISC_IK, IDX_DIM), (None, LANES - IDX_DIM)]}


LAYOUT_PLANS = _layout_plans()

IN_SPLITS = (GROUP_W, GROUP_W, GROUP_W, HEADS, GROUP_W, GROUP_W, GROUP_W, GROUP_W, GROUP_W,
             GROUP_W, GROUP_W, HEAD_DIM, HEAD_DIM, IDX_HEADS * IDX_DIM, IDX_DIM, IDX_HEADS)
IN_NAMES = ("a_q", "a_k", "a_v", "a_f", "b_q", "b_k", "b_v", "c_q", "c_k", "c_v",
            "d_q", "d_k", "d_v", "i_q", "i_k", "i_w")
IN_OFFSETS = {n: (sum(IN_SPLITS[:i]), IN_SPLITS[i]) for i, n in enumerate(IN_NAMES)}
P_IN = sum(IN_SPLITS)
SECTION_PARTS = {name: (name,) for name in SOURCES if name != "d_kv"}
SECTION_PARTS["d_kv"] = ("d_k", "d_v")
SECTION_PARTS["misc"] = ("i_k", "i_w", "a_f")
REALIGN_ROWS = 256


def _realign_weights(w_ref, w_sc):
    d_model = w_ref.shape[0]
    for r0 in range(0, d_model, REALIGN_ROWS):
        rows = slice(r0, r0 + REALIGN_ROWS)
        for i, name in enumerate(SOURCES + ("misc",)):
            width = LANES if name == "misc" else GROUP_W
            pieces, used = [], 0
            for part in SECTION_PARTS[name]:
                off, w = IN_OFFSETS[part]
                base = off // LANES * LANES
                end = min(-(-(off + w) // LANES) * LANES, P_IN)
                window = w_ref[rows, base:end]
                pieces.append(window[:, off - base:off - base + w])
                used += w
            if used < width:
                pieces.append(jnp.zeros((REALIGN_ROWS, width - used), w_sc.dtype))
            block = pieces[0] if len(pieces) == 1 else jnp.concatenate(pieces, axis=1)
            w_sc[rows, i * GROUP_W:i * GROUP_W + width] = block


def _in_proj_kernel(*refs):
    (x_ref, scale_ref, shift_ref, g1_ref, w_in_ref, gains_ref, g64_ref, g32_ref) = refs[:8]
    out_refs = refs[8:-1]
    w_ref = refs[-1]

    @pl.when(pl.program_id(0) == 0)
    def _():
        _realign_weights(w_in_ref, w_ref)

    x = x_ref[...]
    ms = jnp.mean(x * x, axis=-1, keepdims=True)
    h = x * lax.rsqrt(ms + RMS_EPS) * g1_ref[...]
    h = h * (1.0 + scale_ref[...]) + shift_ref[...]

    wm = w_ref[:, N_MAIN:N_MAIN + LANES]
    misc = functools.reduce(lambda a, b: a + b, [_dot(term, wm) for term in _split_bf16(h, 3)])
    out_refs[len(OUTPUTS)][...] = misc
    out_refs[len(OUTPUTS) + 1][...] = misc.T

    final = {"misc": misc}
    raw = {name: _dot(h, w_ref[:, i * GROUP_W:(i + 1) * GROUP_W])
           for i, name in enumerate(SOURCES)}
    msqs = {name: _dot(raw[name] * raw[name],
                       (g64_ref if SRC_NORM[name][0] == "n64" else g32_ref)[...])
            for name in SOURCES if name in SRC_NORM}
    for name in SOURCES:
        sec = raw[name]
        if name in SRC_NORM:
            r = SRC_NORM[name][1]
            fac = lax.rsqrt(msqs[name] + RMS_EPS)
            if name == "d_kv":
                lane = lax.broadcasted_iota(I32, sec.shape, 1)
                fac = jnp.where(lane < HEAD_DIM, fac, 1.0)
            sec = sec * fac * gains_ref[r:r + 1, :]
        elif name in SRC_SCALE:
            sec = sec * SRC_SCALE[name]
        final[name] = sec

    tokens = x.shape[0]
    transposed = {}
    for (name, src, width, orient, ones), o_ref in zip(OUTPUTS, out_refs):
        if orient == "cols" and src not in transposed:
            transposed[src] = final[src].T
        pieces = []
        for start, size in LAYOUT_PLANS[name]:
            if orient == "rows":
                pieces.append(jnp.zeros((tokens, size), F32) if start is None
                              else final[src][:, start:start + size])
            else:
                pieces.append(jnp.full((size, tokens), 1.0 if ones else 0.0, F32) if start is None
                              else transposed[src][start:start + size])
        o_ref[...] = jnp.concatenate(pieces, axis=1 if orient == "rows" else 0)


def _in_proj_call(xf, scale1, shift1, g1, w_in, layer, gains, g64, g32, batch, seq, tm):
    N, D = xf.shape
    per_b = seq // tm
    row = lambda i: (i, 0)
    full = lambda i: (0, 0)
    col4 = lambda i: (i // per_b, i % per_b, 0, 0)
    out_shape, out_specs = [], []
    for name, src, width, orient, ones in OUTPUTS:
        if orient == "rows":
            out_shape.append(jax.ShapeDtypeStruct((N, width), F32))
            out_specs.append(pl.BlockSpec((tm, width), row))
        else:
            out_shape.append(jax.ShapeDtypeStruct((batch, per_b, width, tm), F32))
            out_specs.append(pl.BlockSpec((None, None, width, tm), col4))
    out_shape += [jax.ShapeDtypeStruct((N, LANES), F32),
                  jax.ShapeDtypeStruct((batch, per_b, LANES, tm), F32)]
    out_specs += [pl.BlockSpec((tm, LANES), row), pl.BlockSpec((None, None, LANES, tm), col4)]
    return pl.pallas_call(
        _in_proj_kernel,
        grid=(N // tm,),
        in_specs=([pl.BlockSpec((tm, D), row),
                   pl.BlockSpec((None, 1, D), lambda i: (i // per_b, 0, 0)),
                   pl.BlockSpec((None, 1, D), lambda i: (i // per_b, 0, 0)),
                   pl.BlockSpec((1, D), full),
                   pl.BlockSpec((None,) + w_in.shape[1:], lambda i: (layer, 0, 0),
                                pipeline_mode=pl.Buffered(1)),
                   pl.BlockSpec(gains.shape, full),
                   pl.BlockSpec(g64.shape, full),
                   pl.BlockSpec(g32.shape, full)]),
        out_specs=out_specs,
        out_shape=out_shape,
        scratch_shapes=[pltpu.VMEM((D, N_MAIN + LANES), F32)],
        compiler_params=_params("arbitrary"),
        name="in_proj",
    )(xf, scale1, shift1, g1, w_in, gains, g64, g32)


def _cum_kernel(misc_ref, bf_ref, tri_ref, sel_ref, o_ref, *, blk):
    seq = misc_ref.shape[0]
    carry = jnp.zeros((1, LANES), F32)
    tri = tri_ref[...]
    for j in range(seq // blk):
        rows = slice(j * blk, (j + 1) * blk)
        lf = _log_sigmoid(misc_ref[rows, :] + bf_ref[...])
        c = carry
        for part in _split_bf16(lf, 3):
            c = c + _dot(tri, part)
        carry = c[blk - 1:blk, :]
        parts = _split_bf16(c, 3)
        for h in range(HEADS):
            rep = _dot(parts[0], sel_ref[h]) + _dot(parts[1], sel_ref[h]) + _dot(parts[2], sel_ref[h])
            o_ref[h, rows, :] = rep * LOG2E


def _cum_call(misc, bf_row, tri, sel):
    B, S, _ = misc.shape
    blk = tri.shape[0]
    return pl.pallas_call(
        functools.partial(_cum_kernel, blk=blk),
        grid=(B,),
        in_specs=[pl.BlockSpec((None, S, LANES), lambda b: (b, 0, 0)),
                  pl.BlockSpec((1, LANES), lambda b: (0, 0)),
                  pl.BlockSpec((blk, blk), lambda b: (0, 0)),
                  pl.BlockSpec((HEADS, LANES, LANES), lambda b: (0, 0, 0))],
        out_specs=pl.BlockSpec((None, HEADS, S, LANES), lambda b: (b, 0, 0, 0)),
        out_shape=jax.ShapeDtypeStruct((B, HEADS, S, LANES), F32),
        compiler_params=_params("arbitrary"),
        name="forget_cumsum",
    )(misc, bf_row, tri, sel)


def _seq_rows_spec(seq, width):
    return pl.BlockSpec((None, seq, width), lambda b: (b, 0, 0))


def _seq_cols_spec(nb, width, t):
    return pl.BlockSpec((None, nb, width, t), lambda b: (b, 0, 0, 0))


def _const_spec(shape):
    return pl.BlockSpec(shape, lambda b: (0,) * len(shape))


def _for_each_query_tile(q_ref, o_ref, t, tile_fn):
    def body(qi, carry):
        o_ref[pl.ds(pl.multiple_of(qi * t, t), t), :] = tile_fn(qi, q_ref.at[qi])
        return carry

    lax.fori_loop(0, q_ref.shape[0], body, 0)


def _lane_repeat(x, t):
    return jnp.concatenate([x] * (t // LANES), axis=1)


def _softmax_steps(scores, values, m_refs, acc_refs):
    probs, alphas = [], []
    for s, m_ref in zip(scores, m_refs):
        m_old = m_ref[...]
        m_new = jnp.maximum(m_old, jnp.max(s, axis=0, keepdims=True))
        alphas.append(jnp.exp2(m_old - m_new))
        probs.append(jnp.exp2(s - m_new))
        m_ref[...] = m_new
    for p, vx, alpha, acc_ref in zip(probs, values, alphas, acc_refs):
        acc_ref[...] = alpha * acc_ref[...] + _dot(vx, p)


def _softmax_init(m_scs, acc_scs):
    for m_sc, acc_sc in zip(m_scs, acc_scs):
        m_sc[...] = jnp.full(m_sc.shape, NEG_INF, F32)
        acc_sc[...] = jnp.zeros(acc_sc.shape, F32)


def _softmax_result(acc):
    return acc[:HEAD_DIM] / acc[HEAD_DIM:]


def _alibi(slope, j, qi, t, nblk=1):
    key = lax.broadcasted_iota(I32, (nblk * t, LANES), 0)
    return (slope * LOG2E) * (key + (j - qi) * t).astype(F32)


def _sweep_earlier_blocks(qi, step):
    def body(i, carry):
        step(2 * i, 2)
        return carry

    lax.fori_loop(0, qi // 2, body, 0)

    @pl.when(qi % 2 == 1)
    def _():
        step(qi - 1, 1)


def _sweep_causal_blocks(qi, step):
    @pl.when(qi == 0)
    def _():
        step(0, 1, True)

    @pl.when(qi > 0)
    def _():
        step(qi - 1, 2, True)

    _sweep_earlier_blocks(jnp.maximum(qi - 1, 0), step)


def _causal_mask(t, nblk, strict=False):
    key = lax.broadcasted_iota(I32, (nblk * t, t), 0) - (nblk - 1) * t
    query = lax.broadcasted_iota(I32, (nblk * t, t), 1)
    return key < query if strict else key <= query


def _key_rows(j, nblk, t):
    return pl.ds(pl.multiple_of(j * t, t), nblk * t)


def _value_cols(vx_ref, j, nblk, rows):
    tiles = [vx_ref[j + b, rows, :] for b in range(nblk)]
    return tiles[0] if nblk == 1 else jnp.concatenate(tiles, axis=1)


def _attn_a_kernel(q_ref, k_ref, vx_ref, cum_ref, o_ref, *scratch, t):
    m_sc, acc_sc = scratch[:HEADS], scratch[HEADS:]

    def tile(qi, q):
        _softmax_init(m_sc, acc_sc)

        def step(j, nblk, masked=False):
            rows = _key_rows(j, nblk, t)
            heads = [slice(LANES * h, LANES * (h + 1)) for h in range(HEADS)]
            scores = [_dot(k_ref[rows, hs], q[hs, :]) for hs in heads]
            for h in range(HEADS):
                s = scores[h] - _lane_repeat(cum_ref[h, rows, :], t)
                scores[h] = jnp.where(_causal_mask(t, nblk), s, -jnp.inf) if masked else s
            _softmax_steps(scores, [_value_cols(vx_ref, j, nblk, hs) for hs in heads],
                           m_sc, acc_sc)

        _sweep_causal_blocks(qi, step)
        return jnp.concatenate([_softmax_result(acc_sc[h][...]) for h in range(HEADS)], axis=0).T

    _for_each_query_tile(q_ref, o_ref, t, tile)


def _attn_scratch(n, t, rows=LANES):
    return [pltpu.VMEM((1, t), F32)] * n + [pltpu.VMEM((rows, t), F32)] * n


def _attn_a_call(qT, kp, vxT, cumrep):
    B, nb, _, t = qT.shape
    S = nb * t
    return pl.pallas_call(
        functools.partial(_attn_a_kernel, t=t),
        grid=(B,),
        in_specs=[_seq_cols_spec(nb, HW, t), _seq_rows_spec(S, HW), _seq_cols_spec(nb, HW, t),
                  pl.BlockSpec((None, HEADS, S, LANES), lambda b: (b, 0, 0, 0))],
        out_specs=_seq_rows_spec(S, GROUP_W),
        out_shape=jax.ShapeDtypeStruct((B, S, GROUP_W), F32),
        scratch_shapes=_attn_scratch(HEADS, t),
        compiler_params=_params("arbitrary"),
        name="attn_forget",
    )(qT, kp, vxT, cumrep)


STICK_SPLIT_TERMS = 3


def _attn_b_kernel(q_ref, k_ref, v_ref, after_ref, o_ref, *scratch, t):
    tile = functools.partial(_attn_b_tile, k_ref=k_ref, v_ref=v_ref, after_ref=after_ref,
                             r_sc=scratch[:HEADS], acc_sc=scratch[HEADS:], t=t)
    _for_each_query_tile(q_ref, o_ref, t, tile)


def _attn_b_tile(qi, q_ref, *, k_ref, v_ref, after_ref, r_sc, acc_sc, t):
    for h in range(HEADS):
        r_sc[h][...] = jnp.zeros(r_sc[h].shape, F32)
        acc_sc[h][...] = jnp.zeros(acc_sc[h].shape, F32)

    def step(j, nblk, masked=False):
        after = after_ref[...]
        rows = _key_rows(j, nblk, t)
        heads = [slice(LANES * h, LANES * (h + 1)) for h in range(HEADS)]
        zs = [_dot(k_ref[rows, hs], q_ref[hs, :]) for hs in heads]
        lbs, splits, later_sums = [], [], []
        for h in range(HEADS):
            lb = _log2_sigmoid(zs[h])
            lm = lb - zs[h]
            if masked:
                lm = jnp.where(_causal_mask(t, nblk, strict=True), lm, 0.0)
            blocks = [lm[b * t:(b + 1) * t] for b in range(nblk)]
            splits.append([_split_bf16(blk, STICK_SPLIT_TERMS) for blk in blocks])
            sums = [jnp.sum(blk, axis=0, keepdims=True) for blk in blocks]
            r_old = r_sc[h][...]
            total = sums[0]
            for s_ in sums[1:]:
                total = total + s_
            r_sc[h][...] = r_old + total
            lbs.append(lb + r_old)
            later, run = [], None
            for b in reversed(range(nblk)):
                later.append(run)
                run = sums[b] if run is None else run + sums[b]
            later_sums.append(later[::-1])
        suffixes = [[functools.reduce(lambda a, b: a + b, [_dot(after, term) for term in terms])
                     for terms in splits[h]] for h in range(HEADS)]
        ws = []
        for h in range(HEADS):
            parts = [suffixes[h][b] if later_sums[h][b] is None else suffixes[h][b] + later_sums[h][b]
                     for b in range(nblk)]
            suffix = parts[0] if nblk == 1 else jnp.concatenate(parts, axis=0)
            w = jnp.exp2(lbs[h] + suffix)
            if masked:
                w = jnp.where(_causal_mask(t, nblk, strict=True), w, 0.0)
            ws.append(w)
        for h in range(HEADS):
            acc_sc[h][...] += _dot(_value_cols(v_ref, j, nblk, slice(HEAD_DIM * h, HEAD_DIM * (h + 1))),
                                   ws[h])

    @pl.when(qi == 0)
    def _():
        step(0, 1, True)

    @pl.when(qi > 0)
    def _():
        step(qi - 1, 2, True)

    rest = jnp.maximum(qi - 1, 0)

    @pl.when(rest % 2 == 1)
    def _():
        step(rest - 1, 1)

    pairs = rest // 2

    def body(i, carry):
        step(2 * (pairs - 1 - i), 2)
        return carry

    lax.fori_loop(0, pairs, body, 0)
    return jnp.concatenate([acc_sc[h][...] for h in range(HEADS)], axis=0).T


def _attn_b_call(qT, kp, vT, after):
    B, nb, _, t = qT.shape
    S = nb * t
    return pl.pallas_call(
        functools.partial(_attn_b_kernel, t=t),
        grid=(B,),
        in_specs=[_seq_cols_spec(nb, HW, t), _seq_rows_spec(S, HW), _seq_cols_spec(nb, GROUP_W, t),
                  _const_spec((t, t))],
        out_specs=_seq_rows_spec(S, GROUP_W),
        out_shape=jax.ShapeDtypeStruct((B, S, GROUP_W), F32),
        scratch_shapes=_attn_scratch(HEADS, t, HEAD_DIM),
        compiler_params=_params("arbitrary"),
        name="attn_stick",
    )(qT, kp, vT, after)


def _attn_c_kernel(q_ref, k_ref, vx_ref, lamv_ref, subg_ref, o_ref, *scratch,
                   t, slopes, lambda_init):
    tile = functools.partial(_attn_c_tile, k_ref=k_ref, vx_ref=vx_ref, lamv_ref=lamv_ref,
                             subg_ref=subg_ref, m_sc=scratch[:2 * HEADS], acc_sc=scratch[2 * HEADS:],
                             t=t, slopes=slopes, lambda_init=lambda_init)
    _for_each_query_tile(q_ref, o_ref, t, tile)


def _attn_c_tile(qi, q_ref, *, k_ref, vx_ref, lamv_ref, subg_ref, m_sc, acc_sc,
                 t, slopes, lambda_init):
    _softmax_init(m_sc, acc_sc)

    def step(j, nblk, masked=False):
        rows = _key_rows(j, nblk, t)
        heads = [slice(LANES * h, LANES * (h + 1)) for h in range(HEADS)]
        scores = [_dot(k_ref[rows, heads[g // 2]], q_ref[LANES * g:LANES * (g + 1), :])
                  for g in range(2 * HEADS)]
        for g in range(2 * HEADS):
            s = scores[g] + _lane_repeat(_alibi(slopes[g // 2], j, qi, t, nblk), t)
            scores[g] = jnp.where(_causal_mask(t, nblk), s, -jnp.inf) if masked else s
        _softmax_steps(scores, [_value_cols(vx_ref, j, nblk, heads[g // 2])
                                for g in range(2 * HEADS)], m_sc, acc_sc)

    _sweep_causal_blocks(qi, step)

    lv = lamv_ref[...]
    lam = (jnp.exp(jnp.sum(lv[0:1] * lv[1:2], axis=-1, keepdims=True))
           - jnp.exp(jnp.sum(lv[2:3] * lv[3:4], axis=-1, keepdims=True)) + lambda_init)
    outs = []
    for h in range(HEADS):
        o = (_softmax_result(acc_sc[2 * h][...])
             - lam * _softmax_result(acc_sc[2 * h + 1][...]))
        ms = jnp.mean(o * o, axis=0, keepdims=True)
        outs.append(o * lax.rsqrt(ms + RMS_EPS) * subg_ref[...] * (1.0 - lambda_init))
    return jnp.concatenate(outs, axis=0).T


def _attn_c_call(qT, kp, vxT, lamv, subg_col, slopes, lambda_init):
    B, nb, _, t = qT.shape
    S = nb * t
    return pl.pallas_call(
        functools.partial(_attn_c_kernel, t=t, slopes=slopes, lambda_init=lambda_init),
        grid=(B,),
        in_specs=[_seq_cols_spec(nb, 2 * HW, t), _seq_rows_spec(S, HW), _seq_cols_spec(nb, HW, t),
                  _const_spec(lamv.shape), _const_spec(subg_col.shape)],
        out_specs=_seq_rows_spec(S, GROUP_W),
        out_shape=jax.ShapeDtypeStruct((B, S, GROUP_W), F32),
        scratch_shapes=_attn_scratch(2 * HEADS, t),
        compiler_params=_params("arbitrary"),
        name="attn_diff",
    )(qT, kp, vxT, lamv, subg_col)


def _fold_rows(x, group=SUBLANES):
    return jnp.sum(x.reshape(x.shape[0] // group, group, x.shape[1]), axis=0)


PACKED_ROWS = 2 * SUBLANES
DIGIT_BITS = 8
N_DIGITS = 32 // DIGIT_BITS
DIGIT_MASK = (1 << DIGIT_BITS) - 1


def _fold_packed(x):
    slabs = [x[i * PACKED_ROWS:(i + 1) * PACKED_ROWS] for i in range(x.shape[0] // PACKED_ROWS)]
    while len(slabs) > 1:
        slabs = [a + b for a, b in zip(slabs[0::2], slabs[1::2])]
    return slabs[0]


def _attn_d_kernel(q_ref, iq_all_ref, iw_all_ref, dk_ref, dvx_ref, ik_ref, before_ref, o_ref,
                   keys_sc, tau_sc, *scratch, t, topk, slopes):
    def tile(qi, q):
        return _attn_d_tile(qi, q, iq_all_ref.at[qi], iw_all_ref.at[qi], dk_ref, dvx_ref, ik_ref,
                            before_ref, keys_sc, tau_sc, scratch, t=t, topk=topk, slopes=slopes)

    _for_each_query_tile(q_ref, o_ref, t, tile)


def _attn_d_tile(qi, q_ref, iq_ref, iw_ref, dk_ref, dvx_ref, ik_ref, before_ref,
                 keys_sc, tau_sc, scratch, *, t, topk, slopes):
    digit_sc, scratch = scratch[:N_DIGITS], scratch[N_DIGITS:]
    m_sc, acc_sc = scratch[:HEADS], scratch[HEADS:]

    w = iw_ref[...] * IDX_HEADS ** -0.5

    def index_step(j, nblk, masked=False):
        ikb = ik_ref[_key_rows(j, nblk, t), :]
        zs = [_dot(ikb, iq_ref[LANES * hh:LANES * (hh + 1), :]) for hh in range(IDX_HEADS)]
        sc = w[0:1, :] * jnp.maximum(zs[0], 0.0)
        for hh in range(1, IDX_HEADS):
            sc = sc + w[hh:hh + 1, :] * jnp.maximum(zs[hh], 0.0)
        sc = jnp.where(sc == 0.0, 0.0, sc)
        if masked:
            sc = jnp.where(_causal_mask(t, nblk), sc, NEG_INF)
        bits = pltpu.bitcast(sc, I32)
        keys = jnp.where(bits < 0, bits ^ 0x7FFFFFFF, bits)
        ukeys = keys ^ INT_MIN
        digits = [(lax.shift_right_logical(ukeys, DIGIT_BITS * (N_DIGITS - 1 - d)) & DIGIT_MASK)
                  .astype(F32).astype(BF16) for d in range(N_DIGITS)]
        for b in range(nblk):
            rows = slice(b * t, (b + 1) * t)
            keys_sc[j + b] = keys[rows]
            for d in range(N_DIGITS):
                digit_sc[d][j + b] = digits[d][rows]

    _sweep_causal_blocks(qi, index_step)

    qpos = qi * t + lax.broadcasted_iota(I32, (1, t), 1)
    kt = jnp.minimum(topk, qpos + 1).astype(F32)

    def count_ge(cand):
        def body(j, acc):
            return acc + _fold_rows((keys_sc[j] >= cand).astype(F32))
        acc = lax.fori_loop(0, qi + 1, body, jnp.zeros((SUBLANES, t), F32))
        return jnp.sum(acc, axis=0, keepdims=True)

    one_b = jnp.ones((), BF16)
    zero_b = jnp.zeros((), BF16)

    def count_digit_ge(vals_sc, cand):
        cand_b = cand.astype(F32).astype(BF16)

        def block_count(j):
            return _fold_packed(jnp.where(vals_sc[j] >= cand_b, one_b, zero_b))

        def pair(i, acc):
            return acc + (block_count(2 * i) + block_count(2 * i + 1)).astype(F32)

        acc = lax.fori_loop(0, (qi + 1) // 2, pair, jnp.zeros((PACKED_ROWS, t), F32))
        acc = lax.cond(qi % 2 == 0, lambda a: a + block_count(qi).astype(F32), lambda a: a, acc)
        return jnp.sum(acc, axis=0, keepdims=True)

    def keep_matching(vals_sc, match_sc, match):
        match_b = match.astype(F32).astype(BF16)

        def body(j, carry):
            vals_sc[j] = jnp.where(match_sc[j] == match_b, vals_sc[j], -one_b)
            return carry
        lax.fori_loop(0, qi + 1, body, 0)

    zero = jnp.zeros((1, t), I32)
    rank = kt
    above = jnp.zeros((1, t), F32)
    tau_u = zero
    digit = zero
    for d in range(N_DIGITS):
        if d > 0:
            keep_matching(digit_sc[d], digit_sc[d - 1], digit)

        def bit_body(i, prefix, d=d, rank=rank):
            cand = prefix + lax.shift_left(jnp.int32(1), DIGIT_BITS - 1 - i)
            return jnp.where(count_digit_ge(digit_sc[d], cand) >= rank, cand, prefix)

        digit = lax.fori_loop(0, DIGIT_BITS, bit_body, zero)
        tau_u = lax.shift_left(tau_u, DIGIT_BITS) | digit
        if d < N_DIGITS - 1:
            higher = count_digit_ge(digit_sc[d], digit + 1)
            above = above + higher
            rank = rank - higher
    tau = tau_u ^ INT_MIN
    tau_sc[...] = tau
    excess = jnp.max(above + count_digit_ge(digit_sc[N_DIGITS - 1], digit) - kt)

    @pl.when(excess > 0.0)
    def _():
        need = kt - count_ge(tau + 1)

        def tie_body(j, seen):
            kj = keys_sc[j]
            eq = kj == tau
            eqb = eq.astype(BF16)
            earlier = _dot(before_ref[...], eqb) + seen
            keys_sc[j] = jnp.where(eq & (earlier >= need), INT_MIN, kj)
            return seen + jnp.sum(eqb.astype(F32), axis=0, keepdims=True)

        lax.fori_loop(0, qi + 1, tie_body, jnp.zeros((1, t), F32))

    _softmax_init(m_sc, acc_sc)

    def step(j, nblk):
        keys = [keys_sc[j + b] for b in range(nblk)]
        sel = (keys[0] if nblk == 1 else jnp.concatenate(keys, axis=0)) >= tau_sc[...]
        kb = dk_ref[_key_rows(j, nblk, t), :]
        vx = _value_cols(dvx_ref, j, nblk, slice(None))
        scores = [_dot(kb, q_ref[LANES * h:LANES * (h + 1), :]) for h in range(HEADS)]
        for h in range(HEADS):
            bias = _lane_repeat(_alibi(slopes[h], j, qi, t, nblk), t)
            scores[h] = jnp.where(sel, scores[h] + bias, -jnp.inf)
        _softmax_steps(scores, [vx] * HEADS, m_sc, acc_sc)

    _sweep_earlier_blocks(qi + 1, step)
    return jnp.concatenate([_softmax_result(acc_sc[h][...]) for h in range(HEADS)], axis=0).T


def _attn_d_call(qT, iqT, miscT, dkp, dvxT, ikp, before, topk, slopes):
    B, nb, _, t = qT.shape
    S = nb * t
    iw_block = MISC_IW // IDX_HEADS
    return pl.pallas_call(
        functools.partial(_attn_d_kernel, t=t, topk=topk, slopes=slopes),
        grid=(B,),
        in_specs=[_seq_cols_spec(nb, HW, t), _seq_cols_spec(nb, IDX_HEADS * LANES, t),
                  pl.BlockSpec((None, nb, IDX_HEADS, t), lambda b: (b, 0, iw_block, 0)),
                  _seq_rows_spec(S, LANES), _seq_cols_spec(nb, LANES, t), _seq_rows_spec(S, LANES),
                  _const_spec((t, t))],
        out_specs=_seq_rows_spec(S, GROUP_W),
        out_shape=jax.ShapeDtypeStruct((B, S, GROUP_W), F32),
        scratch_shapes=([pltpu.VMEM((nb, t, t), I32), pltpu.VMEM((1, t), I32)]
                        + [pltpu.VMEM((nb, t, t), BF16)] * N_DIGITS + _attn_scratch(HEADS, t)),
        compiler_params=_params("arbitrary"),
        name="attn_sparse",
    )(qT, iqT, miscT, dkp, dvxT, ikp, before)


ROUTER_TERMS = 3
ROUTER_STRIDE = 40


def _out_proj_kernel(x_ref, oa_ref, ob_ref, oc_ref, od_ref, beta_ref, wo_ref, gate_ref,
                     g2_ref, scale_ref, shift_ref, wr_ref, br_ref, earlier_ref,
                     x1_ref, h2x_ref, route_ref, cnt_ref, cnt_sc):
    d_model = x_ref.shape[1]
    acc = None
    for i, o_ref in enumerate((oa_ref, ob_ref, oc_ref, od_ref)):
        sl = slice(GROUP_W * i, GROUP_W * (i + 1))
        mix = o_ref[...] * beta_ref[:, sl]
        part = jnp.dot(mix, wo_ref[sl, :], preferred_element_type=F32)
        acc = part if acc is None else acc + part
    x1 = x_ref[...] + gate_ref[...] * acc
    x1_ref[...] = x1
    ms = jnp.mean(x1 * x1, axis=-1, keepdims=True)
    h2 = x1 * lax.rsqrt(ms + RMS_EPS) * g2_ref[...]
    h2 = h2 * (1.0 + scale_ref[...]) + shift_ref[...]
    h2x_ref[:, :d_model] = h2

    packed = functools.reduce(lambda a, b: a + b,
                              [_dot(term, wr_ref[...]) for term in _split_bf16(h2, ROUTER_TERMS)])
    logits = packed
    for k in range(1, ROUTER_TERMS):
        logits = logits + pltpu.roll(packed, LANES - k * ROUTER_STRIDE, 1)
    logits = logits + br_ref[...]
    lt = logits.T
    tm = lt.shape[1]
    g = lt[0:N_GROUPS]
    gmax = jnp.max(g, axis=0, keepdims=True)
    gi = lax.broadcasted_iota(I32, g.shape, 0)
    gidx = jnp.min(jnp.where(g == gmax, gi, N_GROUPS), axis=0, keepdims=True)
    g_prob = 1.0 / jnp.sum(jnp.exp(g - gmax), axis=0, keepdims=True)
    e_sel = jnp.zeros((EXPERTS_PER_GROUP, tm), F32)
    for gg in range(N_GROUPS):
        lo = N_GROUPS + EXPERTS_PER_GROUP * gg
        e_sel = e_sel + jnp.where(gidx == gg, lt[lo:lo + EXPERTS_PER_GROUP], 0.0)
    ei = lax.broadcasted_iota(I32, e_sel.shape, 0)
    v1 = jnp.max(e_sel, axis=0, keepdims=True)
    i1 = jnp.min(jnp.where(e_sel == v1, ei, EXPERTS_PER_GROUP), axis=0, keepdims=True)
    rest = jnp.where(ei == i1, -jnp.inf, e_sel)
    v2 = jnp.max(rest, axis=0, keepdims=True)
    i2 = jnp.min(jnp.where(rest == v2, ei, EXPERTS_PER_GROUP), axis=0, keepdims=True)
    e2 = jnp.exp(v2 - v1)
    w1 = g_prob / (1.0 + e2)
    w2 = g_prob * e2 / (1.0 + e2)
    in_group = jnp.where(ei == i1, w1, 0.0) + jnp.where(ei == i2, w2, 0.0)
    cw = jnp.concatenate([in_group, jnp.zeros((LANES - EXPERTS_PER_GROUP, tm), F32)], axis=0)
    h2x_ref[:, d_model:] = cw.T

    @pl.when(pl.program_id(0) == 0)
    def _():
        cnt_sc[...] = jnp.zeros(cnt_sc.shape, F32)

    rows = lax.broadcasted_iota(I32, (SUBLANES, tm), 0)
    onehot = (rows == gidx).astype(F32)
    seen = _dot(onehot.astype(BF16), earlier_ref[...]) + cnt_sc[...]
    rank = jnp.sum(onehot * seen, axis=0, keepdims=True)
    route_ref[...] = jnp.concatenate(
        [gidx, rank.astype(I32), jnp.zeros((SUBLANES - 2, tm), I32)], axis=0)
    cnt_sc[...] = cnt_sc[...] + jnp.sum(onehot, axis=1, keepdims=True)
    cnt_ref[...] = cnt_sc[:, :LANES]


def _out_proj_call(xf, outs, beta, w_out, gate1, g2, scale2, shift2, w_r, b_r, earlier, seq, tm):
    N, D = xf.shape
    per_b = seq // tm
    row = lambda i: (i, 0)
    full = lambda i: (0, 0)
    per_batch = pl.BlockSpec((None, 1, D), lambda i: (i // per_b, 0, 0))
    return pl.pallas_call(
        _out_proj_kernel,
        grid=(N // tm,),
        in_specs=([pl.BlockSpec((tm, D), row)] + [pl.BlockSpec((tm, GROUP_W), row)] * 4
                  + [pl.BlockSpec((1, D), full), pl.BlockSpec((D, D), full), per_batch,
                     pl.BlockSpec((1, D), full), per_batch, per_batch,
                     pl.BlockSpec((D, LANES), full), pl.BlockSpec((1, LANES), full),
                     pl.BlockSpec((tm, tm), full)]),
        out_specs=[pl.BlockSpec((tm, D), row), pl.BlockSpec((tm, D + LANES), row),
                   pl.BlockSpec((SUBLANES, tm), lambda i: (0, i)),
                   pl.BlockSpec((SUBLANES, LANES), full)],
        out_shape=[jax.ShapeDtypeStruct((N, D), F32), jax.ShapeDtypeStruct((N, D + LANES), F32),
                   jax.ShapeDtypeStruct((SUBLANES, N), I32),
                   jax.ShapeDtypeStruct((SUBLANES, LANES), F32)],
        scratch_shapes=[pltpu.VMEM((SUBLANES, tm), F32)],
        compiler_params=_params("arbitrary"),
        name="out_proj_router",
    )(xf, *outs, beta, w_out, gate1, g2, scale2, shift2, w_r, b_r, earlier)


MOE_TILE = 256

def _dispatch_kernel(slot_ref, h2x_ref, zeros_ref, xs_ref, sem):
    del zeros_ref

    def row_copy(g, u):
        return pltpu.make_async_copy(h2x_ref.at[g, pl.ds(u, 1), :],
                                     xs_ref.at[pl.ds(slot_ref[0, g * SUBLANES + u], 1), :], sem)

    def issue(g, carry):
        for u in range(SUBLANES):
            row_copy(g, u).start()
        return carry

    lax.fori_loop(0, h2x_ref.shape[0], issue, 0)

    def drain(g, carry):
        for u in range(SUBLANES):
            row_copy(g, u).wait()
        return carry

    lax.fori_loop(0, h2x_ref.shape[0], drain, 0)


def _dispatch_call(slot, h2x, xs_zero, tm):
    N, DX = h2x.shape
    return pl.pallas_call(
        _dispatch_kernel,
        grid=(N // tm,),
        in_specs=[pl.BlockSpec((None, 1, tm), lambda i: (i, 0, 0), memory_space=pltpu.SMEM),
                  pl.BlockSpec((tm // SUBLANES, SUBLANES, DX), lambda i: (i, 0, 0)),
                  pl.BlockSpec(memory_space=pl.ANY)],
        out_specs=pl.BlockSpec(memory_space=pl.ANY),
        out_shape=jax.ShapeDtypeStruct(xs_zero.shape, F32),
        scratch_shapes=[pltpu.SemaphoreType.DMA(())],
        input_output_aliases={2: 0},
        compiler_params=_params("arbitrary"),
        name="moe_dispatch",
    )(slot.reshape(N // tm, 1, tm), h2x.reshape(N // SUBLANES, SUBLANES, DX), xs_zero)


def _expert_kernel(group_ref, valid_ref, xs_ref, w1_ref, w3_ref, w2_ref, y_ref):
    del group_ref
    valid = valid_ref[pl.program_id(0)]
    d_model = y_ref.shape[1]

    @pl.when(valid == 0)
    def _():
        y_ref[...] = jnp.zeros(y_ref.shape, F32)

    @pl.when(valid > 0)
    def _():
        x = xs_ref[:, :d_model]
        cw = xs_ref[:, d_model:]
        acc = None
        for e in range(EXPERTS_PER_GROUP):
            a = _dot(x, w1_ref[e])
            b = _dot(x, w3_ref[e])
            hid = a * jax.nn.sigmoid(a) * b * jnp.broadcast_to(cw[:, e:e + 1], a.shape)
            part = _dot(hid, w2_ref[e])
            acc = part if acc is None else acc + part
        y_ref[...] = acc


def _expert_call(tile_group, tile_valid, xs, w1, w3, w2, layer):
    P, DX = xs.shape
    D, FF = w1.shape[-2:]
    tr = MOE_TILE

    def group_spec(rows, cols):
        return pl.BlockSpec((None, None, EXPERTS_PER_GROUP, rows, cols),
                            lambda i, g, v: (layer, g[i], 0, 0, 0),
                            pipeline_mode=pl.Buffered(1))

    grid_spec = pltpu.PrefetchScalarGridSpec(
        num_scalar_prefetch=2,
        grid=(P // tr,),
        in_specs=[pl.BlockSpec((tr, DX), lambda i, g, v: (i, 0)),
                  group_spec(D, FF), group_spec(D, FF), group_spec(FF, D)],
        out_specs=pl.BlockSpec((tr, D), lambda i, g, v: (i, 0)),
    )
    return pl.pallas_call(
        _expert_kernel,
        grid_spec=grid_spec,
        out_shape=jax.ShapeDtypeStruct((P, D), F32),
        compiler_params=_params("arbitrary"),
        name="moe_experts",
    )(tile_group, tile_valid, xs, w1, w3, w2)


def _residual_kernel(slot_ref, slot_next_ref, x1_ref, gate_ref, ys_ref, o_ref, buf, sems):
    i = pl.program_id(0)

    def gather(slots, b, wait):
        def body(g, carry):
            for u in range(SUBLANES):
                cp = pltpu.make_async_copy(ys_ref.at[pl.ds(slots[0, g * SUBLANES + u], 1), :],
                                           buf.at[b, g, pl.ds(u, 1), :], sems.at[b])
                cp.wait() if wait else cp.start()
            return carry
        lax.fori_loop(0, buf.shape[1], body, 0)

    @pl.when(i == 0)
    def _():
        gather(slot_ref, 0, False)

    @pl.when(i + 1 < pl.num_programs(0))
    def _():
        gather(slot_next_ref, (i + 1) % 2, False)

    gather(slot_ref, i % 2, True)
    o_ref[...] = x1_ref[...] + gate_ref[...] * buf[i % 2].reshape(x1_ref.shape)


def _residual_call(slot, x1, gate2, ys, seq, tm):
    N, D = x1.shape
    n = N // tm
    per_b = seq // tm
    row = pl.BlockSpec((tm, D), lambda i: (i, 0))
    slot3 = slot.reshape(n, 1, tm)
    return pl.pallas_call(
        _residual_kernel,
        grid=(n,),
        in_specs=[pl.BlockSpec((None, 1, tm), lambda i: (i, 0, 0), memory_space=pltpu.SMEM),
                  pl.BlockSpec((None, 1, tm), lambda i: (jnp.minimum(i + 1, n - 1), 0, 0),
                               memory_space=pltpu.SMEM),
                  row, pl.BlockSpec((None, 1, D), lambda i: (i // per_b, 0, 0)),
                  pl.BlockSpec(memory_space=pl.ANY)],
        out_specs=row,
        out_shape=jax.ShapeDtypeStruct((N, D), F32),
        scratch_shapes=[pltpu.VMEM((2, tm // SUBLANES, SUBLANES, D), F32),
                        pltpu.SemaphoreType.DMA((2,))],
        compiler_params=_params("arbitrary"),
        name="moe_residual",
    )(slot3, slot3, x1, gate2, ys)


def _moe_routing(route, cnt, n_tokens):
    tr = MOE_TILE
    n_tiles = n_tokens // tr + N_GROUPS
    counts = cnt[:N_GROUPS, 0].astype(I32)
    padded = (counts + tr - 1) // tr * tr
    ends = jnp.cumsum(padded)
    starts = ends - padded
    group, rank = route[0], route[1]
    slot = starts[group] + rank
    tile_start = jnp.arange(n_tiles, dtype=I32) * tr
    tile_group = jnp.minimum(jnp.sum((tile_start[:, None] >= ends[None, :]).astype(I32), axis=1),
                             N_GROUPS - 1)
    tile_valid = jnp.clip(starts[tile_group] + counts[tile_group] - tile_start, 0, tr)
    tile_valid = jnp.where(tile_start < ends[-1], tile_valid, 0)
    return slot, tile_group, tile_valid


def _block_diag_mean(width, group, valid_in_128=None):
    i = jnp.arange(width)
    same = (i[:, None] // group) == (i[None, :] // group)
    if valid_in_128 is not None:
        same = same & ((i[:, None] % LANES) < valid_in_128) & ((i[None, :] % LANES) < valid_in_128)
    return jnp.where(same, 1.0 / group, 0.0).astype(F32)


def kernel(x, c, ada_w, ada_b, norm1_g, norm2_g, w_in, b_f, qn_a, kn_a, qn_c, kn_c,
           lam_q1, lam_k1, lam_q2, lam_k2, subln_g, qn_d, kn_d, mix_beta, w_out,
           w_group, b_group, w_expert, b_expert, w1, w3, w2):
    B, S, D = x.shape
    L = ada_w.shape[0]
    N = B * S
    topk = min(TOPK_MAX, S // 4)
    t = ATTN_BLOCK
    tm = t
    slopes = [2.0 ** (-8.0 * i / (2 * HEADS)) for i in range(1, 2 * HEADS + 1)]
    slopes_c, slopes_d = tuple(slopes[0::2]), tuple(slopes[1::2])

    idx_t = jnp.arange(t)
    after = (idx_t[None, :] > idx_t[:, None]).astype(BF16)
    before = (idx_t[None, :] < idx_t[:, None]).astype(BF16)
    upto = (idx_t[None, :] <= idx_t[:, None]).astype(BF16)
    g64 = _block_diag_mean(GROUP_W, HEAD_DIM)
    g32 = _block_diag_mean(GROUP_W, DIFF_DIM)
    lane = jnp.arange(LANES)
    cum_sel = jnp.stack([jnp.broadcast_to((lane == MISC_AF + h)[:, None], (LANES, LANES))
                         for h in range(HEADS)]).astype(BF16)

    mod = _ada_call(c, ada_w, ada_b)
    xf = x.reshape(N, D)

    for l in range(L):
        m6 = mod[l].reshape(B, 6, 1, D)
        shift1, scale1, gate1, shift2, scale2, gate2 = (m6[:, i] for i in range(6))

        ones = jnp.ones((GROUP_W - HEAD_DIM,), F32)
        gains = jnp.stack([jnp.tile(qn_a[l], HEADS) * (HEAD_DIM ** -0.5 * LOG2E),
                           jnp.tile(kn_a[l], HEADS),
                           jnp.tile(qn_c[l], 2 * HEADS) * (DIFF_DIM ** -0.5 * LOG2E),
                           jnp.tile(kn_c[l], 2 * HEADS),
                           jnp.tile(qn_d[l], HEADS) * (HEAD_DIM ** -0.5 * LOG2E),
                           jnp.concatenate([kn_d[l], ones])]).astype(F32)
        gains = jnp.concatenate([gains, jnp.zeros((2, GROUP_W), F32)], axis=0)

        outs = _in_proj_call(xf, scale1, shift1, norm1_g[l].reshape(1, D), w_in, l, gains,
                             g64, g32, B, S, tm)
        sec = {name: o for (name, _, _, _, _), o in zip(OUTPUTS, outs)}
        for name, src, width, orient, ones_ in OUTPUTS:
            if orient == "rows":
                sec[name] = sec[name].reshape(B, S, width)
        misc = outs[len(OUTPUTS)].reshape(B, S, LANES)
        miscT = outs[len(OUTPUTS) + 1]

        bf_row = jnp.zeros((1, LANES), F32).at[0, MISC_AF:MISC_AF + HEADS].set(b_f[l].astype(F32))
        cumrep = _cum_call(misc, bf_row, upto, cum_sel)

        o_a = _attn_a_call(sec["a_qT"], sec["a_kp"], sec["a_vxT"], cumrep)
        o_b = _attn_b_call(sec["b_qT"], sec["b_kp"], sec["b_vT"], after)
        lambda_init = 0.8 - 0.6 * math.exp(-0.3 * l)
        lamv = jnp.stack([lam_q1[l], lam_k1[l], lam_q2[l], lam_k2[l]]).astype(F32)
        o_c = _attn_c_call(sec["c_qT"], sec["c_kp"], sec["c_vxT"], lamv,
                           subln_g[l].reshape(HEAD_DIM, 1).astype(F32), slopes_c, lambda_init)
        o_d = _attn_d_call(sec["d_qT"], sec["i_qT"], miscT, sec["d_kp"], sec["d_vxT"], sec["i_kp"],
                           before, topk, slopes_d)

        w_router = jnp.concatenate([w_group[l], w_expert[l]], axis=1)
        terms = _split_bf16(w_router, ROUTER_TERMS)
        gap = jnp.zeros((D, ROUTER_STRIDE - N_GROUPS - N_EXPERTS), BF16)
        w_r = jnp.concatenate([piece for term in terms for piece in (term, gap)], axis=1)
        w_r = jnp.concatenate([w_r, jnp.zeros((D, LANES - ROUTER_TERMS * ROUTER_STRIDE), BF16)],
                              axis=1)
        b_r = jnp.concatenate([b_group[l], b_expert[l],
                               jnp.zeros((LANES - N_GROUPS - N_EXPERTS,), F32)]).reshape(1, LANES)
        x1, h2x, route, cnt = _out_proj_call(
            xf, [o.reshape(N, GROUP_W) for o in (o_a, o_b, o_c, o_d)], mix_beta[l].reshape(1, D),
            w_out[l], gate1, norm2_g[l].reshape(1, D), scale2, shift2, w_r, b_r,
            after, S, tm)

        slot, tile_group, tile_valid = _moe_routing(route, cnt, N)
        xs = _dispatch_call(slot, h2x, jnp.zeros((tile_group.shape[0] * MOE_TILE, D + LANES), F32), tm)
        ys = _expert_call(tile_group, tile_valid, xs, w1, w3, w2, l)
        xf = _residual_call(slot, x1, gate2, ys, S, tm)

    return xf.reshape(B, S, D)
```

```python
import functools
import math

import jax
import jax.numpy as jnp
from jax import lax
from jax.experimental import pallas as pl
from jax.experimental.pallas import tpu as pltpu

F32 = jnp.float32
BF16 = jnp.bfloat16
I32 = jnp.int32

HEAD_DIM = 64
HEADS = 4
GROUP_W = HEADS * HEAD_DIM
DIFF_DIM = HEAD_DIM // 2
IDX_HEADS = 8
IDX_DIM = 32
TOPK_MAX = 256
N_GROUPS = 4
EXPERTS_PER_GROUP = 8
N_EXPERTS = N_GROUPS * EXPERTS_PER_GROUP
EXPERT_FF = 256
RMS_EPS = 1e-6
NEG_INF = -1e30
INT_MIN = -(2 ** 31)
LOG2E = math.log2(math.e)

LANES = 128
SUBLANES = 8
ATTN_BLOCK = 256
VMEM_LIMIT = 56 * 1024 * 1024

MISC_IK = 0
MISC_IW = IDX_DIM
MISC_AF = IDX_DIM + IDX_HEADS


def _params(*sem):
    return pltpu.CompilerParams(dimension_semantics=sem, vmem_limit_bytes=VMEM_LIMIT)


def _log_sigmoid(z):
    return jnp.minimum(z, 0.0) - jnp.log1p(jnp.exp(-jnp.abs(z)))


def _log2_sigmoid(z2):
    return jnp.minimum(z2, 0.0) - jnp.log2(1.0 + jnp.exp2(-jnp.abs(z2)))


def _split_bf16(x, parts):
    out = []
    rem = x
    for _ in range(parts):
        p = rem.astype(BF16)
        out.append(p)
        rem = rem - p.astype(F32)
    return out


def _dot(a, b):
    return jnp.dot(a, b, preferred_element_type=F32)


def _ada_kernel(c_ref, w_ref, b_ref, o_ref):
    c = c_ref[...]
    ca = c * jax.nn.sigmoid(c)
    o_ref[...] = jnp.dot(ca, w_ref[...], precision=lax.Precision.HIGHEST,
                         preferred_element_type=F32) + b_ref[...]


def _ada_call(c, ada_w, ada_b):
    L, D, E = ada_w.shape
    B = c.shape[0]
    tn = 1536
    return pl.pallas_call(
        _ada_kernel,
        grid=(L, E // tn),
        in_specs=[pl.BlockSpec((B, D), lambda l, j: (0, 0)),
                  pl.BlockSpec((None, D, tn), lambda l, j: (l, 0, j)),
                  pl.BlockSpec((None, 1, tn), lambda l, j: (l, 0, j))],
        out_specs=pl.BlockSpec((None, B, tn), lambda l, j: (l, 0, j)),
        out_shape=jax.ShapeDtypeStruct((L, B, E), F32),
        compiler_params=_params("arbitrary", "arbitrary"),
        name="ada_mod",
    )(c, ada_w, ada_b.reshape(L, 1, E))


SOURCES = ("a_q", "a_k", "a_v", "b_q", "b_k", "b_v", "c_q", "c_k", "c_v", "d_q", "i_q", "d_kv")
SRC_NORM = {"a_q": ("n64", 0), "a_k": ("n64", 1), "c_q": ("n32", 2), "c_k": ("n32", 3),
            "d_q": ("n64", 4), "d_kv": ("n64", 5)}
SRC_SCALE = {"b_q": HEAD_DIM ** -0.5 * LOG2E, "i_q": IDX_DIM ** -0.5}
HW = HEADS * LANES
OUTPUTS = (
    ("a_qT", "a_q", HW, "cols", False), ("a_kp", "a_k", HW, "rows", False),
    ("a_vxT", "a_v", HW, "cols", True),
    ("b_qT", "b_q", HW, "cols", False), ("b_kp", "b_k", HW, "rows", False),
    ("b_vT", "b_v", GROUP_W, "cols", False),
    ("c_qT", "c_q", 2 * HW, "cols", False), ("c_kp", "c_k", HW, "rows", False),
    ("c_vxT", "c_v", HW, "cols", True),
    ("d_qT", "d_q", HW, "cols", False), ("i_qT", "i_q", IDX_HEADS * LANES, "cols", False),
    ("d_kp", "d_kv", LANES, "rows", False), ("d_vxT", "d_kv", LANES, "cols", True),
    ("i_kp", "misc", LANES, "rows", False),
)
N_MAIN = len(SOURCES) * GROUP_W


def _layout_plans():
    per_head = [p for h in range(HEADS) for p in ((HEAD_DIM * h, HEAD_DIM), (None, HEAD_DIM))]
    diff_q = []
    for h in range(HEADS):
        for c in range(2):
            lead = DIFF_DIM * c
            if lead:
                diff_q.append((None, lead))
            diff_q.append((HEAD_DIM * h + DIFF_DIM * c, DIFF_DIM))
            diff_q.append((None, LANES - lead - DIFF_DIM))
    idx_q = [p for hh in range(IDX_HEADS) for p in ((IDX_DIM * hh, IDX_DIM), (None, LANES - IDX_DIM))]
    return {"a_qT": per_head, "a_vxT": per_head, "b_qT": per_head, "b_vT": [(0, GROUP_W)],
            "c_qT": diff_q, "c_vxT": per_head, "d_qT": per_head, "i_qT": idx_q,
            "d_vxT": [(HEAD_DIM, HEAD_DIM), (None, HEAD_DIM)],
            "a_kp": per_head, "b_kp": per_head, "c_kp": per_head,
            "d_kp": [(0, HEAD_DIM), (None, HEAD_DIM)],
            "i_kp": [(MISC_IK, IDX_DIM), (None, LANES - IDX_DIM)]}


LAYOUT_PLANS = _layout_plans()

IN_SPLITS = (GROUP_W, GROUP_W, GROUP_W, HEADS, GROUP_W, GROUP_W, GROUP_W, GROUP_W, GROUP_W,
             GROUP_W, GROUP_W, HEAD_DIM, HEAD_DIM, IDX_HEADS * IDX_DIM, IDX_DIM, IDX_HEADS)
IN_NAMES = ("a_q", "a_k", "a_v", "a_f", "b_q", "b_k", "b_v", "c_q", "c_k", "c_v",
            "d_q", "d_k", "d_v", "i_q", "i_k", "i_w")
IN_OFFSETS = {n: (sum(IN_SPLITS[:i]), IN_SPLITS[i]) for i, n in enumerate(IN_NAMES)}
P_IN = sum(IN_SPLITS)
SECTION_PARTS = {name: (name,) for name in SOURCES if name != "d_kv"}
SECTION_PARTS["d_kv"] = ("d_k", "d_v")
SECTION_PARTS["misc"] = ("i_k", "i_w", "a_f")
REALIGN_ROWS = 256


def _realign_weights(w_ref, w_sc):
    d_model = w_ref.shape[0]
    for r0 in range(0, d_model, REALIGN_ROWS):
        rows = slice(r0, r0 + REALIGN_ROWS)
        for i, name in enumerate(SOURCES + ("misc",)):
            width = LANES if name == "misc" else GROUP_W
            pieces, used = [], 0
            for part in SECTION_PARTS[name]:
                off, w = IN_OFFSETS[part]
                base = off // LANES * LANES
                end = min(-(-(off + w) // LANES) * LANES, P_IN)
                window = w_ref[rows, base:end]
                pieces.append(window[:, off - base:off - base + w])
                used += w
            if used < width:
                pieces.append(jnp.zeros((REALIGN_ROWS, width - used), w_sc.dtype))
            block = pieces[0] if len(pieces) == 1 else jnp.concatenate(pieces, axis=1)
            w_sc[rows, i * GROUP_W:i * GROUP_W + width] = block


def _in_proj_kernel(*refs):
    (x_ref, scale_ref, shift_ref, g1_ref, w_in_ref, gains_ref, g64_ref, g32_ref) = refs[:8]
    out_refs = refs[8:-1]
    w_ref = refs[-1]

    @pl.when(pl.program_id(0) == 0)
    def _():
        _realign_weights(w_in_ref, w_ref)

    x = x_ref[...]
    ms = jnp.mean(x * x, axis=-1, keepdims=True)
    h = x * lax.rsqrt(ms + RMS_EPS) * g1_ref[...]
    h = h * (1.0 + scale_ref[...]) + shift_ref[...]

    wm = w_ref[:, N_MAIN:N_MAIN + LANES]
    misc = functools.reduce(lambda a, b: a + b, [_dot(term, wm) for term in _split_bf16(h, 3)])
    out_refs[len(OUTPUTS)][...] = misc
    out_refs[len(OUTPUTS) + 1][...] = misc.T

    final = {"misc": misc}
    raw = {name: _dot(h, w_ref[:, i * GROUP_W:(i + 1) * GROUP_W])
           for i, name in enumerate(SOURCES)}
    msqs = {name: _dot(raw[name] * raw[name],
                       (g64_ref if SRC_NORM[name][0] == "n64" else g32_ref)[...])
            for name in SOURCES if name in SRC_NORM}
    for name in SOURCES:
        sec = raw[name]
        if name in SRC_NORM:
            r = SRC_NORM[name][1]
            fac = lax.rsqrt(msqs[name] + RMS_EPS)
            if name == "d_kv":
                lane = lax.broadcasted_iota(I32, sec.shape, 1)
                fac = jnp.where(lane < HEAD_DIM, fac, 1.0)
            sec = sec * fac * gains_ref[r:r + 1, :]
        elif name in SRC_SCALE:
            sec = sec * SRC_SCALE[name]
        final[name] = sec

    tokens = x.shape[0]
    transposed = {}
    for (name, src, width, orient, ones), o_ref in zip(OUTPUTS, out_refs):
        if orient == "cols" and src not in transposed:
            transposed[src] = final[src].T
        pieces = []
        for start, size in LAYOUT_PLANS[name]:
            if orient == "rows":
                pieces.append(jnp.zeros((tokens, size), F32) if start is None
                              else final[src][:, start:start + size])
            else:
                pieces.append(jnp.full((size, tokens), 1.0 if ones else 0.0, F32) if start is None
                              else transposed[src][start:start + size])
        o_ref[...] = jnp.concatenate(pieces, axis=1 if orient == "rows" else 0)


def _in_proj_call(xf, scale1, shift1, g1, w_in, layer, gains, g64, g32, batch, seq, tm):
    N, D = xf.shape
    per_b = seq // tm
    row = lambda i: (i, 0)
    full = lambda i: (0, 0)
    col4 = lambda i: (i // per_b, i % per_b, 0, 0)
    out_shape, out_specs = [], []
    for name, src, width, orient, ones in OUTPUTS:
        if orient == "rows":
            out_shape.append(jax.ShapeDtypeStruct((N, width), F32))
            out_specs.append(pl.BlockSpec((tm, width), row))
        else:
            out_shape.append(jax.ShapeDtypeStruct((batch, per_b, width, tm), F32))
            out_specs.append(pl.BlockSpec((None, None, width, tm), col4))
    out_shape += [jax.ShapeDtypeStruct((N, LANES), F32),
                  jax.ShapeDtypeStruct((batch, per_b, LANES, tm), F32)]
    out_specs += [pl.BlockSpec((tm, LANES), row), pl.BlockSpec((None, None, LANES, tm), col4)]
    return pl.pallas_call(
        _in_proj_kernel,
        grid=(N // tm,),
        in_specs=([pl.BlockSpec((tm, D), row),
                   pl.BlockSpec((None, 1, D), lambda i: (i // per_b, 0, 0)),
                   pl.BlockSpec((None, 1, D), lambda i: (i // per_b, 0, 0)),
                   pl.BlockSpec((1, D), full),
                   pl.BlockSpec((None,) + w_in.shape[1:], lambda i: (layer, 0, 0),
                                pipeline_mode=pl.Buffered(1)),
                   pl.BlockSpec(gains.shape, full),
                   pl.BlockSpec(g64.shape, full),
                   pl.BlockSpec(g32.shape, full)]),
        out_specs=out_specs,
        out_shape=out_shape,
        scratch_shapes=[pltpu.VMEM((D, N_MAIN + LANES), F32)],
        compiler_params=_params("arbitrary"),
        name="in_proj",
    )(xf, scale1, shift1, g1, w_in, gains, g64, g32)


def _cum_kernel(misc_ref, bf_ref, tri_ref, sel_ref, o_ref, *, blk):
    seq = misc_ref.shape[0]
    carry = jnp.zeros((1, LANES), F32)
    tri = tri_ref[...]
    for j in range(seq // blk):
        rows = slice(j * blk, (j + 1) * blk)
        lf = _log_sigmoid(misc_ref[rows, :] + bf_ref[...])
        c = carry
        for part in _split_bf16(lf, 3):
            c = c + _dot(tri, part)
        carry = c[blk - 1:blk, :]
        parts = _split_bf16(c, 3)
        for h in range(HEADS):
            rep = _dot(parts[0], sel_ref[h]) + _dot(parts[1], sel_ref[h]) + _dot(parts[2], sel_ref[h])
            o_ref[h, rows, :] = rep * LOG2E


def _cum_call(misc, bf_row, tri, sel):
    B, S, _ = misc.shape
    blk = tri.shape[0]
    return pl.pallas_call(
        functools.partial(_cum_kernel, blk=blk),
        grid=(B,),
        in_specs=[pl.BlockSpec((None, S, LANES), lambda b: (b, 0, 0)),
                  pl.BlockSpec((1, LANES), lambda b: (0, 0)),
                  pl.BlockSpec((blk, blk), lambda b: (0, 0)),
                  pl.BlockSpec((HEADS, LANES, LANES), lambda b: (0, 0, 0))],
        out_specs=pl.BlockSpec((None, HEADS, S, LANES), lambda b: (b, 0, 0, 0)),
        out_shape=jax.ShapeDtypeStruct((B, HEADS, S, LANES), F32),
        compiler_params=_params("arbitrary"),
        name="forget_cumsum",
    )(misc, bf_row, tri, sel)


def _seq_rows_spec(seq, width):
    return pl.BlockSpec((None, seq, width), lambda b: (b, 0, 0))


def _seq_cols_spec(nb, width, t):
    return pl.BlockSpec((None, nb, width, t), lambda b: (b, 0, 0, 0))


def _const_spec(shape):
    return pl.BlockSpec(shape, lambda b: (0,) * len(shape))


def _for_each_query_tile(q_ref, o_ref, t, tile_fn):
    def body(qi, carry):
        o_ref[pl.ds(pl.multiple_of(qi * t, t), t), :] = tile_fn(qi, q_ref.at[qi])
        return carry

    lax.fori_loop(0, q_ref.shape[0], body, 0)


def _lane_repeat(x, t):
    return jnp.concatenate([x] * (t // LANES), axis=1)


def _softmax_steps(scores, values, m_refs, acc_refs):
    probs, alphas = [], []
    for s, m_ref in zip(scores, m_refs):
        m_old = m_ref[...]
        m_new = jnp.maximum(m_old, jnp.max(s, axis=0, keepdims=True))
        alphas.append(jnp.exp2(m_old - m_new))
        probs.append(jnp.exp2(s - m_new))
        m_ref[...] = m_new
    for p, vx, alpha, acc_ref in zip(probs, values, alphas, acc_refs):
        acc_ref[...] = alpha * acc_ref[...] + _dot(vx, p)


def _softmax_init(m_scs, acc_scs):
    for m_sc, acc_sc in zip(m_scs, acc_scs):
        m_sc[...] = jnp.full(m_sc.shape, NEG_INF, F32)
        acc_sc[...] = jnp.zeros(acc_sc.shape, F32)


def _softmax_result(acc):
    return acc[:HEAD_DIM] / acc[HEAD_DIM:]


def _alibi(slope, j, qi, t, nblk=1):
    key = lax.broadcasted_iota(I32, (nblk * t, LANES), 0)
    return (slope * LOG2E) * (key + (j - qi) * t).astype(F32)


def _sweep_earlier_blocks(qi, step):
    def body(i, carry):
        step(2 * i, 2)
        return carry

    lax.fori_loop(0, qi // 2, body, 0)

    @pl.when(qi % 2 == 1)
    def _():
        step(qi - 1, 1)


def _sweep_causal_blocks(qi, step):
    @pl.when(qi == 0)
    def _():
        step(0, 1, True)

    @pl.when(qi > 0)
    def _():
        step(qi - 1, 2, True)

    _sweep_earlier_blocks(jnp.maximum(qi - 1, 0), step)


def _causal_mask(t, nblk, strict=False):
    key = lax.broadcasted_iota(I32, (nblk * t, t), 0) - (nblk - 1) * t
    query = lax.broadcasted_iota(I32, (nblk * t, t), 1)
    return key < query if strict else key <= query


def _key_rows(j, nblk, t):
    return pl.ds(pl.multiple_of(j * t, t), nblk * t)


def _value_cols(vx_ref, j, nblk, rows):
    tiles = [vx_ref[j + b, rows, :] for b in range(nblk)]
    return tiles[0] if nblk == 1 else jnp.concatenate(tiles, axis=1)


def _attn_a_kernel(q_ref, k_ref, vx_ref, cum_ref, o_ref, *scratch, t):
    m_sc, acc_sc = scratch[:HEADS], scratch[HEADS:]

    def tile(qi, q):
        _softmax_init(m_sc, acc_sc)

        def step(j, nblk, masked=False):
            rows = _key_rows(j, nblk, t)
            heads = [slice(LANES * h, LANES * (h + 1)) for h in range(HEADS)]
            scores = [_dot(k_ref[rows, hs], q[hs, :]) for hs in heads]
            for h in range(HEADS):
                s = scores[h] - _lane_repeat(cum_ref[h, rows, :], t)
                scores[h] = jnp.where(_causal_mask(t, nblk), s, -jnp.inf) if masked else s
            _softmax_steps(scores, [_value_cols(vx_ref, j, nblk, hs) for hs in heads],
                           m_sc, acc_sc)

        _sweep_causal_blocks(qi, step)
        return jnp.concatenate([_softmax_result(acc_sc[h][...]) for h in range(HEADS)], axis=0).T

    _for_each_query_tile(q_ref, o_ref, t, tile)


def _attn_scratch(n, t, rows=LANES):
    return [pltpu.VMEM((1, t), F32)] * n + [pltpu.VMEM((rows, t), F32)] * n


def _attn_a_call(qT, kp, vxT, cumrep):
    B, nb, _, t = qT.shape
    S = nb * t
    return pl.pallas_call(
        functools.partial(_attn_a_kernel, t=t),
        grid=(B,),
        in_specs=[_seq_cols_spec(nb, HW, t), _seq_rows_spec(S, HW), _seq_cols_spec(nb, HW, t),
                  pl.BlockSpec((None, HEADS, S, LANES), lambda b: (b, 0, 0, 0))],
        out_specs=_seq_rows_spec(S, GROUP_W),
        out_shape=jax.ShapeDtypeStruct((B, S, GROUP_W), F32),
        scratch_shapes=_attn_scratch(HEADS, t),
        compiler_params=_params("arbitrary"),
        name="attn_forget",
    )(qT, kp, vxT, cumrep)


STICK_SPLIT_TERMS = 3


def _attn_b_kernel(q_ref, k_ref, v_ref, after_ref, o_ref, *scratch, t):
    tile = functools.partial(_attn_b_tile, k_ref=k_ref, v_ref=v_ref, after_ref=after_ref,
                             r_sc=scratch[:HEADS], acc_sc=scratch[HEADS:], t=t)
    _for_each_query_tile(q_ref, o_ref, t, tile)


def _attn_b_tile(qi, q_ref, *, k_ref, v_ref, after_ref, r_sc, acc_sc, t):
    for h in range(HEADS):
        r_sc[h][...] = jnp.zeros(r_sc[h].shape, F32)
        acc_sc[h][...] = jnp.zeros(acc_sc[h].shape, F32)

    def step(j, nblk, masked=False):
        after = after_ref[...]
        rows = _key_rows(j, nblk, t)
        heads = [slice(LANES * h, LANES * (h + 1)) for h in range(HEADS)]
        zs = [_dot(k_ref[rows, hs], q_ref[hs, :]) for hs in heads]
        lbs, splits, later_sums = [], [], []
        for h in range(HEADS):
            lb = _log2_sigmoid(zs[h])
            lm = lb - zs[h]
            if masked:
                lm = jnp.where(_causal_mask(t, nblk, strict=True), lm, 0.0)
            blocks = [lm[b * t:(b + 1) * t] for b in range(nblk)]
            splits.append([_split_bf16(blk, STICK_SPLIT_TERMS) for blk in blocks])
            sums = [jnp.sum(blk, axis=0, keepdims=True) for blk in blocks]
            r_old = r_sc[h][...]
            total = sums[0]
            for s_ in sums[1:]:
                total = total + s_
            r_sc[h][...] = r_old + total
            lbs.append(lb + r_old)
            later, run = [], None
            for b in reversed(range(nblk)):
                later.append(run)
                run = sums[b] if run is None else run + sums[b]
            later_sums.append(later[::-1])
        suffixes = [[functools.reduce(lambda a, b: a + b, [_dot(after, term) for term in terms])
                     for terms in splits[h]] for h in range(HEADS)]
        ws = []
        for h in range(HEADS):
            parts = [suffixes[h][b] if later_sums[h][b] is None else suffixes[h][b] + later_sums[h][b]
                     for b in range(nblk)]
            suffix = parts[0] if nblk == 1 else jnp.concatenate(parts, axis=0)
            w = jnp.exp2(lbs[h] + suffix)
            if masked:
                w = jnp.where(_causal_mask(t, nblk, strict=True), w, 0.0)
            ws.append(w)
        for h in range(HEADS):
            acc_sc[h][...] += _dot(_value_cols(v_ref, j, nblk, slice(HEAD_DIM * h, HEAD_DIM * (h + 1))),
                                   ws[h])

    @pl.when(qi == 0)
    def _():
        step(0, 1, True)

    @pl.when(qi > 0)
    def _():
        step(qi - 1, 2, True)

    rest = jnp.maximum(qi - 1, 0)

    @pl.when(rest % 2 == 1)
    def _():
        step(rest - 1, 1)

    pairs = rest // 2

    def body(i, carry):
        step(2 * (pairs - 1 - i), 2)
        return carry

    lax.fori_loop(0, pairs, body, 0)
    return jnp.concatenate([acc_sc[h][...] for h in range(HEADS)], axis=0).T


def _attn_b_call(qT, kp, vT, after):
    B, nb, _, t = qT.shape
    S = nb * t
    return pl.pallas_call(
        functools.partial(_attn_b_kernel, t=t),
        grid=(B,),
        in_specs=[_seq_cols_spec(nb, HW, t), _seq_rows_spec(S, HW), _seq_cols_spec(nb, GROUP_W, t),
                  _const_spec((t, t))],
        out_specs=_seq_rows_spec(S, GROUP_W),
        out_shape=jax.ShapeDtypeStruct((B, S, GROUP_W), F32),
        scratch_shapes=_attn_scratch(HEADS, t, HEAD_DIM),
        compiler_params=_params("arbitrary"),
        name="attn_stick",
    )(qT, kp, vT, after)


def _attn_c_kernel(q_ref, k_ref, vx_ref, lamv_ref, subg_ref, o_ref, *scratch,
                   t, slopes, lambda_init):
    tile = functools.partial(_attn_c_tile, k_ref=k_ref, vx_ref=vx_ref, lamv_ref=lamv_ref,
                             subg_ref=subg_ref, m_sc=scratch[:2 * HEADS], acc_sc=scratch[2 * HEADS:],
                             t=t, slopes=slopes, lambda_init=lambda_init)
    _for_each_query_tile(q_ref, o_ref, t, tile)


def _attn_c_tile(qi, q_ref, *, k_ref, vx_ref, lamv_ref, subg_ref, m_sc, acc_sc,
                 t, slopes, lambda_init):
    _softmax_init(m_sc, acc_sc)

    def step(j, nblk, masked=False):
        rows = _key_rows(j, nblk, t)
        heads = [slice(LANES * h, LANES * (h + 1)) for h in range(HEADS)]
        scores = [_dot(k_ref[rows, heads[g // 2]], q_ref[LANES * g:LANES * (g + 1), :])
                  for g in range(2 * HEADS)]
        for g in range(2 * HEADS):
            s = scores[g] + _lane_repeat(_alibi(slopes[g // 2], j, qi, t, nblk), t)
            scores[g] = jnp.where(_causal_mask(t, nblk), s, -jnp.inf) if masked else s
        _softmax_steps(scores, [_value_cols(vx_ref, j, nblk, heads[g // 2])
                                for g in range(2 * HEADS)], m_sc, acc_sc)

    _sweep_causal_blocks(qi, step)

    lv = lamv_ref[...]
    lam = (jnp.exp(jnp.sum(lv[0:1] * lv[1:2], axis=-1, keepdims=True))
           - jnp.exp(jnp.sum(lv[2:3] * lv[3:4], axis=-1, keepdims=True)) + lambda_init)
    outs = []
    for h in range(HEADS):
        o = (_softmax_result(acc_sc[2 * h][...])
             - lam * _softmax_result(acc_sc[2 * h + 1][...]))
        ms = jnp.mean(o * o, axis=0, keepdims=True)
        outs.append(o * lax.rsqrt(ms + RMS_EPS) * subg_ref[...] * (1.0 - lambda_init))
    return jnp.concatenate(outs, axis=0).T


def _attn_c_call(qT, kp, vxT, lamv, subg_col, slopes, lambda_init):
    B, nb, _, t = qT.shape
    S = nb * t
    return pl.pallas_call(
        functools.partial(_attn_c_kernel, t=t, slopes=slopes, lambda_init=lambda_init),
        grid=(B,),
        in_specs=[_seq_cols_spec(nb, 2 * HW, t), _seq_rows_spec(S, HW), _seq_cols_spec(nb, HW, t),
                  _const_spec(lamv.shape), _const_spec(subg_col.shape)],
        out_specs=_seq_rows_spec(S, GROUP_W),
        out_shape=jax.ShapeDtypeStruct((B, S, GROUP_W), F32),
        scratch_shapes=_attn_scratch(2 * HEADS, t),
        compiler_params=_params("arbitrary"),
        name="attn_diff",
    )(qT, kp, vxT, lamv, subg_col)


def _fold_rows(x, group=SUBLANES):
    return jnp.sum(x.reshape(x.shape[0] // group, group, x.shape[1]), axis=0)


PACKED_ROWS = 2 * SUBLANES
DIGIT_BITS = 8
N_DIGITS = 32 // DIGIT_BITS
DIGIT_MASK = (1 << DIGIT_BITS) - 1


def _fold_packed(x):
    slabs = [x[i * PACKED_ROWS:(i + 1) * PACKED_ROWS] for i in range(x.shape[0] // PACKED_ROWS)]
    while len(slabs) > 1:
        slabs = [a + b for a, b in zip(slabs[0::2], slabs[1::2])]
    return slabs[0]


def _attn_d_kernel(q_ref, iq_all_ref, iw_all_ref, dk_ref, dvx_ref, ik_ref, before_ref, o_ref,
                   keys_sc, tau_sc, *scratch, t, topk, slopes):
    def tile(qi, q):
        return _attn_d_tile(qi, q, iq_all_ref.at[qi], iw_all_ref.at[qi], dk_ref, dvx_ref, ik_ref,
                            before_ref, keys_sc, tau_sc, scratch, t=t, topk=topk, slopes=slopes)

    _for_each_query_tile(q_ref, o_ref, t, tile)


def _attn_d_tile(qi, q_ref, iq_ref, iw_ref, dk_ref, dvx_ref, ik_ref, before_ref,
                 keys_sc, tau_sc, scratch, *, t, topk, slopes):
    digit_sc, scratch = scratch[:N_DIGITS], scratch[N_DIGITS:]
    m_sc, acc_sc = scratch[:HEADS], scratch[HEADS:]

    w = iw_ref[...] * IDX_HEADS ** -0.5

    def index_step(j, nblk, masked=False):
        ikb = ik_ref[_key_rows(j, nblk, t), :]
        zs = [_dot(ikb, iq_ref[LANES * hh:LANES * (hh + 1), :]) for hh in range(IDX_HEADS)]
        sc = w[0:1, :] * jnp.maximum(zs[0], 0.0)
        for hh in range(1, IDX_HEADS):
            sc = sc + w[hh:hh + 1, :] * jnp.maximum(zs[hh], 0.0)
        sc = jnp.where(sc == 0.0, 0.0, sc)
        if masked:
            sc = jnp.where(_causal_mask(t, nblk), sc, NEG_INF)
        bits = pltpu.bitcast(sc, I32)
        keys = jnp.where(bits < 0, bits ^ 0x7FFFFFFF, bits)
        ukeys = keys ^ INT_MIN
        digits = [(lax.shift_right_logical(ukeys, DIGIT_BITS * (N_DIGITS - 1 - d)) & DIGIT_MASK)
                  .astype(F32).astype(BF16) for d in range(N_DIGITS)]
        for b in range(nblk):
            rows = slice(b * t, (b + 1) * t)
            keys_sc[j + b] = keys[rows]
            for d in range(N_DIGITS):
                digit_sc[d][j + b] = digits[d][rows]

    _sweep_causal_blocks(qi, index_step)

    qpos = qi * t + lax.broadcasted_iota(I32, (1, t), 1)
    kt = jnp.minimum(topk, qpos + 1).astype(F32)

    def count_ge(cand):
        def body(j, acc):
            return acc + _fold_rows((keys_sc[j] >= cand).astype(F32))
        acc = lax.fori_loop(0, qi + 1, body, jnp.zeros((SUBLANES, t), F32))
        return jnp.sum(acc, axis=0, keepdims=True)

    one_b = jnp.ones((), BF16)
    zero_b = jnp.zeros((), BF16)

    def count_digit_ge(vals_sc, cand):
        cand_b = cand.astype(F32).astype(BF16)

        def block_count(j):
            return _fold_packed(jnp.where(vals_sc[j] >= cand_b, one_b, zero_b))

        def pair(i, acc):
            return acc + (block_count(2 * i) + block_count(2 * i + 1)).astype(F32)

        acc = lax.fori_loop(0, (qi + 1) // 2, pair, jnp.zeros((PACKED_ROWS, t), F32))
        acc = lax.cond(qi % 2 == 0, lambda a: a + block_count(qi).astype(F32), lambda a: a, acc)
        return jnp.sum(acc, axis=0, keepdims=True)

    def keep_matching(vals_sc, match_sc, match):
        match_b = match.astype(F32).astype(BF16)

        def body(j, carry):
            vals_sc[j] = jnp.where(match_sc[j] == match_b, vals_sc[j], -one_b)
            return carry
        lax.fori_loop(0, qi + 1, body, 0)

    zero = jnp.zeros((1, t), I32)
    rank = kt
    above = jnp.zeros((1, t), F32)
    tau_u = zero
    digit = zero
    for d in range(N_DIGITS):
        if d > 0:
            keep_matching(digit_sc[d], digit_sc[d - 1], digit)

        def bit_body(i, prefix, d=d, rank=rank):
            cand = prefix + lax.shift_left(jnp.int32(1), DIGIT_BITS - 1 - i)
            return jnp.where(count_digit_ge(digit_sc[d], cand) >= rank, cand, prefix)

        digit = lax.fori_loop(0, DIGIT_BITS, bit_body, zero)
        tau_u = lax.shift_left(tau_u, DIGIT_BITS) | digit
        if d < N_DIGITS - 1:
            higher = count_digit_ge(digit_sc[d], digit + 1)
            above = above + higher
            rank = rank - higher
    tau = tau_u ^ INT_MIN
    tau_sc[...] = tau
    excess = jnp.max(above + count_digit_ge(digit_sc[N_DIGITS - 1], digit) - kt)

    @pl.when(excess > 0.0)
    def _():
        need = kt - count_ge(tau + 1)

        def tie_body(j, seen):
            kj = keys_sc[j]
            eq = kj == tau
            eqb = eq.astype(BF16)
            earlier = _dot(before_ref[...], eqb) + seen
            keys_sc[j] = jnp.where(eq & (earlier >= need), INT_MIN, kj)
            return seen + jnp.sum(eqb.astype(F32), axis=0, keepdims=True)

        lax.fori_loop(0, qi + 1, tie_body, jnp.zeros((1, t), F32))

    _softmax_init(m_sc, acc_sc)

    def step(j, nblk):
        keys = [keys_sc[j + b] for b in range(nblk)]
        sel = (keys[0] if nblk == 1 else jnp.concatenate(keys, axis=0)) >= tau_sc[...]
        kb = dk_ref[_key_rows(j, nblk, t), :]
        vx = _value_cols(dvx_ref, j, nblk, slice(None))
        scores = [_dot(kb, q_ref[LANES * h:LANES * (h + 1), :]) for h in range(HEADS)]
        for h in range(HEADS):
            bias = _lane_repeat(_alibi(slopes[h], j, qi, t, nblk), t)
            scores[h] = jnp.where(sel, scores[h] + bias, -jnp.inf)
        _softmax_steps(scores, [vx] * HEADS, m_sc, acc_sc)

    _sweep_earlier_blocks(qi + 1, step)
    return jnp.concatenate([_softmax_result(acc_sc[h][...]) for h in range(HEADS)], axis=0).T


def _attn_d_call(qT, iqT, miscT, dkp, dvxT, ikp, before, topk, slopes):
    B, nb, _, t = qT.shape
    S = nb * t
    iw_block = MISC_IW // IDX_HEADS
    return pl.pallas_call(
        functools.partial(_attn_d_kernel, t=t, topk=topk, slopes=slopes),
        grid=(B,),
        in_specs=[_seq_cols_spec(nb, HW, t), _seq_cols_spec(nb, IDX_HEADS * LANES, t),
                  pl.BlockSpec((None, nb, IDX_HEADS, t), lambda b: (b, 0, iw_block, 0)),
                  _seq_rows_spec(S, LANES), _seq_cols_spec(nb, LANES, t), _seq_rows_spec(S, LANES),
                  _const_spec((t, t))],
        out_specs=_seq_rows_spec(S, GROUP_W),
        out_shape=jax.ShapeDtypeStruct((B, S, GROUP_W), F32),
        scratch_shapes=([pltpu.VMEM((nb, t, t), I32), pltpu.VMEM((1, t), I32)]
                        + [pltpu.VMEM((nb, t, t), BF16)] * N_DIGITS + _attn_scratch(HEADS, t)),
        compiler_params=_params("arbitrary"),
        name="attn_sparse",
    )(qT, iqT, miscT, dkp, dvxT, ikp, before)


ROUTER_TERMS = 3
ROUTER_STRIDE = 40


def _out_proj_kernel(x_ref, oa_ref, ob_ref, oc_ref, od_ref, beta_ref, wo_ref, gate_ref,
                     g2_ref, scale_ref, shift_ref, wr_ref, br_ref, earlier_ref,
                     x1_ref, h2x_ref, route_ref, cnt_ref, cnt_sc):
    d_model = x_ref.shape[1]
    acc = None
    for i, o_ref in enumerate((oa_ref, ob_ref, oc_ref, od_ref)):
        sl = slice(GROUP_W * i, GROUP_W * (i + 1))
        mix = o_ref[...] * beta_ref[:, sl]
        part = jnp.dot(mix, wo_ref[sl, :], preferred_element_type=F32)
        acc = part if acc is None else acc + part
    x1 = x_ref[...] + gate_ref[...] * acc
    x1_ref[...] = x1
    ms = jnp.mean(x1 * x1, axis=-1, keepdims=True)
    h2 = x1 * lax.rsqrt(ms + RMS_EPS) * g2_ref[...]
    h2 = h2 * (1.0 + scale_ref[...]) + shift_ref[...]
    h2x_ref[:, :d_model] = h2

    packed = functools.reduce(lambda a, b: a + b,
                              [_dot(term, wr_ref[...]) for term in _split_bf16(h2, ROUTER_TERMS)])
    logits = packed
    for k in range(1, ROUTER_TERMS):
        logits = logits + pltpu.roll(packed, LANES - k * ROUTER_STRIDE, 1)
    logits = logits + br_ref[...]
    lt = logits.T
    tm = lt.shape[1]
    g = lt[0:N_GROUPS]
    gmax = jnp.max(g, axis=0, keepdims=True)
    gi = lax.broadcasted_iota(I32, g.shape, 0)
    gidx = jnp.min(jnp.where(g == gmax, gi, N_GROUPS), axis=0, keepdims=True)
    g_prob = 1.0 / jnp.sum(jnp.exp(g - gmax), axis=0, keepdims=True)
    e_sel = jnp.zeros((EXPERTS_PER_GROUP, tm), F32)
    for gg in range(N_GROUPS):
        lo = N_GROUPS + EXPERTS_PER_GROUP * gg
        e_sel = e_sel + jnp.where(gidx == gg, lt[lo:lo + EXPERTS_PER_GROUP], 0.0)
    ei = lax.broadcasted_iota(I32, e_sel.shape, 0)
    v1 = jnp.max(e_sel, axis=0, keepdims=True)
    i1 = jnp.min(jnp.where(e_sel == v1, ei, EXPERTS_PER_GROUP), axis=0, keepdims=True)
    rest = jnp.where(ei == i1, -jnp.inf, e_sel)
    v2 = jnp.max(rest, axis=0, keepdims=True)
    i2 = jnp.min(jnp.where(rest == v2, ei, EXPERTS_PER_GROUP), axis=0, keepdims=True)
    e2 = jnp.exp(v2 - v1)
    w1 = g_prob / (1.0 + e2)
    w2 = g_prob * e2 / (1.0 + e2)
    in_group = jnp.where(ei == i1, w1, 0.0) + jnp.where(ei == i2, w2, 0.0)
    cw = jnp.concatenate([in_group, jnp.zeros((LANES - EXPERTS_PER_GROUP, tm), F32)], axis=0)
    h2x_ref[:, d_model:] = cw.T

    @pl.when(pl.program_id(0) == 0)
    def _():
        cnt_sc[...] = jnp.zeros(cnt_sc.shape, F32)

    rows = lax.broadcasted_iota(I32, (SUBLANES, tm), 0)
    onehot = (rows == gidx).astype(F32)
    seen = _dot(onehot.astype(BF16), earlier_ref[...]) + cnt_sc[...]
    rank = jnp.sum(onehot * seen, axis=0, keepdims=True)
    route_ref[...] = jnp.concatenate(
        [gidx, rank.astype(I32), jnp.zeros((SUBLANES - 2, tm), I32)], axis=0)
    cnt_sc[...] = cnt_sc[...] + jnp.sum(onehot, axis=1, keepdims=True)
    cnt_ref[...] = cnt_sc[:, :LANES]


def _out_proj_call(xf, outs, beta, w_out, gate1, g2, scale2, shift2, w_r, b_r, earlier, seq, tm):
    N, D = xf.shape
    per_b = seq // tm
    row = lambda i: (i, 0)
    full = lambda i: (0, 0)
    per_batch = pl.BlockSpec((None, 1, D), lambda i: (i // per_b, 0, 0))
    return pl.pallas_call(
        _out_proj_kernel,
        grid=(N // tm,),
        in_specs=([pl.BlockSpec((tm, D), row)] + [pl.BlockSpec((tm, GROUP_W), row)] * 4
                  + [pl.BlockSpec((1, D), full), pl.BlockSpec((D, D), full), per_batch,
                     pl.BlockSpec((1, D), full), per_batch, per_batch,
                     pl.BlockSpec((D, LANES), full), pl.BlockSpec((1, LANES), full),
                     pl.BlockSpec((tm, tm), full)]),
        out_specs=[pl.BlockSpec((tm, D), row), pl.BlockSpec((tm, D + LANES), row),
                   pl.BlockSpec((SUBLANES, tm), lambda i: (0, i)),
                   pl.BlockSpec((SUBLANES, LANES), full)],
        out_shape=[jax.ShapeDtypeStruct((N, D), F32), jax.ShapeDtypeStruct((N, D + LANES), F32),
                   jax.ShapeDtypeStruct((SUBLANES, N), I32),
                   jax.ShapeDtypeStruct((SUBLANES, LANES), F32)],
        scratch_shapes=[pltpu.VMEM((SUBLANES, tm), F32)],
        compiler_params=_params("arbitrary"),
        name="out_proj_router",
    )(xf, *outs, beta, w_out, gate1, g2, scale2, shift2, w_r, b_r, earlier)


MOE_TILE = 512

def _dispatch_kernel(slot_ref, h2x_ref, zeros_ref, xs_ref, sem):
    del zeros_ref

    def row_copy(g, u):
        return pltpu.make_async_copy(h2x_ref.at[g, pl.ds(u, 1), :],
                                     xs_ref.at[pl.ds(slot_ref[0, g * SUBLANES + u], 1), :], sem)

    def issue(g, carry):
        for u in range(SUBLANES):
            row_copy(g, u).start()
        return carry

    lax.fori_loop(0, h2x_ref.shape[0], issue, 0)

    def drain(g, carry):
        for u in range(SUBLANES):
            row_copy(g, u).wait()
        return carry

    lax.fori_loop(0, h2x_ref.shape[0], drain, 0)


def _dispatch_call(slot, h2x, xs_zero, tm):
    N, DX = h2x.shape
    return pl.pallas_call(
        _dispatch_kernel,
        grid=(N // tm,),
        in_specs=[pl.BlockSpec((None, 1, tm), lambda i: (i, 0, 0), memory_space=pltpu.SMEM),
                  pl.BlockSpec((tm // SUBLANES, SUBLANES, DX), lambda i: (i, 0, 0)),
                  pl.BlockSpec(memory_space=pl.ANY)],
        out_specs=pl.BlockSpec(memory_space=pl.ANY),
        out_shape=jax.ShapeDtypeStruct(xs_zero.shape, F32),
        scratch_shapes=[pltpu.SemaphoreType.DMA(())],
        input_output_aliases={2: 0},
        compiler_params=_params("arbitrary"),
        name="moe_dispatch",
    )(slot.reshape(N // tm, 1, tm), h2x.reshape(N // SUBLANES, SUBLANES, DX), xs_zero)


def _expert_kernel(group_ref, valid_ref, xs_ref, w1_ref, w3_ref, w2_ref, y_ref):
    del group_ref
    valid = valid_ref[pl.program_id(0)]
    d_model = y_ref.shape[1]

    @pl.when(valid == 0)
    def _():
        y_ref[...] = jnp.zeros(y_ref.shape, F32)

    @pl.when(valid > 0)
    def _():
        x = xs_ref[:, :d_model]
        cw = xs_ref[:, d_model:]
        acc = None
        for e in range(EXPERTS_PER_GROUP):
            a = _dot(x, w1_ref[e])
            b = _dot(x, w3_ref[e])
            hid = a * jax.nn.sigmoid(a) * b * jnp.broadcast_to(cw[:, e:e + 1], a.shape)
            part = _dot(hid, w2_ref[e])
            acc = part if acc is None else acc + part
        y_ref[...] = acc


def _expert_call(tile_group, tile_valid, xs, w1, w3, w2, layer):
    P, DX = xs.shape
    D, FF = w1.shape[-2:]
    tr = MOE_TILE

    def group_spec(rows, cols):
        return pl.BlockSpec((None, None, EXPERTS_PER_GROUP, rows, cols),
                            lambda i, g, v: (layer, g[i], 0, 0, 0),
                            pipeline_mode=pl.Buffered(1))

    grid_spec = pltpu.PrefetchScalarGridSpec(
        num_scalar_prefetch=2,
        grid=(P // tr,),
        in_specs=[pl.BlockSpec((tr, DX), lambda i, g, v: (i, 0)),
                  group_spec(D, FF), group_spec(D, FF), group_spec(FF, D)],
        out_specs=pl.BlockSpec((tr, D), lambda i, g, v: (i, 0)),
    )
    return pl.pallas_call(
        _expert_kernel,
        grid_spec=grid_spec,
        out_shape=jax.ShapeDtypeStruct((P, D), F32),
        compiler_params=_params("arbitrary"),
        name="moe_experts",
    )(tile_group, tile_valid, xs, w1, w3, w2)


def _residual_kernel(slot_ref, slot_next_ref, x1_ref, gate_ref, ys_ref, o_ref, buf, sems):
    i = pl.program_id(0)

    def gather(slots, b, wait):
        def body(g, carry):
            for u in range(SUBLANES):
                cp = pltpu.make_async_copy(ys_ref.at[pl.ds(slots[0, g * SUBLANES + u], 1), :],
                                           buf.at[b, g, pl.ds(u, 1), :], sems.at[b])
                cp.wait() if wait else cp.start()
            return carry
        lax.fori_loop(0, buf.shape[1], body, 0)

    @pl.when(i == 0)
    def _():
        gather(slot_ref, 0, False)

    @pl.when(i + 1 < pl.num_programs(0))
    def _():
        gather(slot_next_ref, (i + 1) % 2, False)

    gather(slot_ref, i % 2, True)
    o_ref[...] = x1_ref[...] + gate_ref[...] * buf[i % 2].reshape(x1_ref.shape)


def _residual_call(slot, x1, gate2, ys, seq, tm):
    N, D = x1.shape
    n = N // tm
    per_b = seq // tm
    row = pl.BlockSpec((tm, D), lambda i: (i, 0))
    slot3 = slot.reshape(n, 1, tm)
    return pl.pallas_call(
        _residual_kernel,
        grid=(n,),
        in_specs=[pl.BlockSpec((None, 1, tm), lambda i: (i, 0, 0), memory_space=pltpu.SMEM),
                  pl.BlockSpec((None, 1, tm), lambda i: (jnp.minimum(i + 1, n - 1), 0, 0),
                               memory_space=pltpu.SMEM),
                  row, pl.BlockSpec((None, 1, D), lambda i: (i // per_b, 0, 0)),
                  pl.BlockSpec(memory_space=pl.ANY)],
        out_specs=row,
        out_shape=jax.ShapeDtypeStruct((N, D), F32),
        scratch_shapes=[pltpu.VMEM((2, tm // SUBLANES, SUBLANES, D), F32),
                        pltpu.SemaphoreType.DMA((2,))],
        compiler_params=_params("arbitrary"),
        name="moe_residual",
    )(slot3, slot3, x1, gate2, ys)


def _moe_routing(route, cnt, n_tokens):
    tr = MOE_TILE
    n_tiles = n_tokens // tr + N_GROUPS
    counts = cnt[:N_GROUPS, 0].astype(I32)
    padded = (counts + tr - 1) // tr * tr
    ends = jnp.cumsum(padded)
    starts = ends - padded
    group, rank = route[0], route[1]
    slot = starts[group] + rank
    tile_start = jnp.arange(n_tiles, dtype=I32) * tr
    tile_group = jnp.minimum(jnp.sum((tile_start[:, None] >= ends[None, :]).astype(I32), axis=1),
                             N_GROUPS - 1)
    tile_valid = jnp.clip(starts[tile_group] + counts[tile_group] - tile_start, 0, tr)
    tile_valid = jnp.where(tile_start < ends[-1], tile_valid, 0)
    return slot, tile_group, tile_valid


def _block_diag_mean(width, group, valid_in_128=None):
    i = jnp.arange(width)
    same = (i[:, None] // group) == (i[None, :] // group)
    if valid_in_128 is not None:
        same = same & ((i[:, None] % LANES) < valid_in_128) & ((i[None, :] % LANES) < valid_in_128)
    return jnp.where(same, 1.0 / group, 0.0).astype(F32)


def kernel(x, c, ada_w, ada_b, norm1_g, norm2_g, w_in, b_f, qn_a, kn_a, qn_c, kn_c,
           lam_q1, lam_k1, lam_q2, lam_k2, subln_g, qn_d, kn_d, mix_beta, w_out,
           w_group, b_group, w_expert, b_expert, w1, w3, w2):
    B, S, D = x.shape
    L = ada_w.shape[0]
    N = B * S
    topk = min(TOPK_MAX, S // 4)
    t = ATTN_BLOCK
    tm = t
    slopes = [2.0 ** (-8.0 * i / (2 * HEADS)) for i in range(1, 2 * HEADS + 1)]
    slopes_c, slopes_d = tuple(slopes[0::2]), tuple(slopes[1::2])

    idx_t = jnp.arange(t)
    after = (idx_t[None, :] > idx_t[:, None]).astype(BF16)
    before = (idx_t[None, :] < idx_t[:, None]).astype(BF16)
    upto = (idx_t[None, :] <= idx_t[:, None]).astype(BF16)
    g64 = _block_diag_mean(GROUP_W, HEAD_DIM)
    g32 = _block_diag_mean(GROUP_W, DIFF_DIM)
    lane = jnp.arange(LANES)
    cum_sel = jnp.stack([jnp.broadcast_to((lane == MISC_AF + h)[:, None], (LANES, LANES))
                         for h in range(HEADS)]).astype(BF16)

    mod = _ada_call(c, ada_w, ada_b)
    xf = x.reshape(N, D)

    for l in range(L):
        m6 = mod[l].reshape(B, 6, 1, D)
        shift1, scale1, gate1, shift2, scale2, gate2 = (m6[:, i] for i in range(6))

        ones = jnp.ones((GROUP_W - HEAD_DIM,), F32)
        gains = jnp.stack([jnp.tile(qn_a[l], HEADS) * (HEAD_DIM ** -0.5 * LOG2E),
                           jnp.tile(kn_a[l], HEADS),
                           jnp.tile(qn_c[l], 2 * HEADS) * (DIFF_DIM ** -0.5 * LOG2E),
                           jnp.tile(kn_c[l], 2 * HEADS),
                           jnp.tile(qn_d[l], HEADS) * (HEAD_DIM ** -0.5 * LOG2E),
                           jnp.concatenate([kn_d[l], ones])]).astype(F32)
        gains = jnp.concatenate([gains, jnp.zeros((2, GROUP_W), F32)], axis=0)

        outs = _in_proj_call(xf, scale1, shift1, norm1_g[l].reshape(1, D), w_in, l, gains,
                             g64, g32, B, S, tm)
        sec = {name: o for (name, _, _, _, _), o in zip(OUTPUTS, outs)}
        for name, src, width, orient, ones_ in OUTPUTS:
            if orient == "rows":
                sec[name] = sec[name].reshape(B, S, width)
        misc = outs[len(OUTPUTS)].reshape(B, S, LANES)
        miscT = outs[len(OUTPUTS) + 1]

        bf_row = jnp.zeros((1, LANES), F32).at[0, MISC_AF:MISC_AF + HEADS].set(b_f[l].astype(F32))
        cumrep = _cum_call(misc, bf_row, upto, cum_sel)

        o_a = _attn_a_call(sec["a_qT"], sec["a_kp"], sec["a_vxT"], cumrep)
        o_b = _attn_b_call(sec["b_qT"], sec["b_kp"], sec["b_vT"], after)
        lambda_init = 0.8 - 0.6 * math.exp(-0.3 * l)
        lamv = jnp.stack([lam_q1[l], lam_k1[l], lam_q2[l], lam_k2[l]]).astype(F32)
        o_c = _attn_c_call(sec["c_qT"], sec["c_kp"], sec["c_vxT"], lamv,
                           subln_g[l].reshape(HEAD_DIM, 1).astype(F32), slopes_c, lambda_init)
        o_d = _attn_d_call(sec["d_qT"], sec["i_qT"], miscT, sec["d_kp"], sec["d_vxT"], sec["i_kp"],
                           before, topk, slopes_d)

        w_router = jnp.concatenate([w_group[l], w_expert[l]], axis=1)
        terms = _split_bf16(w_router, ROUTER_TERMS)
        gap = jnp.zeros((D, ROUTER_STRIDE - N_GROUPS - N_EXPERTS), BF16)
        w_r = jnp.concatenate([piece for term in terms for piece in (term, gap)], axis=1)
        w_r = jnp.concatenate([w_r, jnp.zeros((D, LANES - ROUTER_TERMS * ROUTER_STRIDE), BF16)],
                              axis=1)
        b_r = jnp.concatenate([b_group[l], b_expert[l],
                               jnp.zeros((LANES - N_GROUPS - N_EXPERTS,), F32)]).reshape(1, LANES)
        x1, h2x, route, cnt = _out_proj_call(
            xf, [o.reshape(N, GROUP_W) for o in (o_a, o_b, o_c, o_d)], mix_beta[l].reshape(1, D),
            w_out[l], gate1, norm2_g[l].reshape(1, D), scale2, shift2, w_r, b_r,
            after, S, tm)

        slot, tile_group, tile_valid = _moe_routing(route, cnt, N)
        xs = _dispatch_call(slot, h2x, jnp.zeros((tile_group.shape[0] * MOE_TILE, D + LANES), F32), tm)
        ys = _expert_call(tile_group, tile_valid, xs, w1, w3, w2, l)
        xf = _residual_call(slot, x1, gate2, ys, S, tm)

    return xf.reshape(B, S, D)
```

```python
import functools
import math

import jax
import jax.numpy as jnp
from jax import lax
from jax.experimental import pallas as pl
from jax.experimental.pallas import tpu as pltpu

F32 = jnp.float32
BF16 = jnp.bfloat16
I32 = jnp.int32

HEAD_DIM = 64
HEADS = 4
GROUP_W = HEADS * HEAD_DIM
DIFF_DIM = HEAD_DIM // 2
IDX_HEADS = 8
IDX_DIM = 32
TOPK_MAX = 256
N_GROUPS = 4
EXPERTS_PER_GROUP = 8
N_EXPERTS = N_GROUPS * EXPERTS_PER_GROUP
EXPERT_FF = 256
RMS_EPS = 1e-6
NEG_INF = -1e30
INT_MIN = -(2 ** 31)
LOG2E = math.log2(math.e)

LANES = 128
SUBLANES = 8
ATTN_BLOCK = 256
VMEM_LIMIT = 56 * 1024 * 1024

MISC_IK = 0
MISC_IW = IDX_DIM
MISC_AF = IDX_DIM + IDX_HEADS


def _params(*sem):
    return pltpu.CompilerParams(dimension_semantics=sem, vmem_limit_bytes=VMEM_LIMIT)


def _log_sigmoid(z):
    return jnp.minimum(z, 0.0) - jnp.log1p(jnp.exp(-jnp.abs(z)))


def _log2_sigmoid(z2):
    return jnp.minimum(z2, 0.0) - jnp.log2(1.0 + jnp.exp2(-jnp.abs(z2)))


def _split_bf16(x, parts):
    out = []
    rem = x
    for _ in range(parts):
        p = rem.astype(BF16)
        out.append(p)
        rem = rem - p.astype(F32)
    return out


def _dot(a, b):
    return jnp.dot(a, b, preferred_element_type=F32)


def _ada_kernel(c_ref, w_ref, b_ref, o_ref):
    c = c_ref[...]
    ca = c * jax.nn.sigmoid(c)
    o_ref[...] = jnp.dot(ca, w_ref[...], precision=lax.Precision.HIGHEST,
                         preferred_element_type=F32) + b_ref[...]


def _ada_call(c, ada_w, ada_b):
    L, D, E = ada_w.shape
    B = c.shape[0]
    tn = 1536
    return pl.pallas_call(
        _ada_kernel,
        grid=(L, E // tn),
        in_specs=[pl.BlockSpec((B, D), lambda l, j: (0, 0)),
                  pl.BlockSpec((None, D, tn), lambda l, j: (l, 0, j)),
                  pl.BlockSpec((None, 1, tn), lambda l, j: (l, 0, j))],
        out_specs=pl.BlockSpec((None, B, tn), lambda l, j: (l, 0, j)),
        out_shape=jax.ShapeDtypeStruct((L, B, E), F32),
        compiler_params=_params("arbitrary", "arbitrary"),
        name="ada_mod",
    )(c, ada_w, ada_b.reshape(L, 1, E))


SOURCES = ("a_q", "a_k", "a_v", "b_q", "b_k", "b_v", "c_q", "c_k", "c_v", "d_q", "i_q", "d_kv")
SRC_NORM = {"a_q": ("n64", 0), "a_k": ("n64", 1), "c_q": ("n32", 2), "c_k": ("n32", 3),
            "d_q": ("n64", 4), "d_kv": ("n64", 5)}
SRC_SCALE = {"b_q": HEAD_DIM ** -0.5 * LOG2E, "i_q": IDX_DIM ** -0.5}
HW = HEADS * LANES
OUTPUTS = (
    ("a_qT", "a_q", HW, "cols", False), ("a_kp", "a_k", HW, "rows", False),
    ("a_vxT", "a_v", HW, "cols", True),
    ("b_qT", "b_q", HW, "cols", False), ("b_kp", "b_k", HW, "rows", False),
    ("b_vT", "b_v", GROUP_W, "cols", False),
    ("c_qT", "c_q", 2 * HW, "cols", False), ("c_kp", "c_k", HW, "rows", False),
    ("c_vxT", "c_v", HW, "cols", True),
    ("d_qT", "d_q", HW, "cols", False), ("i_qT", "i_q", IDX_HEADS * LANES, "cols", False),
    ("d_kp", "d_kv", LANES, "rows", False), ("d_vxT", "d_kv", LANES, "cols", True),
    ("i_kp", "misc", LANES, "rows", False),
)
N_MAIN = len(SOURCES) * GROUP_W


def _layout_plans():
    per_head = [p for h in range(HEADS) for p in ((HEAD_DIM * h, HEAD_DIM), (None, HEAD_DIM))]
    diff_q = []
    for h in range(HEADS):
        for c in range(2):
            lead = DIFF_DIM * c
            if lead:
                diff_q.append((None, lead))
            diff_q.append((HEAD_DIM * h + DIFF_DIM * c, DIFF_DIM))
            diff_q.append((None, LANES - lead - DIFF_DIM))
    idx_q = [p for hh in range(IDX_HEADS) for p in ((IDX_DIM * hh, IDX_DIM), (None, LANES - IDX_DIM))]
    return {"a_qT": per_head, "a_vxT": per_head, "b_qT": per_head, "b_vT": [(0, GROUP_W)],
            "c_qT": diff_q, "c_vxT": per_head, "d_qT": per_head, "i_qT": idx_q,
            "d_vxT": [(HEAD_DIM, HEAD_DIM), (None, HEAD_DIM)],
            "a_kp": per_head, "b_kp": per_head, "c_kp": per_head,
            "d_kp": [(0, HEAD_DIM), (None, HEAD_DIM)],
            "i_kp": [(MISC_IK, IDX_DIM), (None, LANES - IDX_DIM)]}


LAYOUT_PLANS = _layout_plans()

IN_SPLITS = (GROUP_W, GROUP_W, GROUP_W, HEADS, GROUP_W, GROUP_W, GROUP_W, GROUP_W, GROUP_W,
             GROUP_W, GROUP_W, HEAD_DIM, HEAD_DIM, IDX_HEADS * IDX_DIM, IDX_DIM, IDX_HEADS)
IN_NAMES = ("a_q", "a_k", "a_v", "a_f", "b_q", "b_k", "b_v", "c_q", "c_k", "c_v",
            "d_q", "d_k", "d_v", "i_q", "i_k", "i_w")
IN_OFFSETS = {n: (sum(IN_SPLITS[:i]), IN_SPLITS[i]) for i, n in enumerate(IN_NAMES)}
P_IN = sum(IN_SPLITS)
SECTION_PARTS = {name: (name,) for name in SOURCES if name != "d_kv"}
SECTION_PARTS["d_kv"] = ("d_k", "d_v")
SECTION_PARTS["misc"] = ("i_k", "i_w", "a_f")
REALIGN_ROWS = 256


def _realign_weights(w_ref, w_sc):
    d_model = w_ref.shape[0]
    for r0 in range(0, d_model, REALIGN_ROWS):
        rows = slice(r0, r0 + REALIGN_ROWS)
        for i, name in enumerate(SOURCES + ("misc",)):
            width = LANES if name == "misc" else GROUP_W
            pieces, used = [], 0
            for part in SECTION_PARTS[name]:
                off, w = IN_OFFSETS[part]
                base = off // LANES * LANES
                end = min(-(-(off + w) // LANES) * LANES, P_IN)
                window = w_ref[rows, base:end]
                pieces.append(window[:, off - base:off - base + w])
                used += w
            if used < width:
                pieces.append(jnp.zeros((REALIGN_ROWS, width - used), w_sc.dtype))
            block = pieces[0] if len(pieces) == 1 else jnp.concatenate(pieces, axis=1)
            w_sc[rows, i * GROUP_W:i * GROUP_W + width] = block


def _in_proj_kernel(*refs):
    (x_ref, scale_ref, shift_ref, g1_ref, w_in_ref, gains_ref, g64_ref, g32_ref) = refs[:8]
    out_refs = refs[8:-1]
    w_ref = refs[-1]

    @pl.when(pl.program_id(0) == 0)
    def _():
        _realign_weights(w_in_ref, w_ref)

    x = x_ref[...]
    ms = jnp.mean(x * x, axis=-1, keepdims=True)
    h = x * lax.rsqrt(ms + RMS_EPS) * g1_ref[...]
    h = h * (1.0 + scale_ref[...]) + shift_ref[...]

    wm = w_ref[:, N_MAIN:N_MAIN + LANES]
    misc = functools.reduce(lambda a, b: a + b, [_dot(term, wm) for term in _split_bf16(h, 3)])
    out_refs[len(OUTPUTS)][...] = misc
    out_refs[len(OUTPUTS) + 1][...] = misc.T

    final = {"misc": misc}
    raw = {name: _dot(h, w_ref[:, i * GROUP_W:(i + 1) * GROUP_W])
           for i, name in enumerate(SOURCES)}
    msqs = {name: _dot(raw[name] * raw[name],
                       (g64_ref if SRC_NORM[name][0] == "n64" else g32_ref)[...])
            for name in SOURCES if name in SRC_NORM}
    for name in SOURCES:
        sec = raw[name]
        if name in SRC_NORM:
            r = SRC_NORM[name][1]
            fac = lax.rsqrt(msqs[name] + RMS_EPS)
            if name == "d_kv":
                lane = lax.broadcasted_iota(I32, sec.shape, 1)
                fac = jnp.where(lane < HEAD_DIM, fac, 1.0)
            sec = sec * fac * gains_ref[r:r + 1, :]
        elif name in SRC_SCALE:
            sec = sec * SRC_SCALE[name]
        final[name] = sec

    tokens = x.shape[0]
    transposed = {}
    for (name, src, width, orient, ones), o_ref in zip(OUTPUTS, out_refs):
        if orient == "cols" and src not in transposed:
            transposed[src] = final[src].T
        pieces = []
        for start, size in LAYOUT_PLANS[name]:
            if orient == "rows":
                pieces.append(jnp.zeros((tokens, size), F32) if start is None
                              else final[src][:, start:start + size])
            else:
                pieces.append(jnp.full((size, tokens), 1.0 if ones else 0.0, F32) if start is None
                              else transposed[src][start:start + size])
        o_ref[...] = jnp.concatenate(pieces, axis=1 if orient == "rows" else 0)


def _in_proj_call(xf, scale1, shift1, g1, w_in, layer, gains, g64, g32, batch, seq, tm):
    N, D = xf.shape
    per_b = seq // tm
    row = lambda i: (i, 0)
    full = lambda i: (0, 0)
    col4 = lambda i: (i // per_b, i % per_b, 0, 0)
    out_shape, out_specs = [], []
    for name, src, width, orient, ones in OUTPUTS:
        if orient == "rows":
            out_shape.append(jax.ShapeDtypeStruct((N, width), F32))
            out_specs.append(pl.BlockSpec((tm, width), row))
        else:
            out_shape.append(jax.ShapeDtypeStruct((batch, per_b, width, tm), F32))
            out_specs.append(pl.BlockSpec((None, None, width, tm), col4))
    out_shape += [jax.ShapeDtypeStruct((N, LANES), F32),
                  jax.ShapeDtypeStruct((batch, per_b, LANES, tm), F32)]
    out_specs += [pl.BlockSpec((tm, LANES), row), pl.BlockSpec((None, None, LANES, tm), col4)]
    return pl.pallas_call(
        _in_proj_kernel,
        grid=(N // tm,),
        in_specs=([pl.BlockSpec((tm, D), row),
                   pl.BlockSpec((None, 1, D), lambda i: (i // per_b, 0, 0)),
                   pl.BlockSpec((None, 1, D), lambda i: (i // per_b, 0, 0)),
                   pl.BlockSpec((1, D), full),
                   pl.BlockSpec((None,) + w_in.shape[1:], lambda i: (layer, 0, 0),
                                pipeline_mode=pl.Buffered(1)),
                   pl.BlockSpec(gains.shape, full),
                   pl.BlockSpec(g64.shape, full),
                   pl.BlockSpec(g32.shape, full)]),
        out_specs=out_specs,
        out_shape=out_shape,
        scratch_shapes=[pltpu.VMEM((D, N_MAIN + LANES), F32)],
        compiler_params=_params("arbitrary"),
        name="in_proj",
    )(xf, scale1, shift1, g1, w_in, gains, g64, g32)


def _cum_kernel(misc_ref, bf_ref, tri_ref, sel_ref, o_ref, *, blk):
    seq = misc_ref.shape[0]
    carry = jnp.zeros((1, LANES), F32)
    tri = tri_ref[...]
    for j in range(seq // blk):
        rows = slice(j * blk, (j + 1) * blk)
        lf = _log_sigmoid(misc_ref[rows, :] + bf_ref[...])
        c = carry
        for part in _split_bf16(lf, 3):
            c = c + _dot(tri, part)
        carry = c[blk - 1:blk, :]
        parts = _split_bf16(c, 3)
        for h in range(HEADS):
            rep = _dot(parts[0], sel_ref[h]) + _dot(parts[1], sel_ref[h]) + _dot(parts[2], sel_ref[h])
            o_ref[h, rows, :] = rep * LOG2E


def _cum_call(misc, bf_row, tri, sel):
    B, S, _ = misc.shape
    blk = tri.shape[0]
    return pl.pallas_call(
        functools.partial(_cum_kernel, blk=blk),
        grid=(B,),
        in_specs=[pl.BlockSpec((None, S, LANES), lambda b: (b, 0, 0)),
                  pl.BlockSpec((1, LANES), lambda b: (0, 0)),
                  pl.BlockSpec((blk, blk), lambda b: (0, 0)),
                  pl.BlockSpec((HEADS, LANES, LANES), lambda b: (0, 0, 0))],
        out_specs=pl.BlockSpec((None, HEADS, S, LANES), lambda b: (b, 0, 0, 0)),
        out_shape=jax.ShapeDtypeStruct((B, HEADS, S, LANES), F32),
        compiler_params=_params("arbitrary"),
        name="forget_cumsum",
    )(misc, bf_row, tri, sel)


def _seq_rows_spec(seq, width):
    return pl.BlockSpec((None, seq, width), lambda b: (b, 0, 0))


def _seq_cols_spec(nb, width, t):
    return pl.BlockSpec((None, nb, width, t), lambda b: (b, 0, 0, 0))


def _const_spec(shape):
    return pl.BlockSpec(shape, lambda b: (0,) * len(shape))


def _for_each_query_tile(q_ref, o_ref, t, tile_fn):
    def body(qi, carry):
        o_ref[pl.ds(pl.multiple_of(qi * t, t), t), :] = tile_fn(qi, q_ref.at[qi])
        return carry

    lax.fori_loop(0, q_ref.shape[0], body, 0)


def _lane_repeat(x, t):
    return jnp.concatenate([x] * (t // LANES), axis=1)


def _softmax_steps(scores, values, m_refs, acc_refs):
    probs, alphas = [], []
    for s, m_ref in zip(scores, m_refs):
        m_old = m_ref[...]
        m_new = jnp.maximum(m_old, jnp.max(s, axis=0, keepdims=True))
        alphas.append(jnp.exp2(m_old - m_new))
        probs.append(jnp.exp2(s - m_new))
        m_ref[...] = m_new
    for p, vx, alpha, acc_ref in zip(probs, values, alphas, acc_refs):
        acc_ref[...] = alpha * acc_ref[...] + _dot(vx, p)


def _softmax_init(m_scs, acc_scs):
    for m_sc, acc_sc in zip(m_scs, acc_scs):
        m_sc[...] = jnp.full(m_sc.shape, NEG_INF, F32)
        acc_sc[...] = jnp.zeros(acc_sc.shape, F32)


def _softmax_result(acc):
    return acc[:HEAD_DIM] / acc[HEAD_DIM:]


def _alibi(slope, j, qi, t, nblk=1):
    key = lax.broadcasted_iota(I32, (nblk * t, LANES), 0)
    return (slope * LOG2E) * (key + (j - qi) * t).astype(F32)


def _sweep_earlier_blocks(qi, step):
    def body(i, carry):
        step(2 * i, 2)
        return carry

    lax.fori_loop(0, qi // 2, body, 0)

    @pl.when(qi % 2 == 1)
    def _():
        step(qi - 1, 1)


def _sweep_causal_blocks(qi, step):
    @pl.when(qi == 0)
    def _():
        step(0, 1, True)

    @pl.when(qi > 0)
    def _():
        step(qi - 1, 2, True)

    _sweep_earlier_blocks(jnp.maximum(qi - 1, 0), step)


def _causal_mask(t, nblk, strict=False):
    key = lax.broadcasted_iota(I32, (nblk * t, t), 0) - (nblk - 1) * t
    query = lax.broadcasted_iota(I32, (nblk * t, t), 1)
    return key < query if strict else key <= query


def _key_rows(j, nblk, t):
    return pl.ds(pl.multiple_of(j * t, t), nblk * t)


def _value_cols(vx_ref, j, nblk, rows):
    tiles = [vx_ref[j + b, rows, :] for b in range(nblk)]
    return tiles[0] if nblk == 1 else jnp.concatenate(tiles, axis=1)


def _attn_a_kernel(q_ref, k_ref, vx_ref, cum_ref, o_ref, *scratch, t):
    m_sc, acc_sc = scratch[:HEADS], scratch[HEADS:]

    def tile(qi, q):
        _softmax_init(m_sc, acc_sc)

        def step(j, nblk, masked=False):
            rows = _key_rows(j, nblk, t)
            heads = [slice(LANES * h, LANES * (h + 1)) for h in range(HEADS)]
            scores = [_dot(k_ref[rows, hs], q[hs, :]) for hs in heads]
            for h in range(HEADS):
                s = scores[h] - _lane_repeat(cum_ref[h, rows, :], t)
                scores[h] = jnp.where(_causal_mask(t, nblk), s, -jnp.inf) if masked else s
            _softmax_steps(scores, [_value_cols(vx_ref, j, nblk, hs) for hs in heads],
                           m_sc, acc_sc)

        _sweep_causal_blocks(qi, step)
        return jnp.concatenate([_softmax_result(acc_sc[h][...]) for h in range(HEADS)], axis=0).T

    _for_each_query_tile(q_ref, o_ref, t, tile)


def _attn_scratch(n, t, rows=LANES):
    return [pltpu.VMEM((1, t), F32)] * n + [pltpu.VMEM((rows, t), F32)] * n


def _attn_a_call(qT, kp, vxT, cumrep):
    B, nb, _, t = qT.shape
    S = nb * t
    return pl.pallas_call(
        functools.partial(_attn_a_kernel, t=t),
        grid=(B,),
        in_specs=[_seq_cols_spec(nb, HW, t), _seq_rows_spec(S, HW), _seq_cols_spec(nb, HW, t),
                  pl.BlockSpec((None, HEADS, S, LANES), lambda b: (b, 0, 0, 0))],
        out_specs=_seq_rows_spec(S, GROUP_W),
        out_shape=jax.ShapeDtypeStruct((B, S, GROUP_W), F32),
        scratch_shapes=_attn_scratch(HEADS, t),
        compiler_params=_params("arbitrary"),
        name="attn_forget",
    )(qT, kp, vxT, cumrep)


STICK_SPLIT_TERMS = 3


def _attn_b_kernel(q_ref, k_ref, v_ref, after_ref, o_ref, *scratch, t):
    tile = functools.partial(_attn_b_tile, k_ref=k_ref, v_ref=v_ref, after_ref=after_ref,
                             r_sc=scratch[:HEADS], acc_sc=scratch[HEADS:], t=t)
    _for_each_query_tile(q_ref, o_ref, t, tile)


def _attn_b_tile(qi, q_ref, *, k_ref, v_ref, after_ref, r_sc, acc_sc, t):
    for h in range(HEADS):
        r_sc[h][...] = jnp.zeros(r_sc[h].shape, F32)
        acc_sc[h][...] = jnp.zeros(acc_sc[h].shape, F32)

    def step(j, nblk, masked=False):
        after = after_ref[...]
        rows = _key_rows(j, nblk, t)
        heads = [slice(LANES * h, LANES * (h + 1)) for h in range(HEADS)]
        zs = [_dot(k_ref[rows, hs], q_ref[hs, :]) for hs in heads]
        lbs, splits, later_sums = [], [], []
        for h in range(HEADS):
            lb = _log2_sigmoid(zs[h])
            lm = lb - zs[h]
            if masked:
                lm = jnp.where(_causal_mask(t, nblk, strict=True), lm, 0.0)
            blocks = [lm[b * t:(b + 1) * t] for b in range(nblk)]
            splits.append([_split_bf16(blk, STICK_SPLIT_TERMS) for blk in blocks])
            sums = [jnp.sum(blk, axis=0, keepdims=True) for blk in blocks]
            r_old = r_sc[h][...]
            total = sums[0]
            for s_ in sums[1:]:
                total = total + s_
            r_sc[h][...] = r_old + total
            lbs.append(lb + r_old)
            later, run = [], None
            for b in reversed(range(nblk)):
                later.append(run)
                run = sums[b] if run is None else run + sums[b]
            later_sums.append(later[::-1])
        suffixes = [[functools.reduce(lambda a, b: a + b, [_dot(after, term) for term in terms])
                     for terms in splits[h]] for h in range(HEADS)]
        ws = []
        for h in range(HEADS):
            parts = [suffixes[h][b] if later_sums[h][b] is None else suffixes[h][b] + later_sums[h][b]
                     for b in range(nblk)]
            suffix = parts[0] if nblk == 1 else jnp.concatenate(parts, axis=0)
            w = jnp.exp2(lbs[h] + suffix)
            if masked:
                w = jnp.where(_causal_mask(t, nblk, strict=True), w, 0.0)
            ws.append(w)
        for h in range(HEADS):
            acc_sc[h][...] += _dot(_value_cols(v_ref, j, nblk, slice(HEAD_DIM * h, HEAD_DIM * (h + 1))),
                                   ws[h])

    @pl.when(qi == 0)
    def _():
        step(0, 1, True)

    @pl.when(qi > 0)
    def _():
        step(qi - 1, 2, True)

    rest = jnp.maximum(qi - 1, 0)

    @pl.when(rest % 2 == 1)
    def _():
        step(rest - 1, 1)

    pairs = rest // 2

    def body(i, carry):
        step(2 * (pairs - 1 - i), 2)
        return carry

    lax.fori_loop(0, pairs, body, 0)
    return jnp.concatenate([acc_sc[h][...] for h in range(HEADS)], axis=0).T


def _attn_b_call(qT, kp, vT, after):
    B, nb, _, t = qT.shape
    S = nb * t
    return pl.pallas_call(
        functools.partial(_attn_b_kernel, t=t),
        grid=(B,),
        in_specs=[_seq_cols_spec(nb, HW, t), _seq_rows_spec(S, HW), _seq_cols_spec(nb, GROUP_W, t),
                  _const_spec((t, t))],
        out_specs=_seq_rows_spec(S, GROUP_W),
        out_shape=jax.ShapeDtypeStruct((B, S, GROUP_W), F32),
        scratch_shapes=_attn_scratch(HEADS, t, HEAD_DIM),
        compiler_params=_params("arbitrary"),
        name="attn_stick",
    )(qT, kp, vT, after)


def _attn_c_kernel(q_ref, k_ref, vx_ref, lamv_ref, subg_ref, o_ref, *scratch,
                   t, slopes, lambda_init):
    tile = functools.partial(_attn_c_tile, k_ref=k_ref, vx_ref=vx_ref, lamv_ref=lamv_ref,
                             subg_ref=subg_ref, m_sc=scratch[:2 * HEADS], acc_sc=scratch[2 * HEADS:],
                             t=t, slopes=slopes, lambda_init=lambda_init)
    _for_each_query_tile(q_ref, o_ref, t, tile)


def _attn_c_tile(qi, q_ref, *, k_ref, vx_ref, lamv_ref, subg_ref, m_sc, acc_sc,
                 t, slopes, lambda_init):
    _softmax_init(m_sc, acc_sc)

    def step(j, nblk, masked=False):
        rows = _key_rows(j, nblk, t)
        heads = [slice(LANES * h, LANES * (h + 1)) for h in range(HEADS)]
        scores = [_dot(k_ref[rows, heads[g // 2]], q_ref[LANES * g:LANES * (g + 1), :])
                  for g in range(2 * HEADS)]
        for g in range(2 * HEADS):
            s = scores[g] + _lane_repeat(_alibi(slopes[g // 2], j, qi, t, nblk), t)
            scores[g] = jnp.where(_causal_mask(t, nblk), s, -jnp.inf) if masked else s
        _softmax_steps(scores, [_value_cols(vx_ref, j, nblk, heads[g // 2])
                                for g in range(2 * HEADS)], m_sc, acc_sc)

    _sweep_causal_blocks(qi, step)

    lv = lamv_ref[...]
    lam = (jnp.exp(jnp.sum(lv[0:1] * lv[1:2], axis=-1, keepdims=True))
           - jnp.exp(jnp.sum(lv[2:3] * lv[3:4], axis=-1, keepdims=True)) + lambda_init)
    outs = []
    for h in range(HEADS):
        o = (_softmax_result(acc_sc[2 * h][...])
             - lam * _softmax_result(acc_sc[2 * h + 1][...]))
        ms = jnp.mean(o * o, axis=0, keepdims=True)
        outs.append(o * lax.rsqrt(ms + RMS_EPS) * subg_ref[...] * (1.0 - lambda_init))
    return jnp.concatenate(outs, axis=0).T


def _attn_c_call(qT, kp, vxT, lamv, subg_col, slopes, lambda_init):
    B, nb, _, t = qT.shape
    S = nb * t
    return pl.pallas_call(
        functools.partial(_attn_c_kernel, t=t, slopes=slopes, lambda_init=lambda_init),
        grid=(B,),
        in_specs=[_seq_cols_spec(nb, 2 * HW, t), _seq_rows_spec(S, HW), _seq_cols_spec(nb, HW, t),
                  _const_spec(lamv.shape), _const_spec(subg_col.shape)],
        out_specs=_seq_rows_spec(S, GROUP_W),
        out_shape=jax.ShapeDtypeStruct((B, S, GROUP_W), F32),
        scratch_shapes=_attn_scratch(2 * HEADS, t),
        compiler_params=_params("arbitrary"),
        name="attn_diff",
    )(qT, kp, vxT, lamv, subg_col)


def _fold_rows(x, group=SUBLANES):
    return jnp.sum(x.reshape(x.shape[0] // group, group, x.shape[1]), axis=0)


PACKED_ROWS = 2 * SUBLANES
DIGIT_BITS = 8
N_DIGITS = 32 // DIGIT_BITS
DIGIT_MASK = (1 << DIGIT_BITS) - 1


def _fold_packed(x):
    slabs = [x[i * PACKED_ROWS:(i + 1) * PACKED_ROWS] for i in range(x.shape[0] // PACKED_ROWS)]
    while len(slabs) > 1:
        slabs = [a + b for a, b in zip(slabs[0::2], slabs[1::2])]
    return slabs[0]


def _attn_d_kernel(q_ref, iq_all_ref, iw_all_ref, dk_ref, dvx_ref, ik_ref, before_ref, o_ref,
                   keys_sc, tau_sc, *scratch, t, topk, slopes):
    def tile(qi, q):
        return _attn_d_tile(qi, q, iq_all_ref.at[qi], iw_all_ref.at[qi], dk_ref, dvx_ref, ik_ref,
                            before_ref, keys_sc, tau_sc, scratch, t=t, topk=topk, slopes=slopes)

    _for_each_query_tile(q_ref, o_ref, t, tile)


def _attn_d_tile(qi, q_ref, iq_ref, iw_ref, dk_ref, dvx_ref, ik_ref, before_ref,
                 keys_sc, tau_sc, scratch, *, t, topk, slopes):
    digit_sc, scratch = scratch[:N_DIGITS], scratch[N_DIGITS:]
    m_sc, acc_sc = scratch[:HEADS], scratch[HEADS:]

    w = iw_ref[...] * IDX_HEADS ** -0.5

    def index_step(j, nblk, masked=False):
        ikb = ik_ref[_key_rows(j, nblk, t), :]
        zs = [_dot(ikb, iq_ref[LANES * hh:LANES * (hh + 1), :]) for hh in range(IDX_HEADS)]
        sc = w[0:1, :] * jnp.maximum(zs[0], 0.0)
        for hh in range(1, IDX_HEADS):
            sc = sc + w[hh:hh + 1, :] * jnp.maximum(zs[hh], 0.0)
        sc = jnp.where(sc == 0.0, 0.0, sc)
        if masked:
            sc = jnp.where(_causal_mask(t, nblk), sc, NEG_INF)
        bits = pltpu.bitcast(sc, I32)
        keys = jnp.where(bits < 0, bits ^ 0x7FFFFFFF, bits)
        ukeys = keys ^ INT_MIN
        digits = [(lax.shift_right_logical(ukeys, DIGIT_BITS * (N_DIGITS - 1 - d)) & DIGIT_MASK)
                  .astype(F32).astype(BF16) for d in range(N_DIGITS)]
        for b in range(nblk):
            rows = slice(b * t, (b + 1) * t)
            keys_sc[j + b] = keys[rows]
            for d in range(N_DIGITS):
                digit_sc[d][j + b] = digits[d][rows]

    _sweep_causal_blocks(qi, index_step)

    qpos = qi * t + lax.broadcasted_iota(I32, (1, t), 1)
    kt = jnp.minimum(topk, qpos + 1).astype(F32)

    def count_ge(cand):
        def body(j, acc):
            return acc + _fold_rows((keys_sc[j] >= cand).astype(F32))
        acc = lax.fori_loop(0, qi + 1, body, jnp.zeros((SUBLANES, t), F32))
        return jnp.sum(acc, axis=0, keepdims=True)

    one_b = jnp.ones((), BF16)
    zero_b = jnp.zeros((), BF16)

    def count_digit_ge(vals_sc, cand):
        cand_b = cand.astype(F32).astype(BF16)

        def block_count(j):
            return _fold_packed(jnp.where(vals_sc[j] >= cand_b, one_b, zero_b))

        def pair(i, acc):
            return acc + (block_count(2 * i) + block_count(2 * i + 1)).astype(F32)

        acc = lax.fori_loop(0, (qi + 1) // 2, pair, jnp.zeros((PACKED_ROWS, t), F32))
        acc = lax.cond(qi % 2 == 0, lambda a: a + block_count(qi).astype(F32), lambda a: a, acc)
        return jnp.sum(acc, axis=0, keepdims=True)

    def keep_matching(vals_sc, match_sc, match):
        match_b = match.astype(F32).astype(BF16)

        def body(j, carry):
            vals_sc[j] = jnp.where(match_sc[j] == match_b, vals_sc[j], -one_b)
            return carry
        lax.fori_loop(0, qi + 1, body, 0)

    zero = jnp.zeros((1, t), I32)
    rank = kt
    above = jnp.zeros((1, t), F32)
    tau_u = zero
    digit = zero
    for d in range(N_DIGITS):
        if d > 0:
            keep_matching(digit_sc[d], digit_sc[d - 1], digit)

        def bit_body(i, prefix, d=d, rank=rank):
            cand = prefix + lax.shift_left(jnp.int32(1), DIGIT_BITS - 1 - i)
            return jnp.where(count_digit_ge(digit_sc[d], cand) >= rank, cand, prefix)

        digit = lax.fori_loop(0, DIGIT_BITS, bit_body, zero)
        tau_u = lax.shift_left(tau_u, DIGIT_BITS) | digit
        if d < N_DIGITS - 1:
            higher = count_digit_ge(digit_sc[d], digit + 1)
            above = above + higher
            rank = rank - higher
    tau = tau_u ^ INT_MIN
    tau_sc[...] = tau
    excess = jnp.max(above + count_digit_ge(digit_sc[N_DIGITS - 1], digit) - kt)

    @pl.when(excess > 0.0)
    def _():
        need = kt - count_ge(tau + 1)

        def tie_body(j, seen):
            kj = keys_sc[j]
            eq = kj == tau
            eqb = eq.astype(BF16)
            earlier = _dot(before_ref[...], eqb) + seen
            keys_sc[j] = jnp.where(eq & (earlier >= need), INT_MIN, kj)
            return seen + jnp.sum(eqb.astype(F32), axis=0, keepdims=True)

        lax.fori_loop(0, qi + 1, tie_body, jnp.zeros((1, t), F32))

    _softmax_init(m_sc, acc_sc)

    def step(j, nblk):
        keys = [keys_sc[j + b] for b in range(nblk)]
        sel = (keys[0] if nblk == 1 else jnp.concatenate(keys, axis=0)) >= tau_sc[...]
        kb = dk_ref[_key_rows(j, nblk, t), :]
        vx = _value_cols(dvx_ref, j, nblk, slice(None))
        scores = [_dot(kb, q_ref[LANES * h:LANES * (h + 1), :]) for h in range(HEADS)]
        for h in range(HEADS):
            bias = _lane_repeat(_alibi(slopes[h], j, qi, t, nblk), t)
            scores[h] = jnp.where(sel, scores[h] + bias, -jnp.inf)
        _softmax_steps(scores, [vx] * HEADS, m_sc, acc_sc)

    _sweep_earlier_blocks(qi + 1, step)
    return jnp.concatenate([_softmax_result(acc_sc[h][...]) for h in range(HEADS)], axis=0).T


def _attn_d_call(qT, iqT, miscT, dkp, dvxT, ikp, before, topk, slopes):
    B, nb, _, t = qT.shape
    S = nb * t
    iw_block = MISC_IW // IDX_HEADS
    return pl.pallas_call(
        functools.partial(_attn_d_kernel, t=t, topk=topk, slopes=slopes),
        grid=(B,),
        in_specs=[_seq_cols_spec(nb, HW, t), _seq_cols_spec(nb, IDX_HEADS * LANES, t),
                  pl.BlockSpec((None, nb, IDX_HEADS, t), lambda b: (b, 0, iw_block, 0)),
                  _seq_rows_spec(S, LANES), _seq_cols_spec(nb, LANES, t), _seq_rows_spec(S, LANES),
                  _const_spec((t, t))],
        out_specs=_seq_rows_spec(S, GROUP_W),
        out_shape=jax.ShapeDtypeStruct((B, S, GROUP_W), F32),
        scratch_shapes=([pltpu.VMEM((nb, t, t), I32), pltpu.VMEM((1, t), I32)]
                        + [pltpu.VMEM((nb, t, t), BF16)] * N_DIGITS + _attn_scratch(HEADS, t)),
        compiler_params=_params("arbitrary"),
        name="attn_sparse",
    )(qT, iqT, miscT, dkp, dvxT, ikp, before)


ROUTER_TERMS = 3
ROUTER_STRIDE = 40


def _out_proj_kernel(x_ref, oa_ref, ob_ref, oc_ref, od_ref, beta_ref, wo_ref, gate_ref,
                     g2_ref, scale_ref, shift_ref, wr_ref, br_ref, earlier_ref,
                     x1_ref, h2x_ref, route_ref, cnt_ref, cnt_sc):
    d_model = x_ref.shape[1]
    acc = None
    for i, o_ref in enumerate((oa_ref, ob_ref, oc_ref, od_ref)):
        sl = slice(GROUP_W * i, GROUP_W * (i + 1))
        mix = o_ref[...] * beta_ref[:, sl]
        part = jnp.dot(mix, wo_ref[sl, :], preferred_element_type=F32)
        acc = part if acc is None else acc + part
    x1 = x_ref[...] + gate_ref[...] * acc
    x1_ref[...] = x1
    ms = jnp.mean(x1 * x1, axis=-1, keepdims=True)
    h2 = x1 * lax.rsqrt(ms + RMS_EPS) * g2_ref[...]
    h2 = h2 * (1.0 + scale_ref[...]) + shift_ref[...]
    h2x_ref[:, :d_model] = h2

    packed = functools.reduce(lambda a, b: a + b,
                              [_dot(term, wr_ref[...]) for term in _split_bf16(h2, ROUTER_TERMS)])
    logits = packed
    for k in range(1, ROUTER_TERMS):
        logits = logits + pltpu.roll(packed, LANES - k * ROUTER_STRIDE, 1)
    logits = logits + br_ref[...]
    lt = logits.T
    tm = lt.shape[1]
    g = lt[0:N_GROUPS]
    gmax = jnp.max(g, axis=0, keepdims=True)
    gi = lax.broadcasted_iota(I32, g.shape, 0)
    gidx = jnp.min(jnp.where(g == gmax, gi, N_GROUPS), axis=0, keepdims=True)
    g_prob = 1.0 / jnp.sum(jnp.exp(g - gmax), axis=0, keepdims=True)
    e_sel = jnp.zeros((EXPERTS_PER_GROUP, tm), F32)
    for gg in range(N_GROUPS):
        lo = N_GROUPS + EXPERTS_PER_GROUP * gg
        e_sel = e_sel + jnp.where(gidx == gg, lt[lo:lo + EXPERTS_PER_GROUP], 0.0)
    ei = lax.broadcasted_iota(I32, e_sel.shape, 0)
    v1 = jnp.max(e_sel, axis=0, keepdims=True)
    i1 = jnp.min(jnp.where(e_sel == v1, ei, EXPERTS_PER_GROUP), axis=0, keepdims=True)
    rest = jnp.where(ei == i1, -jnp.inf, e_sel)
    v2 = jnp.max(rest, axis=0, keepdims=True)
    i2 = jnp.min(jnp.where(rest == v2, ei, EXPERTS_PER_GROUP), axis=0, keepdims=True)
    e2 = jnp.exp(v2 - v1)
    w1 = g_prob / (1.0 + e2)
    w2 = g_prob * e2 / (1.0 + e2)
    in_group = jnp.where(ei == i1, w1, 0.0) + jnp.where(ei == i2, w2, 0.0)
    cw = jnp.concatenate([in_group, jnp.zeros((LANES - EXPERTS_PER_GROUP, tm), F32)], axis=0)
    h2x_ref[:, d_model:] = cw.T

    @pl.when(pl.program_id(0) == 0)
    def _():
        cnt_sc[...] = jnp.zeros(cnt_sc.shape, F32)

    rows = lax.broadcasted_iota(I32, (SUBLANES, tm), 0)
    onehot = (rows == gidx).astype(F32)
    seen = _dot(onehot.astype(BF16), earlier_ref[...]) + cnt_sc[...]
    rank = jnp.sum(onehot * seen, axis=0, keepdims=True)
    route_ref[...] = jnp.concatenate(
        [gidx, rank.astype(I32), jnp.zeros((SUBLANES - 2, tm), I32)], axis=0)
    cnt_sc[...] = cnt_sc[...] + jnp.sum(onehot, axis=1, keepdims=True)
    cnt_ref[...] = cnt_sc[:, :LANES]


def _out_proj_call(xf, outs, beta, w_out, gate1, g2, scale2, shift2, w_r, b_r, earlier, seq, tm):
    N, D = xf.shape
    per_b = seq // tm
    row = lambda i: (i, 0)
    full = lambda i: (0, 0)
    per_batch = pl.BlockSpec((None, 1, D), lambda i: (i // per_b, 0, 0))
    return pl.pallas_call(
        _out_proj_kernel,
        grid=(N // tm,),
        in_specs=([pl.BlockSpec((tm, D), row)] + [pl.BlockSpec((tm, GROUP_W), row)] * 4
                  + [pl.BlockSpec((1, D), full), pl.BlockSpec((D, D), full), per_batch,
                     pl.BlockSpec((1, D), full), per_batch, per_batch,
                     pl.BlockSpec((D, LANES), full), pl.BlockSpec((1, LANES), full),
                     pl.BlockSpec((tm, tm), full)]),
        out_specs=[pl.BlockSpec((tm, D), row), pl.BlockSpec((tm, D + LANES), row),
                   pl.BlockSpec((SUBLANES, tm), lambda i: (0, i)),
                   pl.BlockSpec((SUBLANES, LANES), full)],
        out_shape=[jax.ShapeDtypeStruct((N, D), F32), jax.ShapeDtypeStruct((N, D + LANES), F32),
                   jax.ShapeDtypeStruct((SUBLANES, N), I32),
                   jax.ShapeDtypeStruct((SUBLANES, LANES), F32)],
        scratch_shapes=[pltpu.VMEM((SUBLANES, tm), F32)],
        compiler_params=_params("arbitrary"),
        name="out_proj_router",
    )(xf, *outs, beta, w_out, gate1, g2, scale2, shift2, w_r, b_r, earlier)


MOE_TILE = 512
DMA_PRIORITIES = 2

def _dispatch_kernel(slot_ref, h2x_ref, zeros_ref, xs_ref, sem):
    del zeros_ref

    def row_copy(g, u):
        return pltpu.make_async_copy(h2x_ref.at[g, pl.ds(u, 1), :],
                                     xs_ref.at[pl.ds(slot_ref[0, g * SUBLANES + u], 1), :], sem)

    def issue(g, carry):
        for u in range(SUBLANES):
            row_copy(g, u).start(priority=u % DMA_PRIORITIES)
        return carry

    lax.fori_loop(0, h2x_ref.shape[0], issue, 0)

    def drain(g, carry):
        for u in range(SUBLANES):
            row_copy(g, u).wait()
        return carry

    lax.fori_loop(0, h2x_ref.shape[0], drain, 0)


def _dispatch_call(slot, h2x, xs_zero, tm):
    N, DX = h2x.shape
    return pl.pallas_call(
        _dispatch_kernel,
        grid=(N // tm,),
        in_specs=[pl.BlockSpec((None, 1, tm), lambda i: (i, 0, 0), memory_space=pltpu.SMEM),
                  pl.BlockSpec((tm // SUBLANES, SUBLANES, DX), lambda i: (i, 0, 0)),
                  pl.BlockSpec(memory_space=pl.ANY)],
        out_specs=pl.BlockSpec(memory_space=pl.ANY),
        out_shape=jax.ShapeDtypeStruct(xs_zero.shape, F32),
        scratch_shapes=[pltpu.SemaphoreType.DMA(())],
        input_output_aliases={2: 0},
        compiler_params=_params("arbitrary"),
        name="moe_dispatch",
    )(slot.reshape(N // tm, 1, tm), h2x.reshape(N // SUBLANES, SUBLANES, DX), xs_zero)


def _expert_kernel(group_ref, valid_ref, xs_ref, w1_ref, w3_ref, w2_ref, y_ref):
    del group_ref
    valid = valid_ref[pl.program_id(0)]
    d_model = y_ref.shape[1]

    @pl.when(valid == 0)
    def _():
        y_ref[...] = jnp.zeros(y_ref.shape, F32)

    @pl.when(valid > 0)
    def _():
        x = xs_ref[:, :d_model]
        cw = xs_ref[:, d_model:]
        acc = None
        for e in range(EXPERTS_PER_GROUP):
            a = _dot(x, w1_ref[e])
            b = _dot(x, w3_ref[e])
            hid = a * jax.nn.sigmoid(a) * b * jnp.broadcast_to(cw[:, e:e + 1], a.shape)
            part = _dot(hid, w2_ref[e])
            acc = part if acc is None else acc + part
        y_ref[...] = acc


def _expert_call(tile_group, tile_valid, xs, w1, w3, w2, layer):
    P, DX = xs.shape
    D, FF = w1.shape[-2:]
    tr = MOE_TILE

    def group_spec(rows, cols):
        return pl.BlockSpec((None, None, EXPERTS_PER_GROUP, rows, cols),
                            lambda i, g, v: (layer, g[i], 0, 0, 0),
                            pipeline_mode=pl.Buffered(1))

    grid_spec = pltpu.PrefetchScalarGridSpec(
        num_scalar_prefetch=2,
        grid=(P // tr,),
        in_specs=[pl.BlockSpec((tr, DX), lambda i, g, v: (i, 0)),
                  group_spec(D, FF), group_spec(D, FF), group_spec(FF, D)],
        out_specs=pl.BlockSpec((tr, D), lambda i, g, v: (i, 0)),
    )
    return pl.pallas_call(
        _expert_kernel,
        grid_spec=grid_spec,
        out_shape=jax.ShapeDtypeStruct((P, D), F32),
        compiler_params=_params("arbitrary"),
        name="moe_experts",
    )(tile_group, tile_valid, xs, w1, w3, w2)


def _residual_kernel(slot_ref, slot_next_ref, x1_ref, gate_ref, ys_ref, o_ref, buf, sems):
    i = pl.program_id(0)

    def gather(slots, b, wait):
        def body(g, carry):
            for u in range(SUBLANES):
                cp = pltpu.make_async_copy(ys_ref.at[pl.ds(slots[0, g * SUBLANES + u], 1), :],
                                           buf.at[b, g, pl.ds(u, 1), :], sems.at[b])
                cp.wait() if wait else cp.start(priority=u % DMA_PRIORITIES)
            return carry
        lax.fori_loop(0, buf.shape[1], body, 0)

    @pl.when(i == 0)
    def _():
        gather(slot_ref, 0, False)

    @pl.when(i + 1 < pl.num_programs(0))
    def _():
        gather(slot_next_ref, (i + 1) % 2, False)

    gather(slot_ref, i % 2, True)
    o_ref[...] = x1_ref[...] + gate_ref[...] * buf[i % 2].reshape(x1_ref.shape)


def _residual_call(slot, x1, gate2, ys, seq, tm):
    N, D = x1.shape
    n = N // tm
    per_b = seq // tm
    row = pl.BlockSpec((tm, D), lambda i: (i, 0))
    slot3 = slot.reshape(n, 1, tm)
    return pl.pallas_call(
        _residual_kernel,
        grid=(n,),
        in_specs=[pl.BlockSpec((None, 1, tm), lambda i: (i, 0, 0), memory_space=pltpu.SMEM),
                  pl.BlockSpec((None, 1, tm), lambda i: (jnp.minimum(i + 1, n - 1), 0, 0),
                               memory_space=pltpu.SMEM),
                  row, pl.BlockSpec((None, 1, D), lambda i: (i // per_b, 0, 0)),
                  pl.BlockSpec(memory_space=pl.ANY)],
        out_specs=row,
        out_shape=jax.ShapeDtypeStruct((N, D), F32),
        scratch_shapes=[pltpu.VMEM((2, tm // SUBLANES, SUBLANES, D), F32),
                        pltpu.SemaphoreType.DMA((2,))],
        compiler_params=_params("arbitrary"),
        name="moe_residual",
    )(slot3, slot3, x1, gate2, ys)


def _moe_routing(route, cnt, n_tokens):
    tr = MOE_TILE
    n_tiles = n_tokens // tr + N_GROUPS
    counts = cnt[:N_GROUPS, 0].astype(I32)
    padded = (counts + tr - 1) // tr * tr
    ends = jnp.cumsum(padded)
    starts = ends - padded
    group, rank = route[0], route[1]
    slot = starts[group] + rank
    tile_start = jnp.arange(n_tiles, dtype=I32) * tr
    tile_group = jnp.minimum(jnp.sum((tile_start[:, None] >= ends[None, :]).astype(I32), axis=1),
                             N_GROUPS - 1)
    tile_valid = jnp.clip(starts[tile_group] + counts[tile_group] - tile_start, 0, tr)
    tile_valid = jnp.where(tile_start < ends[-1], tile_valid, 0)
    return slot, tile_group, tile_valid


def _block_diag_mean(width, group, valid_in_128=None):
    i = jnp.arange(width)
    same = (i[:, None] // group) == (i[None, :] // group)
    if valid_in_128 is not None:
        same = same & ((i[:, None] % LANES) < valid_in_128) & ((i[None, :] % LANES) < valid_in_128)
    return jnp.where(same, 1.0 / group, 0.0).astype(F32)


def kernel(x, c, ada_w, ada_b, norm1_g, norm2_g, w_in, b_f, qn_a, kn_a, qn_c, kn_c,
           lam_q1, lam_k1, lam_q2, lam_k2, subln_g, qn_d, kn_d, mix_beta, w_out,
           w_group, b_group, w_expert, b_expert, w1, w3, w2):
    B, S, D = x.shape
    L = ada_w.shape[0]
    N = B * S
    topk = min(TOPK_MAX, S // 4)
    t = ATTN_BLOCK
    tm = t
    slopes = [2.0 ** (-8.0 * i / (2 * HEADS)) for i in range(1, 2 * HEADS + 1)]
    slopes_c, slopes_d = tuple(slopes[0::2]), tuple(slopes[1::2])

    idx_t = jnp.arange(t)
    after = (idx_t[None, :] > idx_t[:, None]).astype(BF16)
    before = (idx_t[None, :] < idx_t[:, None]).astype(BF16)
    upto = (idx_t[None, :] <= idx_t[:, None]).astype(BF16)
    g64 = _block_diag_mean(GROUP_W, HEAD_DIM)
    g32 = _block_diag_mean(GROUP_W, DIFF_DIM)
    lane = jnp.arange(LANES)
    cum_sel = jnp.stack([jnp.broadcast_to((lane == MISC_AF + h)[:, None], (LANES, LANES))
                         for h in range(HEADS)]).astype(BF16)

    mod = _ada_call(c, ada_w, ada_b)
    xf = x.reshape(N, D)

    for l in range(L):
        m6 = mod[l].reshape(B, 6, 1, D)
        shift1, scale1, gate1, shift2, scale2, gate2 = (m6[:, i] for i in range(6))

        ones = jnp.ones((GROUP_W - HEAD_DIM,), F32)
        gains = jnp.stack([jnp.tile(qn_a[l], HEADS) * (HEAD_DIM ** -0.5 * LOG2E),
                           jnp.tile(kn_a[l], HEADS),
                           jnp.tile(qn_c[l], 2 * HEADS) * (DIFF_DIM ** -0.5 * LOG2E),
                           jnp.tile(kn_c[l], 2 * HEADS),
                           jnp.tile(qn_d[l], HEADS) * (HEAD_DIM ** -0.5 * LOG2E),
                           jnp.concatenate([kn_d[l], ones])]).astype(F32)
        gains = jnp.concatenate([gains, jnp.zeros((2, GROUP_W), F32)], axis=0)

        outs = _in_proj_call(xf, scale1, shift1, norm1_g[l].reshape(1, D), w_in, l, gains,
                             g64, g32, B, S, tm)
        sec = {name: o for (name, _, _, _, _), o in zip(OUTPUTS, outs)}
        for name, src, width, orient, ones_ in OUTPUTS:
            if orient == "rows":
                sec[name] = sec[name].reshape(B, S, width)
        misc = outs[len(OUTPUTS)].reshape(B, S, LANES)
        miscT = outs[len(OUTPUTS) + 1]

        bf_row = jnp.zeros((1, LANES), F32).at[0, MISC_AF:MISC_AF + HEADS].set(b_f[l].astype(F32))
        cumrep = _cum_call(misc, bf_row, upto, cum_sel)

        o_a = _attn_a_call(sec["a_qT"], sec["a_kp"], sec["a_vxT"], cumrep)
        o_b = _attn_b_call(sec["b_qT"], sec["b_kp"], sec["b_vT"], after)
        lambda_init = 0.8 - 0.6 * math.exp(-0.3 * l)
        lamv = jnp.stack([lam_q1[l], lam_k1[l], lam_q2[l], lam_k2[l]]).astype(F32)
        o_c = _attn_c_call(sec["c_qT"], sec["c_kp"], sec["c_vxT"], lamv,
                           subln_g[l].reshape(HEAD_DIM, 1).astype(F32), slopes_c, lambda_init)
        o_d = _attn_d_call(sec["d_qT"], sec["i_qT"], miscT, sec["d_kp"], sec["d_vxT"], sec["i_kp"],
                           before, topk, slopes_d)

        w_router = jnp.concatenate([w_group[l], w_expert[l]], axis=1)
        terms = _split_bf16(w_router, ROUTER_TERMS)
        gap = jnp.zeros((D, ROUTER_STRIDE - N_GROUPS - N_EXPERTS), BF16)
        w_r = jnp.concatenate([piece for term in terms for piece in (term, gap)], axis=1)
        w_r = jnp.concatenate([w_r, jnp.zeros((D, LANES - ROUTER_TERMS * ROUTER_STRIDE), BF16)],
                              axis=1)
        b_r = jnp.concatenate([b_group[l], b_expert[l],
                               jnp.zeros((LANES - N_GROUPS - N_EXPERTS,), F32)]).reshape(1, LANES)
        x1, h2x, route, cnt = _out_proj_call(
            xf, [o.reshape(N, GROUP_W) for o in (o_a, o_b, o_c, o_d)], mix_beta[l].reshape(1, D),
            w_out[l], gate1, norm2_g[l].reshape(1, D), scale2, shift2, w_r, b_r,
            after, S, tm)

        slot, tile_group, tile_valid = _moe_routing(route, cnt, N)
        xs = _dispatch_call(slot, h2x, jnp.zeros((tile_group.shape[0] * MOE_TILE, D + LANES), F32), tm)
        ys = _expert_call(tile_group, tile_valid, xs, w1, w3, w2, l)
        xf = _residual_call(slot, x1, gate2, ys, S, tm)

    return xf.reshape(B, S, D)
```
